```python
import math
import jax, jax.numpy as jnp
from jax import lax
import numpy as np

D_MODEL = 1024
BATCH = 32
SEQ = 256
DEPTH = 2
DEC_BATCH = 4
DEC_SEQ = 2048
PAST_LEN = 256

F32 = jnp.float32
EPS = 1e-6
GRID_W = 64

HY_WIDTH = 256
HY_ORDER = 2
HY_SHORT = 3
HY_EMB = 33
HY_BANDS = (HY_EMB - 1) // 2
HY_FFN = 64
HY_FAST_DECAY = 0.3
HY_SLOW_DECAY = 1.5
HY_TARGET = 1e-2

MLA_HEADS = 4
MLA_Q_LORA = 256
MLA_KV_LORA = 128
MLA_NOPE = 64
MLA_ROPE = 32
MLA_V = 64
ROPE_BASE = 10000.0
Q_BLOCK = 128

RET_HEADS = 4
RET_DK = 64
RET_DV = 64
RET_CHUNK = 128

CONF_WIDTH = 256
CONF_KERNEL = 31

D_FF = ((8 * D_MODEL // 3 + 255) // 256) * 256

N_BRANCH = 4
HY_COLS = 3 * HY_WIDTH
MLA_COLS = MLA_Q_LORA + MLA_KV_LORA + MLA_ROPE
RET_COLS = 2 * RET_HEADS * RET_DK + 2 * RET_HEADS * RET_DV
CONF_COLS = 2 * CONF_WIDTH
IN_COLS = HY_COLS + MLA_COLS + RET_COLS + CONF_COLS
IN_SPLITS = [HY_COLS, HY_COLS + MLA_COLS, HY_COLS + MLA_COLS + RET_COLS]

kernel_name = 'hybrid_diffusion_prefix_trunk_step'


def rmsnorm(x, g):
    xf = x.astype(F32)
    y = xf * lax.rsqrt(jnp.mean(xf * xf, axis=-1, keepdims=True) + EPS)
    return (y * g.astype(F32)).astype(x.dtype)


def layernorm(x, g=None, b=None):
    xf = x.astype(F32)
    mu = jnp.mean(xf, axis=-1, keepdims=True)
    xc = xf - mu
    y = xc * lax.rsqrt(jnp.mean(xc * xc, axis=-1, keepdims=True) + EPS)
    if g is not None:
        y = y * g.astype(F32) + b.astype(F32)
    return y.astype(x.dtype)


def depthwise_conv(x, w, b):
    K = w.shape[0]
    y = lax.conv_general_dilated(
        x, w[:, None, :].astype(x.dtype), window_strides=(1,),
        padding=[(K // 2, K // 2)], dimension_numbers=('NWC', 'WIO', 'NWC'),
        feature_group_count=x.shape[-1])
    return y + b.astype(x.dtype)


def axial_rope_tables(L):
    rows = L // GRID_W
    row = jnp.repeat(jnp.arange(rows), GRID_W).astype(F32)
    col = jnp.tile(jnp.arange(GRID_W), rows).astype(F32)
    per_axis = MLA_ROPE // 4
    inv = ROPE_BASE ** (-jnp.arange(per_axis, dtype=F32) / per_axis)
    ang = jnp.concatenate([row[:, None] * inv, col[:, None] * inv], axis=-1)
    return jnp.cos(ang), jnp.sin(ang)


def apply_rope(x, cos, sin):
    xf = x.astype(F32)
    h = xf.shape[-1] // 2
    x1, x2 = xf[..., :h], xf[..., h:]
    return jnp.concatenate([x1 * cos - x2 * sin, x1 * sin + x2 * cos], axis=-1).astype(x.dtype)


def hyena_filters(L, w1, b1, w2, b2, w3):
    t = jnp.linspace(0.0, 1.0, L, dtype=F32)[:, None]
    w = 2.0 * math.pi * jnp.arange(L, dtype=F32)[:, None] / L
    f = jnp.linspace(1e-4, HY_BANDS - 1, HY_BANDS, dtype=F32)[None, :]
    z = jnp.concatenate([t, jnp.cos(f * w), -jnp.sin(f * w)], axis=-1)
    h = jnp.sin(z @ w1.astype(F32) + b1.astype(F32))
    h = jnp.sin(h @ w2.astype(F32) + b2.astype(F32))
    h = (h @ w3.astype(F32)).reshape(L, HY_ORDER, 2, HY_WIDTH)
    max_decay = math.log(HY_TARGET) / HY_FAST_DECAY
    min_decay = math.log(HY_TARGET) / HY_SLOW_DECAY
    deltas = jnp.abs(jnp.linspace(min_decay, max_decay, HY_WIDTH, dtype=F32))
    h = h * jnp.exp(-t[:, :, None, None] * deltas)
    return h / jnp.sum(jnp.abs(h), axis=(0, 2), keepdims=True)


def long_conv(v, h_fwd, h_bwd, bias):
    L, C = h_fwd.shape
    k = jnp.concatenate([h_fwd, jnp.zeros((1, C), F32), h_bwd[:0:-1]], axis=0)
    vf = jnp.fft.rfft(v.astype(F32), n=2 * L, axis=1)
    kf = jnp.fft.rfft(k, axis=0)
    y = jnp.fft.irfft(vf * kf[None], n=2 * L, axis=1)[:, :L]
    return (y + v.astype(F32) * bias.astype(F32)).astype(v.dtype)


def hyena_mixer(u, p):
    L = u.shape[1]
    z = depthwise_conv(u, p['hy_conv_w'], p['hy_conv_b'])
    v, x1, x2 = jnp.split(z, 3, axis=-1)
    filt = hyena_filters(L, p['hy_w1'], p['hy_b1'], p['hy_w2'], p['hy_b2'], p['hy_w3'])
    gates = (x1, x2)
    for o in range(HY_ORDER):
        v = gates[o] * long_conv(v, filt[:, o, 0], filt[:, o, 1], p['hy_bias'][o])
    return v


def block_attention(q_nope, q_rope, k_nope, k_rope, v):
    B, L, H, _ = q_nope.shape
    nb = L // Q_BLOCK
    scale = (MLA_NOPE + MLA_ROPE) ** -0.5
    qn = q_nope.reshape(B, nb, Q_BLOCK, H, MLA_NOPE).transpose(1, 0, 2, 3, 4)
    qr = q_rope.reshape(B, nb, Q_BLOCK, H, MLA_ROPE).transpose(1, 0, 2, 3, 4)

    def one_block(args):
        qn_b, qr_b = args
        s = (jnp.einsum('bqhd,bkhd->bhqk', qn_b, k_nope)
             + jnp.einsum('bqhr,bkr->bhqk', qr_b, k_rope))
        pr = jax.nn.softmax(s.astype(F32) * scale, axis=-1)
        return jnp.einsum('bhqk,bkhd->bqhd', pr.astype(v.dtype), v)

    o = lax.map(one_block, (qn, qr))
    return o.transpose(1, 0, 2, 3, 4).reshape(B, L, H * MLA_V)


def retention_scan(q, k, v, log_g, S0):
    B, L, H, _ = q.shape
    n = L // RET_CHUNK
    idx = jnp.arange(RET_CHUNK, dtype=F32)
    diff = idx[:, None] - idx[None, :]
    decay_in = jnp.where(diff >= 0, jnp.exp(jnp.maximum(diff, 0.0)[None] * log_g[:, None, None]), 0.0)
    decay_q = jnp.exp((idx[:, None] + 1.0) * log_g[None, :])
    decay_k = jnp.exp((RET_CHUNK - 1.0 - idx)[:, None] * log_g[None, :])
    decay_c = jnp.exp(RET_CHUNK * log_g)

    def chunk(a):
        return a.reshape(B, n, RET_CHUNK, H, a.shape[-1]).transpose(1, 0, 2, 3, 4)

    def step(S, blk):
        qc, kc, vc = blk
        att = jnp.einsum('bnhd,bmhd->bhnm', qc, kc) * decay_in
        o = (jnp.einsum('bhnm,bmhe->bnhe', att, vc)
             + jnp.einsum('bnhd,bhde->bnhe', qc, S) * decay_q[None, :, :, None])
        S = S * decay_c[None, :, None, None] + jnp.einsum('bmhd,bmhe->bhde', kc * decay_k[None, :, :, None], vc)
        return S, o

    S, o = lax.scan(step, S0, (chunk(q), chunk(k), chunk(v)))
    return o.transpose(1, 0, 2, 3, 4).reshape(B, L, H, v.shape[-1]), S


def retention_mixer(u, decay_logit, S0):
    B, L, _ = u.shape
    qk = RET_HEADS * RET_DK
    vd = RET_HEADS * RET_DV
    q, k, v, g = jnp.split(u, [qk, 2 * qk, 2 * qk + vd], axis=-1)
    q = q.astype(F32).reshape(B, L, RET_HEADS, RET_DK)
    k = k.astype(F32).reshape(B, L, RET_HEADS, RET_DK) * (RET_DK ** -0.5)
    v = v.astype(F32).reshape(B, L, RET_HEADS, RET_DV)
    log_g = jax.nn.log_sigmoid(decay_logit.astype(F32))
    S0 = S0.astype(F32)
    o_f, S_f = retention_scan(q, k, v, log_g[0], S0[:, 0])
    o_b, S_b = retention_scan(q[:, ::-1], k[:, ::-1], v[:, ::-1], log_g[1], S0[:, 1])
    o = layernorm(o_f + o_b[:, ::-1])
    y = jax.nn.silu(g.astype(F32)) * o.reshape(B, L, vd)
    return y.astype(u.dtype), jnp.stack([S_f, S_b], axis=1).astype(u.dtype)


def conformer_mixer(u, p):
    a, b = jnp.split(u, 2, axis=-1)
    z = a * jax.nn.sigmoid(b)
    z = depthwise_conv(z, p['conf_dw_w'], p['conf_dw_b'])
    z = layernorm(z, p['conf_ln_g'], p['conf_ln_b'])
    return jax.nn.silu(z)


def trunk_layer(x, cond, p, ctx_kv=None, ctx_state=None):
    B, L, _ = x.shape
    latent = ctx_kv is not None
    mod = (jax.nn.silu(cond) @ p['ada_w'] + p['ada_b'])[:, None, :]
    sh1, sc1, g1, sh2, sc2, g2 = jnp.split(mod, 6, axis=-1)
    h = rmsnorm(x, p['norm1_g']) * (1 + sc1) + sh1
    proj = h @ p['w_in']
    u_hy, u_mla, u_ret, u_conf = jnp.split(proj, IN_SPLITS, axis=-1)

    y_hy = hyena_mixer(u_hy, p)

    cq, ckv, krope = jnp.split(u_mla, [MLA_Q_LORA, MLA_Q_LORA + MLA_KV_LORA], axis=-1)
    ckv = rmsnorm(ckv, p['mla_kv_norm'])
    q = (rmsnorm(cq, p['mla_q_norm']) @ p['mla_w_uq']).reshape(B, L, MLA_HEADS, MLA_NOPE + MLA_ROPE)
    q_nope, q_rope = q[..., :MLA_NOPE], q[..., MLA_NOPE:]
    if latent:
        cos, sin = axial_rope_tables(L)
        q_rope = apply_rope(q_rope, cos[:, None, :], sin[:, None, :])
        keys_ckv = jnp.concatenate([ckv, ctx_kv[0].astype(ckv.dtype)], axis=1)
        keys_rope = jnp.concatenate([apply_rope(krope, cos, sin), ctx_kv[1].astype(krope.dtype)], axis=1)
    else:
        keys_ckv, keys_rope = ckv, krope
    kv = (keys_ckv @ p['mla_w_ukv']).reshape(B, keys_ckv.shape[1], MLA_HEADS, MLA_NOPE + MLA_V)
    y_mla = block_attention(q_nope, q_rope, kv[..., :MLA_NOPE], keys_rope, kv[..., MLA_NOPE:])

    S0 = ctx_state if latent else jnp.zeros((B, 2, RET_HEADS, RET_DK, RET_DV), F32)
    y_ret, S = retention_mixer(u_ret, p['ret_decay'], S0)

    y_conf = conformer_mixer(u_conf, p)

    gates = jax.nn.sigmoid(h @ p['gate_w'] + p['gate_b'])
    g_hy, g_mla, g_ret, g_conf = jnp.split(gates, N_BRANCH, axis=-1)
    merged = (g_hy * (y_hy @ p['hy_out']) + g_mla * (y_mla @ p['mla_out'])
              + g_ret * (y_ret @ p['ret_out']) + g_conf * (y_conf @ p['conf_out']))
    x = x + g1 * (merged @ p['w_o'])

    h2 = rmsnorm(x, p['norm2_g']) * (1 + sc2) + sh2
    a, b = jnp.split(h2 @ p['ffn_w1'], 2, axis=-1)
    x = x + g2 * ((jax.nn.silu(a) * b) @ p['ffn_w2'])
    return x, ckv, krope, S


def setup_inputs(seed: int = 0) -> dict:
    key = jax.random.key(seed)
    ks = iter(jax.random.split(key, 48))

    def nrm(shape, scale):
        return jax.random.normal(next(ks), shape, F32) * scale

    def gain(shape):
        return 1.0 + nrm(shape, 0.05)

    D = D_MODEL
    pr = 2.0 ** (-5.0 - np.arange(RET_HEADS))
    base = np.log((1.0 - pr) / pr).astype(np.float32)
    ret_decay = jnp.asarray(base)[None, None, :] + nrm((DEPTH, 2, RET_HEADS), 0.1)
    return {
        'x_prompt': nrm((BATCH, SEQ, D), 1.0),
        'x_sample': nrm((DEC_BATCH, DEC_SEQ, D), 1.0),
        'cache_mla_ckv': nrm((DEC_BATCH, DEPTH, PAST_LEN, MLA_KV_LORA), 1.0),
        'cache_mla_krope': nrm((DEC_BATCH, DEPTH, PAST_LEN, MLA_ROPE), 1.0),
        'state_ret': nrm((DEC_BATCH, DEPTH, 2, RET_HEADS, RET_DK, RET_DV), 1.0),
        'c': nrm((DEC_BATCH, D), 1.0),
        'c_ctx': nrm((D,), 1.0),
        'ada_w': nrm((DEPTH, D, 6 * D), 0.5 * D ** -0.5),
        'ada_b': nrm((DEPTH, 6 * D), 0.02),
        'norm1_g': gain((DEPTH, D)),
        'w_in': nrm((DEPTH, D, IN_COLS), D ** -0.5),
        'hy_conv_w': nrm((DEPTH, HY_SHORT, HY_COLS), HY_SHORT ** -0.5),
        'hy_conv_b': nrm((DEPTH, HY_COLS), 0.02),
        'hy_w1': nrm((DEPTH, HY_EMB, HY_FFN), 1.0),
        'hy_b1': nrm((DEPTH, HY_FFN), 0.1),
        'hy_w2': nrm((DEPTH, HY_FFN, HY_FFN), HY_FFN ** -0.5),
        'hy_b2': nrm((DEPTH, HY_FFN), 0.1),
        'hy_w3': nrm((DEPTH, HY_FFN, HY_ORDER * 2 * HY_WIDTH), HY_FFN ** -0.5),
        'hy_bias': nrm((DEPTH, HY_ORDER, HY_WIDTH), 0.5),
        'hy_out': nrm((DEPTH, HY_WIDTH, D), HY_WIDTH ** -0.5),
        'mla_q_norm': gain((DEPTH, MLA_Q_LORA)),
        'mla_w_uq': nrm((DEPTH, MLA_Q_LORA, MLA_HEADS * (MLA_NOPE + MLA_ROPE)), MLA_Q_LORA ** -0.5),
        'mla_kv_norm': gain((DEPTH, MLA_KV_LORA)),
        'mla_w_ukv': nrm((DEPTH, MLA_KV_LORA, MLA_HEADS * (MLA_NOPE + MLA_V)), MLA_KV_LORA ** -0.5),
        'mla_out': nrm((DEPTH, MLA_HEADS * MLA_V, D), (MLA_HEADS * MLA_V) ** -0.5),
        'ret_decay': ret_decay,
        'ret_out': nrm((DEPTH, RET_HEADS * RET_DV, D), (RET_HEADS * RET_DV) ** -0.5),
        'conf_dw_w': nrm((DEPTH, CONF_KERNEL, CONF_WIDTH), CONF_KERNEL ** -0.5),
        'conf_dw_b': nrm((DEPTH, CONF_WIDTH), 0.02),
        'conf_ln_g': gain((DEPTH, CONF_WIDTH)),
        'conf_ln_b': nrm((DEPTH, CONF_WIDTH), 0.02),
        'conf_out': nrm((DEPTH, CONF_WIDTH, D), CONF_WIDTH ** -0.5),
        'gate_w': nrm((DEPTH, D, N_BRANCH * D), D ** -0.5),
        'gate_b': nrm((DEPTH, N_BRANCH * D), 0.02),
        'w_o': nrm((DEPTH, D, D), D ** -0.5),
        'norm2_g': gain((DEPTH, D)),
        'ffn_w1': nrm((DEPTH, D, 2 * D_FF), D ** -0.5),
        'ffn_w2': nrm((DEPTH, D_FF, D), D_FF ** -0.5),
        'final_norm_g': gain((D,)),
    }


def reference(x_prompt, x_sample, cache_mla_ckv, cache_mla_krope, state_ret, c, c_ctx,
              ada_w, ada_b, norm1_g, w_in, hy_conv_w, hy_conv_b, hy_w1, hy_b1, hy_w2, hy_b2,
              hy_w3, hy_bias, hy_out, mla_q_norm, mla_w_uq, mla_kv_norm, mla_w_ukv, mla_out,
              ret_decay, ret_out, conf_dw_w, conf_dw_b, conf_ln_g, conf_ln_b, conf_out,
              gate_w, gate_b, w_o, norm2_g, ffn_w1, ffn_w2, final_norm_g):
    xp, xs = x_prompt, x_sample
    ckvs, kropes, rets = [], [], []
    for l in range(DEPTH):
        p = {
            'ada_w': ada_w[l], 'ada_b': ada_b[l], 'norm1_g': norm1_g[l], 'w_in': w_in[l],
            'hy_conv_w': hy_conv_w[l], 'hy_conv_b': hy_conv_b[l], 'hy_w1': hy_w1[l], 'hy_b1': hy_b1[l],
            'hy_w2': hy_w2[l], 'hy_b2': hy_b2[l], 'hy_w3': hy_w3[l], 'hy_bias': hy_bias[l],
            'hy_out': hy_out[l], 'mla_q_norm': mla_q_norm[l], 'mla_w_uq': mla_w_uq[l],
            'mla_kv_norm': mla_kv_norm[l], 'mla_w_ukv': mla_w_ukv[l], 'mla_out': mla_out[l],
            'ret_decay': ret_decay[l], 'ret_out': ret_out[l], 'conf_dw_w': conf_dw_w[l],
            'conf_dw_b': conf_dw_b[l], 'conf_ln_g': conf_ln_g[l], 'conf_ln_b': conf_ln_b[l],
            'conf_out': conf_out[l], 'gate_w': gate_w[l], 'gate_b': gate_b[l], 'w_o': w_o[l],
            'norm2_g': norm2_g[l], 'ffn_w1': ffn_w1[l], 'ffn_w2': ffn_w2[l],
        }
        xp, ckv_l, krope_l, ret_l = trunk_layer(xp, c_ctx[None, :], p)
        ckvs.append(ckv_l)
        kropes.append(krope_l)
        rets.append(ret_l)
        xs, _, _, _ = trunk_layer(xs, c, p, ctx_kv=(cache_mla_ckv[:, l], cache_mla_krope[:, l]),
                                  ctx_state=state_ret[:, l])
    y_prompt = rmsnorm(xp, final_norm_g)
    y_sample = rmsnorm(xs, final_norm_g)
    new_mla_ckv = jnp.stack(ckvs, axis=1)
    new_mla_krope = jnp.stack(kropes, axis=1)
    new_ret_state = jnp.stack(rets, axis=1)
    return (y_prompt, y_sample, new_mla_ckv, new_mla_krope, new_ret_state)
```

```python
import functools
import math

import jax
import jax.numpy as jnp
from jax import lax
from jax.experimental import pallas as pl
from jax.experimental.pallas import tpu as pltpu

F32 = jnp.float32
BF16 = jnp.bfloat16

D_MODEL = 1024
DEPTH = 2
PAST_LEN = 256
EPS = 1e-6
GRID_W = 64

HY_WIDTH = 256
HY_EMB = 33
HY_BANDS = (HY_EMB - 1) // 2
HY_FFN = 64
HY_FAST_DECAY = 0.3
HY_SLOW_DECAY = 1.5
HY_TARGET = 1e-2

MLA_HEADS = 4
MLA_Q_LORA = 256
MLA_KV_LORA = 128
MLA_NOPE = 64
MLA_ROPE = 32
MLA_V = 64
ROPE_BASE = 10000.0

RET_HEADS = 4
RET_DK = 64
RET_DV = 64
RET_CHUNK = 128

CONF_WIDTH = 256
CONF_KERNEL = 31

D_FF = ((8 * D_MODEL // 3 + 255) // 256) * 256
N_BRANCH = 4

HY_COLS = 3 * HY_WIDTH
MLA_COLS = MLA_Q_LORA + MLA_KV_LORA + MLA_ROPE
RET_COLS = 2 * RET_HEADS * RET_DK + 2 * RET_HEADS * RET_DV
CONF_COLS = 2 * CONF_WIDTH

VMEM_BYTES_V7X = 64 * 1024 * 1024
TOKEN_TILE = 512
Q_TILE = 256
DFT_TILE = 256
HYENA_GROUP_ROWS = 1024
CONF_ROWS = 128
CONF_HALO = 16
FFN_CHUNK = D_FF // 2
MOD_ROWS = 8


def _cparams(sem, vmem_mb):
    return pltpu.CompilerParams(dimension_semantics=sem, vmem_limit_bytes=vmem_mb * 1024 * 1024)


def _dot(a, b):
    return jnp.dot(a, b, preferred_element_type=F32)


def _dot_nt(a, b):
    return lax.dot_general(a, b, (((1,), (1,)), ((), ())), preferred_element_type=F32)


def _dot_exact(a, b):
    return jnp.dot(a, b, preferred_element_type=F32, precision=lax.Precision.HIGHEST)


def _rms(x, g):
    return x * lax.rsqrt(jnp.mean(x * x, axis=-1, keepdims=True) + EPS) * g


def _silu(x):
    return x * jax.nn.sigmoid(x)


def _mod_kernel(c_ref, w_ref, b_ref, o_ref):
    s = _silu(c_ref[...]).astype(BF16)
    o_ref[...] = _dot(s, w_ref[...].astype(BF16)) + b_ref[...]


def _modulation(cond, ada_w, ada_b):
    depth, d, cols = ada_w.shape
    blk = 1024
    out = pl.pallas_call(
        _mod_kernel,
        grid=(depth, cols // blk),
        in_specs=[
            pl.BlockSpec((MOD_ROWS, d), lambda l, j: (0, 0)),
            pl.BlockSpec((None, d, blk), lambda l, j: (l, 0, j)),
            pl.BlockSpec((None, 1, blk), lambda l, j: (l, 0, j)),
        ],
        out_specs=pl.BlockSpec((None, MOD_ROWS, blk), lambda l, j: (l, 0, j)),
        out_shape=jax.ShapeDtypeStruct((depth, MOD_ROWS, cols), F32),
        compiler_params=_cparams(("arbitrary", "arbitrary"), 32),
        name="modulation",
    )(cond, ada_w, ada_b.reshape(depth, 1, cols))
    return out.reshape(depth, MOD_ROWS, 6, d)


def _mod_row(tiles_per_seq, latent):
    if latent:
        return lambda i: (1 + i // tiles_per_seq, 0, 0)
    return lambda i: (0, 0, 0)


def _inproj_kernel(x_ref, mod_ref, g1_ref, why_ref, wcq_ref, wck_ref, wret_ref, wconf_ref,
                   qn_ref, kvn_ref, wuq_ref, wukv_ref,
                   hv_ref, hx1_ref, hx2_ref, q_ref, ckv_ref, kr_ref, kv_ref, ret_ref, conf_ref,
                   *, seqs_per_tile, seq_len):
    x = x_ref[...]
    h = _rms(x, g1_ref[...]) * (1.0 + mod_ref[1:2, :]) + mod_ref[0:1, :]
    hb = h.astype(BF16)

    u = _dot(hb, why_ref[...])
    for p, o_ref in enumerate((hv_ref, hx1_ref, hx2_ref)):
        part = u[:, p * HY_WIDTH:(p + 1) * HY_WIDTH]
        if seqs_per_tile == 1:
            o_ref[...] = part
        else:
            for s in range(seqs_per_tile):
                o_ref[:, s * HY_WIDTH:(s + 1) * HY_WIDTH] = part[s * seq_len:(s + 1) * seq_len]

    cq = _dot(hb, wcq_ref[...])
    q_ref[...] = _dot(_rms(cq, qn_ref[...]).astype(BF16), wuq_ref[...])
    ck = _dot(hb, wck_ref[...])
    ckv = _rms(ck[:, :MLA_KV_LORA], kvn_ref[...])
    ckv_ref[...] = ckv
    kr_ref[...] = ck[:, MLA_KV_LORA:MLA_KV_LORA + MLA_ROPE]
    kv_ref[...] = _dot(ckv.astype(BF16), wukv_ref[...])
    ret_ref[...] = _dot(hb, wret_ref[...])
    conf_ref[...] = _dot(hb, wconf_ref[...])


def _inproj(x, mod, lw, B, L, latent):
    T = B * L
    TM = TOKEN_TILE
    nt = T // TM
    if L >= TM:
        tiles_per_seq, seqs_per_tile = L // TM, 1
        hy_block = (TM, HY_WIDTH)
        hy_map = lambda i: (i % tiles_per_seq, i // tiles_per_seq)
    else:
        tiles_per_seq, seqs_per_tile = 1, TM // L
        hy_block = (L, seqs_per_tile * HY_WIDTH)
        hy_map = lambda i: (0, i)
    full = lambda a: pl.BlockSpec(a.shape, lambda i: (0,) * a.ndim)
    row = lambda w: pl.BlockSpec((TM, w), lambda i: (i, 0))
    weights = (lw["norm1_g"], lw["w_hy"], lw["w_cq"], lw["w_ck"], lw["w_ret"], lw["w_conf"],
               lw["mla_q_norm"], lw["mla_kv_norm"], lw["mla_w_uq"], lw["mla_w_ukv"])
    hy_shape = jax.ShapeDtypeStruct((L, B * HY_WIDTH), F32)
    q_cols = MLA_HEADS * (MLA_NOPE + MLA_ROPE)
    kv_cols = MLA_HEADS * (MLA_NOPE + MLA_V)
    return pl.pallas_call(
        functools.partial(_inproj_kernel, seqs_per_tile=seqs_per_tile, seq_len=L),
        grid=(nt,),
        in_specs=[row(D_MODEL), pl.BlockSpec((None, 6, D_MODEL), _mod_row(tiles_per_seq, latent))]
        + [full(w) for w in weights],
        out_specs=[pl.BlockSpec(hy_block, hy_map)] * 3
        + [row(q_cols), row(MLA_KV_LORA), row(MLA_ROPE), row(kv_cols), row(RET_COLS), row(CONF_COLS)],
        out_shape=[hy_shape] * 3 + [
            jax.ShapeDtypeStruct((T, q_cols), F32),
            jax.ShapeDtypeStruct((T, MLA_KV_LORA), F32),
            jax.ShapeDtypeStruct((T, MLA_ROPE), F32),
            jax.ShapeDtypeStruct((T, kv_cols), F32),
            jax.ShapeDtypeStruct((T, RET_COLS), F32),
            jax.ShapeDtypeStruct((T, CONF_COLS), F32),
        ],
        compiler_params=_cparams(("arbitrary",), 48),
        name="inproj",
    )(x, mod, *weights)


def _dft_tables(L):
    N = 2 * L
    half = N // 2
    t_lo = min(L, 64)
    t_hi = L // t_lo
    k = jnp.arange(half, dtype=jnp.int32)[:, None]

    def cs(t):
        m = (k * t[None, :]) % N
        ang = m.astype(F32) * (2.0 * math.pi / N)
        return jnp.cos(ang), jnp.sin(ang)

    ca, sa = cs(jnp.arange(t_hi, dtype=jnp.int32) * t_lo)
    cb, sb = cs(jnp.arange(t_lo, dtype=jnp.int32))
    cos = (ca[:, :, None] * cb[:, None, :] - sa[:, :, None] * sb[:, None, :]).reshape(half, L)
    sin = (sa[:, :, None] * cb[:, None, :] + ca[:, :, None] * sb[:, None, :]).reshape(half, L)
    alt = jnp.where(jnp.arange(L) % 2 == 0, 1.0, -1.0).astype(F32)
    im = jnp.where(k == 0, alt[None, :], -sin)
    fwd = jnp.concatenate([cos, im], axis=0)
    scale = jnp.where(k == 0, 1.0 / N, 2.0 / N).astype(F32)
    inv = (fwd * jnp.concatenate([scale, scale], axis=0)).T
    return fwd.astype(BF16), inv.astype(BF16)


def _filter_features(L):
    t = jnp.linspace(0.0, 1.0, L, dtype=F32)[:, None]
    w = 2.0 * math.pi * jnp.arange(L, dtype=F32)[:, None] / L
    f = jnp.linspace(1e-4, HY_BANDS - 1, HY_BANDS, dtype=F32)[None, :]
    z = jnp.concatenate([t, jnp.cos(f * w), -jnp.sin(f * w)], axis=-1)
    z = jnp.pad(z, ((0, 0), (0, 128 - HY_EMB)))
    max_decay = math.log(HY_TARGET) / HY_FAST_DECAY
    min_decay = math.log(HY_TARGET) / HY_SLOW_DECAY
    deltas = jnp.abs(jnp.linspace(min_decay, max_decay, HY_WIDTH, dtype=F32))
    decay = jnp.exp(-t * deltas[None, :])
    return z, jnp.tile(decay, (1, 4))


def _filter_kernel(z_ref, dec_ref, w1_ref, b1_ref, w2_ref, b2_ref, w3_ref, fr_ref, fi_ref,
                   kr_ref, ki_ref, h_ref, *, L):
    j = pl.program_id(0)
    W = HY_WIDTH

    @pl.when(j == 0)
    def _():
        h = jnp.sin(_dot_exact(z_ref[...], w1_ref[...]) + b1_ref[...])
        h = jnp.sin(_dot_exact(h, w2_ref[...]) + b2_ref[...])
        h = _dot_exact(h, w3_ref[...]) * dec_ref[...]
        cs = jnp.sum(jnp.abs(h), axis=0, keepdims=True)
        s0 = cs[:, 0:W] + cs[:, W:2 * W]
        s1 = cs[:, 2 * W:3 * W] + cs[:, 3 * W:4 * W]
        h = h / jnp.concatenate([s0, s0, s1, s1], axis=1)
        row = lax.broadcasted_iota(jnp.int32, h.shape, 0)
        col = lax.broadcasted_iota(jnp.int32, h.shape, 1)
        backward = (col // W) % 2 == 1
        h_ref[...] = jnp.where(backward & (row == 0), 0.0, h).astype(BF16)

    hb = h_ref[...]
    sr = _dot(fr_ref[...], hb)
    si = _dot(fi_ref[...], hb)
    row = lax.broadcasted_iota(jnp.int32, (sr.shape[0], W), 0)
    packed_real_row = (row == 0) & (j == 0)
    for o in range(2):
        c = 2 * o * W
        kr_ref[o] = sr[:, c:c + W] + sr[:, c + W:c + 2 * W]
        fwd, bwd = si[:, c:c + W], si[:, c + W:c + 2 * W]
        ki_ref[o] = jnp.where(packed_real_row, fwd + bwd, fwd - bwd)


def _filter_spectra(lw, L, tables):
    z, decay = _filter_features(L)
    fwd, _ = tables
    half = L
    Tk = min(DFT_TILE, half)
    nj = half // Tk
    full = lambda a: pl.BlockSpec(a.shape, lambda j: (0,) * a.ndim)
    ins = (z, decay, lw["hy_w1"], lw["hy_b1"], lw["hy_w2"], lw["hy_b2"], lw["hy_w3"])
    return pl.pallas_call(
        functools.partial(_filter_kernel, L=L),
        grid=(nj,),
        in_specs=[full(a) for a in ins]
        + [pl.BlockSpec((Tk, L), lambda j: (j, 0)), pl.BlockSpec((Tk, L), lambda j: (j + nj, 0))],
        out_specs=[pl.BlockSpec((2, Tk, HY_WIDTH), lambda j: (0, j, 0))] * 2,
        out_shape=[jax.ShapeDtypeStruct((2, half, HY_WIDTH), F32)] * 2,
        scratch_shapes=[pltpu.VMEM((L, 4 * HY_WIDTH), BF16)],
        compiler_params=_cparams(("arbitrary",), 48),
        name="hyena_filter",
    )(*ins, fwd, fwd)


def _hyena_kernel(v_ref, x1_ref, x2_ref, cw_ref, cb_ref, bias_ref, kr_ref, ki_ref,
                  fr_ref, fi_ref, gr_ref, gi_ref, y_ref, cur_ref, curb_ref, acc_ref,
                  *, L, W, nj):
    o = pl.program_id(1)
    j = pl.program_id(2)

    def short_conv(u_ref, p):
        u = u_ref[...]
        row = lax.broadcasted_iota(jnp.int32, (L, W), 0)
        prev = jnp.where(row == 0, 0.0, pltpu.roll(u, 1, 0))
        nxt = jnp.where(row == L - 1, 0.0, pltpu.roll(u, L - 1, 0))
        w = cw_ref[p]
        return prev * w[0:1] + u * w[1:2] + nxt * w[2:3] + cb_ref[p]

    @pl.when((o == 0) & (j == 0))
    def _():
        z = short_conv(v_ref, 0)
        cur_ref[...] = z
        curb_ref[...] = z.astype(BF16)

    @pl.when(j == 0)
    def _():
        acc_ref[...] = jnp.zeros_like(acc_ref)

    cb = curb_ref[...]
    xr = _dot(fr_ref[...], cb)
    xi = _dot(fi_ref[...], cb)
    reps = W // HY_WIDTH
    kr, ki = kr_ref[...], ki_ref[...]
    if reps > 1:
        kr = jnp.concatenate([kr] * reps, axis=1)
        ki = jnp.concatenate([ki] * reps, axis=1)
    row = lax.broadcasted_iota(jnp.int32, xr.shape, 0)
    packed_real_row = (row == 0) & (j == 0)
    yr = jnp.where(packed_real_row, xr * kr, xr * kr - xi * ki)
    yi = jnp.where(packed_real_row, xi * ki, xr * ki + xi * kr)
    acc_ref[...] += _dot(gr_ref[...], yr.astype(BF16)) + _dot(gi_ref[...], yi.astype(BF16))

    @pl.when(j == nj - 1)
    def _():
        conv = acc_ref[...] + cur_ref[...] * bias_ref[...]

        @pl.when(o == 0)
        def _():
            nv = short_conv(x1_ref, 1) * conv
            cur_ref[...] = nv
            curb_ref[...] = nv.astype(BF16)

        @pl.when(o == 1)
        def _():
            y_ref[...] = short_conv(x2_ref, 2) * conv


def _hyena(hv, hx1, hx2, lw, spectra, tables, B, L):
    kr, ki = spectra
    fwd, inv = tables
    half = L
    Tk = min(DFT_TILE, half)
    nj = half // Tk
    W = HY_WIDTH * max(1, min(B, HYENA_GROUP_ROWS // L))
    reps = W // HY_WIDTH
    ng = (B * HY_WIDTH) // W
    cw = jnp.tile(lw["hy_conv_w"].reshape(3, 3, 1, HY_WIDTH).transpose(1, 0, 2, 3), (1, 1, reps, 1))
    cw = cw.reshape(3, 3, W)
    cb = jnp.tile(lw["hy_conv_b"].reshape(3, 1, 1, HY_WIDTH), (1, 1, reps, 1)).reshape(3, 1, W)
    bias = jnp.tile(lw["hy_bias"].reshape(2, 1, 1, HY_WIDTH), (1, 1, reps, 1)).reshape(2, 1, W)
    col = pl.BlockSpec((L, W), lambda g, o, j: (0, g))
    full = lambda a: pl.BlockSpec(a.shape, lambda g, o, j: (0,) * a.ndim)
    spec = pl.BlockSpec((None, Tk, HY_WIDTH), lambda g, o, j: (o, j, 0))
    return pl.pallas_call(
        functools.partial(_hyena_kernel, L=L, W=W, nj=nj),
        grid=(ng, 2, nj),
        in_specs=[col, col, col, full(cw), full(cb),
                  pl.BlockSpec((None, 1, W), lambda g, o, j: (o, 0, 0)), spec, spec,
                  pl.BlockSpec((Tk, L), lambda g, o, j: (j, 0)),
                  pl.BlockSpec((Tk, L), lambda g, o, j: (j + nj, 0)),
                  pl.BlockSpec((L, Tk), lambda g, o, j: (0, j)),
                  pl.BlockSpec((L, Tk), lambda g, o, j: (0, j + nj))],
        out_specs=col,
        out_shape=jax.ShapeDtypeStruct((L, B * HY_WIDTH), F32),
        scratch_shapes=[pltpu.VMEM((L, W), F32), pltpu.VMEM((L, W), BF16), pltpu.VMEM((L, W), F32)],
        compiler_params=_cparams(("arbitrary", "arbitrary", "arbitrary"), 48),
        name="hyena_conv",
    )(hv, hx1, hx2, cw, cb, bias, kr, ki, fwd, fwd, inv, inv)


def _rope(x, cos, sin):
    h = MLA_ROPE // 2
    x1, x2 = x[:, :h], x[:, h:]
    return jnp.concatenate([x1 * cos - x2 * sin, x1 * sin + x2 * cos], axis=-1)


def _mla_kernel(*refs, L, latent):
    if latent:
        (q_ref, kv_ref, kr_ref, cosq_ref, sinq_ref, cos_ref, sin_ref, cckv_ref, ckr_ref, wukv_ref,
         o_ref, kc_ref, vc_ref) = refs
    else:
        q_ref, kv_ref, kr_ref, o_ref, kc_ref, vc_ref = refs
    qi = pl.program_id(1)
    dk = MLA_NOPE + MLA_V
    dq = MLA_NOPE + MLA_ROPE

    @pl.when(qi == 0)
    def _():
        kv = kv_ref[...]
        kr = kr_ref[...]
        if latent:
            kr = _rope(kr, cos_ref[...], sin_ref[...])
            kvc = _dot(cckv_ref[...].astype(BF16), wukv_ref[...])
            krc = ckr_ref[...]
        for h in range(MLA_HEADS):
            kc_ref[h, 0:L, :] = jnp.concatenate([kv[:, h * dk:h * dk + MLA_NOPE], kr], axis=-1).astype(BF16)
            vc_ref[h, 0:L, :] = kv[:, h * dk + MLA_NOPE:(h + 1) * dk].astype(BF16)
            if latent:
                kc_ref[h, L:L + PAST_LEN, :] = jnp.concatenate(
                    [kvc[:, h * dk:h * dk + MLA_NOPE], krc], axis=-1).astype(BF16)
                vc_ref[h, L:L + PAST_LEN, :] = kvc[:, h * dk + MLA_NOPE:(h + 1) * dk].astype(BF16)

    q = q_ref[...]
    scale = dq ** -0.5
    outs = []
    for h in range(MLA_HEADS):
        qn = q[:, h * dq:h * dq + MLA_NOPE]
        qr = q[:, h * dq + MLA_NOPE:(h + 1) * dq]
        if latent:
            qr = _rope(qr, cosq_ref[...], sinq_ref[...])
        qh = jnp.concatenate([qn, qr], axis=-1).astype(BF16)
        s = _dot_nt(qh, kc_ref[h]) * scale
        p = jnp.exp(s - jnp.max(s, axis=-1, keepdims=True))
        denom = jnp.sum(p, axis=-1, keepdims=True)
        outs.append(_dot(p.astype(BF16), vc_ref[h]) / denom)
    o_ref[...] = jnp.concatenate(outs, axis=-1)


def _rope_tables(L):
    rows = L // GRID_W
    row = jnp.repeat(jnp.arange(rows), GRID_W).astype(F32)
    col = jnp.tile(jnp.arange(GRID_W), rows).astype(F32)
    per_axis = MLA_ROPE // 4
    inv = ROPE_BASE ** (-jnp.arange(per_axis, dtype=F32) / per_axis)
    ang = jnp.concatenate([row[:, None] * inv, col[:, None] * inv], axis=-1)
    return jnp.cos(ang), jnp.sin(ang)


def _mla(q, kv, kr, lw, B, L, ctx):
    latent = ctx is not None
    TQ = Q_TILE
    nq = L // TQ
    Lk = L + PAST_LEN if latent else L
    q_cols = MLA_HEADS * (MLA_NOPE + MLA_ROPE)
    kv_cols = MLA_HEADS * (MLA_NOPE + MLA_V)
    seq = lambda w: pl.BlockSpec((L, w), lambda b, i: (b, 0))
    ins = [q, kv, kr]
    specs = [pl.BlockSpec((TQ, q_cols), lambda b, i: (b * nq + i, 0)), seq(kv_cols), seq(MLA_ROPE)]
    if latent:
        cos, sin = _rope_tables(L)
        ins += [cos, sin, cos, sin, ctx[0], ctx[1], lw["mla_w_ukv"]]
        half = MLA_ROPE // 2
        specs += [pl.BlockSpec((TQ, half), lambda b, i: (i, 0))] * 2
        specs += [pl.BlockSpec((L, half), lambda b, i: (0, 0))] * 2
        specs += [pl.BlockSpec((None, PAST_LEN, MLA_KV_LORA), lambda b, i: (b, 0, 0)),
                  pl.BlockSpec((None, PAST_LEN, MLA_ROPE), lambda b, i: (b, 0, 0)),
                  pl.BlockSpec(lw["mla_w_ukv"].shape, lambda b, i: (0, 0))]
    return pl.pallas_call(
        functools.partial(_mla_kernel, L=L, latent=latent),
        grid=(B, nq),
        in_specs=specs,
        out_specs=pl.BlockSpec((TQ, MLA_HEADS * MLA_V), lambda b, i: (b * nq + i, 0)),
        out_shape=jax.ShapeDtypeStruct((B * L, MLA_HEADS * MLA_V), F32),
        scratch_shapes=[pltpu.VMEM((MLA_HEADS, Lk, MLA_NOPE + MLA_ROPE), BF16),
                        pltpu.VMEM((MLA_HEADS, Lk, MLA_V), BF16)],
        compiler_params=_cparams(("arbitrary", "arbitrary"), 48),
        name="mla_attention",
    )(*ins)


def _ret_kernel(*refs, L, has_state, emit_state):
    refs = list(refs)
    u_ref, dl_ref = refs[:2]
    pos = 2
    s0_ref = None
    if has_state:
        s0_ref = refs[pos]
        pos += 1
    y_ref = refs[pos]
    pos += 1
    sout_ref = None
    if emit_state:
        sout_ref = refs[pos]
        pos += 1
    of_ref, s_ref = refs[pos:]

    C = RET_CHUNK
    n = L // C
    H, DK, DV = RET_HEADS, RET_DK, RET_DV
    qk = H * DK
    x = dl_ref[...]
    log_g = jnp.minimum(x, 0.0) - jnp.log1p(jnp.exp(-jnp.abs(x)))

    if has_state:
        s_ref[...] = s0_ref[...]
    else:
        s_ref[...] = jnp.zeros_like(s_ref)

    ii = lax.broadcasted_iota(jnp.int32, (C, C), 0).astype(F32)
    jj = lax.broadcasted_iota(jnp.int32, (C, C), 1).astype(F32)
    idx = lax.broadcasted_iota(jnp.int32, (C, 1), 0).astype(F32)

    for d in range(2):
        diff = ii - jj if d == 0 else jj - ii
        lag_q = idx + 1.0 if d == 0 else C - idx
        lag_k = C - 1.0 - idx if d == 0 else idx
        decay_in, decay_q, decay_c, dk_cols = [], [], [], []
        for h in range(H):
            g = log_g[d:d + 1, h:h + 1]
            decay_in.append(jnp.where(diff >= 0, jnp.exp(jnp.maximum(diff, 0.0) * g), 0.0))
            decay_q.append(jnp.exp(lag_q * g))
            decay_c.append(jnp.exp(C * g))
            dk_cols.append(jnp.broadcast_to(jnp.exp(lag_k * g), (C, DK)))
        decay_k = jnp.concatenate(dk_cols, axis=1)

        def chunk(c, carry, d=d, decay_in=decay_in, decay_q=decay_q, decay_c=decay_c, decay_k=decay_k):
            ci = c if d == 0 else n - 1 - c
            rows = pl.ds(pl.multiple_of(ci * C, C), C)
            q = u_ref[rows, 0:qk]
            k = u_ref[rows, qk:2 * qk] * (DK ** -0.5)
            v = u_ref[rows, 2 * qk:2 * qk + H * DV]
            kdt = (k * decay_k).T.astype(BF16)
            outs = []
            for h in range(H):
                qh = q[:, h * DK:(h + 1) * DK].astype(BF16)
                kh = k[:, h * DK:(h + 1) * DK].astype(BF16)
                vh = v[:, h * DV:(h + 1) * DV].astype(BF16)
                att = _dot_nt(qh, kh) * decay_in[h]
                S = s_ref[d, h]
                outs.append(_dot(att.astype(BF16), vh) + _dot(qh, S.astype(BF16)) * decay_q[h])
                s_ref[d, h] = S * decay_c[h] + _dot(kdt[h * DK:(h + 1) * DK, :], vh)
            o = jnp.concatenate(outs, axis=1)
            if d == 0:
                of_ref[rows, :] = o
            else:
                tot = of_ref[rows, :] + o
                normed = []
                for h in range(H):
                    th = tot[:, h * DV:(h + 1) * DV]
                    xc = th - jnp.mean(th, axis=-1, keepdims=True)
                    normed.append(xc * lax.rsqrt(jnp.mean(xc * xc, axis=-1, keepdims=True) + EPS))
                gate = u_ref[rows, 2 * qk + H * DV:2 * qk + 2 * H * DV]
                y_ref[rows, :] = _silu(gate) * jnp.concatenate(normed, axis=1)
            return carry

        lax.fori_loop(0, n, chunk, 0)

    if emit_state:
        sout_ref[...] = s_ref[...]


def _retention(u_ret, decay_logit, B, L, state, emit_state):
    has_state = state is not None
    st_block = (None, 2, RET_HEADS, RET_DK, RET_DV)
    st_spec = pl.BlockSpec(st_block, lambda b: (b, 0, 0, 0, 0))
    ins = [u_ret, decay_logit]
    specs = [pl.BlockSpec((L, RET_COLS), lambda b: (b, 0)), pl.BlockSpec(decay_logit.shape, lambda b: (0, 0))]
    if has_state:
        ins.append(state)
        specs.append(st_spec)
    vd = RET_HEADS * RET_DV
    out_specs = [pl.BlockSpec((L, vd), lambda b: (b, 0))]
    out_shape = [jax.ShapeDtypeStruct((B * L, vd), F32)]
    if emit_state:
        out_specs.append(st_spec)
        out_shape.append(jax.ShapeDtypeStruct((B, 2, RET_HEADS, RET_DK, RET_DV), F32))
    res = pl.pallas_call(
        functools.partial(_ret_kernel, L=L, has_state=has_state, emit_state=emit_state),
        grid=(B,),
        in_specs=specs,
        out_specs=out_specs,
        out_shape=out_shape,
        scratch_shapes=[pltpu.VMEM((L, vd), F32), pltpu.VMEM((2, RET_HEADS, RET_DK, RET_DV), F32)],
        compiler_params=_cparams(("arbitrary",), 48),
        name="retention",
    )(*ins)
    return (res[0], res[1]) if emit_state else (res[0], None)


def _conf_kernel(u_ref, w_ref, b_ref, g_ref, be_ref, y_ref, zp_ref, *, L):
    Wc = CONF_WIDTH
    halo = CONF_HALO
    zp_ref[0:halo, :] = jnp.zeros((halo, Wc), F32)
    zp_ref[halo + L:2 * halo + L, :] = jnp.zeros((halo, Wc), F32)
    zp_ref[halo:halo + L, :] = u_ref[:, 0:Wc] * jax.nn.sigmoid(u_ref[:, Wc:2 * Wc])
    first = halo - CONF_KERNEL // 2
    R = CONF_ROWS
    for c in range(L // R):
        acc = jnp.broadcast_to(b_ref[...], (R, Wc))
        for k in range(CONF_KERNEL):
            acc = acc + w_ref[k:k + 1, :] * zp_ref[c * R + first + k:c * R + first + k + R, :]
        mu = jnp.mean(acc, axis=-1, keepdims=True)
        xc = acc - mu
        z = xc * lax.rsqrt(jnp.mean(xc * xc, axis=-1, keepdims=True) + EPS) * g_ref[...] + be_ref[...]
        y_ref[c * R:(c + 1) * R, :] = _silu(z)


def _conformer(u_conf, lw, B, L):
    full = lambda a: pl.BlockSpec(a.shape, lambda b: (0,) * a.ndim)
    ws = (lw["conf_dw_w"], lw["conf_dw_b"], lw["conf_ln_g"], lw["conf_ln_b"])
    return pl.pallas_call(
        functools.partial(_conf_kernel, L=L),
        grid=(B,),
        in_specs=[pl.BlockSpec((L, CONF_COLS), lambda b: (b, 0))] + [full(w) for w in ws],
        out_specs=pl.BlockSpec((L, CONF_WIDTH), lambda b: (b, 0)),
        out_shape=jax.ShapeDtypeStruct((B * L, CONF_WIDTH), F32),
        scratch_shapes=[pltpu.VMEM((L + 2 * CONF_HALO, CONF_WIDTH), F32)],
        compiler_params=_cparams(("arbitrary",), 32),
        name="conformer",
    )(u_conf, *ws)


def _merge_kernel(x_ref, mod_ref, g1_ref, yhy_ref, ymla_ref, yret_ref, yconf_ref,
                  gw_ref, gb_ref, wout_ref, wo_ref, o_ref, *, seqs_per_tile):
    x = x_ref[...]
    h = _rms(x, g1_ref[...]) * (1.0 + mod_ref[1:2, :]) + mod_ref[0:1, :]
    hb = h.astype(BF16)
    if seqs_per_tile == 1:
        yhy = yhy_ref[...]
    else:
        yhy = jnp.concatenate(
            [yhy_ref[:, s * HY_WIDTH:(s + 1) * HY_WIDTH] for s in range(seqs_per_tile)], axis=0)
    branches = (yhy, ymla_ref[...], yret_ref[...], yconf_ref[...])
    D = D_MODEL
    merged = None
    for i, y in enumerate(branches):
        gate = jax.nn.sigmoid(_dot(hb, gw_ref[:, i * D:(i + 1) * D]) + gb_ref[:, i * D:(i + 1) * D])
        term = gate * _dot(y.astype(BF16), wout_ref[i])
        merged = term if merged is None else merged + term
    o_ref[...] = x + mod_ref[2:3, :] * _dot(merged.astype(BF16), wo_ref[...])


def _merge(x, mod, lw, y_hy, y_mla, y_ret, y_conf, B, L, latent):
    T = B * L
    TM = TOKEN_TILE
    if L >= TM:
        tiles_per_seq, seqs_per_tile = L // TM, 1
        hy_spec = pl.BlockSpec((TM, HY_WIDTH), lambda i: (i % tiles_per_seq, i // tiles_per_seq))
    else:
        tiles_per_seq, seqs_per_tile = 1, TM // L
        hy_spec = pl.BlockSpec((L, seqs_per_tile * HY_WIDTH), lambda i: (0, i))
    single = lambda a: pl.BlockSpec(a.shape, lambda i: (0,) * a.ndim, pipeline_mode=pl.Buffered(1))
    row = lambda w: pl.BlockSpec((TM, w), lambda i: (i, 0))
    ws = (lw["gate_w"], lw["gate_b"], lw["w_out"], lw["w_o"])
    return pl.pallas_call(
        functools.partial(_merge_kernel, seqs_per_tile=seqs_per_tile),
        grid=(T // TM,),
        in_specs=[row(D_MODEL), pl.BlockSpec((None, 6, D_MODEL), _mod_row(tiles_per_seq, latent)),
                  single(lw["norm1_g"]), hy_spec, row(256), row(256), row(256)] + [single(w) for w in ws],
        out_specs=row(D_MODEL),
        out_shape=jax.ShapeDtypeStruct((T, D_MODEL), F32),
        compiler_params=_cparams(("arbitrary",), 56),
        name="merge",
    )(x, mod, lw["norm1_g"], y_hy, y_mla, y_ret, y_conf, *ws)


def _ffn_kernel(x_ref, mod_ref, g2_ref, w1_ref, w2_ref, fg_ref, o_ref, *, final):
    x = x_ref[...]
    h2 = (_rms(x, g2_ref[...]) * (1.0 + mod_ref[4:5, :]) + mod_ref[3:4, :]).astype(BF16)
    acc = None
    for c in range(D_FF // FFN_CHUNK):
        c0 = c * FFN_CHUNK
        a = _dot(h2, w1_ref[:, c0:c0 + FFN_CHUNK])
        b = _dot(h2, w1_ref[:, D_FF + c0:D_FF + c0 + FFN_CHUNK])
        part = _dot((_silu(a) * b).astype(BF16), w2_ref[c0:c0 + FFN_CHUNK, :])
        acc = part if acc is None else acc + part
    out = x + mod_ref[5:6, :] * acc
    if final:
        out = _rms(out, fg_ref[...])
    o_ref[...] = out


def _ffn(x, mod, lw, final_g, B, L, latent, final):
    T = B * L
    TM = TOKEN_TILE
    tiles_per_seq = max(L // TM, 1)
    single = lambda a: pl.BlockSpec(a.shape, lambda i: (0,) * a.ndim, pipeline_mode=pl.Buffered(1))
    row = pl.BlockSpec((TM, D_MODEL), lambda i: (i, 0))
    ws = (lw["norm2_g"], lw["ffn_w1"], lw["ffn_w2"], final_g)
    return pl.pallas_call(
        functools.partial(_ffn_kernel, final=final),
        grid=(T // TM,),
        in_specs=[row, pl.BlockSpec((None, 6, D_MODEL), _mod_row(tiles_per_seq, latent))]
        + [single(w) for w in ws],
        out_specs=row,
        out_shape=jax.ShapeDtypeStruct((T, D_MODEL), F32),
        compiler_params=_cparams(("arbitrary",), 56),
        name="ffn",
    )(x, mod, *ws)


def _trunk_layer(x, mod, lw, final_g, tables, B, L, ctx, state, final):
    latent = ctx is not None
    hv, hx1, hx2, q, ckv, kr, kv, u_ret, u_conf = _inproj(x, mod, lw, B, L, latent)
    spectra = _filter_spectra(lw, L, tables)
    y_hy = _hyena(hv, hx1, hx2, lw, spectra, tables, B, L)
    y_mla = _mla(q, kv, kr, lw, B, L, ctx)
    y_ret, S = _retention(u_ret, lw["ret_decay"], B, L, state, emit_state=not latent)
    y_conf = _conformer(u_conf, lw, B, L)
    x = _merge(x, mod, lw, y_hy, y_mla, y_ret, y_conf, B, L, latent)
    x = _ffn(x, mod, lw, final_g, B, L, latent, final)
    return x, ckv, kr, S


def _layer_weights(l, w_in, p):
    hy, mla = HY_COLS, HY_COLS + MLA_COLS
    ret = mla + RET_COLS
    w = w_in[l]
    ck = w[:, hy + MLA_Q_LORA:mla]
    ck = jnp.pad(ck, ((0, 0), (0, 256 - ck.shape[1])))
    row = lambda a: a[l].reshape(1, -1)
    return {
        "norm1_g": row(p["norm1_g"]), "norm2_g": row(p["norm2_g"]),
        "w_hy": w[:, :hy].astype(BF16), "w_cq": w[:, hy:hy + MLA_Q_LORA].astype(BF16),
        "w_ck": ck.astype(BF16), "w_ret": w[:, mla:ret].astype(BF16), "w_conf": w[:, ret:].astype(BF16),
        "mla_q_norm": row(p["mla_q_norm"]), "mla_kv_norm": row(p["mla_kv_norm"]),
        "mla_w_uq": p["mla_w_uq"][l].astype(BF16), "mla_w_ukv": p["mla_w_ukv"][l].astype(BF16),
        "hy_conv_w": p["hy_conv_w"][l], "hy_conv_b": p["hy_conv_b"][l],
        "hy_w1": jnp.pad(p["hy_w1"][l], ((0, 128 - HY_EMB), (0, 0))), "hy_b1": row(p["hy_b1"]),
        "hy_w2": p["hy_w2"][l], "hy_b2": row(p["hy_b2"]), "hy_w3": p["hy_w3"][l],
        "hy_bias": p["hy_bias"][l],
        "ret_decay": p["ret_decay"][l],
        "conf_dw_w": p["conf_dw_w"][l], "conf_dw_b": row(p["conf_dw_b"]),
        "conf_ln_g": row(p["conf_ln_g"]), "conf_ln_b": row(p["conf_ln_b"]),
        "gate_w": p["gate_w"][l].astype(BF16), "gate_b": row(p["gate_b"]),
        "w_out": jnp.stack([p["hy_out"][l], p["mla_out"][l], p["ret_out"][l], p["conf_out"][l]]).astype(BF16),
        "w_o": p["w_o"][l].astype(BF16),
        "ffn_w1": p["ffn_w1"][l].astype(BF16), "ffn_w2": p["ffn_w2"][l].astype(BF16),
    }


def kernel(x_prompt, x_sample, cache_mla_ckv, cache_mla_krope, state_ret, c, c_ctx, ada_w, ada_b, norm1_g, w_in, hy_conv_w, hy_conv_b, hy_w1, hy_b1, hy_w2, hy_b2, hy_w3, hy_bias, hy_out, mla_q_norm, mla_w_uq, mla_kv_norm, mla_w_ukv, mla_out, ret_decay, ret_out, conf_dw_w, conf_dw_b, conf_ln_g, conf_ln_b, conf_out, gate_w, gate_b, w_o, norm2_g, ffn_w1, ffn_w2, final_norm_g):
    p = dict(norm1_g=norm1_g, hy_conv_w=hy_conv_w, hy_conv_b=hy_conv_b, hy_w1=hy_w1, hy_b1=hy_b1,
             hy_w2=hy_w2, hy_b2=hy_b2, hy_w3=hy_w3, hy_bias=hy_bias, hy_out=hy_out,
             mla_q_norm=mla_q_norm, mla_w_uq=mla_w_uq, mla_kv_norm=mla_kv_norm, mla_w_ukv=mla_w_ukv,
             mla_out=mla_out, ret_decay=ret_decay, ret_out=ret_out, conf_dw_w=conf_dw_w,
             conf_dw_b=conf_dw_b, conf_ln_g=conf_ln_g, conf_ln_b=conf_ln_b, conf_out=conf_out,
             gate_w=gate_w, gate_b=gate_b, w_o=w_o, norm2_g=norm2_g, ffn_w1=ffn_w1, ffn_w2=ffn_w2)
    Bp, Lp, D = x_prompt.shape
    Bs, Ls, _ = x_sample.shape
    depth = w_in.shape[0]

    cond = jnp.concatenate([c_ctx[None, :], c, jnp.zeros((MOD_ROWS - 1 - Bs, D), F32)], axis=0)
    mod = _modulation(cond, ada_w, ada_b)
    tables_p = _dft_tables(Lp)
    tables_s = _dft_tables(Ls)
    final_g = final_norm_g.reshape(1, D)

    xp = x_prompt.reshape(Bp * Lp, D)
    xs = x_sample.reshape(Bs * Ls, D)
    ckvs, kropes, rets = [], [], []
    for l in range(depth):
        lw = _layer_weights(l, w_in, p)
        final = l == depth - 1
        xp, ckv, kr, S = _trunk_layer(xp, mod[l], lw, final_g, tables_p, Bp, Lp, None, None, final)
        ckvs.append(ckv.reshape(Bp, Lp, MLA_KV_LORA))
        kropes.append(kr.reshape(Bp, Lp, MLA_ROPE))
        rets.append(S)
        xs, _, _, _ = _trunk_layer(xs, mod[l], lw, final_g, tables_s, Bs, Ls,
                                   (cache_mla_ckv[:, l], cache_mla_krope[:, l]), state_ret[:, l], final)
    return (xp.reshape(Bp, Lp, D), xs.reshape(Bs, Ls, D),
            jnp.stack(ckvs, axis=1), jnp.stack(kropes, axis=1), jnp.stack(rets, axis=1))
```

```python
import functools
import math

import jax
import jax.numpy as jnp
from jax import lax
from jax.experimental import pallas as pl
from jax.experimental.pallas import tpu as pltpu

F32 = jnp.float32
BF16 = jnp.bfloat16

D_MODEL = 1024
DEPTH = 2
PAST_LEN = 256
EPS = 1e-6
GRID_W = 64

HY_WIDTH = 256
HY_EMB = 33
HY_BANDS = (HY_EMB - 1) // 2
HY_FFN = 64
HY_FAST_DECAY = 0.3
HY_SLOW_DECAY = 1.5
HY_TARGET = 1e-2

MLA_HEADS = 4
MLA_Q_LORA = 256
MLA_KV_LORA = 128
MLA_NOPE = 64
MLA_ROPE = 32
MLA_V = 64
ROPE_BASE = 10000.0

RET_HEADS = 4
RET_DK = 64
RET_DV = 64
RET_BLOCK = 256

CONF_WIDTH = 256
CONF_KERNEL = 31

D_FF = ((8 * D_MODEL // 3 + 255) // 256) * 256
N_BRANCH = 4

HY_COLS = 3 * HY_WIDTH
MLA_COLS = MLA_Q_LORA + MLA_KV_LORA + MLA_ROPE
RET_COLS = 2 * RET_HEADS * RET_DK + 2 * RET_HEADS * RET_DV
CONF_COLS = 2 * CONF_WIDTH

VMEM_BYTES_V7X = 64 * 1024 * 1024
TOKEN_TILE = 512
Q_TILE = 256
DFT_TILE = 512
HYENA_GROUP_ROWS = 1024
CONF_ROWS = 128
CONF_HALO = 16
FFN_CHUNK = D_FF // 2
MOD_ROWS = 8


def _cparams(sem, vmem_mb):
    return pltpu.CompilerParams(dimension_semantics=sem, vmem_limit_bytes=vmem_mb * 1024 * 1024)


def _dot(a, b):
    return jnp.dot(a, b, preferred_element_type=F32)


def _dot_nt(a, b):
    return lax.dot_general(a, b, (((1,), (1,)), ((), ())), preferred_element_type=F32)


def _dot_exact(a, b):
    return jnp.dot(a, b, preferred_element_type=F32, precision=lax.Precision.HIGHEST)


def _rms(x, g):
    return x * lax.rsqrt(jnp.mean(x * x, axis=-1, keepdims=True) + EPS) * g


def _silu(x):
    return x * jax.nn.sigmoid(x)


def _mod_kernel(c_ref, w_ref, b_ref, o_ref):
    s = _silu(c_ref[...]).astype(BF16)
    o_ref[...] = _dot(s, w_ref[...].astype(BF16)) + b_ref[...]


def _modulation(cond, ada_w, ada_b):
    depth, d, cols = ada_w.shape
    blk = 1024
    out = pl.pallas_call(
        _mod_kernel,
        grid=(depth, cols // blk),
        in_specs=[
            pl.BlockSpec((MOD_ROWS, d), lambda l, j: (0, 0)),
            pl.BlockSpec((None, d, blk), lambda l, j: (l, 0, j)),
            pl.BlockSpec((None, 1, blk), lambda l, j: (l, 0, j)),
        ],
        out_specs=pl.BlockSpec((None, MOD_ROWS, blk), lambda l, j: (l, 0, j)),
        out_shape=jax.ShapeDtypeStruct((depth, MOD_ROWS, cols), F32),
        compiler_params=_cparams(("arbitrary", "arbitrary"), 32),
        name="modulation",
    )(cond, ada_w, ada_b.reshape(depth, 1, cols))
    return out.reshape(depth, MOD_ROWS, 6, d)


def _mod_row(tiles_per_seq, latent):
    if latent:
        return lambda i: (1 + i // tiles_per_seq, 0, 0)
    return lambda i: (0, 0, 0)


def _inproj_kernel(x_ref, mod_ref, g1_ref, why_ref, wcq_ref, wck_ref, wret_ref, wconf_ref,
                   qn_ref, kvn_ref, wuq_ref, wukv_ref,
                   hv_ref, hx1_ref, hx2_ref, q_ref, ckv_ref, kr_ref, kv_ref, ret_ref, conf_ref,
                   *, seqs_per_tile, seq_len):
    x = x_ref[...]
    h = _rms(x, g1_ref[...]) * (1.0 + mod_ref[1:2, :]) + mod_ref[0:1, :]
    hb = h.astype(BF16)

    u = _dot(hb, why_ref[...])
    for p, o_ref in enumerate((hv_ref, hx1_ref, hx2_ref)):
        part = u[:, p * HY_WIDTH:(p + 1) * HY_WIDTH]
        if seqs_per_tile == 1:
            o_ref[...] = part
        else:
            for s in range(seqs_per_tile):
                o_ref[:, s * HY_WIDTH:(s + 1) * HY_WIDTH] = part[s * seq_len:(s + 1) * seq_len]

    cq = _dot(hb, wcq_ref[...])
    q_ref[...] = _dot(_rms(cq, qn_ref[...]).astype(BF16), wuq_ref[...])
    ck = _dot(hb, wck_ref[...])
    ckv = _rms(ck[:, :MLA_KV_LORA], kvn_ref[...])
    ckv_ref[...] = ckv
    kr_ref[...] = ck[:, MLA_KV_LORA:MLA_KV_LORA + MLA_ROPE]
    kv_ref[...] = _dot(ckv.astype(BF16), wukv_ref[...])
    ret_ref[...] = _dot(hb, wret_ref[...])
    conf_ref[...] = _dot(hb, wconf_ref[...])


def _inproj(x, mod, lw, B, L, latent):
    T = B * L
    TM = TOKEN_TILE
    nt = T // TM
    if L >= TM:
        tiles_per_seq, seqs_per_tile = L // TM, 1
        hy_block = (TM, HY_WIDTH)
        hy_map = lambda i: (i % tiles_per_seq, i // tiles_per_seq)
    else:
        tiles_per_seq, seqs_per_tile = 1, TM // L
        hy_block = (L, seqs_per_tile * HY_WIDTH)
        hy_map = lambda i: (0, i)
    full = lambda a: pl.BlockSpec(a.shape, lambda i: (0,) * a.ndim)
    row = lambda w: pl.BlockSpec((TM, w), lambda i: (i, 0))
    weights = (lw["norm1_g"], lw["w_hy"], lw["w_cq"], lw["w_ck"], lw["w_ret"], lw["w_conf"],
               lw["mla_q_norm"], lw["mla_kv_norm"], lw["mla_w_uq"], lw["mla_w_ukv"])
    hy_shape = jax.ShapeDtypeStruct((L, B * HY_WIDTH), F32)
    q_cols = MLA_HEADS * (MLA_NOPE + MLA_ROPE)
    kv_cols = MLA_HEADS * (MLA_NOPE + MLA_V)
    return pl.pallas_call(
        functools.partial(_inproj_kernel, seqs_per_tile=seqs_per_tile, seq_len=L),
        grid=(nt,),
        in_specs=[row(D_MODEL), pl.BlockSpec((None, 6, D_MODEL), _mod_row(tiles_per_seq, latent))]
        + [full(w) for w in weights],
        out_specs=[pl.BlockSpec(hy_block, hy_map)] * 3
        + [row(q_cols), row(MLA_KV_LORA), row(MLA_ROPE), row(kv_cols), row(RET_COLS), row(CONF_COLS)],
        out_shape=[hy_shape] * 3 + [
            jax.ShapeDtypeStruct((T, q_cols), F32),
            jax.ShapeDtypeStruct((T, MLA_KV_LORA), F32),
            jax.ShapeDtypeStruct((T, MLA_ROPE), F32),
            jax.ShapeDtypeStruct((T, kv_cols), F32),
            jax.ShapeDtypeStruct((T, RET_COLS), F32),
            jax.ShapeDtypeStruct((T, CONF_COLS), F32),
        ],
        compiler_params=_cparams(("arbitrary",), 48),
        name="inproj",
    )(x, mod, *weights)


def _dft_tables(L):
    N = 2 * L
    k_lo = min(L, 32)
    k_hi = L // k_lo
    t = jnp.arange(L, dtype=jnp.int32)[None, :]

    def cs(k):
        m = (k[:, None] * t) % N
        ang = m.astype(F32) * (2.0 * math.pi / N)
        return jnp.cos(ang), jnp.sin(ang)

    ca, sa = cs(jnp.arange(k_hi, dtype=jnp.int32) * k_lo)
    cb, sb = cs(jnp.arange(k_lo, dtype=jnp.int32))
    cos = (ca[:, None, :] * cb[None, :, :] - sa[:, None, :] * sb[None, :, :]).reshape(L, L)
    sin = (sa[:, None, :] * cb[None, :, :] + ca[:, None, :] * sb[None, :, :]).reshape(L, L)
    return cos.astype(BF16), (-sin).astype(BF16)


def _filter_features(L):
    t = jnp.linspace(0.0, 1.0, L, dtype=F32)[:, None]
    w = 2.0 * math.pi * jnp.arange(L, dtype=F32)[:, None] / L
    f = jnp.linspace(1e-4, HY_BANDS - 1, HY_BANDS, dtype=F32)[None, :]
    z = jnp.concatenate([t, jnp.cos(f * w), -jnp.sin(f * w)], axis=-1)
    z = jnp.pad(z, ((0, 0), (0, 128 - HY_EMB)))
    max_decay = math.log(HY_TARGET) / HY_FAST_DECAY
    min_decay = math.log(HY_TARGET) / HY_SLOW_DECAY
    deltas = jnp.abs(jnp.linspace(min_decay, max_decay, HY_WIDTH, dtype=F32))
    decay = jnp.exp(-t * deltas[None, :])
    return z, decay


def _alternating(rows):
    t = lax.broadcasted_iota(jnp.int32, (rows, 1), 0)
    return (1 - 2 * (t & 1)).astype(F32)


def _filter_kernel(z_ref, dec_ref, w1_ref, b1_ref, w2_ref, b2_ref, w3_ref, cos_ref, sin_ref,
                   kr_ref, ki_ref, kn_ref, h_ref, *, L):
    j = pl.program_id(0)
    W = HY_WIDTH
    N = 2 * L

    @pl.when(j == 0)
    def _():
        h = jnp.sin(_dot_exact(z_ref[...], w1_ref[...]) + b1_ref[...])
        h = jnp.sin(_dot_exact(h, w2_ref[...]) + b2_ref[...])
        h = _dot_exact(h, w3_ref[...]) * jnp.concatenate([dec_ref[...]] * 4, axis=1)
        cs = jnp.sum(jnp.abs(h), axis=0, keepdims=True)
        s0 = cs[:, 0:W] + cs[:, W:2 * W]
        s1 = cs[:, 2 * W:3 * W] + cs[:, 3 * W:4 * W]
        h = h / jnp.concatenate([s0, s0, s1, s1], axis=1)
        row = lax.broadcasted_iota(jnp.int32, h.shape, 0)
        col = lax.broadcasted_iota(jnp.int32, h.shape, 1)
        backward = (col // W) % 2 == 1
        h = jnp.where(backward & (row == 0), 0.0, h)
        h_ref[...] = h.astype(BF16)
        nyq = jnp.sum(h * _alternating(L), axis=0, keepdims=True) * (1.0 / N)
        for o in range(2):
            c = 2 * o * W
            kn_ref[o] = nyq[:, c:c + W] + nyq[:, c + W:c + 2 * W]

    hb = h_ref[...]
    sr = _dot(cos_ref[...], hb)
    si = _dot(sin_ref[...], hb)
    row = lax.broadcasted_iota(jnp.int32, (sr.shape[0], W), 0)
    scale = jnp.where((row == 0) & (j == 0), 1.0 / N, 2.0 / N)
    for o in range(2):
        c = 2 * o * W
        kr_ref[o] = (sr[:, c:c + W] + sr[:, c + W:c + 2 * W]) * scale
        ki_ref[o] = (si[:, c:c + W] - si[:, c + W:c + 2 * W]) * scale


def _filter_spectra(lw, L, tables):
    z, decay = _filter_features(L)
    cos, msin = tables
    Tk = min(DFT_TILE, L)
    full = lambda a: pl.BlockSpec(a.shape, lambda j: (0,) * a.ndim)
    ins = (z, decay, lw["hy_w1"], lw["hy_b1"], lw["hy_w2"], lw["hy_b2"], lw["hy_w3"])
    tile = pl.BlockSpec((Tk, L), lambda j: (j, 0))
    spec = pl.BlockSpec((2, Tk, HY_WIDTH), lambda j: (0, j, 0))
    return pl.pallas_call(
        functools.partial(_filter_kernel, L=L),
        grid=(L // Tk,),
        in_specs=[full(a) for a in ins] + [tile, tile],
        out_specs=[spec, spec, pl.BlockSpec((2, 1, HY_WIDTH), lambda j: (0, 0, 0))],
        out_shape=[jax.ShapeDtypeStruct((2, L, HY_WIDTH), F32)] * 2
        + [jax.ShapeDtypeStruct((2, 1, HY_WIDTH), F32)],
        scratch_shapes=[pltpu.VMEM((L, 4 * HY_WIDTH), BF16)],
        compiler_params=_cparams(("arbitrary",), 48),
        name="hyena_filter",
    )(*ins, cos, msin)


def _hyena_kernel(v_ref, x1_ref, x2_ref, cw_ref, cb_ref, bias_ref, kr_ref, ki_ref, kn_ref,
                  fr_ref, fi_ref, gr_ref, gi_ref, y_ref, cur_ref, curb_ref, acc_ref,
                  *, L, W, nj):
    o = pl.program_id(1)
    j = pl.program_id(2)

    def short_conv(u_ref, p):
        u = u_ref[...]
        row = lax.broadcasted_iota(jnp.int32, (L, W), 0)
        prev = jnp.where(row == 0, 0.0, pltpu.roll(u, 1, 0))
        nxt = jnp.where(row == L - 1, 0.0, pltpu.roll(u, L - 1, 0))
        w = cw_ref[p]
        return prev * w[0:1] + u * w[1:2] + nxt * w[2:3] + cb_ref[p]

    @pl.when((o == 0) & (j == 0))
    def _():
        z = short_conv(v_ref, 0)
        cur_ref[...] = z
        curb_ref[...] = z.astype(BF16)

    reps = W // HY_WIDTH
    tiled = lambda a: jnp.concatenate([a] * reps, axis=1) if reps > 1 else a

    @pl.when(j == 0)
    def _():
        alt = _alternating(L)
        nyq = jnp.sum(cur_ref[...] * alt, axis=0, keepdims=True) * tiled(kn_ref[...])
        acc_ref[...] = alt * nyq

    cb = curb_ref[...]
    xr = _dot(fr_ref[...], cb)
    xi = _dot(fi_ref[...], cb)
    kr, ki = tiled(kr_ref[...]), tiled(ki_ref[...])
    yr = xr * kr - xi * ki
    yi = xr * ki + xi * kr
    acc_ref[...] += _dot(gr_ref[...], yr.astype(BF16)) + _dot(gi_ref[...], yi.astype(BF16))

    @pl.when(j == nj - 1)
    def _():
        conv = acc_ref[...] + cur_ref[...] * bias_ref[...]

        @pl.when(o == 0)
        def _():
            nv = short_conv(x1_ref, 1) * conv
            cur_ref[...] = nv
            curb_ref[...] = nv.astype(BF16)

        @pl.when(o == 1)
        def _():
            y_ref[...] = short_conv(x2_ref, 2) * conv


def _hyena(hv, hx1, hx2, lw, spectra, tables, B, L):
    kr, ki, kn = spectra
    cos, msin = tables
    Tk = min(DFT_TILE, L)
    nj = L // Tk
    W = HY_WIDTH * max(1, min(B, HYENA_GROUP_ROWS // L))
    reps = W // HY_WIDTH
    ng = (B * HY_WIDTH) // W
    cw = jnp.tile(lw["hy_conv_w"].reshape(3, 3, 1, HY_WIDTH).transpose(1, 0, 2, 3), (1, 1, reps, 1))
    cw = cw.reshape(3, 3, W)
    cb = jnp.tile(lw["hy_conv_b"].reshape(3, 1, 1, HY_WIDTH), (1, 1, reps, 1)).reshape(3, 1, W)
    bias = jnp.tile(lw["hy_bias"].reshape(2, 1, 1, HY_WIDTH), (1, 1, reps, 1)).reshape(2, 1, W)
    col = pl.BlockSpec((L, W), lambda g, o, j: (0, g))
    full = lambda a: pl.BlockSpec(a.shape, lambda g, o, j: (0,) * a.ndim)
    spec = pl.BlockSpec((None, Tk, HY_WIDTH), lambda g, o, j: (o, j, 0))
    rows = pl.BlockSpec((Tk, L), lambda g, o, j: (j, 0))
    cols = pl.BlockSpec((L, Tk), lambda g, o, j: (0, j))
    return pl.pallas_call(
        functools.partial(_hyena_kernel, L=L, W=W, nj=nj),
        grid=(ng, 2, nj),
        in_specs=[col, col, col, full(cw), full(cb),
                  pl.BlockSpec((None, 1, W), lambda g, o, j: (o, 0, 0)), spec, spec,
                  pl.BlockSpec((None, 1, HY_WIDTH), lambda g, o, j: (o, 0, 0)),
                  rows, rows, cols, cols],
        out_specs=col,
        out_shape=jax.ShapeDtypeStruct((L, B * HY_WIDTH), F32),
        scratch_shapes=[pltpu.VMEM((L, W), F32), pltpu.VMEM((L, W), BF16), pltpu.VMEM((L, W), F32)],
        compiler_params=_cparams(("arbitrary", "arbitrary", "arbitrary"), 56),
        name="hyena_conv",
    )(hv, hx1, hx2, cw, cb, bias, kr, ki, kn, cos, msin, cos, msin)


def _rope(x, cos, sin):
    h = MLA_ROPE // 2
    x1, x2 = x[:, :h], x[:, h:]
    return jnp.concatenate([x1 * cos - x2 * sin, x1 * sin + x2 * cos], axis=-1)


def _mla_kernel(*refs, L, latent):
    if latent:
        (q_ref, kv_ref, kr_ref, cosq_ref, sinq_ref, cos_ref, sin_ref, cckv_ref, ckr_ref, wukv_ref,
         o_ref, kc_ref, vc_ref) = refs
    else:
        q_ref, kv_ref, kr_ref, o_ref, kc_ref, vc_ref = refs
    qi = pl.program_id(1)
    dk = MLA_NOPE + MLA_V
    dq = MLA_NOPE + MLA_ROPE

    @pl.when(qi == 0)
    def _():
        kv = kv_ref[...]
        kr = kr_ref[...]
        if latent:
            kr = _rope(kr, cos_ref[...], sin_ref[...])
            kvc = _dot(cckv_ref[...].astype(BF16), wukv_ref[...])
            krc = ckr_ref[...]
        for h in range(MLA_HEADS):
            kc_ref[h, 0:L, :] = jnp.concatenate([kv[:, h * dk:h * dk + MLA_NOPE], kr], axis=-1).astype(BF16)
            vc_ref[h, 0:L, :] = kv[:, h * dk + MLA_NOPE:(h + 1) * dk].astype(BF16)
            if latent:
                kc_ref[h, L:L + PAST_LEN, :] = jnp.concatenate(
                    [kvc[:, h * dk:h * dk + MLA_NOPE], krc], axis=-1).astype(BF16)
                vc_ref[h, L:L + PAST_LEN, :] = kvc[:, h * dk + MLA_NOPE:(h + 1) * dk].astype(BF16)

    q = q_ref[...]
    scale = dq ** -0.5
    outs = []
    for h in range(MLA_HEADS):
        qn = q[:, h * dq:h * dq + MLA_NOPE]
        qr = q[:, h * dq + MLA_NOPE:(h + 1) * dq]
        if latent:
            qr = _rope(qr, cosq_ref[...], sinq_ref[...])
        qh = jnp.concatenate([qn, qr], axis=-1).astype(BF16)
        s = _dot_nt(qh, kc_ref[h]) * scale
        p = jnp.exp(s - jnp.max(s, axis=-1, keepdims=True))
        denom = jnp.sum(p, axis=-1, keepdims=True)
        outs.append(_dot(p.astype(BF16), vc_ref[h]) / denom)
    o_ref[...] = jnp.concatenate(outs, axis=-1)


def _rope_tables(L):
    rows = L // GRID_W
    row = jnp.repeat(jnp.arange(rows), GRID_W).astype(F32)
    col = jnp.tile(jnp.arange(GRID_W), rows).astype(F32)
    per_axis = MLA_ROPE // 4
    inv = ROPE_BASE ** (-jnp.arange(per_axis, dtype=F32) / per_axis)
    ang = jnp.concatenate([row[:, None] * inv, col[:, None] * inv], axis=-1)
    return jnp.cos(ang), jnp.sin(ang)


def _mla(q, kv, kr, lw, B, L, ctx):
    latent = ctx is not None
    TQ = Q_TILE
    nq = L // TQ
    Lk = L + PAST_LEN if latent else L
    q_cols = MLA_HEADS * (MLA_NOPE + MLA_ROPE)
    kv_cols = MLA_HEADS * (MLA_NOPE + MLA_V)
    seq = lambda w: pl.BlockSpec((L, w), lambda b, i: (b, 0))
    ins = [q, kv, kr]
    specs = [pl.BlockSpec((TQ, q_cols), lambda b, i: (b * nq + i, 0)), seq(kv_cols), seq(MLA_ROPE)]
    if latent:
        cos, sin = _rope_tables(L)
        ins += [cos, sin, cos, sin, ctx[0], ctx[1], lw["mla_w_ukv"]]
        half = MLA_ROPE // 2
        specs += [pl.BlockSpec((TQ, half), lambda b, i: (i, 0))] * 2
        specs += [pl.BlockSpec((L, half), lambda b, i: (0, 0))] * 2
        specs += [pl.BlockSpec((None, PAST_LEN, MLA_KV_LORA), lambda b, i: (b, 0, 0)),
                  pl.BlockSpec((None, PAST_LEN, MLA_ROPE), lambda b, i: (b, 0, 0)),
                  pl.BlockSpec(lw["mla_w_ukv"].shape, lambda b, i: (0, 0))]
    return pl.pallas_call(
        functools.partial(_mla_kernel, L=L, latent=latent),
        grid=(B, nq),
        in_specs=specs,
        out_specs=pl.BlockSpec((TQ, MLA_HEADS * MLA_V), lambda b, i: (b * nq + i, 0)),
        out_shape=jax.ShapeDtypeStruct((B * L, MLA_HEADS * MLA_V), F32),
        scratch_shapes=[pltpu.VMEM((MLA_HEADS, Lk, MLA_NOPE + MLA_ROPE), BF16),
                        pltpu.VMEM((MLA_HEADS, Lk, MLA_V), BF16)],
        compiler_params=_cparams(("arbitrary", "arbitrary"), 48),
        name="mla_attention",
    )(*ins)


def _ret_kernel(*refs, L, has_state, emit_state):
    refs = list(refs)
    u_ref, dl_ref = refs[:2]
    pos = 2
    s0_ref = None
    if has_state:
        s0_ref = refs[pos]
        pos += 1
    y_ref = refs[pos]
    pos += 1
    sout_ref = None
    if emit_state:
        sout_ref = refs[pos]
        pos += 1
    of_ref, kv_ref, sin_ref, dcomb_ref, dtab_ref = refs[pos:]

    C = RET_BLOCK
    n = L // C
    H, DK, DV = RET_HEADS, RET_DK, RET_DV
    qk = H * DK
    DK_F, DK_B, DQ_F, DQ_B, DC_F, DC_B, MASK = range(7)
    use_cross = has_state or n > 1

    @pl.when(pl.program_id(0) == 0)
    def _():
        x = dl_ref[...]
        log_g = jnp.minimum(x, 0.0) - jnp.log1p(jnp.exp(-jnp.abs(x)))
        gf = [log_g[0:1, h:h + 1] for h in range(H)]
        gb = [log_g[1:2, h:h + 1] for h in range(H)]
        diff = (lax.broadcasted_iota(jnp.int32, (C, C), 0)
                - lax.broadcasted_iota(jnp.int32, (C, C), 1)).astype(F32)
        idx = lax.broadcasted_iota(jnp.int32, (C, 1), 0).astype(F32)
        for h in range(H):
            dcomb_ref[h] = (jnp.where(diff >= 0, jnp.exp(jnp.maximum(diff, 0.0) * gf[h]), 0.0)
                            + jnp.where(diff <= 0, jnp.exp(jnp.maximum(-diff, 0.0) * gb[h]), 0.0))

        def head_cols(lag, g):
            return jnp.concatenate([jnp.broadcast_to(jnp.exp(lag * g[h]), (C, DK)) for h in range(H)], axis=1)

        dtab_ref[DK_F] = head_cols(C - 1.0 - idx, gf)
        dtab_ref[DK_B] = head_cols(idx, gb)
        dtab_ref[DQ_F] = head_cols(idx + 1.0, gf)
        dtab_ref[DQ_B] = head_cols(C - idx, gb)
        row_head = lax.broadcasted_iota(jnp.int32, (qk, qk), 0) // DK
        col_head = lax.broadcasted_iota(jnp.int32, (qk, qk), 1) // DV
        same_head = row_head == col_head
        for slot, g in ((DC_F, gf), (DC_B, gb)):
            dc = jnp.zeros((qk, qk), F32)
            for h in range(H):
                dc = jnp.where(same_head & (row_head == h), jnp.exp(C * g[h]), dc)
            dtab_ref[slot] = dc
        dtab_ref[MASK] = same_head.astype(F32)

    def heads(a, width):
        return [a[:, h * width:(h + 1) * width] for h in range(H)]

    for c in range(n):
        rows = slice(c * C, (c + 1) * C)
        q = u_ref[rows, 0:qk]
        k = u_ref[rows, qk:2 * qk] * (DK ** -0.5)
        v = u_ref[rows, 2 * qk:2 * qk + H * DV]
        vb = v.astype(BF16)
        intra = []
        for h, (qh, kh, vh) in enumerate(zip(heads(q, DK), heads(k, DK), heads(v, DV))):
            att = _dot_nt(qh.astype(BF16), kh.astype(BF16)) * dcomb_ref[h]
            intra.append(_dot(att.astype(BF16), vh.astype(BF16)))
        of_ref[rows, :] = jnp.concatenate(intra, axis=1)
        for d, slot in ((0, DK_F), (1, DK_B)):
            kv_ref[d, c] = _dot((k * dtab_ref[slot]).T.astype(BF16), vb)

    finals = []
    for d, slot in ((0, DC_F), (1, DC_B)):
        if has_state:
            zero = jnp.zeros((DK, DV), F32)
            S = jnp.concatenate(
                [jnp.concatenate([s0_ref[d, h] if g == h else zero for g in range(H)], axis=1)
                 for h in range(H)], axis=0)
        else:
            S = jnp.zeros((qk, H * DV), F32)
        for c in (range(n) if d == 0 else reversed(range(n))):
            if use_cross:
                sin_ref[d, c] = S.astype(BF16)
            S = S * dtab_ref[slot] + kv_ref[d, c] * dtab_ref[MASK]
        finals.append(S)
    if emit_state:
        for d in range(2):
            for h in range(H):
                sout_ref[d, h] = finals[d][h * DK:(h + 1) * DK, h * DV:(h + 1) * DV]

    for c in range(n):
        rows = slice(c * C, (c + 1) * C)
        tot = of_ref[rows, :]
        if use_cross:
            qb = u_ref[rows, 0:qk].astype(BF16)
            tot = (tot + _dot(qb, sin_ref[0, c]) * dtab_ref[DQ_F]
                   + _dot(qb, sin_ref[1, c]) * dtab_ref[DQ_B])
        normed = []
        for th in heads(tot, DV):
            xc = th - jnp.mean(th, axis=-1, keepdims=True)
            normed.append(xc * lax.rsqrt(jnp.mean(xc * xc, axis=-1, keepdims=True) + EPS))
        gate = u_ref[rows, 2 * qk + H * DV:2 * qk + 2 * H * DV]
        y_ref[rows, :] = _silu(gate) * jnp.concatenate(normed, axis=1)


def _retention(u_ret, decay_logit, B, L, state, emit_state):
    has_state = state is not None
    st_block = (None, 2, RET_HEADS, RET_DK, RET_DV)
    st_spec = pl.BlockSpec(st_block, lambda b: (b, 0, 0, 0, 0))
    ins = [u_ret, decay_logit]
    specs = [pl.BlockSpec((L, RET_COLS), lambda b: (b, 0)), pl.BlockSpec(decay_logit.shape, lambda b: (0, 0))]
    if has_state:
        ins.append(state)
        specs.append(st_spec)
    vd = RET_HEADS * RET_DV
    out_specs = [pl.BlockSpec((L, vd), lambda b: (b, 0))]
    out_shape = [jax.ShapeDtypeStruct((B * L, vd), F32)]
    if emit_state:
        out_specs.append(st_spec)
        out_shape.append(jax.ShapeDtypeStruct((B, 2, RET_HEADS, RET_DK, RET_DV), F32))
    res = pl.pallas_call(
        functools.partial(_ret_kernel, L=L, has_state=has_state, emit_state=emit_state),
        grid=(B,),
        in_specs=specs,
        out_specs=out_specs,
        out_shape=out_shape,
        scratch_shapes=[pltpu.VMEM((L, vd), F32),
                        pltpu.VMEM((2, L // RET_BLOCK, vd, vd), F32),
                        pltpu.VMEM((2, L // RET_BLOCK, vd, vd), BF16),
                        pltpu.VMEM((RET_HEADS, RET_BLOCK, RET_BLOCK), F32),
                        pltpu.VMEM((7, RET_BLOCK, vd), F32)],
        compiler_params=_cparams(("arbitrary",), 48),
        name="retention",
    )(*ins)
    return (res[0], res[1]) if emit_state else (res[0], None)


def _conf_kernel(u_ref, w_ref, b_ref, g_ref, be_ref, y_ref, zp_ref, *, L):
    Wc = CONF_WIDTH
    halo = CONF_HALO
    zp_ref[0:halo, :] = jnp.zeros((halo, Wc), F32)
    zp_ref[halo + L:2 * halo + L, :] = jnp.zeros((halo, Wc), F32)
    zp_ref[halo:halo + L, :] = u_ref[:, 0:Wc] * jax.nn.sigmoid(u_ref[:, Wc:2 * Wc])
    first = halo - CONF_KERNEL // 2
    R = CONF_ROWS
    for c in range(L // R):
        acc = jnp.broadcast_to(b_ref[...], (R, Wc))
        for k in range(CONF_KERNEL):
            acc = acc + w_ref[k:k + 1, :] * zp_ref[c * R + first + k:c * R + first + k + R, :]
        mu = jnp.mean(acc, axis=-1, keepdims=True)
        xc = acc - mu
        z = xc * lax.rsqrt(jnp.mean(xc * xc, axis=-1, keepdims=True) + EPS) * g_ref[...] + be_ref[...]
        y_ref[c * R:(c + 1) * R, :] = _silu(z)


def _conformer(u_conf, lw, B, L):
    full = lambda a: pl.BlockSpec(a.shape, lambda b: (0,) * a.ndim)
    ws = (lw["conf_dw_w"], lw["conf_dw_b"], lw["conf_ln_g"], lw["conf_ln_b"])
    return pl.pallas_call(
        functools.partial(_conf_kernel, L=L),
        grid=(B,),
        in_specs=[pl.BlockSpec((L, CONF_COLS), lambda b: (b, 0))] + [full(w) for w in ws],
        out_specs=pl.BlockSpec((L, CONF_WIDTH), lambda b: (b, 0)),
        out_shape=jax.ShapeDtypeStruct((B * L, CONF_WIDTH), F32),
        scratch_shapes=[pltpu.VMEM((L + 2 * CONF_HALO, CONF_WIDTH), F32)],
        compiler_params=_cparams(("arbitrary",), 32),
        name="conformer",
    )(u_conf, *ws)


def _layer_block(layer):
    def spec(a):
        return pl.BlockSpec((None,) + a.shape[1:], lambda i: (layer,) + (0,) * (a.ndim - 1),
                            pipeline_mode=pl.Buffered(1))
    return spec


def _merge_kernel(x_ref, mod_ref, g1_ref, yhy_ref, ymla_ref, yret_ref, yconf_ref,
                  gw_ref, gb_ref, why_ref, wmla_ref, wret_ref, wconf_ref, wo_ref, o_ref, *, seqs_per_tile):
    x = x_ref[...]
    h = _rms(x, g1_ref[...]) * (1.0 + mod_ref[1:2, :]) + mod_ref[0:1, :]
    hb = h.astype(BF16)
    if seqs_per_tile == 1:
        yhy = yhy_ref[...]
    else:
        yhy = jnp.concatenate(
            [yhy_ref[:, s * HY_WIDTH:(s + 1) * HY_WIDTH] for s in range(seqs_per_tile)], axis=0)
    branches = ((yhy, why_ref), (ymla_ref[...], wmla_ref), (yret_ref[...], wret_ref), (yconf_ref[...], wconf_ref))
    D = D_MODEL
    merged = None
    for i, (y, w_ref) in enumerate(branches):
        gate = jax.nn.sigmoid(_dot(hb, gw_ref[:, i * D:(i + 1) * D]) + gb_ref[:, i * D:(i + 1) * D])
        term = gate * _dot(y.astype(BF16), w_ref[...])
        merged = term if merged is None else merged + term
    o_ref[...] = x + mod_ref[2:3, :] * _dot(merged.astype(BF16), wo_ref[...])


def _merge(x, mod, lw, y_hy, y_mla, y_ret, y_conf, B, L, latent):
    T = B * L
    TM = TOKEN_TILE
    if L >= TM:
        tiles_per_seq, seqs_per_tile = L // TM, 1
        hy_spec = pl.BlockSpec((TM, HY_WIDTH), lambda i: (i % tiles_per_seq, i // tiles_per_seq))
    else:
        tiles_per_seq, seqs_per_tile = 1, TM // L
        hy_spec = pl.BlockSpec((L, seqs_per_tile * HY_WIDTH), lambda i: (0, i))
    stacked = _layer_block(lw["layer"])
    small = lambda a: pl.BlockSpec(a.shape, lambda i: (0,) * a.ndim)
    row = lambda w: pl.BlockSpec((TM, w), lambda i: (i, 0))
    ws = (lw["gate_w"], lw["gate_b"], lw["hy_out"], lw["mla_out"], lw["ret_out"], lw["conf_out"], lw["w_o"])
    return pl.pallas_call(
        functools.partial(_merge_kernel, seqs_per_tile=seqs_per_tile),
        grid=(T // TM,),
        in_specs=[row(D_MODEL), pl.BlockSpec((None, 6, D_MODEL), _mod_row(tiles_per_seq, latent)),
                  small(lw["norm1_g"]), hy_spec, row(256), row(256), row(256)]
        + [small(w) if w.ndim == 2 else stacked(w) for w in ws],
        out_specs=row(D_MODEL),
        out_shape=jax.ShapeDtypeStruct((T, D_MODEL), F32),
        compiler_params=_cparams(("arbitrary",), 56),
        name="merge",
    )(x, mod, lw["norm1_g"], y_hy, y_mla, y_ret, y_conf, *ws)


def _ffn_kernel(x_ref, mod_ref, g2_ref, w1_ref, w2_ref, fg_ref, o_ref, *, final):
    x = x_ref[...]
    h2 = (_rms(x, g2_ref[...]) * (1.0 + mod_ref[4:5, :]) + mod_ref[3:4, :]).astype(BF16)
    acc = None
    for c in range(D_FF // FFN_CHUNK):
        c0 = c * FFN_CHUNK
        a = _dot(h2, w1_ref[:, c0:c0 + FFN_CHUNK])
        b = _dot(h2, w1_ref[:, D_FF + c0:D_FF + c0 + FFN_CHUNK])
        part = _dot((_silu(a) * b).astype(BF16), w2_ref[c0:c0 + FFN_CHUNK, :])
        acc = part if acc is None else acc + part
    out = x + mod_ref[5:6, :] * acc
    if final:
        out = _rms(out, fg_ref[...])
    o_ref[...] = out


def _ffn(x, mod, lw, final_g, B, L, latent, final):
    T = B * L
    TM = TOKEN_TILE
    tiles_per_seq = max(L // TM, 1)
    stacked = _layer_block(lw["layer"])
    small = lambda a: pl.BlockSpec(a.shape, lambda i: (0,) * a.ndim)
    row = pl.BlockSpec((TM, D_MODEL), lambda i: (i, 0))
    ws = (lw["norm2_g"], lw["ffn_w1"], lw["ffn_w2"])
    return pl.pallas_call(
        functools.partial(_ffn_kernel, final=final),
        grid=(T // TM,),
        in_specs=[row, pl.BlockSpec((None, 6, D_MODEL), _mod_row(tiles_per_seq, latent))]
        + [small(w) if w.ndim == 2 else stacked(w) for w in ws] + [small(final_g)],
        out_specs=row,
        out_shape=jax.ShapeDtypeStruct((T, D_MODEL), F32),
        compiler_params=_cparams(("arbitrary",), 56),
        name="ffn",
    )(x, mod, *ws, final_g)


def _trunk_layer(x, mod, lw, final_g, tables, B, L, ctx, state, final):
    latent = ctx is not None
    hv, hx1, hx2, q, ckv, kr, kv, u_ret, u_conf = _inproj(x, mod, lw, B, L, latent)
    spectra = _filter_spectra(lw, L, tables)
    y_hy = _hyena(hv, hx1, hx2, lw, spectra, tables, B, L)
    y_mla = _mla(q, kv, kr, lw, B, L, ctx)
    y_ret, S = _retention(u_ret, lw["ret_decay"], B, L, state, emit_state=not latent)
    y_conf = _conformer(u_conf, lw, B, L)
    x = _merge(x, mod, lw, y_hy, y_mla, y_ret, y_conf, B, L, latent)
    x = _ffn(x, mod, lw, final_g, B, L, latent, final)
    return x, ckv, kr, S


_STACKED = ("gate_w", "hy_out", "mla_out", "ret_out", "conf_out", "w_o", "ffn_w1", "ffn_w2")


def _layer_weights(l, w_in, p, stacks):
    hy, mla = HY_COLS, HY_COLS + MLA_COLS
    ret = mla + RET_COLS
    w = w_in[l]
    ck = w[:, hy + MLA_Q_LORA:mla]
    ck = jnp.pad(ck, ((0, 0), (0, 256 - ck.shape[1])))
    row = lambda a: a[l].reshape(1, -1)
    return {
        "norm1_g": row(p["norm1_g"]), "norm2_g": row(p["norm2_g"]),
        "w_hy": w[:, :hy].astype(BF16), "w_cq": w[:, hy:hy + MLA_Q_LORA].astype(BF16),
        "w_ck": ck.astype(BF16), "w_ret": w[:, mla:ret].astype(BF16), "w_conf": w[:, ret:].astype(BF16),
        "mla_q_norm": row(p["mla_q_norm"]), "mla_kv_norm": row(p["mla_kv_norm"]),
        "mla_w_uq": p["mla_w_uq"][l].astype(BF16), "mla_w_ukv": p["mla_w_ukv"][l].astype(BF16),
        "hy_conv_w": p["hy_conv_w"][l], "hy_conv_b": p["hy_conv_b"][l],
        "hy_w1": jnp.pad(p["hy_w1"][l], ((0, 128 - HY_EMB), (0, 0))), "hy_b1": row(p["hy_b1"]),
        "hy_w2": p["hy_w2"][l], "hy_b2": row(p["hy_b2"]), "hy_w3": p["hy_w3"][l],
        "hy_bias": p["hy_bias"][l],
        "ret_decay": p["ret_decay"][l],
        "conf_dw_w": p["conf_dw_w"][l], "conf_dw_b": row(p["conf_dw_b"]),
        "conf_ln_g": row(p["conf_ln_g"]), "conf_ln_b": row(p["conf_ln_b"]),
        "gate_b": row(p["gate_b"]),
        "layer": l,
        **{name: stacks[name] for name in _STACKED},
    }


def kernel(x_prompt, x_sample, cache_mla_ckv, cache_mla_krope, state_ret, c, c_ctx, ada_w, ada_b, norm1_g, w_in, hy_conv_w, hy_conv_b, hy_w1, hy_b1, hy_w2, hy_b2, hy_w3, hy_bias, hy_out, mla_q_norm, mla_w_uq, mla_kv_norm, mla_w_ukv, mla_out, ret_decay, ret_out, conf_dw_w, conf_dw_b, conf_ln_g, conf_ln_b, conf_out, gate_w, gate_b, w_o, norm2_g, ffn_w1, ffn_w2, final_norm_g):
    p = dict(norm1_g=norm1_g, hy_conv_w=hy_conv_w, hy_conv_b=hy_conv_b, hy_w1=hy_w1, hy_b1=hy_b1,
             hy_w2=hy_w2, hy_b2=hy_b2, hy_w3=hy_w3, hy_bias=hy_bias, hy_out=hy_out,
             mla_q_norm=mla_q_norm, mla_w_uq=mla_w_uq, mla_kv_norm=mla_kv_norm, mla_w_ukv=mla_w_ukv,
             mla_out=mla_out, ret_decay=ret_decay, ret_out=ret_out, conf_dw_w=conf_dw_w,
             conf_dw_b=conf_dw_b, conf_ln_g=conf_ln_g, conf_ln_b=conf_ln_b, conf_out=conf_out,
             gate_w=gate_w, gate_b=gate_b, w_o=w_o, norm2_g=norm2_g, ffn_w1=ffn_w1, ffn_w2=ffn_w2)
    Bp, Lp, D = x_prompt.shape
    Bs, Ls, _ = x_sample.shape
    depth = w_in.shape[0]

    cond = jnp.concatenate([c_ctx[None, :], c, jnp.zeros((MOD_ROWS - 1 - Bs, D), F32)], axis=0)
    mod = _modulation(cond, ada_w, ada_b)
    tables_p = _dft_tables(Lp)
    tables_s = _dft_tables(Ls)
    final_g = final_norm_g.reshape(1, D)

    xp = x_prompt.reshape(Bp * Lp, D)
    xs = x_sample.reshape(Bs * Ls, D)
    ckvs, kropes, rets = [], [], []
    stacks = {name: p[name].astype(BF16) for name in _STACKED}
    for l in range(depth):
        lw = _layer_weights(l, w_in, p, stacks)
        final = l == depth - 1
        xp, ckv, kr, S = _trunk_layer(xp, mod[l], lw, final_g, tables_p, Bp, Lp, None, None, final)
        ckvs.append(ckv.reshape(Bp, Lp, MLA_KV_LORA))
        kropes.append(kr.reshape(Bp, Lp, MLA_ROPE))
        rets.append(S)
        xs, _, _, _ = _trunk_layer(xs, mod[l], lw, final_g, tables_s, Bs, Ls,
                                   (cache_mla_ckv[:, l], cache_mla_krope[:, l]), state_ret[:, l], final)
    return (xp.reshape(Bp, Lp, D), xs.reshape(Bs, Ls, D),
            jnp.stack(ckvs, axis=1), jnp.stack(kropes, axis=1), jnp.stack(rets, axis=1))
```

```python
import functools
import math

import jax
import jax.numpy as jnp
from jax import lax
from jax.experimental import pallas as pl
from jax.experimental.pallas import tpu as pltpu

F32 = jnp.float32
BF16 = jnp.bfloat16

D_MODEL = 1024
DEPTH = 2
PAST_LEN = 256
EPS = 1e-6
GRID_W = 64

HY_WIDTH = 256
HY_EMB = 33
HY_BANDS = (HY_EMB - 1) // 2
HY_FFN = 64
HY_FAST_DECAY = 0.3
HY_SLOW_DECAY = 1.5
HY_TARGET = 1e-2

MLA_HEADS = 4
MLA_Q_LORA = 256
MLA_KV_LORA = 128
MLA_NOPE = 64
MLA_ROPE = 32
MLA_V = 64
ROPE_BASE = 10000.0

RET_HEADS = 4
RET_DK = 64
RET_DV = 64
RET_BLOCK = 256

CONF_WIDTH = 256
CONF_KERNEL = 31

D_FF = ((8 * D_MODEL // 3 + 255) // 256) * 256
N_BRANCH = 4

HY_COLS = 3 * HY_WIDTH
MLA_COLS = MLA_Q_LORA + MLA_KV_LORA + MLA_ROPE
RET_COLS = 2 * RET_HEADS * RET_DK + 2 * RET_HEADS * RET_DV
CONF_COLS = 2 * CONF_WIDTH

VMEM_BYTES_V7X = 64 * 1024 * 1024
SUBLANES = 8
TOKEN_TILE = 512
Q_TILE = 256
DFT_TILE = 512
HYENA_GROUP_ROWS = 1024
HYENA_ROWS = 1024
CONF_ROWS = 128
CONF_HALO = 16
FFN_CHUNK = D_FF // 2
MOD_ROWS = 8


def _cparams(sem, vmem_mb):
    return pltpu.CompilerParams(dimension_semantics=sem, vmem_limit_bytes=vmem_mb * 1024 * 1024)


def _dot(a, b):
    return jnp.dot(a, b, preferred_element_type=F32)


def _dot_nt(a, b):
    return lax.dot_general(a, b, (((1,), (1,)), ((), ())), preferred_element_type=F32)


def _dot_exact(a, b):
    return jnp.dot(a, b, preferred_element_type=F32, precision=lax.Precision.HIGHEST)


def _rms(x, g):
    return x * lax.rsqrt(jnp.mean(x * x, axis=-1, keepdims=True) + EPS) * g


def _silu(x):
    return x * jax.nn.sigmoid(x)


def _mod_kernel(c_ref, w_ref, b_ref, o_ref):
    s = _silu(c_ref[...]).astype(BF16)
    o_ref[...] = _dot(s, w_ref[...].astype(BF16)) + b_ref[...]


def _modulation(cond, ada_w, ada_b):
    depth, d, cols = ada_w.shape
    blk = 1024
    out = pl.pallas_call(
        _mod_kernel,
        grid=(depth, cols // blk),
        in_specs=[
            pl.BlockSpec((MOD_ROWS, d), lambda l, j: (0, 0)),
            pl.BlockSpec((None, d, blk), lambda l, j: (l, 0, j)),
            pl.BlockSpec((None, 1, blk), lambda l, j: (l, 0, j)),
        ],
        out_specs=pl.BlockSpec((None, MOD_ROWS, blk), lambda l, j: (l, 0, j)),
        out_shape=jax.ShapeDtypeStruct((depth, MOD_ROWS, cols), F32),
        compiler_params=_cparams(("arbitrary", "arbitrary"), 32),
        name="modulation",
    )(cond, ada_w, ada_b.reshape(depth, 1, cols))
    return out.reshape(depth, MOD_ROWS, 6, d)


def _mod_row(tiles_per_seq, latent):
    if latent:
        return lambda i: (1 + i // tiles_per_seq, 0, 0)
    return lambda i: (0, 0, 0)


def _inproj_kernel(x_ref, mod_ref, g1_ref, why_ref, wcq_ref, wck_ref, wret_ref, wconf_ref,
                   qn_ref, kvn_ref, wuq_ref, wukv_ref, wrkt_ref,
                   hv_ref, hx1_ref, hx2_ref, q_ref, ckv_ref, kr_ref, kv_ref, ret_ref, conf_ref, rkt_ref,
                   *, seqs_per_tile, seq_len):
    x = x_ref[...]
    h = _rms(x, g1_ref[...]) * (1.0 + mod_ref[1:2, :]) + mod_ref[0:1, :]
    hb = h.astype(BF16)

    u = _dot(hb, why_ref[...])
    for p, o_ref in enumerate((hv_ref, hx1_ref, hx2_ref)):
        part = u[:, p * HY_WIDTH:(p + 1) * HY_WIDTH]
        if seqs_per_tile == 1:
            o_ref[...] = part
        else:
            for s in range(seqs_per_tile):
                o_ref[:, s * HY_WIDTH:(s + 1) * HY_WIDTH] = part[s * seq_len:(s + 1) * seq_len]

    cq = _dot(hb, wcq_ref[...])
    q_ref[...] = _dot(_rms(cq, qn_ref[...]).astype(BF16), wuq_ref[...])
    ck = _dot(hb, wck_ref[...])
    ckv = _rms(ck[:, :MLA_KV_LORA], kvn_ref[...])
    ckv_ref[...] = ckv
    kr_ref[...] = ck[:, MLA_KV_LORA:MLA_KV_LORA + MLA_ROPE]
    kv_ref[...] = _dot(ckv.astype(BF16), wukv_ref[...])
    ret_ref[...] = _dot(hb, wret_ref[...])
    rkt_ref[...] = _dot_nt(wrkt_ref[...], hb)
    conf_ref[...] = _dot(hb, wconf_ref[...])


def _inproj(x, mod, lw, B, L, latent):
    T = B * L
    TM = TOKEN_TILE
    nt = T // TM
    if L >= TM:
        tiles_per_seq, seqs_per_tile = L // TM, 1
        hy_block = (TM, HY_WIDTH)
        hy_map = lambda i: (i % tiles_per_seq, i // tiles_per_seq)
    else:
        tiles_per_seq, seqs_per_tile = 1, TM // L
        hy_block = (L, seqs_per_tile * HY_WIDTH)
        hy_map = lambda i: (0, i)
    full = lambda a: pl.BlockSpec(a.shape, lambda i: (0,) * a.ndim)
    row = lambda w: pl.BlockSpec((TM, w), lambda i: (i, 0))
    weights = (lw["norm1_g"], lw["w_hy"], lw["w_cq"], lw["w_ck"], lw["w_ret"], lw["w_conf"],
               lw["mla_q_norm"], lw["mla_kv_norm"], lw["mla_w_uq"], lw["mla_w_ukv"], lw["w_ret_kt"])
    hy_shape = jax.ShapeDtypeStruct((L, B * HY_WIDTH), F32)
    qk = RET_HEADS * RET_DK
    q_cols = MLA_HEADS * (MLA_NOPE + MLA_ROPE)
    kv_cols = MLA_HEADS * (MLA_NOPE + MLA_V)
    return pl.pallas_call(
        functools.partial(_inproj_kernel, seqs_per_tile=seqs_per_tile, seq_len=L),
        grid=(nt,),
        in_specs=[row(D_MODEL), pl.BlockSpec((None, 6, D_MODEL), _mod_row(tiles_per_seq, latent))]
        + [full(w) for w in weights],
        out_specs=[pl.BlockSpec(hy_block, hy_map)] * 3
        + [row(q_cols), row(MLA_KV_LORA), row(MLA_ROPE), row(kv_cols), row(RET_COLS), row(CONF_COLS),
           pl.BlockSpec((qk, TM), lambda i: (0, i))],
        out_shape=[hy_shape] * 3 + [
            jax.ShapeDtypeStruct((T, q_cols), F32),
            jax.ShapeDtypeStruct((T, MLA_KV_LORA), F32),
            jax.ShapeDtypeStruct((T, MLA_ROPE), F32),
            jax.ShapeDtypeStruct((T, kv_cols), F32),
            jax.ShapeDtypeStruct((T, RET_COLS), F32),
            jax.ShapeDtypeStruct((T, CONF_COLS), F32),
            jax.ShapeDtypeStruct((qk, T), F32),
        ],
        compiler_params=_cparams(("arbitrary",), 48),
        name="inproj",
    )(x, mod, *weights)


def _dft_tables(L):
    N = 2 * L
    k_lo = min(L, 32)
    k_hi = L // k_lo
    t = jnp.arange(L, dtype=jnp.int32)[None, :]

    def cs(k):
        m = (k[:, None] * t) % N
        ang = m.astype(F32) * (2.0 * math.pi / N)
        return jnp.cos(ang), jnp.sin(ang)

    ca, sa = cs(jnp.arange(k_hi, dtype=jnp.int32) * k_lo)
    cb, sb = cs(jnp.arange(k_lo, dtype=jnp.int32))
    cos = (ca[:, None, :] * cb[None, :, :] - sa[:, None, :] * sb[None, :, :]).reshape(L, L)
    sin = (sa[:, None, :] * cb[None, :, :] + ca[:, None, :] * sb[None, :, :]).reshape(L, L)
    return cos.astype(BF16), (-sin).astype(BF16)


def _filter_features(L):
    t = jnp.linspace(0.0, 1.0, L, dtype=F32)[:, None]
    w = 2.0 * math.pi * jnp.arange(L, dtype=F32)[:, None] / L
    f = jnp.linspace(1e-4, HY_BANDS - 1, HY_BANDS, dtype=F32)[None, :]
    z = jnp.concatenate([t, jnp.cos(f * w), -jnp.sin(f * w)], axis=-1)
    z = jnp.pad(z, ((0, 0), (0, 128 - HY_EMB)))
    max_decay = math.log(HY_TARGET) / HY_FAST_DECAY
    min_decay = math.log(HY_TARGET) / HY_SLOW_DECAY
    deltas = jnp.abs(jnp.linspace(min_decay, max_decay, HY_WIDTH, dtype=F32))
    decay = jnp.exp(-t * deltas[None, :])
    return z, decay


def _alternating(rows):
    t = lax.broadcasted_iota(jnp.int32, (rows, 1), 0)
    return (1 - 2 * (t & 1)).astype(F32)


def _filter_kernel(z_ref, dec_ref, w1_ref, b1_ref, w2_ref, b2_ref, w3_ref, cos_ref, sin_ref,
                   kr_ref, ki_ref, kn_ref, h_ref, *, L):
    j = pl.program_id(0)
    W = HY_WIDTH
    N = 2 * L

    @pl.when(j == 0)
    def _():
        h = jnp.sin(_dot_exact(z_ref[...], w1_ref[...]) + b1_ref[...])
        h = jnp.sin(_dot_exact(h, w2_ref[...]) + b2_ref[...])
        h = _dot_exact(h, w3_ref[...]) * jnp.concatenate([dec_ref[...]] * 4, axis=1)
        cs = jnp.sum(jnp.abs(h), axis=0, keepdims=True)
        s0 = cs[:, 0:W] + cs[:, W:2 * W]
        s1 = cs[:, 2 * W:3 * W] + cs[:, 3 * W:4 * W]
        h = h / jnp.concatenate([s0, s0, s1, s1], axis=1)
        row = lax.broadcasted_iota(jnp.int32, h.shape, 0)
        col = lax.broadcasted_iota(jnp.int32, h.shape, 1)
        backward = (col // W) % 2 == 1
        h = jnp.where(backward & (row == 0), 0.0, h)
        h_ref[...] = h.astype(BF16)
        nyq = jnp.sum(h * _alternating(L), axis=0, keepdims=True) * (1.0 / N)
        for o in range(2):
            c = 2 * o * W
            kn_ref[o] = nyq[:, c:c + W] + nyq[:, c + W:c + 2 * W]

    hb = h_ref[...]
    sr = _dot(cos_ref[...], hb)
    si = _dot(sin_ref[...], hb)
    row = lax.broadcasted_iota(jnp.int32, (sr.shape[0], W), 0)
    scale = jnp.where((row == 0) & (j == 0), 1.0 / N, 2.0 / N)
    for o in range(2):
        c = 2 * o * W
        kr_ref[o] = (sr[:, c:c + W] + sr[:, c + W:c + 2 * W]) * scale
        ki_ref[o] = (si[:, c:c + W] - si[:, c + W:c + 2 * W]) * scale


def _filter_spectra(lw, L, tables):
    z, decay = _filter_features(L)
    cos, msin = tables
    Tk = min(DFT_TILE, L)
    full = lambda a: pl.BlockSpec(a.shape, lambda j: (0,) * a.ndim)
    ins = (z, decay, lw["hy_w1"], lw["hy_b1"], lw["hy_w2"], lw["hy_b2"], lw["hy_w3"])
    tile = pl.BlockSpec((Tk, L), lambda j: (j, 0))
    spec = pl.BlockSpec((2, Tk, HY_WIDTH), lambda j: (0, j, 0))
    return pl.pallas_call(
        functools.partial(_filter_kernel, L=L),
        grid=(L // Tk,),
        in_specs=[full(a) for a in ins] + [tile, tile],
        out_specs=[spec, spec, pl.BlockSpec((2, 1, HY_WIDTH), lambda j: (0, 0, 0))],
        out_shape=[jax.ShapeDtypeStruct((2, L, HY_WIDTH), F32)] * 2
        + [jax.ShapeDtypeStruct((2, 1, HY_WIDTH), F32)],
        scratch_shapes=[pltpu.VMEM((L, 4 * HY_WIDTH), BF16)],
        compiler_params=_cparams(("arbitrary",), 48),
        name="hyena_filter",
    )(*ins, cos, msin)


def _hyena_kernel(v_ref, x1_ref, x2_ref, cw_ref, cb_ref, bias_ref, kr_ref, ki_ref, kn_ref,
                  cos_ref, sin_ref, y_ref, cur_ref, curb_ref, yr_ref, yi_ref, gate_ref,
                  *, L, W, T):
    reps = W // HY_WIDTH
    tiled = lambda a: jnp.concatenate([a] * reps, axis=1) if reps > 1 else a
    alt = _alternating(L)

    def short_conv(u_ref, p):
        u = u_ref[...]
        row = lax.broadcasted_iota(jnp.int32, (L, W), 0)
        prev = jnp.where(row == 0, 0.0, pltpu.roll(u, 1, 0))
        nxt = jnp.where(row == L - 1, 0.0, pltpu.roll(u, L - 1, 0))
        w = cw_ref[p]
        return prev * w[0:1] + u * w[1:2] + nxt * w[2:3] + cb_ref[p]

    cur_ref[...] = short_conv(v_ref, 0)
    for o, x_ref in enumerate((x1_ref, x2_ref)):
        cur = cur_ref[...]
        curb_ref[...] = cur.astype(BF16)
        gate_ref[...] = short_conv(x_ref, o + 1)
        nyq = jnp.sum(cur * alt, axis=0, keepdims=True) * tiled(kn_ref[o])
        for f in range(L // T):
            rows = slice(f * T, (f + 1) * T)
            cb = curb_ref[...]
            xr = _dot(cos_ref[rows, :], cb)
            xi = _dot(sin_ref[rows, :], cb)
            kr, ki = tiled(kr_ref[o, rows, :]), tiled(ki_ref[o, rows, :])
            yr_ref[rows, :] = (xr * kr - xi * ki).astype(BF16)
            yi_ref[rows, :] = (xr * ki + xi * kr).astype(BF16)
        for t in range(L // T):
            rows = slice(t * T, (t + 1) * T)
            conv = (_dot(cos_ref[rows, :], yr_ref[...]) + _dot(sin_ref[rows, :], yi_ref[...])
                    + alt[rows] * nyq + cur_ref[rows, :] * bias_ref[o])
            out = gate_ref[rows, :] * conv
            if o == 0:
                cur_ref[rows, :] = out
            else:
                y_ref[rows, :] = out


def _hyena(hv, hx1, hx2, lw, spectra, tables, B, L):
    kr, ki, kn = spectra
    cos, msin = tables
    T = min(HYENA_ROWS, L)
    W = HY_WIDTH * max(1, min(B, HYENA_GROUP_ROWS // L))
    reps = W // HY_WIDTH
    ng = (B * HY_WIDTH) // W
    cw = jnp.tile(lw["hy_conv_w"].reshape(3, 3, 1, HY_WIDTH).transpose(1, 0, 2, 3), (1, 1, reps, 1))
    cw = cw.reshape(3, 3, W)
    cb = jnp.tile(lw["hy_conv_b"].reshape(3, 1, 1, HY_WIDTH), (1, 1, reps, 1)).reshape(3, 1, W)
    bias = jnp.tile(lw["hy_bias"].reshape(2, 1, 1, HY_WIDTH), (1, 1, reps, 1)).reshape(2, 1, W)
    col = pl.BlockSpec((L, W), lambda g: (0, g))
    once = lambda a: pl.BlockSpec(a.shape, lambda g: (0,) * a.ndim, pipeline_mode=pl.Buffered(1))
    consts = (cw, cb, bias, kr, ki, kn, cos, msin)
    return pl.pallas_call(
        functools.partial(_hyena_kernel, L=L, W=W, T=T),
        grid=(ng,),
        in_specs=[col, col, col] + [once(a) for a in consts],
        out_specs=col,
        out_shape=jax.ShapeDtypeStruct((L, B * HY_WIDTH), F32),
        scratch_shapes=[pltpu.VMEM((L, W), F32), pltpu.VMEM((L, W), BF16), pltpu.VMEM((L, W), BF16),
                        pltpu.VMEM((L, W), BF16), pltpu.VMEM((L, W), F32)],
        compiler_params=_cparams(("arbitrary",), 60),
        name="hyena_conv",
    )(hv, hx1, hx2, *consts)


def _rope(x, cos, sin):
    h = MLA_ROPE // 2
    x1, x2 = x[:, :h], x[:, h:]
    return jnp.concatenate([x1 * cos - x2 * sin, x1 * sin + x2 * cos], axis=-1)


def _mla_kernel(*refs, L, latent):
    if latent:
        (q_ref, kv_ref, kr_ref, cosq_ref, sinq_ref, cos_ref, sin_ref, cckv_ref, ckr_ref, wukv_ref,
         o_ref, kc_ref, vc_ref) = refs
    else:
        q_ref, kv_ref, kr_ref, o_ref, kc_ref, vc_ref = refs
    qi = pl.program_id(1)
    dk = MLA_NOPE + MLA_V
    dq = MLA_NOPE + MLA_ROPE

    @pl.when(qi == 0)
    def _():
        kv = kv_ref[...]
        kr = kr_ref[...]
        if latent:
            kr = _rope(kr, cos_ref[...], sin_ref[...])
            kvc = _dot(cckv_ref[...].astype(BF16), wukv_ref[...])
            krc = ckr_ref[...]
        for h in range(MLA_HEADS):
            kc_ref[h, 0:L, :] = jnp.concatenate([kv[:, h * dk:h * dk + MLA_NOPE], kr], axis=-1).astype(BF16)
            vc_ref[h, 0:L, :] = kv[:, h * dk + MLA_NOPE:(h + 1) * dk].astype(BF16)
            if latent:
                kc_ref[h, L:L + PAST_LEN, :] = jnp.concatenate(
                    [kvc[:, h * dk:h * dk + MLA_NOPE], krc], axis=-1).astype(BF16)
                vc_ref[h, L:L + PAST_LEN, :] = kvc[:, h * dk + MLA_NOPE:(h + 1) * dk].astype(BF16)

    q = q_ref[...]
    scale = dq ** -0.5 * math.log2(math.e)
    outs = []
    for h in range(MLA_HEADS):
        qn = q[:, h * dq:h * dq + MLA_NOPE]
        qr = q[:, h * dq + MLA_NOPE:(h + 1) * dq]
        if latent:
            qr = _rope(qr, cosq_ref[...], sinq_ref[...])
        qh = (jnp.concatenate([qn, qr], axis=-1) * scale).astype(BF16)
        s = _dot_nt(qh, kc_ref[h])
        p = jnp.exp2(s - jnp.max(s, axis=-1, keepdims=True))
        denom = jnp.sum(p, axis=-1, keepdims=True)
        outs.append(_dot(p.astype(BF16), vc_ref[h]) / denom)
    o_ref[...] = jnp.concatenate(outs, axis=-1)


def _rope_tables(L):
    rows = L // GRID_W
    row = jnp.repeat(jnp.arange(rows), GRID_W).astype(F32)
    col = jnp.tile(jnp.arange(GRID_W), rows).astype(F32)
    per_axis = MLA_ROPE // 4
    inv = ROPE_BASE ** (-jnp.arange(per_axis, dtype=F32) / per_axis)
    ang = jnp.concatenate([row[:, None] * inv, col[:, None] * inv], axis=-1)
    return jnp.cos(ang), jnp.sin(ang)


def _mla(q, kv, kr, lw, B, L, ctx):
    latent = ctx is not None
    TQ = Q_TILE
    nq = L // TQ
    Lk = L + PAST_LEN if latent else L
    q_cols = MLA_HEADS * (MLA_NOPE + MLA_ROPE)
    kv_cols = MLA_HEADS * (MLA_NOPE + MLA_V)
    seq = lambda w: pl.BlockSpec((L, w), lambda b, i: (b, 0))
    ins = [q, kv, kr]
    specs = [pl.BlockSpec((TQ, q_cols), lambda b, i: (b * nq + i, 0)), seq(kv_cols), seq(MLA_ROPE)]
    if latent:
        cos, sin = _rope_tables(L)
        ins += [cos, sin, cos, sin, ctx[0], ctx[1], lw["mla_w_ukv"]]
        half = MLA_ROPE // 2
        specs += [pl.BlockSpec((TQ, half), lambda b, i: (i, 0))] * 2
        specs += [pl.BlockSpec((L, half), lambda b, i: (0, 0))] * 2
        specs += [pl.BlockSpec((None, PAST_LEN, MLA_KV_LORA), lambda b, i: (b, 0, 0)),
                  pl.BlockSpec((None, PAST_LEN, MLA_ROPE), lambda b, i: (b, 0, 0)),
                  pl.BlockSpec(lw["mla_w_ukv"].shape, lambda b, i: (0, 0))]
    return pl.pallas_call(
        functools.partial(_mla_kernel, L=L, latent=latent),
        grid=(B, nq),
        in_specs=specs,
        out_specs=pl.BlockSpec((TQ, MLA_HEADS * MLA_V), lambda b, i: (b * nq + i, 0)),
        out_shape=jax.ShapeDtypeStruct((B * L, MLA_HEADS * MLA_V), F32),
        scratch_shapes=[pltpu.VMEM((MLA_HEADS, Lk, MLA_NOPE + MLA_ROPE), BF16),
                        pltpu.VMEM((MLA_HEADS, Lk, MLA_V), BF16)],
        compiler_params=_cparams(("arbitrary", "arbitrary"), 48),
        name="mla_attention",
    )(*ins)


def _ret_kernel(*refs, L, has_state, emit_state):
    refs = list(refs)
    u_ref, kt_ref, dl_ref = refs[:3]
    pos = 3
    s0_ref = None
    if has_state:
        s0_ref = refs[pos]
        pos += 1
    y_ref = refs[pos]
    pos += 1
    sout_ref = None
    if emit_state:
        sout_ref = refs[pos]
        pos += 1
    of_ref, kv_ref, sin_ref, dcomb_ref, dtab_ref, hmask_ref, avg_ref = refs[pos:]

    C = RET_BLOCK
    n = L // C
    H, DK, DV = RET_HEADS, RET_DK, RET_DV
    qk = H * DK
    DK_F, DK_B, DQ_F, DQ_B, DC_F, DC_B, MASK = range(7)
    use_cross = has_state or n > 1

    @pl.when(pl.program_id(0) == 0)
    def _():
        x = dl_ref[...]
        log_g = jnp.minimum(x, 0.0) - jnp.log1p(jnp.exp(-jnp.abs(x)))
        gf = [log_g[0:1, h:h + 1] for h in range(H)]
        gb = [log_g[1:2, h:h + 1] for h in range(H)]
        diff = (lax.broadcasted_iota(jnp.int32, (C, C), 0)
                - lax.broadcasted_iota(jnp.int32, (C, C), 1)).astype(F32)
        idx = lax.broadcasted_iota(jnp.int32, (C, 1), 0).astype(F32)
        for h in range(H):
            dcomb_ref[h] = (jnp.where(diff >= 0, jnp.exp(jnp.maximum(diff, 0.0) * gf[h]), 0.0)
                            + jnp.where(diff <= 0, jnp.exp(jnp.maximum(-diff, 0.0) * gb[h]), 0.0))

        def head_cols(lag, g):
            return jnp.concatenate([jnp.broadcast_to(jnp.exp(lag * g[h]), (C, DK)) for h in range(H)], axis=1)

        def head_rows(lag, g):
            return jnp.concatenate(
                [jnp.broadcast_to(jnp.exp(lag * g[h]) * (DK ** -0.5), (DK, C)) for h in range(H)], axis=0)

        tok = lax.broadcasted_iota(jnp.int32, (1, C), 1).astype(F32)
        dtab_ref[DK_F] = head_rows(C - 1.0 - tok, gf)
        dtab_ref[DK_B] = head_rows(tok, gb)
        dtab_ref[DQ_F] = head_cols(idx + 1.0, gf)
        dtab_ref[DQ_B] = head_cols(C - idx, gb)
        row_head = lax.broadcasted_iota(jnp.int32, (qk, qk), 0) // DK
        col_head = lax.broadcasted_iota(jnp.int32, (qk, qk), 1) // DV
        same_head = row_head == col_head
        for slot, g in ((DC_F, gf), (DC_B, gb)):
            dc = jnp.zeros((qk, qk), F32)
            for h in range(H):
                dc = jnp.where(same_head & (row_head == h), jnp.exp(C * g[h]), dc)
            dtab_ref[slot] = dc
        dtab_ref[MASK] = same_head.astype(F32)
        avg_ref[...] = jnp.where(same_head, 1.0 / DV, 0.0).astype(BF16)
        lane_head = lax.broadcasted_iota(jnp.int32, (C, qk), 1) // DK
        for h in range(H):
            hmask_ref[h] = (lane_head == h).astype(F32)

    for c in range(n):
        rows = slice(c * C, (c + 1) * C)
        q = u_ref[rows, 0:qk]
        kb = (u_ref[rows, qk:2 * qk] * (DK ** -0.5)).astype(BF16)
        v = u_ref[rows, 2 * qk:2 * qk + H * DV]
        att = [(_dot_nt((q * hmask_ref[h]).astype(BF16), kb) * dcomb_ref[h]).astype(BF16) for h in range(H)]
        v_heads = jnp.concatenate([(v * hmask_ref[h]).astype(BF16) for h in range(H)], axis=0)
        of_ref[rows, :] = _dot(jnp.concatenate(att, axis=1), v_heads)
        vb = v.astype(BF16)
        kt = kt_ref[:, rows]
        for d, slot in ((0, DK_F), (1, DK_B)):
            kv_ref[d, c] = _dot((kt * dtab_ref[slot]).astype(BF16), vb)

    finals = []
    for d, slot in ((0, DC_F), (1, DC_B)):
        if has_state:
            zero = jnp.zeros((DK, DV), F32)
            S = jnp.concatenate(
                [jnp.concatenate([s0_ref[d, h] if g == h else zero for g in range(H)], axis=1)
                 for h in range(H)], axis=0)
        else:
            S = jnp.zeros((qk, H * DV), F32)
        for c in (range(n) if d == 0 else reversed(range(n))):
            if use_cross:
                sin_ref[d, c] = S.astype(BF16)
            S = S * dtab_ref[slot] + kv_ref[d, c] * dtab_ref[MASK]
        finals.append(S)
    if emit_state:
        for d in range(2):
            for h in range(H):
                sout_ref[d, h] = finals[d][h * DK:(h + 1) * DK, h * DV:(h + 1) * DV]

    def head_mean(a):
        total = None
        for _ in range(3):
            part = a.astype(BF16)
            a = a - part.astype(F32)
            term = _dot(part, avg_ref[...])
            total = term if total is None else total + term
        return total

    for c in range(n):
        rows = slice(c * C, (c + 1) * C)
        tot = of_ref[rows, :]
        if use_cross:
            qb = u_ref[rows, 0:qk].astype(BF16)
            tot = (tot + _dot(qb, sin_ref[0, c]) * dtab_ref[DQ_F]
                   + _dot(qb, sin_ref[1, c]) * dtab_ref[DQ_B])
        xc = tot - head_mean(tot)
        normed = xc * lax.rsqrt(head_mean(xc * xc) + EPS)
        gate = u_ref[rows, 2 * qk + H * DV:2 * qk + 2 * H * DV]
        y_ref[rows, :] = _silu(gate) * normed


def _retention(u_ret, kt_ret, decay_logit, B, L, state, emit_state):
    has_state = state is not None
    st_block = (None, 2, RET_HEADS, RET_DK, RET_DV)
    st_spec = pl.BlockSpec(st_block, lambda b: (b, 0, 0, 0, 0))
    ins = [u_ret, kt_ret, decay_logit]
    specs = [pl.BlockSpec((L, RET_COLS), lambda b: (b, 0)),
             pl.BlockSpec((RET_HEADS * RET_DK, L), lambda b: (0, b)),
             pl.BlockSpec(decay_logit.shape, lambda b: (0, 0))]
    if has_state:
        ins.append(state)
        specs.append(st_spec)
    vd = RET_HEADS * RET_DV
    out_specs = [pl.BlockSpec((L, vd), lambda b: (b, 0))]
    out_shape = [jax.ShapeDtypeStruct((B * L, vd), F32)]
    if emit_state:
        out_specs.append(st_spec)
        out_shape.append(jax.ShapeDtypeStruct((B, 2, RET_HEADS, RET_DK, RET_DV), F32))
    res = pl.pallas_call(
        functools.partial(_ret_kernel, L=L, has_state=has_state, emit_state=emit_state),
        grid=(B,),
        in_specs=specs,
        out_specs=out_specs,
        out_shape=out_shape,
        scratch_shapes=[pltpu.VMEM((L, vd), F32),
                        pltpu.VMEM((2, L // RET_BLOCK, vd, vd), F32),
                        pltpu.VMEM((2, L // RET_BLOCK, vd, vd), BF16),
                        pltpu.VMEM((RET_HEADS, RET_BLOCK, RET_BLOCK), F32),
                        pltpu.VMEM((7, RET_BLOCK, vd), F32),
                        pltpu.VMEM((RET_HEADS, RET_BLOCK, vd), F32),
                        pltpu.VMEM((vd, vd), BF16)],
        compiler_params=_cparams(("arbitrary",), 48),
        name="retention",
    )(*ins)
    return (res[0], res[1]) if emit_state else (res[0], None)


def _conf_kernel(u_ref, w_ref, b_ref, g_ref, be_ref, y_ref, zp_ref, sh_ref, *, L):
    Wc = CONF_WIDTH
    halo = CONF_HALO
    zp_ref[0:halo, :] = jnp.zeros((halo, Wc), F32)
    zp_ref[halo + L:2 * halo + L, :] = jnp.zeros((halo, Wc), F32)
    zp_ref[halo:halo + L, :] = u_ref[:, 0:Wc] * jax.nn.sigmoid(u_ref[:, Wc:2 * Wc])
    first = halo - CONF_KERNEL // 2
    span = sh_ref.shape[1]
    for s in range(SUBLANES):
        sh_ref[s] = zp_ref[first + s:first + s + span, :]
    R = CONF_ROWS
    for c in range(L // R):
        acc = jnp.broadcast_to(b_ref[...], (R, Wc))
        for k in range(CONF_KERNEL):
            a, s = divmod(k, SUBLANES)
            acc = acc + w_ref[k:k + 1, :] * sh_ref[s, c * R + SUBLANES * a:c * R + SUBLANES * a + R, :]
        mu = jnp.mean(acc, axis=-1, keepdims=True)
        xc = acc - mu
        z = xc * lax.rsqrt(jnp.mean(xc * xc, axis=-1, keepdims=True) + EPS) * g_ref[...] + be_ref[...]
        y_ref[c * R:(c + 1) * R, :] = _silu(z)


def _conformer(u_conf, lw, B, L):
    full = lambda a: pl.BlockSpec(a.shape, lambda b: (0,) * a.ndim)
    ws = (lw["conf_dw_w"], lw["conf_dw_b"], lw["conf_ln_g"], lw["conf_ln_b"])
    return pl.pallas_call(
        functools.partial(_conf_kernel, L=L),
        grid=(B,),
        in_specs=[pl.BlockSpec((L, CONF_COLS), lambda b: (b, 0))] + [full(w) for w in ws],
        out_specs=pl.BlockSpec((L, CONF_WIDTH), lambda b: (b, 0)),
        out_shape=jax.ShapeDtypeStruct((B * L, CONF_WIDTH), F32),
        scratch_shapes=[pltpu.VMEM((L + 2 * CONF_HALO, CONF_WIDTH), F32),
                        pltpu.VMEM((SUBLANES, L + SUBLANES * ((CONF_KERNEL - 1) // SUBLANES), CONF_WIDTH), F32)],
        compiler_params=_cparams(("arbitrary",), 48),
        name="conformer",
    )(u_conf, *ws)


def _layer_block(layer):
    def spec(a):
        return pl.BlockSpec((None,) + a.shape[1:], lambda i: (layer,) + (0,) * (a.ndim - 1),
                            pipeline_mode=pl.Buffered(1))
    return spec


def _merge_kernel(x_ref, mod_ref, g1_ref, yhy_ref, ymla_ref, yret_ref, yconf_ref,
                  gw_ref, gb_ref, why_ref, wmla_ref, wret_ref, wconf_ref, wo_ref, o_ref, *, seqs_per_tile):
    x = x_ref[...]
    h = _rms(x, g1_ref[...]) * (1.0 + mod_ref[1:2, :]) + mod_ref[0:1, :]
    hb = h.astype(BF16)
    if seqs_per_tile == 1:
        yhy = yhy_ref[...]
    else:
        yhy = jnp.concatenate(
            [yhy_ref[:, s * HY_WIDTH:(s + 1) * HY_WIDTH] for s in range(seqs_per_tile)], axis=0)
    branches = ((yhy, why_ref), (ymla_ref[...], wmla_ref), (yret_ref[...], wret_ref), (yconf_ref[...], wconf_ref))
    D = D_MODEL
    merged = None
    for i, (y, w_ref) in enumerate(branches):
        gate = jax.nn.sigmoid(_dot(hb, gw_ref[:, i * D:(i + 1) * D]) + gb_ref[:, i * D:(i + 1) * D])
        term = gate * _dot(y.astype(BF16), w_ref[...])
        merged = term if merged is None else merged + term
    o_ref[...] = x + mod_ref[2:3, :] * _dot(merged.astype(BF16), wo_ref[...])


def _merge(x, mod, lw, y_hy, y_mla, y_ret, y_conf, B, L, latent):
    T = B * L
    TM = TOKEN_TILE
    if L >= TM:
        tiles_per_seq, seqs_per_tile = L // TM, 1
        hy_spec = pl.BlockSpec((TM, HY_WIDTH), lambda i: (i % tiles_per_seq, i // tiles_per_seq))
    else:
        tiles_per_seq, seqs_per_tile = 1, TM // L
        hy_spec = pl.BlockSpec((L, seqs_per_tile * HY_WIDTH), lambda i: (0, i))
    stacked = _layer_block(lw["layer"])
    small = lambda a: pl.BlockSpec(a.shape, lambda i: (0,) * a.ndim)
    row = lambda w: pl.BlockSpec((TM, w), lambda i: (i, 0))
    ws = (lw["gate_w"], lw["gate_b"], lw["hy_out"], lw["mla_out"], lw["ret_out"], lw["conf_out"], lw["w_o"])
    return pl.pallas_call(
        functools.partial(_merge_kernel, seqs_per_tile=seqs_per_tile),
        grid=(T // TM,),
        in_specs=[row(D_MODEL), pl.BlockSpec((None, 6, D_MODEL), _mod_row(tiles_per_seq, latent)),
                  small(lw["norm1_g"]), hy_spec, row(256), row(256), row(256)]
        + [small(w) if w.ndim == 2 else stacked(w) for w in ws],
        out_specs=row(D_MODEL),
        out_shape=jax.ShapeDtypeStruct((T, D_MODEL), F32),
        compiler_params=_cparams(("arbitrary",), 56),
        name="merge",
    )(x, mod, lw["norm1_g"], y_hy, y_mla, y_ret, y_conf, *ws)


def _ffn_kernel(x_ref, mod_ref, g2_ref, w1_ref, w2_ref, fg_ref, o_ref, *, final):
    x = x_ref[...]
    h2 = (_rms(x, g2_ref[...]) * (1.0 + mod_ref[4:5, :]) + mod_ref[3:4, :]).astype(BF16)
    acc = None
    for c in range(D_FF // FFN_CHUNK):
        c0 = c * FFN_CHUNK
        a = _dot(h2, w1_ref[:, c0:c0 + FFN_CHUNK])
        b = _dot(h2, w1_ref[:, D_FF + c0:D_FF + c0 + FFN_CHUNK])
        part = _dot((_silu(a) * b).astype(BF16), w2_ref[c0:c0 + FFN_CHUNK, :])
        acc = part if acc is None else acc + part
    out = x + mod_ref[5:6, :] * acc
    if final:
        out = _rms(out, fg_ref[...])
    o_ref[...] = out


def _ffn(x, mod, lw, final_g, B, L, latent, final):
    T = B * L
    TM = TOKEN_TILE
    tiles_per_seq = max(L // TM, 1)
    stacked = _layer_block(lw["layer"])
    small = lambda a: pl.BlockSpec(a.shape, lambda i: (0,) * a.ndim)
    row = pl.BlockSpec((TM, D_MODEL), lambda i: (i, 0))
    ws = (lw["norm2_g"], lw["ffn_w1"], lw["ffn_w2"])
    return pl.pallas_call(
        functools.partial(_ffn_kernel, final=final),
        grid=(T // TM,),
        in_specs=[row, pl.BlockSpec((None, 6, D_MODEL), _mod_row(tiles_per_seq, latent))]
        + [small(w) if w.ndim == 2 else stacked(w) for w in ws] + [small(final_g)],
        out_specs=row,
        out_shape=jax.ShapeDtypeStruct((T, D_MODEL), F32),
        compiler_params=_cparams(("arbitrary",), 56),
        name="ffn",
    )(x, mod, *ws, final_g)


def _trunk_layer(x, mod, lw, final_g, tables, B, L, ctx, state, final):
    latent = ctx is not None
    hv, hx1, hx2, q, ckv, kr, kv, u_ret, u_conf, kt_ret = _inproj(x, mod, lw, B, L, latent)
    spectra = _filter_spectra(lw, L, tables)
    y_hy = _hyena(hv, hx1, hx2, lw, spectra, tables, B, L)
    y_mla = _mla(q, kv, kr, lw, B, L, ctx)
    y_ret, S = _retention(u_ret, kt_ret, lw["ret_decay"], B, L, state, emit_state=not latent)
    y_conf = _conformer(u_conf, lw, B, L)
    x = _merge(x, mod, lw, y_hy, y_mla, y_ret, y_conf, B, L, latent)
    x = _ffn(x, mod, lw, final_g, B, L, latent, final)
    return x, ckv, kr, S


_STACKED = ("gate_w", "hy_out", "mla_out", "ret_out", "conf_out", "w_o", "ffn_w1", "ffn_w2")


def _layer_weights(l, w_in, p, stacks):
    hy, mla = HY_COLS, HY_COLS + MLA_COLS
    ret = mla + RET_COLS
    w = w_in[l]
    ck = w[:, hy + MLA_Q_LORA:mla]
    ck = jnp.pad(ck, ((0, 0), (0, 256 - ck.shape[1])))
    row = lambda a: a[l].reshape(1, -1)
    return {
        "norm1_g": row(p["norm1_g"]), "norm2_g": row(p["norm2_g"]),
        "w_hy": w[:, :hy].astype(BF16), "w_cq": w[:, hy:hy + MLA_Q_LORA].astype(BF16),
        "w_ck": ck.astype(BF16), "w_ret": w[:, mla:ret].astype(BF16), "w_conf": w[:, ret:].astype(BF16),
        "w_ret_kt": w[:, mla + RET_HEADS * RET_DK:mla + 2 * RET_HEADS * RET_DK].T.astype(BF16),
        "mla_q_norm": row(p["mla_q_norm"]), "mla_kv_norm": row(p["mla_kv_norm"]),
        "mla_w_uq": p["mla_w_uq"][l].astype(BF16), "mla_w_ukv": p["mla_w_ukv"][l].astype(BF16),
        "hy_conv_w": p["hy_conv_w"][l], "hy_conv_b": p["hy_conv_b"][l],
        "hy_w1": jnp.pad(p["hy_w1"][l], ((0, 128 - HY_EMB), (0, 0))), "hy_b1": row(p["hy_b1"]),
        "hy_w2": p["hy_w2"][l], "hy_b2": row(p["hy_b2"]), "hy_w3": p["hy_w3"][l],
        "hy_bias": p["hy_bias"][l],
        "ret_decay": p["ret_decay"][l],
        "conf_dw_w": p["conf_dw_w"][l], "conf_dw_b": row(p["conf_dw_b"]),
        "conf_ln_g": row(p["conf_ln_g"]), "conf_ln_b": row(p["conf_ln_b"]),
        "gate_b": row(p["gate_b"]),
        "layer": l,
        **{name: stacks[name] for name in _STACKED},
    }


def kernel(x_prompt, x_sample, cache_mla_ckv, cache_mla_krope, state_ret, c, c_ctx, ada_w, ada_b, norm1_g, w_in, hy_conv_w, hy_conv_b, hy_w1, hy_b1, hy_w2, hy_b2, hy_w3, hy_bias, hy_out, mla_q_norm, mla_w_uq, mla_kv_norm, mla_w_ukv, mla_out, ret_decay, ret_out, conf_dw_w, conf_dw_b, conf_ln_g, conf_ln_b, conf_out, gate_w, gate_b, w_o, norm2_g, ffn_w1, ffn_w2, final_norm_g):
    p = dict(norm1_g=norm1_g, hy_conv_w=hy_conv_w, hy_conv_b=hy_conv_b, hy_w1=hy_w1, hy_b1=hy_b1,
             hy_w2=hy_w2, hy_b2=hy_b2, hy_w3=hy_w3, hy_bias=hy_bias, hy_out=hy_out,
             mla_q_norm=mla_q_norm, mla_w_uq=mla_w_uq, mla_kv_norm=mla_kv_norm, mla_w_ukv=mla_w_ukv,
             mla_out=mla_out, ret_decay=ret_decay, ret_out=ret_out, conf_dw_w=conf_dw_w,
             conf_dw_b=conf_dw_b, conf_ln_g=conf_ln_g, conf_ln_b=conf_ln_b, conf_out=conf_out,
             gate_w=gate_w, gate_b=gate_b, w_o=w_o, norm2_g=norm2_g, ffn_w1=ffn_w1, ffn_w2=ffn_w2)
    Bp, Lp, D = x_prompt.shape
    Bs, Ls, _ = x_sample.shape
    depth = w_in.shape[0]

    cond = jnp.concatenate([c_ctx[None, :], c, jnp.zeros((MOD_ROWS - 1 - Bs, D), F32)], axis=0)
    mod = _modulation(cond, ada_w, ada_b)
    tables_p = _dft_tables(Lp)
    tables_s = _dft_tables(Ls)
    final_g = final_norm_g.reshape(1, D)

    xp = x_prompt.reshape(Bp * Lp, D)
    xs = x_sample.reshape(Bs * Ls, D)
    ckvs, kropes, rets = [], [], []
    stacks = {name: p[name].astype(BF16) for name in _STACKED}
    for l in range(depth):
        lw = _layer_weights(l, w_in, p, stacks)
        final = l == depth - 1
        xp, ckv, kr, S = _trunk_layer(xp, mod[l], lw, final_g, tables_p, Bp, Lp, None, None, final)
        ckvs.append(ckv.reshape(Bp, Lp, MLA_KV_LORA))
        kropes.append(kr.reshape(Bp, Lp, MLA_ROPE))
        rets.append(S)
        xs, _, _, _ = _trunk_layer(xs, mod[l], lw, final_g, tables_s, Bs, Ls,
                                   (cache_mla_ckv[:, l], cache_mla_krope[:, l]), state_ret[:, l], final)
    return (xp.reshape(Bp, Lp, D), xs.reshape(Bs, Ls, D),
            jnp.stack(ckvs, axis=1), jnp.stack(kropes, axis=1), jnp.stack(rets, axis=1))
```

```python
import functools
import math

import jax
import jax.numpy as jnp
from jax import lax
from jax.experimental import pallas as pl
from jax.experimental.pallas import tpu as pltpu

F32 = jnp.float32
BF16 = jnp.bfloat16

D_MODEL = 1024
DEPTH = 2
PAST_LEN = 256
EPS = 1e-6
GRID_W = 64

HY_WIDTH = 256
HY_EMB = 33
HY_BANDS = (HY_EMB - 1) // 2
HY_FFN = 64
HY_FAST_DECAY = 0.3
HY_SLOW_DECAY = 1.5
HY_TARGET = 1e-2

MLA_HEADS = 4
MLA_Q_LORA = 256
MLA_KV_LORA = 128
MLA_NOPE = 64
MLA_ROPE = 32
MLA_V = 64
ROPE_BASE = 10000.0

RET_HEADS = 4
RET_DK = 64
RET_DV = 64
RET_BLOCK = 256

CONF_WIDTH = 256
CONF_KERNEL = 31

D_FF = ((8 * D_MODEL // 3 + 255) // 256) * 256
N_BRANCH = 4

HY_COLS = 3 * HY_WIDTH
MLA_COLS = MLA_Q_LORA + MLA_KV_LORA + MLA_ROPE
RET_COLS = 2 * RET_HEADS * RET_DK + 2 * RET_HEADS * RET_DV
CONF_COLS = 2 * CONF_WIDTH

VMEM_BYTES_V7X = 64 * 1024 * 1024
SUBLANES = 8
TOKEN_TILE = 512
Q_TILE = 256
MLA_SOFTMAX_ROWS = 16
DFT_TILE = 512
HYENA_GROUP_ROWS = 1024
HYENA_ROWS = 1024
CONF_ROWS = 128
CONF_HALO = 16
MXU_DIM_V7X = 256
FFN_CHUNK = 4 * MXU_DIM_V7X
MOD_ROWS = 8


def _cparams(sem, vmem_mb):
    return pltpu.CompilerParams(dimension_semantics=sem, vmem_limit_bytes=vmem_mb * 1024 * 1024)


def _dot(a, b):
    return jnp.dot(a, b, preferred_element_type=F32)


def _dot_nt(a, b):
    return lax.dot_general(a, b, (((1,), (1,)), ((), ())), preferred_element_type=F32)


def _dot_exact(a, b):
    return jnp.dot(a, b, preferred_element_type=F32, precision=lax.Precision.HIGHEST)


def _rms(x, g):
    return x * lax.rsqrt(jnp.mean(x * x, axis=-1, keepdims=True) + EPS) * g


def _silu(x):
    return x * jax.nn.sigmoid(x)


def _mod_kernel(c_ref, w_ref, b_ref, o_ref):
    s = _silu(c_ref[...]).astype(BF16)
    o_ref[...] = _dot(s, w_ref[...].astype(BF16)) + b_ref[...]


def _modulation(cond, ada_w, ada_b):
    depth, d, cols = ada_w.shape
    blk = 1024
    out = pl.pallas_call(
        _mod_kernel,
        grid=(depth, cols // blk),
        in_specs=[
            pl.BlockSpec((MOD_ROWS, d), lambda l, j: (0, 0)),
            pl.BlockSpec((None, d, blk), lambda l, j: (l, 0, j)),
            pl.BlockSpec((None, 1, blk), lambda l, j: (l, 0, j)),
        ],
        out_specs=pl.BlockSpec((None, MOD_ROWS, blk), lambda l, j: (l, 0, j)),
        out_shape=jax.ShapeDtypeStruct((depth, MOD_ROWS, cols), F32),
        compiler_params=_cparams(("arbitrary", "arbitrary"), 32),
        name="modulation",
    )(cond, ada_w, ada_b.reshape(depth, 1, cols))
    return out.reshape(depth, MOD_ROWS, 6, d)


def _mod_row(tiles_per_seq, latent):
    if latent:
        return lambda i: (1 + i // tiles_per_seq, 0, 0)
    return lambda i: (0, 0, 0)


HEAD_LANES = 128


def _rope_lanes(x, cos, sa, sb):
    reps = x.shape[1] // HEAD_LANES
    wide = lambda t: jnp.concatenate([t] * reps, axis=1) if reps > 1 else t
    half = MLA_ROPE // 2
    return (x * wide(cos) + pltpu.roll(x, x.shape[1] - half, 1) * wide(sa)
            + pltpu.roll(x, half, 1) * wide(sb))


def _inproj_kernel(*refs, seqs_per_tile, seq_len, latent):
    (x_ref, mod_ref, g1_ref, why_ref, wcq_ref, wck_ref, wret_ref, wconf_ref,
     qn_ref, kvn_ref, wuq_ref, wukv_ref, wrkt_ref) = refs[:13]
    refs = refs[13:]
    if latent:
        cos_ref, sa_ref, sb_ref = refs[:3]
        refs = refs[3:]
    hv_ref, hx1_ref, hx2_ref, q_ref, kh_ref, vh_ref, ret_ref, conf_ref, rkt_ref = refs[:9]
    if not latent:
        ckv_ref, kr_ref = refs[9:]
    x = x_ref[...]
    h = _rms(x, g1_ref[...]) * (1.0 + mod_ref[1:2, :]) + mod_ref[0:1, :]
    hb = h.astype(BF16)

    u = _dot(hb, why_ref[...])
    for p, o_ref in enumerate((hv_ref, hx1_ref, hx2_ref)):
        part = u[:, p * HY_WIDTH:(p + 1) * HY_WIDTH]
        if seqs_per_tile == 1:
            o_ref[...] = part
        else:
            for s in range(seqs_per_tile):
                o_ref[:, s * HY_WIDTH:(s + 1) * HY_WIDTH] = part[s * seq_len:(s + 1) * seq_len]

    heads_w = MLA_HEADS * HEAD_LANES
    cq = _dot(hb, wcq_ref[...])
    q = _dot(_rms(cq, qn_ref[...]).astype(BF16), wuq_ref[...])
    ck = _dot(hb, wck_ref[...])
    ckv = _rms(ck[:, :MLA_KV_LORA], kvn_ref[...])
    kr_block = ck[:, MLA_KV_LORA:MLA_KV_LORA + HEAD_LANES]
    if latent:
        q = _rope_lanes(q, cos_ref[...], sa_ref[...], sb_ref[...])
        kr_block = _rope_lanes(kr_block, cos_ref[...], sa_ref[...], sb_ref[...])
    else:
        ckv_ref[...] = ckv
        kr_ref[...] = kr_block[:, MLA_NOPE:MLA_NOPE + MLA_ROPE]
    q_ref[...] = (q * ((MLA_NOPE + MLA_ROPE) ** -0.5 * math.log2(math.e))).astype(BF16)
    kv = _dot(ckv.astype(BF16), wukv_ref[...])
    kh_ref[...] = (kv[:, :heads_w] + jnp.concatenate([kr_block] * MLA_HEADS, axis=1)).astype(BF16)
    lane = lax.broadcasted_iota(jnp.int32, (1, heads_w), 1)
    ones_lane = jnp.where(lane % HEAD_LANES == MLA_V, 1.0, 0.0)
    vh_ref[...] = (kv[:, heads_w:] + ones_lane).astype(BF16)
    ret_ref[...] = _dot(hb, wret_ref[...])
    rkt_ref[...] = _dot_nt(wrkt_ref[...], hb)
    conf_ref[...] = _dot(hb, wconf_ref[...])


def _inproj(x, mod, lw, B, L, latent):
    T = B * L
    TM = TOKEN_TILE
    nt = T // TM
    if L >= TM:
        tiles_per_seq, seqs_per_tile = L // TM, 1
        hy_block = (TM, HY_WIDTH)
        hy_map = lambda i: (i % tiles_per_seq, i // tiles_per_seq)
    else:
        tiles_per_seq, seqs_per_tile = 1, TM // L
        hy_block = (L, seqs_per_tile * HY_WIDTH)
        hy_map = lambda i: (0, i)
    full = lambda a: pl.BlockSpec(a.shape, lambda i: (0,) * a.ndim)
    row = lambda w: pl.BlockSpec((TM, w), lambda i: (i, 0))
    weights = (lw["norm1_g"], lw["w_hy"], lw["w_cq"], lw["w_ck"], lw["w_ret"], lw["w_conf"],
               lw["mla_q_norm"], lw["mla_kv_norm"], lw["mla_w_uq"], lw["mla_w_ukv"], lw["w_ret_kt"])
    hy_shape = jax.ShapeDtypeStruct((L, B * HY_WIDTH), F32)
    qk = RET_HEADS * RET_DK
    heads_w = MLA_HEADS * HEAD_LANES
    ins = [x, mod, *weights]
    in_specs = ([row(D_MODEL), pl.BlockSpec((None, 6, D_MODEL), _mod_row(tiles_per_seq, latent))]
                + [full(w) for w in weights])
    out_specs = ([pl.BlockSpec(hy_block, hy_map)] * 3
                 + [row(heads_w)] * 3 + [row(RET_COLS), row(CONF_COLS), pl.BlockSpec((qk, TM), lambda i: (0, i))])
    out_shape = ([hy_shape] * 3 + [jax.ShapeDtypeStruct((T, heads_w), BF16)] * 3
                 + [jax.ShapeDtypeStruct((T, RET_COLS), F32), jax.ShapeDtypeStruct((T, CONF_COLS), F32),
                    jax.ShapeDtypeStruct((qk, T), F32)])
    if latent:
        ins += list(_rope_lane_tables(L))
        in_specs += [pl.BlockSpec((TM, HEAD_LANES), lambda i: (i % tiles_per_seq, 0))] * 3
    else:
        out_specs += [row(MLA_KV_LORA), row(MLA_ROPE)]
        out_shape += [jax.ShapeDtypeStruct((T, MLA_KV_LORA), F32), jax.ShapeDtypeStruct((T, MLA_ROPE), F32)]
    return pl.pallas_call(
        functools.partial(_inproj_kernel, seqs_per_tile=seqs_per_tile, seq_len=L, latent=latent),
        grid=(nt,),
        in_specs=in_specs,
        out_specs=out_specs,
        out_shape=out_shape,
        compiler_params=_cparams(("arbitrary",), 48),
        name="inproj",
    )(*ins)


def _dft_tables(L):
    N = 2 * L
    k_lo = min(L, 32)
    k_hi = L // k_lo
    t = jnp.arange(L, dtype=jnp.int32)[None, :]

    def cs(k):
        m = (k[:, None] * t) % N
        ang = m.astype(F32) * (2.0 * math.pi / N)
        return jnp.cos(ang), jnp.sin(ang)

    ca, sa = cs(jnp.arange(k_hi, dtype=jnp.int32) * k_lo)
    cb, sb = cs(jnp.arange(k_lo, dtype=jnp.int32))
    cos = (ca[:, None, :] * cb[None, :, :] - sa[:, None, :] * sb[None, :, :]).reshape(L, L)
    sin = (sa[:, None, :] * cb[None, :, :] + ca[:, None, :] * sb[None, :, :]).reshape(L, L)
    return cos.astype(BF16), (-sin).astype(BF16)


def _filter_features(L):
    t = jnp.linspace(0.0, 1.0, L, dtype=F32)[:, None]
    w = 2.0 * math.pi * jnp.arange(L, dtype=F32)[:, None] / L
    f = jnp.linspace(1e-4, HY_BANDS - 1, HY_BANDS, dtype=F32)[None, :]
    z = jnp.concatenate([t, jnp.cos(f * w), -jnp.sin(f * w)], axis=-1)
    z = jnp.pad(z, ((0, 0), (0, 128 - HY_EMB)))
    max_decay = math.log(HY_TARGET) / HY_FAST_DECAY
    min_decay = math.log(HY_TARGET) / HY_SLOW_DECAY
    deltas = jnp.abs(jnp.linspace(min_decay, max_decay, HY_WIDTH, dtype=F32))
    decay = jnp.exp(-t * deltas[None, :])
    return z, decay


def _alternating(rows):
    t = lax.broadcasted_iota(jnp.int32, (rows, 1), 0)
    return (1 - 2 * (t & 1)).astype(F32)


def _filter_kernel(z_ref, dec_ref, w1_ref, b1_ref, w2_ref, b2_ref, w3_ref, cos_ref, sin_ref,
                   kr_ref, ki_ref, kn_ref, h_ref, *, L):
    j = pl.program_id(0)
    W = HY_WIDTH
    N = 2 * L

    @pl.when(j == 0)
    def _():
        h = jnp.sin(_dot_exact(z_ref[...], w1_ref[...]) + b1_ref[...])
        h = jnp.sin(_dot_exact(h, w2_ref[...]) + b2_ref[...])
        h = _dot_exact(h, w3_ref[...]) * jnp.concatenate([dec_ref[...]] * 4, axis=1)
        cs = jnp.sum(jnp.abs(h), axis=0, keepdims=True)
        s0 = cs[:, 0:W] + cs[:, W:2 * W]
        s1 = cs[:, 2 * W:3 * W] + cs[:, 3 * W:4 * W]
        h = h / jnp.concatenate([s0, s0, s1, s1], axis=1)
        row = lax.broadcasted_iota(jnp.int32, h.shape, 0)
        col = lax.broadcasted_iota(jnp.int32, h.shape, 1)
        backward = (col // W) % 2 == 1
        h = jnp.where(backward & (row == 0), 0.0, h)
        h_ref[...] = h.astype(BF16)
        nyq = jnp.sum(h * _alternating(L), axis=0, keepdims=True) * (1.0 / N)
        for o in range(2):
            c = 2 * o * W
            kn_ref[o] = nyq[:, c:c + W] + nyq[:, c + W:c + 2 * W]

    hb = h_ref[...]
    sr = _dot(cos_ref[...], hb)
    si = _dot(sin_ref[...], hb)
    row = lax.broadcasted_iota(jnp.int32, (sr.shape[0], W), 0)
    scale = jnp.where((row == 0) & (j == 0), 1.0 / N, 2.0 / N)
    for o in range(2):
        c = 2 * o * W
        kr_ref[o] = (sr[:, c:c + W] + sr[:, c + W:c + 2 * W]) * scale
        ki_ref[o] = (si[:, c:c + W] - si[:, c + W:c + 2 * W]) * scale


def _filter_spectra(lw, L, tables):
    z, decay = _filter_features(L)
    cos, msin = tables
    Tk = min(DFT_TILE, L)
    full = lambda a: pl.BlockSpec(a.shape, lambda j: (0,) * a.ndim)
    ins = (z, decay, lw["hy_w1"], lw["hy_b1"], lw["hy_w2"], lw["hy_b2"], lw["hy_w3"])
    tile = pl.BlockSpec((Tk, L), lambda j: (j, 0))
    spec = pl.BlockSpec((2, Tk, HY_WIDTH), lambda j: (0, j, 0))
    return pl.pallas_call(
        functools.partial(_filter_kernel, L=L),
        grid=(L // Tk,),
        in_specs=[full(a) for a in ins] + [tile, tile],
        out_specs=[spec, spec, pl.BlockSpec((2, 1, HY_WIDTH), lambda j: (0, 0, 0))],
        out_shape=[jax.ShapeDtypeStruct((2, L, HY_WIDTH), F32)] * 2
        + [jax.ShapeDtypeStruct((2, 1, HY_WIDTH), F32)],
        scratch_shapes=[pltpu.VMEM((L, 4 * HY_WIDTH), BF16)],
        compiler_params=_cparams(("arbitrary",), 48),
        name="hyena_filter",
    )(*ins, cos, msin)


def _hyena_kernel(v_ref, x1_ref, x2_ref, cw_ref, cb_ref, bias_ref, kr_ref, ki_ref, kn_ref,
                  cos_ref, sin_ref, y_ref, cur_ref, curb_ref, yr_ref, yi_ref, gate_ref,
                  *, L, W, T):
    reps = W // HY_WIDTH
    tiled = lambda a: jnp.concatenate([a] * reps, axis=1) if reps > 1 else a
    alt = _alternating(L)

    def short_conv(u_ref, p):
        u = u_ref[...]
        row = lax.broadcasted_iota(jnp.int32, (L, W), 0)
        prev = jnp.where(row == 0, 0.0, pltpu.roll(u, 1, 0))
        nxt = jnp.where(row == L - 1, 0.0, pltpu.roll(u, L - 1, 0))
        w = cw_ref[p]
        return prev * w[0:1] + u * w[1:2] + nxt * w[2:3] + cb_ref[p]

    cur_ref[...] = short_conv(v_ref, 0)
    for o, x_ref in enumerate((x1_ref, x2_ref)):
        cur = cur_ref[...]
        curb_ref[...] = cur.astype(BF16)
        gate_ref[...] = short_conv(x_ref, o + 1)
        nyq = jnp.sum(cur * alt, axis=0, keepdims=True) * tiled(kn_ref[o])
        for f in range(L // T):
            rows = slice(f * T, (f + 1) * T)
            cb = curb_ref[...]
            xr = _dot(cos_ref[rows, :], cb)
            xi = _dot(sin_ref[rows, :], cb)
            kr, ki = tiled(kr_ref[o, rows, :]), tiled(ki_ref[o, rows, :])
            yr_ref[rows, :] = (xr * kr - xi * ki).astype(BF16)
            yi_ref[rows, :] = (xr * ki + xi * kr).astype(BF16)
        for t in range(L // T):
            rows = slice(t * T, (t + 1) * T)
            conv = (_dot(cos_ref[rows, :], yr_ref[...]) + _dot(sin_ref[rows, :], yi_ref[...])
                    + alt[rows] * nyq + cur_ref[rows, :] * bias_ref[o])
            out = gate_ref[rows, :] * conv
            if o == 0:
                cur_ref[rows, :] = out
            else:
                y_ref[rows, :] = out


def _hyena(hv, hx1, hx2, lw, spectra, tables, B, L):
    kr, ki, kn = spectra
    cos, msin = tables
    T = min(HYENA_ROWS, L)
    W = HY_WIDTH * max(1, min(B, HYENA_GROUP_ROWS // L))
    reps = W // HY_WIDTH
    ng = (B * HY_WIDTH) // W
    cw = jnp.tile(lw["hy_conv_w"].reshape(3, 3, 1, HY_WIDTH).transpose(1, 0, 2, 3), (1, 1, reps, 1))
    cw = cw.reshape(3, 3, W)
    cb = jnp.tile(lw["hy_conv_b"].reshape(3, 1, 1, HY_WIDTH), (1, 1, reps, 1)).reshape(3, 1, W)
    bias = jnp.tile(lw["hy_bias"].reshape(2, 1, 1, HY_WIDTH), (1, 1, reps, 1)).reshape(2, 1, W)
    col = pl.BlockSpec((L, W), lambda g: (0, g))
    once = lambda a: pl.BlockSpec(a.shape, lambda g: (0,) * a.ndim, pipeline_mode=pl.Buffered(1))
    consts = (cw, cb, bias, kr, ki, kn, cos, msin)
    return pl.pallas_call(
        functools.partial(_hyena_kernel, L=L, W=W, T=T),
        grid=(ng,),
        in_specs=[col, col, col] + [once(a) for a in consts],
        out_specs=col,
        out_shape=jax.ShapeDtypeStruct((L, B * HY_WIDTH), F32),
        scratch_shapes=[pltpu.VMEM((L, W), F32), pltpu.VMEM((L, W), BF16), pltpu.VMEM((L, W), BF16),
                        pltpu.VMEM((L, W), BF16), pltpu.VMEM((L, W), F32)],
        compiler_params=_cparams(("arbitrary",), 60),
        name="hyena_conv",
    )(hv, hx1, hx2, *consts)


def _mla_kernel(*refs, L, latent):
    if latent:
        q_ref, kh_ref, vh_ref, cckv_ref, ckr_ref, wukv_ref, o_ref, s_ref, p_ref, kctx_ref, vctx_ref = refs
    else:
        q_ref, kh_ref, vh_ref, o_ref, s_ref, p_ref = refs
    TQ = q_ref.shape[0]
    heads_w = MLA_HEADS * HEAD_LANES
    block = lambda h: slice(h * HEAD_LANES, (h + 1) * HEAD_LANES)

    if latent:
        @pl.when(pl.program_id(1) == 0)
        def _():
            kv = _dot(cckv_ref[...].astype(BF16), wukv_ref[...])
            zeros = lambda w: jnp.zeros((PAST_LEN, w), F32)
            kr_block = jnp.concatenate(
                [zeros(MLA_NOPE), ckr_ref[...], zeros(HEAD_LANES - MLA_NOPE - MLA_ROPE)], axis=1)
            kctx_ref[...] = (kv[:, :heads_w] + jnp.concatenate([kr_block] * MLA_HEADS, axis=1)).astype(BF16)
            lane = lax.broadcasted_iota(jnp.int32, (1, heads_w), 1)
            vctx_ref[...] = (kv[:, heads_w:] + jnp.where(lane % HEAD_LANES == MLA_V, 1.0, 0.0)).astype(BF16)

    def scores(h):
        qh = q_ref[:, block(h)]
        s_ref[h, :, 0:L] = _dot_nt(qh, kh_ref[:, block(h)])
        if latent:
            s_ref[h, :, L:L + PAST_LEN] = _dot_nt(qh, kctx_ref[:, block(h)])

    outs = []
    scores(0)
    for h in range(MLA_HEADS):
        if h + 1 < MLA_HEADS:
            scores(h + 1)
        for r in range(TQ // MLA_SOFTMAX_ROWS):
            rows = slice(r * MLA_SOFTMAX_ROWS, (r + 1) * MLA_SOFTMAX_ROWS)
            s = s_ref[h, rows, :]
            p_ref[h, rows, :] = jnp.exp2(s - jnp.max(s, axis=-1, keepdims=True)).astype(BF16)
        pv = _dot(p_ref[h, :, 0:L], vh_ref[:, block(h)])
        if latent:
            pv = pv + _dot(p_ref[h, :, L:L + PAST_LEN], vctx_ref[:, block(h)])
        outs.append(pv[:, :MLA_V] / pv[:, MLA_V:MLA_V + 1])
    o_ref[...] = jnp.concatenate(outs, axis=-1)


def _rope_tables(L):
    rows = L // GRID_W
    row = jnp.repeat(jnp.arange(rows), GRID_W).astype(F32)
    col = jnp.tile(jnp.arange(GRID_W), rows).astype(F32)
    per_axis = MLA_ROPE // 4
    inv = ROPE_BASE ** (-jnp.arange(per_axis, dtype=F32) / per_axis)
    ang = jnp.concatenate([row[:, None] * inv, col[:, None] * inv], axis=-1)
    return jnp.cos(ang), jnp.sin(ang)


def _rope_lane_tables(L):
    cos, sin = _rope_tables(L)
    pad = HEAD_LANES - MLA_NOPE - MLA_ROPE
    one, zero = jnp.ones((L, MLA_NOPE), F32), jnp.zeros((L, MLA_NOPE), F32)
    half0 = jnp.zeros_like(sin)
    cos_t = jnp.concatenate([one, cos, cos, one[:, :pad]], axis=1)
    sa_t = jnp.concatenate([zero, -sin, half0, zero[:, :pad]], axis=1)
    sb_t = jnp.concatenate([zero, half0, sin, zero[:, :pad]], axis=1)
    return cos_t, sa_t, sb_t


def _mla(qh, kh, vh, lw, B, L, ctx):
    latent = ctx is not None
    TQ = min(Q_TILE, L)
    nq = L // TQ
    Lk = L + PAST_LEN if latent else L
    heads_w = MLA_HEADS * HEAD_LANES
    seq = pl.BlockSpec((L, heads_w), lambda b, i: (b, 0))
    ins = [qh, kh, vh]
    specs = [pl.BlockSpec((TQ, heads_w), lambda b, i: (b * nq + i, 0)), seq, seq]
    scratch = [pltpu.VMEM((MLA_HEADS, TQ, Lk), F32), pltpu.VMEM((MLA_HEADS, TQ, Lk), BF16)]
    if latent:
        ins += [ctx[0], ctx[1], lw["mla_w_ukv"]]
        specs += [pl.BlockSpec((None, PAST_LEN, MLA_KV_LORA), lambda b, i: (b, 0, 0)),
                  pl.BlockSpec((None, PAST_LEN, MLA_ROPE), lambda b, i: (b, 0, 0)),
                  pl.BlockSpec(lw["mla_w_ukv"].shape, lambda b, i: (0, 0))]
        scratch += [pltpu.VMEM((PAST_LEN, heads_w), BF16)] * 2
    return pl.pallas_call(
        functools.partial(_mla_kernel, L=L, latent=latent),
        grid=(B, nq),
        in_specs=specs,
        out_specs=pl.BlockSpec((TQ, MLA_HEADS * MLA_V), lambda b, i: (b * nq + i, 0)),
        out_shape=jax.ShapeDtypeStruct((B * L, MLA_HEADS * MLA_V), F32),
        scratch_shapes=scratch,
        compiler_params=_cparams(("arbitrary", "arbitrary"), 48),
        name="mla_attention",
    )(*ins)


def _ret_kernel(*refs, L, has_state, emit_state):
    refs = list(refs)
    u_ref, kt_ref, dl_ref = refs[:3]
    pos = 3
    s0_ref = None
    if has_state:
        s0_ref = refs[pos]
        pos += 1
    y_ref = refs[pos]
    pos += 1
    sout_ref = None
    if emit_state:
        sout_ref = refs[pos]
        pos += 1
    of_ref, kv_ref, sin_ref, dcomb_ref, dtab_ref, hmask_ref, avg_ref = refs[pos:]

    C = RET_BLOCK
    n = L // C
    H, DK, DV = RET_HEADS, RET_DK, RET_DV
    qk = H * DK
    DK_F, DK_B, DQ_F, DQ_B, DC_F, DC_B, MASK = range(7)
    use_cross = has_state or n > 1

    @pl.when(pl.program_id(0) == 0)
    def _():
        x = dl_ref[...]
        log_g = jnp.minimum(x, 0.0) - jnp.log1p(jnp.exp(-jnp.abs(x)))
        gf = [log_g[0:1, h:h + 1] for h in range(H)]
        gb = [log_g[1:2, h:h + 1] for h in range(H)]
        diff = (lax.broadcasted_iota(jnp.int32, (C, C), 0)
                - lax.broadcasted_iota(jnp.int32, (C, C), 1)).astype(F32)
        idx = lax.broadcasted_iota(jnp.int32, (C, 1), 0).astype(F32)
        for h in range(H):
            dcomb_ref[h] = (jnp.where(diff >= 0, jnp.exp(jnp.maximum(diff, 0.0) * gf[h]), 0.0)
                            + jnp.where(diff <= 0, jnp.exp(jnp.maximum(-diff, 0.0) * gb[h]), 0.0))

        def head_cols(lag, g):
            return jnp.concatenate([jnp.broadcast_to(jnp.exp(lag * g[h]), (C, DK)) for h in range(H)], axis=1)

        def head_rows(lag, g):
            return jnp.concatenate(
                [jnp.broadcast_to(jnp.exp(lag * g[h]) * (DK ** -0.5), (DK, C)) for h in range(H)], axis=0)

        tok = lax.broadcasted_iota(jnp.int32, (1, C), 1).astype(F32)
        dtab_ref[DK_F] = head_rows(C - 1.0 - tok, gf)
        dtab_ref[DK_B] = head_rows(tok, gb)
        dtab_ref[DQ_F] = head_cols(idx + 1.0, gf)
        dtab_ref[DQ_B] = head_cols(C - idx, gb)
        row_head = lax.broadcasted_iota(jnp.int32, (qk, qk), 0) // DK
        col_head = lax.broadcasted_iota(jnp.int32, (qk, qk), 1) // DV
        same_head = row_head == col_head
        for slot, g in ((DC_F, gf), (DC_B, gb)):
            dc = jnp.zeros((qk, qk), F32)
            for h in range(H):
                dc = jnp.where(same_head & (row_head == h), jnp.exp(C * g[h]), dc)
            dtab_ref[slot] = dc
        dtab_ref[MASK] = same_head.astype(F32)
        avg_ref[...] = jnp.where(same_head, 1.0 / DV, 0.0).astype(BF16)
        lane_head = lax.broadcasted_iota(jnp.int32, (C, qk), 1) // DK
        for h in range(H):
            hmask_ref[h] = (lane_head == h).astype(F32)

    for c in range(n):
        rows = slice(c * C, (c + 1) * C)
        q = u_ref[rows, 0:qk]
        kb = (u_ref[rows, qk:2 * qk] * (DK ** -0.5)).astype(BF16)
        v = u_ref[rows, 2 * qk:2 * qk + H * DV]
        att = [(_dot_nt((q * hmask_ref[h]).astype(BF16), kb) * dcomb_ref[h]).astype(BF16) for h in range(H)]
        v_heads = jnp.concatenate([(v * hmask_ref[h]).astype(BF16) for h in range(H)], axis=0)
        of_ref[rows, :] = _dot(jnp.concatenate(att, axis=1), v_heads)
        vb = v.astype(BF16)
        kt = kt_ref[:, rows]
        for d, slot in ((0, DK_F), (1, DK_B)):
            kv_ref[d, c] = _dot((kt * dtab_ref[slot]).astype(BF16), vb)

    finals = []
    for d, slot in ((0, DC_F), (1, DC_B)):
        if has_state:
            zero = jnp.zeros((DK, DV), F32)
            S = jnp.concatenate(
                [jnp.concatenate([s0_ref[d, h] if g == h else zero for g in range(H)], axis=1)
                 for h in range(H)], axis=0)
        else:
            S = jnp.zeros((qk, H * DV), F32)
        for c in (range(n) if d == 0 else reversed(range(n))):
            if use_cross:
                sin_ref[d, c] = S.astype(BF16)
            S = S * dtab_ref[slot] + kv_ref[d, c] * dtab_ref[MASK]
        finals.append(S)
    if emit_state:
        for d in range(2):
            for h in range(H):
                sout_ref[d, h] = finals[d][h * DK:(h + 1) * DK, h * DV:(h + 1) * DV]

    def head_mean(a):
        total = None
        for _ in range(3):
            part = a.astype(BF16)
            a = a - part.astype(F32)
            term = _dot(part, avg_ref[...])
            total = term if total is None else total + term
        return total

    for c in range(n):
        rows = slice(c * C, (c + 1) * C)
        tot = of_ref[rows, :]
        if use_cross:
            qb = u_ref[rows, 0:qk].astype(BF16)
            tot = (tot + _dot(qb, sin_ref[0, c]) * dtab_ref[DQ_F]
                   + _dot(qb, sin_ref[1, c]) * dtab_ref[DQ_B])
        xc = tot - head_mean(tot)
        normed = xc * lax.rsqrt(head_mean(xc * xc) + EPS)
        gate = u_ref[rows, 2 * qk + H * DV:2 * qk + 2 * H * DV]
        y_ref[rows, :] = _silu(gate) * normed


def _retention(u_ret, kt_ret, decay_logit, B, L, state, emit_state):
    has_state = state is not None
    st_block = (None, 2, RET_HEADS, RET_DK, RET_DV)
    st_spec = pl.BlockSpec(st_block, lambda b: (b, 0, 0, 0, 0))
    ins = [u_ret, kt_ret, decay_logit]
    specs = [pl.BlockSpec((L, RET_COLS), lambda b: (b, 0)),
             pl.BlockSpec((RET_HEADS * RET_DK, L), lambda b: (0, b)),
             pl.BlockSpec(decay_logit.shape, lambda b: (0, 0))]
    if has_state:
        ins.append(state)
        specs.append(st_spec)
    vd = RET_HEADS * RET_DV
    out_specs = [pl.BlockSpec((L, vd), lambda b: (b, 0))]
    out_shape = [jax.ShapeDtypeStruct((B * L, vd), F32)]
    if emit_state:
        out_specs.append(st_spec)
        out_shape.append(jax.ShapeDtypeStruct((B, 2, RET_HEADS, RET_DK, RET_DV), F32))
    res = pl.pallas_call(
        functools.partial(_ret_kernel, L=L, has_state=has_state, emit_state=emit_state),
        grid=(B,),
        in_specs=specs,
        out_specs=out_specs,
        out_shape=out_shape,
        scratch_shapes=[pltpu.VMEM((L, vd), F32),
                        pltpu.VMEM((2, L // RET_BLOCK, vd, vd), F32),
                        pltpu.VMEM((2, L // RET_BLOCK, vd, vd), BF16),
                        pltpu.VMEM((RET_HEADS, RET_BLOCK, RET_BLOCK), F32),
                        pltpu.VMEM((7, RET_BLOCK, vd), F32),
                        pltpu.VMEM((RET_HEADS, RET_BLOCK, vd), F32),
                        pltpu.VMEM((vd, vd), BF16)],
        compiler_params=_cparams(("arbitrary",), 48),
        name="retention",
    )(*ins)
    return (res[0], res[1]) if emit_state else (res[0], None)


def _conf_kernel(u_ref, w_ref, b_ref, g_ref, be_ref, y_ref, zp_ref, sh_ref, *, L):
    Wc = CONF_WIDTH
    halo = CONF_HALO
    zp_ref[0:halo, :] = jnp.zeros((halo, Wc), F32)
    zp_ref[halo + L:2 * halo + L, :] = jnp.zeros((halo, Wc), F32)
    zp_ref[halo:halo + L, :] = u_ref[:, 0:Wc] * jax.nn.sigmoid(u_ref[:, Wc:2 * Wc])
    first = halo - CONF_KERNEL // 2
    span = sh_ref.shape[1]
    for s in range(SUBLANES):
        sh_ref[s] = zp_ref[first + s:first + s + span, :]
    R = CONF_ROWS
    for c in range(L // R):
        acc = jnp.broadcast_to(b_ref[...], (R, Wc))
        for k in range(CONF_KERNEL):
            a, s = divmod(k, SUBLANES)
            acc = acc + w_ref[k:k + 1, :] * sh_ref[s, c * R + SUBLANES * a:c * R + SUBLANES * a + R, :]
        mu = jnp.mean(acc, axis=-1, keepdims=True)
        xc = acc - mu
        z = xc * lax.rsqrt(jnp.mean(xc * xc, axis=-1, keepdims=True) + EPS) * g_ref[...] + be_ref[...]
        y_ref[c * R:(c + 1) * R, :] = _silu(z)


def _conformer(u_conf, lw, B, L):
    full = lambda a: pl.BlockSpec(a.shape, lambda b: (0,) * a.ndim)
    ws = (lw["conf_dw_w"], lw["conf_dw_b"], lw["conf_ln_g"], lw["conf_ln_b"])
    return pl.pallas_call(
        functools.partial(_conf_kernel, L=L),
        grid=(B,),
        in_specs=[pl.BlockSpec((L, CONF_COLS), lambda b: (b, 0))] + [full(w) for w in ws],
        out_specs=pl.BlockSpec((L, CONF_WIDTH), lambda b: (b, 0)),
        out_shape=jax.ShapeDtypeStruct((B * L, CONF_WIDTH), F32),
        scratch_shapes=[pltpu.VMEM((L + 2 * CONF_HALO, CONF_WIDTH), F32),
                        pltpu.VMEM((SUBLANES, L + SUBLANES * ((CONF_KERNEL - 1) // SUBLANES), CONF_WIDTH), F32)],
        compiler_params=_cparams(("arbitrary",), 48),
        name="conformer",
    )(u_conf, *ws)


def _layer_block(layer):
    def spec(a):
        return pl.BlockSpec((None,) + a.shape[1:], lambda i: (layer,) + (0,) * (a.ndim - 1),
                            pipeline_mode=pl.Buffered(1))
    return spec


def _merge_kernel(x_ref, mod_ref, g1_ref, yhy_ref, ymla_ref, yret_ref, yconf_ref,
                  gw_ref, gb_ref, why_ref, wmla_ref, wret_ref, wconf_ref, wo_ref, o_ref, *, seqs_per_tile):
    x = x_ref[...]
    h = _rms(x, g1_ref[...]) * (1.0 + mod_ref[1:2, :]) + mod_ref[0:1, :]
    hb = h.astype(BF16)
    if seqs_per_tile == 1:
        yhy = yhy_ref[...]
    else:
        yhy = jnp.concatenate(
            [yhy_ref[:, s * HY_WIDTH:(s + 1) * HY_WIDTH] for s in range(seqs_per_tile)], axis=0)
    branches = ((yhy, why_ref), (ymla_ref[...], wmla_ref), (yret_ref[...], wret_ref), (yconf_ref[...], wconf_ref))
    D = D_MODEL
    merged = None
    for i, (y, w_ref) in enumerate(branches):
        gate = jax.nn.sigmoid(_dot(hb, gw_ref[:, i * D:(i + 1) * D]) + gb_ref[:, i * D:(i + 1) * D])
        term = gate * _dot(y.astype(BF16), w_ref[...])
        merged = term if merged is None else merged + term
    o_ref[...] = x + mod_ref[2:3, :] * _dot(merged.astype(BF16), wo_ref[...])


def _merge(x, mod, lw, y_hy, y_mla, y_ret, y_conf, B, L, latent):
    T = B * L
    TM = TOKEN_TILE
    if L >= TM:
        tiles_per_seq, seqs_per_tile = L // TM, 1
        hy_spec = pl.BlockSpec((TM, HY_WIDTH), lambda i: (i % tiles_per_seq, i // tiles_per_seq))
    else:
        tiles_per_seq, seqs_per_tile = 1, TM // L
        hy_spec = pl.BlockSpec((L, seqs_per_tile * HY_WIDTH), lambda i: (0, i))
    stacked = _layer_block(lw["layer"])
    small = lambda a: pl.BlockSpec(a.shape, lambda i: (0,) * a.ndim)
    row = lambda w: pl.BlockSpec((TM, w), lambda i: (i, 0))
    ws = (lw["gate_w"], lw["gate_b"], lw["hy_out"], lw["mla_out"], lw["ret_out"], lw["conf_out"], lw["w_o"])
    return pl.pallas_call(
        functools.partial(_merge_kernel, seqs_per_tile=seqs_per_tile),
        grid=(T // TM,),
        in_specs=[row(D_MODEL), pl.BlockSpec((None, 6, D_MODEL), _mod_row(tiles_per_seq, latent)),
                  small(lw["norm1_g"]), hy_spec, row(256), row(256), row(256)]
        + [small(w) if w.ndim == 2 else stacked(w) for w in ws],
        out_specs=row(D_MODEL),
        out_shape=jax.ShapeDtypeStruct((T, D_MODEL), F32),
        compiler_params=_cparams(("arbitrary",), 56),
        name="merge",
    )(x, mod, lw["norm1_g"], y_hy, y_mla, y_ret, y_conf, *ws)


def _ffn_kernel(x_ref, mod_ref, g2_ref, w1_ref, w2_ref, fg_ref, o_ref, *, final):
    x = x_ref[...]
    h2 = (_rms(x, g2_ref[...]) * (1.0 + mod_ref[4:5, :]) + mod_ref[3:4, :]).astype(BF16)
    acc = None
    for c0 in range(0, D_FF, FFN_CHUNK):
        c1 = min(c0 + FFN_CHUNK, D_FF)
        a = _dot(h2, w1_ref[:, c0:c1])
        b = _dot(h2, w1_ref[:, D_FF + c0:D_FF + c1])
        part = _dot((_silu(a) * b).astype(BF16), w2_ref[c0:c1, :])
        acc = part if acc is None else acc + part
    out = x + mod_ref[5:6, :] * acc
    if final:
        out = _rms(out, fg_ref[...])
    o_ref[...] = out


def _ffn(x, mod, lw, final_g, B, L, latent, final):
    T = B * L
    TM = TOKEN_TILE
    tiles_per_seq = max(L // TM, 1)
    stacked = _layer_block(lw["layer"])
    small = lambda a: pl.BlockSpec(a.shape, lambda i: (0,) * a.ndim)
    row = pl.BlockSpec((TM, D_MODEL), lambda i: (i, 0))
    ws = (lw["norm2_g"], lw["ffn_w1"], lw["ffn_w2"])
    return pl.pallas_call(
        functools.partial(_ffn_kernel, final=final),
        grid=(T // TM,),
        in_specs=[row, pl.BlockSpec((None, 6, D_MODEL), _mod_row(tiles_per_seq, latent))]
        + [small(w) if w.ndim == 2 else stacked(w) for w in ws] + [small(final_g)],
        out_specs=row,
        out_shape=jax.ShapeDtypeStruct((T, D_MODEL), F32),
        compiler_params=_cparams(("arbitrary",), 56),
        name="ffn",
    )(x, mod, *ws, final_g)


def _trunk_layer(x, mod, lw, final_g, tables, B, L, ctx, state, final):
    latent = ctx is not None
    hv, hx1, hx2, qh, kh, vh, u_ret, u_conf, kt_ret, *cache = _inproj(x, mod, lw, B, L, latent)
    ckv, kr = cache if cache else (None, None)
    spectra = _filter_spectra(lw, L, tables)
    y_hy = _hyena(hv, hx1, hx2, lw, spectra, tables, B, L)
    y_mla = _mla(qh, kh, vh, lw, B, L, ctx)
    y_ret, S = _retention(u_ret, kt_ret, lw["ret_decay"], B, L, state, emit_state=not latent)
    y_conf = _conformer(u_conf, lw, B, L)
    x = _merge(x, mod, lw, y_hy, y_mla, y_ret, y_conf, B, L, latent)
    x = _ffn(x, mod, lw, final_g, B, L, latent, final)
    return x, ckv, kr, S


_STACKED = ("gate_w", "hy_out", "mla_out", "ret_out", "conf_out", "w_o", "ffn_w1", "ffn_w2")


def _layer_weights(l, w_in, p, stacks):
    hy, mla = HY_COLS, HY_COLS + MLA_COLS
    ret = mla + RET_COLS
    w = w_in[l]
    lanes_after = HEAD_LANES - MLA_NOPE - MLA_ROPE
    ck = jnp.concatenate([w[:, hy + MLA_Q_LORA:hy + MLA_Q_LORA + MLA_KV_LORA],
                          jnp.pad(w[:, mla - MLA_ROPE:mla], ((0, 0), (MLA_NOPE, lanes_after)))], axis=1)
    dq = MLA_NOPE + MLA_ROPE
    w_uq = jnp.pad(p["mla_w_uq"][l].reshape(MLA_Q_LORA, MLA_HEADS, dq),
                   ((0, 0), (0, 0), (0, HEAD_LANES - dq))).reshape(MLA_Q_LORA, MLA_HEADS * HEAD_LANES)
    w_ukv = p["mla_w_ukv"][l].reshape(MLA_KV_LORA, MLA_HEADS, MLA_NOPE + MLA_V)
    head_pad = lambda a: jnp.pad(a, ((0, 0), (0, 0), (0, HEAD_LANES - a.shape[2]))).reshape(MLA_KV_LORA, -1)
    w_ukv = jnp.concatenate([head_pad(w_ukv[:, :, :MLA_NOPE]), head_pad(w_ukv[:, :, MLA_NOPE:])], axis=1)
    row = lambda a: a[l].reshape(1, -1)
    return {
        "norm1_g": row(p["norm1_g"]), "norm2_g": row(p["norm2_g"]),
        "w_hy": w[:, :hy].astype(BF16), "w_cq": w[:, hy:hy + MLA_Q_LORA].astype(BF16),
        "w_ck": ck.astype(BF16), "w_ret": w[:, mla:ret].astype(BF16), "w_conf": w[:, ret:].astype(BF16),
        "w_ret_kt": w[:, mla + RET_HEADS * RET_DK:mla + 2 * RET_HEADS * RET_DK].T.astype(BF16),
        "mla_q_norm": row(p["mla_q_norm"]), "mla_kv_norm": row(p["mla_kv_norm"]),
        "mla_w_uq": w_uq.astype(BF16), "mla_w_ukv": w_ukv.astype(BF16),
        "hy_conv_w": p["hy_conv_w"][l], "hy_conv_b": p["hy_conv_b"][l],
        "hy_w1": jnp.pad(p["hy_w1"][l], ((0, 128 - HY_EMB), (0, 0))), "hy_b1": row(p["hy_b1"]),
        "hy_w2": p["hy_w2"][l], "hy_b2": row(p["hy_b2"]), "hy_w3": p["hy_w3"][l],
        "hy_bias": p["hy_bias"][l],
        "ret_decay": p["ret_decay"][l],
        "conf_dw_w": p["conf_dw_w"][l], "conf_dw_b": row(p["conf_dw_b"]),
        "conf_ln_g": row(p["conf_ln_g"]), "conf_ln_b": row(p["conf_ln_b"]),
        "gate_b": row(p["gate_b"]),
        "layer": l,
        **{name: stacks[name] for name in _STACKED},
    }


def kernel(x_prompt, x_sample, cache_mla_ckv, cache_mla_krope, state_ret, c, c_ctx, ada_w, ada_b, norm1_g, w_in, hy_conv_w, hy_conv_b, hy_w1, hy_b1, hy_w2, hy_b2, hy_w3, hy_bias, hy_out, mla_q_norm, mla_w_uq, mla_kv_norm, mla_w_ukv, mla_out, ret_decay, ret_out, conf_dw_w, conf_dw_b, conf_ln_g, conf_ln_b, conf_out, gate_w, gate_b, w_o, norm2_g, ffn_w1, ffn_w2, final_norm_g):
    p = dict(norm1_g=norm1_g, hy_conv_w=hy_conv_w, hy_conv_b=hy_conv_b, hy_w1=hy_w1, hy_b1=hy_b1,
             hy_w2=hy_w2, hy_b2=hy_b2, hy_w3=hy_w3, hy_bias=hy_bias, hy_out=hy_out,
             mla_q_norm=mla_q_norm, mla_w_uq=mla_w_uq, mla_kv_norm=mla_kv_norm, mla_w_ukv=mla_w_ukv,
             mla_out=mla_out, ret_decay=ret_decay, ret_out=ret_out, conf_dw_w=conf_dw_w,
             conf_dw_b=conf_dw_b, conf_ln_g=conf_ln_g, conf_ln_b=conf_ln_b, conf_out=conf_out,
             gate_w=gate_w, gate_b=gate_b, w_o=w_o, norm2_g=norm2_g, ffn_w1=ffn_w1, ffn_w2=ffn_w2)
    Bp, Lp, D = x_prompt.shape
    Bs, Ls, _ = x_sample.shape
    depth = w_in.shape[0]

    cond = jnp.concatenate([c_ctx[None, :], c, jnp.zeros((MOD_ROWS - 1 - Bs, D), F32)], axis=0)
    mod = _modulation(cond, ada_w, ada_b)
    tables_p = _dft_tables(Lp)
    tables_s = _dft_tables(Ls)
    final_g = final_norm_g.reshape(1, D)

    xp = x_prompt.reshape(Bp * Lp, D)
    xs = x_sample.reshape(Bs * Ls, D)
    ckvs, kropes, rets = [], [], []
    stacks = {name: p[name].astype(BF16) for name in _STACKED}
    for l in range(depth):
        lw = _layer_weights(l, w_in, p, stacks)
        final = l == depth - 1
        xp, ckv, kr, S = _trunk_layer(xp, mod[l], lw, final_g, tables_p, Bp, Lp, None, None, final)
        ckvs.append(ckv.reshape(Bp, Lp, MLA_KV_LORA))
        kropes.append(kr.reshape(Bp, Lp, MLA_ROPE))
        rets.append(S)
        xs, _, _, _ = _trunk_layer(xs, mod[l], lw, final_g, tables_s, Bs, Ls,
                                   (cache_mla_ckv[:, l], cache_mla_krope[:, l]), state_ret[:, l], final)
    return (xp.reshape(Bp, Lp, D), xs.reshape(Bs, Ls, D),
            jnp.stack(ckvs, axis=1), jnp.stack(kropes, axis=1), jnp.stack(rets, axis=1))
```

```python
import functools
import math

import jax
import jax.numpy as jnp
from jax import lax
from jax.experimental import pallas as pl
from jax.experimental.pallas import tpu as pltpu

F32 = jnp.float32
BF16 = jnp.bfloat16

D_MODEL = 1024
DEPTH = 2
PAST_LEN = 256
EPS = 1e-6
GRID_W = 64

HY_WIDTH = 256
HY_EMB = 33
HY_BANDS = (HY_EMB - 1) // 2
HY_FFN = 64
HY_FAST_DECAY = 0.3
HY_SLOW_DECAY = 1.5
HY_TARGET = 1e-2

MLA_HEADS = 4
MLA_Q_LORA = 256
MLA_KV_LORA = 128
MLA_NOPE = 64
MLA_ROPE = 32
MLA_V = 64
ROPE_BASE = 10000.0

RET_HEADS = 4
RET_DK = 64
RET_DV = 64
RET_BLOCK = 256

CONF_WIDTH = 256
CONF_KERNEL = 31

D_FF = ((8 * D_MODEL // 3 + 255) // 256) * 256
N_BRANCH = 4

HY_COLS = 3 * HY_WIDTH
MLA_COLS = MLA_Q_LORA + MLA_KV_LORA + MLA_ROPE
RET_COLS = 2 * RET_HEADS * RET_DK + 2 * RET_HEADS * RET_DV
CONF_COLS = 2 * CONF_WIDTH

VMEM_BYTES_V7X = 64 * 1024 * 1024
SUBLANES = 8
TOKEN_TILE = 512
WIDE_TOKEN_TILE = 1024
Q_TILE = 256
MLA_SOFTMAX_ROWS = 16
DFT_TILE = 512
HYENA_GROUP_ROWS = 1024
HYENA_ROWS = 1024
CONF_ROWS = 128
CONF_HALO = 16
MXU_DIM_V7X = 256
FFN_CHUNK = 4 * MXU_DIM_V7X
MOD_ROWS = 8


def _cparams(sem, vmem_mb):
    return pltpu.CompilerParams(dimension_semantics=sem, vmem_limit_bytes=vmem_mb * 1024 * 1024)


def _dot(a, b):
    return jnp.dot(a, b, preferred_element_type=F32)


def _dot_nt(a, b):
    return lax.dot_general(a, b, (((1,), (1,)), ((), ())), preferred_element_type=F32)


def _dot_exact(a, b):
    return jnp.dot(a, b, preferred_element_type=F32, precision=lax.Precision.HIGHEST)


def _dot_split(a, b):
    a_hi, b_hi = a.astype(BF16), b.astype(BF16)
    a_lo = (a - a_hi.astype(F32)).astype(BF16)
    b_lo = (b - b_hi.astype(F32)).astype(BF16)
    return _dot(a_hi, b_hi) + (_dot(a_lo, b_hi) + _dot(a_hi, b_lo))


def _rms(x, g):
    return x * lax.rsqrt(jnp.mean(x * x, axis=-1, keepdims=True) + EPS) * g


def _silu(x):
    return x * jax.nn.sigmoid(x)


def _mod_kernel(c_ref, w_ref, b_ref, o_ref):
    s = _silu(c_ref[...]).astype(BF16)
    o_ref[...] = _dot(s, w_ref[...].astype(BF16)) + b_ref[...]


def _modulation(cond, ada_w, ada_b):
    depth, d, cols = ada_w.shape
    blk = 1024
    out = pl.pallas_call(
        _mod_kernel,
        grid=(depth, cols // blk),
        in_specs=[
            pl.BlockSpec((MOD_ROWS, d), lambda l, j: (0, 0)),
            pl.BlockSpec((None, d, blk), lambda l, j: (l, 0, j)),
            pl.BlockSpec((None, 1, blk), lambda l, j: (l, 0, j)),
        ],
        out_specs=pl.BlockSpec((None, MOD_ROWS, blk), lambda l, j: (l, 0, j)),
        out_shape=jax.ShapeDtypeStruct((depth, MOD_ROWS, cols), F32),
        compiler_params=_cparams(("arbitrary", "arbitrary"), 32),
        name="modulation",
    )(cond, ada_w, ada_b.reshape(depth, 1, cols))
    return out.reshape(depth, MOD_ROWS, 6, d)


def _per_layer(a, layer, single_buffer=False):
    mode = dict(pipeline_mode=pl.Buffered(1)) if single_buffer else {}
    return pl.BlockSpec((None,) + a.shape[1:], lambda *_: (layer,) + (0,) * (a.ndim - 1), **mode)


def _mod_spec(layer, tiles_per_seq, latent):
    group = (lambda i: 1 + i // tiles_per_seq) if latent else (lambda i: 0)
    return pl.BlockSpec((None, None, 6, D_MODEL), lambda i: (layer, group(i), 0, 0))


HEAD_LANES = 128


def _rope_lanes(x, cos, sa, sb):
    reps = x.shape[1] // HEAD_LANES
    wide = lambda t: jnp.concatenate([t] * reps, axis=1) if reps > 1 else t
    half = MLA_ROPE // 2
    return (x * wide(cos) + pltpu.roll(x, x.shape[1] - half, 1) * wide(sa)
            + pltpu.roll(x, half, 1) * wide(sb))


def _inproj_kernel(*refs, seqs_per_tile, seq_len, latent):
    (x_ref, mod_ref, g1_ref, why_ref, wcq_ref, wck_ref, wret_ref, wconf_ref,
     qn_ref, kvn_ref, wuq_ref, wukv_ref) = refs[:12]
    refs = refs[12:]
    if latent:
        cos_ref, sa_ref, sb_ref = refs[:3]
        refs = refs[3:]
    hv_ref, hx1_ref, hx2_ref, q_ref, kh_ref, vh_ref, ret_ref, conf_ref, rkt_ref = refs[:9]
    if not latent:
        ckv_ref, kr_ref = refs[9:]
    x = x_ref[...]
    h = _rms(x, g1_ref[...]) * (1.0 + mod_ref[1:2, :]) + mod_ref[0:1, :]
    hb = h.astype(BF16)

    u = _dot(hb, why_ref[...])
    for p, o_ref in enumerate((hv_ref, hx1_ref, hx2_ref)):
        part = u[:, p * HY_WIDTH:(p + 1) * HY_WIDTH]
        if seqs_per_tile == 1:
            o_ref[...] = part
        else:
            for s in range(seqs_per_tile):
                o_ref[:, s * HY_WIDTH:(s + 1) * HY_WIDTH] = part[s * seq_len:(s + 1) * seq_len]

    heads_w = MLA_HEADS * HEAD_LANES
    cq = _dot(hb, wcq_ref[...])
    q = _dot(_rms(cq, qn_ref[...]).astype(BF16), wuq_ref[...])
    ck = _dot(hb, wck_ref[...])
    ckv = _rms(ck[:, :MLA_KV_LORA], kvn_ref[...])
    kr_block = ck[:, MLA_KV_LORA:MLA_KV_LORA + HEAD_LANES]
    if latent:
        q = _rope_lanes(q, cos_ref[...], sa_ref[...], sb_ref[...])
        kr_block = _rope_lanes(kr_block, cos_ref[...], sa_ref[...], sb_ref[...])
    else:
        ckv_ref[...] = ckv
        kr_ref[...] = kr_block[:, MLA_NOPE:MLA_NOPE + MLA_ROPE]
    q_ref[...] = (q * ((MLA_NOPE + MLA_ROPE) ** -0.5 * math.log2(math.e))).astype(BF16)
    kv = _dot(ckv.astype(BF16), wukv_ref[...])
    kh_ref[...] = (kv[:, :heads_w] + jnp.concatenate([kr_block] * MLA_HEADS, axis=1)).astype(BF16)
    lane = lax.broadcasted_iota(jnp.int32, (1, heads_w), 1)
    ones_lane = jnp.where(lane % HEAD_LANES == MLA_V, 1.0, 0.0)
    vh_ref[...] = (kv[:, heads_w:] + ones_lane).astype(BF16)
    ret = _dot(hb, wret_ref[...])
    ret_ref[...] = ret
    qk = RET_HEADS * RET_DK
    rkt_ref[...] = ret[:, qk:2 * qk].T
    conf_ref[...] = _dot(hb, wconf_ref[...])


def _inproj(x, mod, lw, B, L, latent):
    T = B * L
    TM = TOKEN_TILE
    nt = T // TM
    if L >= TM:
        tiles_per_seq, seqs_per_tile = L // TM, 1
        hy_block = (TM, HY_WIDTH)
        hy_map = lambda i: (i % tiles_per_seq, i // tiles_per_seq)
    else:
        tiles_per_seq, seqs_per_tile = 1, TM // L
        hy_block = (L, seqs_per_tile * HY_WIDTH)
        hy_map = lambda i: (0, i)
    row = lambda w: pl.BlockSpec((TM, w), lambda i: (i, 0))
    weights = (lw["norm1_g"], lw["w_hy"], lw["w_cq"], lw["w_ck"], lw["w_ret"], lw["w_conf"],
               lw["mla_q_norm"], lw["mla_kv_norm"], lw["mla_w_uq"], lw["mla_w_ukv"])
    hy_shape = jax.ShapeDtypeStruct((L, B * HY_WIDTH), F32)
    qk = RET_HEADS * RET_DK
    heads_w = MLA_HEADS * HEAD_LANES
    ins = [x, mod, *weights]
    in_specs = ([row(D_MODEL), _mod_spec(lw["layer"], tiles_per_seq, latent)]
                + [_per_layer(w, lw["layer"]) for w in weights])
    out_specs = ([pl.BlockSpec(hy_block, hy_map)] * 3
                 + [row(heads_w)] * 3 + [row(RET_COLS), row(CONF_COLS), pl.BlockSpec((qk, TM), lambda i: (0, i))])
    out_shape = ([hy_shape] * 3 + [jax.ShapeDtypeStruct((T, heads_w), BF16)] * 3
                 + [jax.ShapeDtypeStruct((T, RET_COLS), F32), jax.ShapeDtypeStruct((T, CONF_COLS), F32),
                    jax.ShapeDtypeStruct((qk, T), F32)])
    if latent:
        ins += list(_rope_lane_tables(L))
        in_specs += [pl.BlockSpec((TM, HEAD_LANES), lambda i: (i % tiles_per_seq, 0))] * 3
    else:
        out_specs += [row(MLA_KV_LORA), row(MLA_ROPE)]
        out_shape += [jax.ShapeDtypeStruct((T, MLA_KV_LORA), F32), jax.ShapeDtypeStruct((T, MLA_ROPE), F32)]
    return pl.pallas_call(
        functools.partial(_inproj_kernel, seqs_per_tile=seqs_per_tile, seq_len=L, latent=latent),
        grid=(nt,),
        in_specs=in_specs,
        out_specs=out_specs,
        out_shape=out_shape,
        compiler_params=_cparams(("arbitrary",), 48),
        name="inproj",
    )(*ins)


def _dft_tables(L):
    N = 2 * L
    k_lo = min(L, 32)
    k_hi = L // k_lo
    t = jnp.arange(L, dtype=jnp.int32)[None, :]

    def cs(k):
        m = (k[:, None] * t) % N
        ang = m.astype(F32) * (2.0 * math.pi / N)
        return jnp.cos(ang), jnp.sin(ang)

    ca, sa = cs(jnp.arange(k_hi, dtype=jnp.int32) * k_lo)
    cb, sb = cs(jnp.arange(k_lo, dtype=jnp.int32))
    cos = (ca[:, None, :] * cb[None, :, :] - sa[:, None, :] * sb[None, :, :]).reshape(L, L)
    sin = (sa[:, None, :] * cb[None, :, :] + ca[:, None, :] * sb[None, :, :]).reshape(L, L)
    return cos.astype(BF16), (-sin).astype(BF16)


def _filter_features(L):
    t = jnp.linspace(0.0, 1.0, L, dtype=F32)[:, None]
    w = 2.0 * math.pi * jnp.arange(L, dtype=F32)[:, None] / L
    f = jnp.linspace(1e-4, HY_BANDS - 1, HY_BANDS, dtype=F32)[None, :]
    z = jnp.concatenate([t, jnp.cos(f * w), -jnp.sin(f * w)], axis=-1)
    z = jnp.pad(z, ((0, 0), (0, 128 - HY_EMB)))
    max_decay = math.log(HY_TARGET) / HY_FAST_DECAY
    min_decay = math.log(HY_TARGET) / HY_SLOW_DECAY
    deltas = jnp.abs(jnp.linspace(min_decay, max_decay, HY_WIDTH, dtype=F32))
    decay = jnp.exp(-t * deltas[None, :])
    return z, decay


def _alternating(rows):
    t = lax.broadcasted_iota(jnp.int32, (rows, 1), 0)
    return (1 - 2 * (t & 1)).astype(F32)


def _filter_kernel(z_ref, dec_ref, w1_ref, b1_ref, w2_ref, b2_ref, w3_ref, cos_ref, sin_ref,
                   kr_ref, ki_ref, kn_ref, h_ref, *, L):
    j = pl.program_id(0)
    W = HY_WIDTH
    N = 2 * L

    @pl.when(j == 0)
    def _():
        h = jnp.sin(_dot_exact(z_ref[...], w1_ref[...]) + b1_ref[...])
        h = jnp.sin(_dot_exact(h, w2_ref[...]) + b2_ref[...])
        h = _dot_split(h, w3_ref[...]) * jnp.concatenate([dec_ref[...]] * 4, axis=1)
        cs = jnp.sum(jnp.abs(h), axis=0, keepdims=True)
        s0 = cs[:, 0:W] + cs[:, W:2 * W]
        s1 = cs[:, 2 * W:3 * W] + cs[:, 3 * W:4 * W]
        h = h / jnp.concatenate([s0, s0, s1, s1], axis=1)
        row = lax.broadcasted_iota(jnp.int32, h.shape, 0)
        col = lax.broadcasted_iota(jnp.int32, h.shape, 1)
        backward = (col // W) % 2 == 1
        h = jnp.where(backward & (row == 0), 0.0, h)
        h_ref[...] = h.astype(BF16)
        nyq = jnp.sum(h * _alternating(L), axis=0, keepdims=True) * (1.0 / N)
        for o in range(2):
            c = 2 * o * W
            kn_ref[o] = nyq[:, c:c + W] + nyq[:, c + W:c + 2 * W]

    hb = h_ref[...]
    sr = _dot(cos_ref[...], hb)
    si = _dot(sin_ref[...], hb)
    row = lax.broadcasted_iota(jnp.int32, (sr.shape[0], W), 0)
    scale = jnp.where((row == 0) & (j == 0), 1.0 / N, 2.0 / N)
    for o in range(2):
        c = 2 * o * W
        kr_ref[o] = (sr[:, c:c + W] + sr[:, c + W:c + 2 * W]) * scale
        ki_ref[o] = (si[:, c:c + W] - si[:, c + W:c + 2 * W]) * scale


def _filter_spectra(lw, L, tables):
    z, decay = _filter_features(L)
    cos, msin = tables
    Tk = min(DFT_TILE, L)
    full = lambda a: pl.BlockSpec(a.shape, lambda j: (0,) * a.ndim)
    mlp = (lw["hy_w1"], lw["hy_b1"], lw["hy_w2"], lw["hy_b2"], lw["hy_w3"])
    ins = (z, decay, *mlp)
    tile = pl.BlockSpec((Tk, L), lambda j: (j, 0))
    spec = pl.BlockSpec((2, Tk, HY_WIDTH), lambda j: (0, j, 0))
    return pl.pallas_call(
        functools.partial(_filter_kernel, L=L),
        grid=(L // Tk,),
        in_specs=[full(z), full(decay)] + [_per_layer(a, lw["layer"]) for a in mlp] + [tile, tile],
        out_specs=[spec, spec, pl.BlockSpec((2, 1, HY_WIDTH), lambda j: (0, 0, 0))],
        out_shape=[jax.ShapeDtypeStruct((2, L, HY_WIDTH), F32)] * 2
        + [jax.ShapeDtypeStruct((2, 1, HY_WIDTH), F32)],
        scratch_shapes=[pltpu.VMEM((L, 4 * HY_WIDTH), BF16)],
        compiler_params=_cparams(("arbitrary",), 48),
        name="hyena_filter",
    )(*ins, cos, msin)


def _hyena_kernel(v_ref, x1_ref, x2_ref, cw_ref, cb_ref, bias_ref, kr_ref, ki_ref, kn_ref,
                  cos_ref, sin_ref, y_ref, cur_ref, curb_ref, yr_ref, yi_ref, gate_ref,
                  *, L, W, T):
    reps = W // HY_WIDTH
    tiled = lambda a: jnp.concatenate([a] * reps, axis=1) if reps > 1 else a
    alt = _alternating(L)

    def short_conv(u_ref, p):
        u = u_ref[...]
        row = lax.broadcasted_iota(jnp.int32, (L, W), 0)
        prev = jnp.where(row == 0, 0.0, pltpu.roll(u, 1, 0))
        nxt = jnp.where(row == L - 1, 0.0, pltpu.roll(u, L - 1, 0))
        cols = slice(p * HY_WIDTH, (p + 1) * HY_WIDTH)
        w = [tiled(cw_ref[k:k + 1, cols]) for k in range(3)]
        return prev * w[0] + u * w[1] + nxt * w[2] + tiled(cb_ref[:, cols])

    cur_ref[...] = short_conv(v_ref, 0)
    for o, x_ref in enumerate((x1_ref, x2_ref)):
        cur = cur_ref[...]
        curb_ref[...] = cur.astype(BF16)
        gate_ref[...] = short_conv(x_ref, o + 1)
        nyq = jnp.sum(cur * alt, axis=0, keepdims=True) * tiled(kn_ref[o])
        for f in range(L // T):
            rows = slice(f * T, (f + 1) * T)
            cb = curb_ref[...]
            xr = _dot(cos_ref[rows, :], cb)
            xi = _dot(sin_ref[rows, :], cb)
            kr, ki = tiled(kr_ref[o, rows, :]), tiled(ki_ref[o, rows, :])
            yr_ref[rows, :] = (xr * kr - xi * ki).astype(BF16)
            yi_ref[rows, :] = (xr * ki + xi * kr).astype(BF16)
        for t in range(L // T):
            rows = slice(t * T, (t + 1) * T)
            conv = (_dot(cos_ref[rows, :], yr_ref[...]) + _dot(sin_ref[rows, :], yi_ref[...])
                    + alt[rows] * nyq + cur_ref[rows, :] * tiled(bias_ref[o:o + 1, :]))
            out = gate_ref[rows, :] * conv
            if o == 0:
                cur_ref[rows, :] = out
            else:
                y_ref[rows, :] = out


def _hyena(hv, hx1, hx2, lw, spectra, tables, B, L):
    kr, ki, kn = spectra
    cos, msin = tables
    T = min(HYENA_ROWS, L)
    W = HY_WIDTH * max(1, min(B, HYENA_GROUP_ROWS // L))
    ng = (B * HY_WIDTH) // W
    col = pl.BlockSpec((L, W), lambda g: (0, g))
    once = lambda a: pl.BlockSpec(a.shape, lambda g: (0,) * a.ndim, pipeline_mode=pl.Buffered(1))
    params = (lw["hy_conv_w"], lw["hy_conv_b"], lw["hy_bias"])
    consts = (kr, ki, kn, cos, msin)
    return pl.pallas_call(
        functools.partial(_hyena_kernel, L=L, W=W, T=T),
        grid=(ng,),
        in_specs=[col, col, col] + [_per_layer(a, lw["layer"]) for a in params] + [once(a) for a in consts],
        out_specs=col,
        out_shape=jax.ShapeDtypeStruct((L, B * HY_WIDTH), F32),
        scratch_shapes=[pltpu.VMEM((L, W), F32), pltpu.VMEM((L, W), BF16), pltpu.VMEM((L, W), BF16),
                        pltpu.VMEM((L, W), BF16), pltpu.VMEM((L, W), F32)],
        compiler_params=_cparams(("arbitrary",), 60),
        name="hyena_conv",
    )(hv, hx1, hx2, *params, *consts)


def _mla_kernel(*refs, L, latent):
    if latent:
        q_ref, kh_ref, vh_ref, cckv_ref, ckr_ref, wukv_ref, o_ref, s_ref, p_ref, kctx_ref, vctx_ref = refs
    else:
        q_ref, kh_ref, vh_ref, o_ref, s_ref, p_ref = refs
    TQ = q_ref.shape[0]
    heads_w = MLA_HEADS * HEAD_LANES
    block = lambda h: slice(h * HEAD_LANES, (h + 1) * HEAD_LANES)

    if latent:
        @pl.when(pl.program_id(1) == 0)
        def _():
            kv = _dot(cckv_ref[...].astype(BF16), wukv_ref[...])
            zeros = lambda w: jnp.zeros((PAST_LEN, w), F32)
            kr_block = jnp.concatenate(
                [zeros(MLA_NOPE), ckr_ref[...], zeros(HEAD_LANES - MLA_NOPE - MLA_ROPE)], axis=1)
            kctx_ref[...] = (kv[:, :heads_w] + jnp.concatenate([kr_block] * MLA_HEADS, axis=1)).astype(BF16)
            lane = lax.broadcasted_iota(jnp.int32, (1, heads_w), 1)
            vctx_ref[...] = (kv[:, heads_w:] + jnp.where(lane % HEAD_LANES == MLA_V, 1.0, 0.0)).astype(BF16)

    def scores(h):
        qh = q_ref[:, block(h)]
        s_ref[h, :, 0:L] = _dot_nt(qh, kh_ref[:, block(h)])
        if latent:
            s_ref[h, :, L:L + PAST_LEN] = _dot_nt(qh, kctx_ref[:, block(h)])

    outs = []
    scores(0)
    for h in range(MLA_HEADS):
        if h + 1 < MLA_HEADS:
            scores(h + 1)
        for r in range(TQ // MLA_SOFTMAX_ROWS):
            rows = slice(r * MLA_SOFTMAX_ROWS, (r + 1) * MLA_SOFTMAX_ROWS)
            s = s_ref[h, rows, :]
            p_ref[h, rows, :] = jnp.exp2(s - jnp.max(s, axis=-1, keepdims=True)).astype(BF16)
        pv = _dot(p_ref[h, :, 0:L], vh_ref[:, block(h)])
        if latent:
            pv = pv + _dot(p_ref[h, :, L:L + PAST_LEN], vctx_ref[:, block(h)])
        outs.append(pv[:, :MLA_V] / pv[:, MLA_V:MLA_V + 1])
    o_ref[...] = jnp.concatenate(outs, axis=-1)


def _rope_tables(L):
    rows = L // GRID_W
    row = jnp.repeat(jnp.arange(rows), GRID_W).astype(F32)
    col = jnp.tile(jnp.arange(GRID_W), rows).astype(F32)
    per_axis = MLA_ROPE // 4
    inv = ROPE_BASE ** (-jnp.arange(per_axis, dtype=F32) / per_axis)
    ang = jnp.concatenate([row[:, None] * inv, col[:, None] * inv], axis=-1)
    return jnp.cos(ang), jnp.sin(ang)


def _rope_lane_tables(L):
    cos, sin = _rope_tables(L)
    pad = HEAD_LANES - MLA_NOPE - MLA_ROPE
    one, zero = jnp.ones((L, MLA_NOPE), F32), jnp.zeros((L, MLA_NOPE), F32)
    half0 = jnp.zeros_like(sin)
    cos_t = jnp.concatenate([one, cos, cos, one[:, :pad]], axis=1)
    sa_t = jnp.concatenate([zero, -sin, half0, zero[:, :pad]], axis=1)
    sb_t = jnp.concatenate([zero, half0, sin, zero[:, :pad]], axis=1)
    return cos_t, sa_t, sb_t


def _mla(qh, kh, vh, lw, B, L, ctx):
    latent = ctx is not None
    TQ = min(Q_TILE, L)
    nq = L // TQ
    Lk = L + PAST_LEN if latent else L
    heads_w = MLA_HEADS * HEAD_LANES
    seq = pl.BlockSpec((L, heads_w), lambda b, i: (b, 0))
    ins = [qh, kh, vh]
    specs = [pl.BlockSpec((TQ, heads_w), lambda b, i: (b * nq + i, 0)), seq, seq]
    scratch = [pltpu.VMEM((MLA_HEADS, TQ, Lk), F32), pltpu.VMEM((MLA_HEADS, TQ, Lk), BF16)]
    if latent:
        ins += [ctx[0], ctx[1], lw["mla_w_ukv"]]
        layer = lw["layer"]
        specs += [pl.BlockSpec((None, None, PAST_LEN, MLA_KV_LORA), lambda b, i: (b, layer, 0, 0)),
                  pl.BlockSpec((None, None, PAST_LEN, MLA_ROPE), lambda b, i: (b, layer, 0, 0)),
                  _per_layer(lw["mla_w_ukv"], lw["layer"])]
        scratch += [pltpu.VMEM((PAST_LEN, heads_w), BF16)] * 2
    return pl.pallas_call(
        functools.partial(_mla_kernel, L=L, latent=latent),
        grid=(B, nq),
        in_specs=specs,
        out_specs=pl.BlockSpec((TQ, MLA_HEADS * MLA_V), lambda b, i: (b * nq + i, 0)),
        out_shape=jax.ShapeDtypeStruct((B * L, MLA_HEADS * MLA_V), F32),
        scratch_shapes=scratch,
        compiler_params=_cparams(("arbitrary", "arbitrary"), 48),
        name="mla_attention",
    )(*ins)


def _ret_kernel(*refs, L, has_state, emit_state):
    refs = list(refs)
    u_ref, kt_ref, dl_ref = refs[:3]
    pos = 3
    s0_ref = None
    if has_state:
        s0_ref = refs[pos]
        pos += 1
    y_ref = refs[pos]
    pos += 1
    sout_ref = None
    if emit_state:
        sout_ref = refs[pos]
        pos += 1
    of_ref, kv_ref, sin_ref, dcomb_ref, dtab_ref, hmask_ref, avg_ref = refs[pos:]

    C = RET_BLOCK
    n = L // C
    H, DK, DV = RET_HEADS, RET_DK, RET_DV
    qk = H * DK
    DK_F, DK_B, DQ_F, DQ_B, DC_F, DC_B, MASK = range(7)
    use_cross = has_state or n > 1

    @pl.when(pl.program_id(0) == 0)
    def _():
        x = dl_ref[...]
        log_g = jnp.minimum(x, 0.0) - jnp.log1p(jnp.exp(-jnp.abs(x)))
        gf = [log_g[0:1, h:h + 1] for h in range(H)]
        gb = [log_g[1:2, h:h + 1] for h in range(H)]
        diff = (lax.broadcasted_iota(jnp.int32, (C, C), 0)
                - lax.broadcasted_iota(jnp.int32, (C, C), 1)).astype(F32)
        idx = lax.broadcasted_iota(jnp.int32, (C, 1), 0).astype(F32)
        for h in range(H):
            dcomb_ref[h] = (jnp.where(diff >= 0, jnp.exp(jnp.maximum(diff, 0.0) * gf[h]), 0.0)
                            + jnp.where(diff <= 0, jnp.exp(jnp.maximum(-diff, 0.0) * gb[h]), 0.0))

        def head_cols(lag, g):
            return jnp.concatenate([jnp.broadcast_to(jnp.exp(lag * g[h]), (C, DK)) for h in range(H)], axis=1)

        def head_rows(lag, g):
            return jnp.concatenate(
                [jnp.broadcast_to(jnp.exp(lag * g[h]) * (DK ** -0.5), (DK, C)) for h in range(H)], axis=0)

        tok = lax.broadcasted_iota(jnp.int32, (1, C), 1).astype(F32)
        dtab_ref[DK_F] = head_rows(C - 1.0 - tok, gf)
        dtab_ref[DK_B] = head_rows(tok, gb)
        dtab_ref[DQ_F] = head_cols(idx + 1.0, gf)
        dtab_ref[DQ_B] = head_cols(C - idx, gb)
        row_head = lax.broadcasted_iota(jnp.int32, (qk, qk), 0) // DK
        col_head = lax.broadcasted_iota(jnp.int32, (qk, qk), 1) // DV
        same_head = row_head == col_head
        for slot, g in ((DC_F, gf), (DC_B, gb)):
            dc = jnp.zeros((qk, qk), F32)
            for h in range(H):
                dc = jnp.where(same_head & (row_head == h), jnp.exp(C * g[h]), dc)
            dtab_ref[slot] = dc
        dtab_ref[MASK] = same_head.astype(F32)
        avg_ref[...] = jnp.where(same_head, 1.0 / DV, 0.0).astype(BF16)
        lane_head = lax.broadcasted_iota(jnp.int32, (C, qk), 1) // DK
        for h in range(H):
            hmask_ref[h] = (lane_head == h).astype(F32)

    for c in range(n):
        rows = slice(c * C, (c + 1) * C)
        q = u_ref[rows, 0:qk]
        kb = (u_ref[rows, qk:2 * qk] * (DK ** -0.5)).astype(BF16)
        v = u_ref[rows, 2 * qk:2 * qk + H * DV]
        att = [(_dot_nt((q * hmask_ref[h]).astype(BF16), kb) * dcomb_ref[h]).astype(BF16) for h in range(H)]
        v_heads = jnp.concatenate([(v * hmask_ref[h]).astype(BF16) for h in range(H)], axis=0)
        of_ref[rows, :] = _dot(jnp.concatenate(att, axis=1), v_heads)
        vb = v.astype(BF16)
        kt = kt_ref[:, rows]
        for d, slot in ((0, DK_F), (1, DK_B)):
            kv_ref[d, c] = _dot((kt * dtab_ref[slot]).astype(BF16), vb)

    finals = []
    for d, slot in ((0, DC_F), (1, DC_B)):
        if has_state:
            zero = jnp.zeros((DK, DV), F32)
            S = jnp.concatenate(
                [jnp.concatenate([s0_ref[d, h] if g == h else zero for g in range(H)], axis=1)
                 for h in range(H)], axis=0)
        else:
            S = jnp.zeros((qk, H * DV), F32)
        for c in (range(n) if d == 0 else reversed(range(n))):
            if use_cross:
                sin_ref[d, c] = S.astype(BF16)
            S = S * dtab_ref[slot] + kv_ref[d, c] * dtab_ref[MASK]
        finals.append(S)
    if emit_state:
        for d in range(2):
            for h in range(H):
                sout_ref[d, h] = finals[d][h * DK:(h + 1) * DK, h * DV:(h + 1) * DV]

    def head_mean(a):
        total = None
        for _ in range(3):
            part = a.astype(BF16)
            a = a - part.astype(F32)
            term = _dot(part, avg_ref[...])
            total = term if total is None else total + term
        return total

    for c in range(n):
        rows = slice(c * C, (c + 1) * C)
        tot = of_ref[rows, :]
        if use_cross:
            qb = u_ref[rows, 0:qk].astype(BF16)
            tot = (tot + _dot(qb, sin_ref[0, c]) * dtab_ref[DQ_F]
                   + _dot(qb, sin_ref[1, c]) * dtab_ref[DQ_B])
        xc = tot - head_mean(tot)
        normed = xc * lax.rsqrt(head_mean(xc * xc) + EPS)
        gate = u_ref[rows, 2 * qk + H * DV:2 * qk + 2 * H * DV]
        y_ref[rows, :] = _silu(gate) * normed


def _retention(u_ret, kt_ret, lw, B, L, state, emit_state):
    has_state = state is not None
    st_block = (None, 2, RET_HEADS, RET_DK, RET_DV)
    st_spec = pl.BlockSpec(st_block, lambda b: (b, 0, 0, 0, 0))
    ins = [u_ret, kt_ret, lw["ret_decay"]]
    specs = [pl.BlockSpec((L, RET_COLS), lambda b: (b, 0)),
             pl.BlockSpec((RET_HEADS * RET_DK, L), lambda b: (0, b)),
             _per_layer(lw["ret_decay"], lw["layer"])]
    if has_state:
        layer = lw["layer"]
        ins.append(state)
        specs.append(pl.BlockSpec((None,) + st_block, lambda b: (b, layer, 0, 0, 0, 0)))
    vd = RET_HEADS * RET_DV
    out_specs = [pl.BlockSpec((L, vd), lambda b: (b, 0))]
    out_shape = [jax.ShapeDtypeStruct((B * L, vd), F32)]
    if emit_state:
        out_specs.append(st_spec)
        out_shape.append(jax.ShapeDtypeStruct((B, 2, RET_HEADS, RET_DK, RET_DV), F32))
    res = pl.pallas_call(
        functools.partial(_ret_kernel, L=L, has_state=has_state, emit_state=emit_state),
        grid=(B,),
        in_specs=specs,
        out_specs=out_specs,
        out_shape=out_shape,
        scratch_shapes=[pltpu.VMEM((L, vd), F32),
                        pltpu.VMEM((2, L // RET_BLOCK, vd, vd), F32),
                        pltpu.VMEM((2, L // RET_BLOCK, vd, vd), BF16),
                        pltpu.VMEM((RET_HEADS, RET_BLOCK, RET_BLOCK), F32),
                        pltpu.VMEM((7, RET_BLOCK, vd), F32),
                        pltpu.VMEM((RET_HEADS, RET_BLOCK, vd), F32),
                        pltpu.VMEM((vd, vd), BF16)],
        compiler_params=_cparams(("arbitrary",), 48),
        name="retention",
    )(*ins)
    return (res[0], res[1]) if emit_state else (res[0], None)


def _conf_kernel(u_ref, w_ref, b_ref, g_ref, be_ref, y_ref, zp_ref, sh_ref, *, L):
    Wc = CONF_WIDTH
    halo = CONF_HALO
    zp_ref[0:halo, :] = jnp.zeros((halo, Wc), F32)
    zp_ref[halo + L:2 * halo + L, :] = jnp.zeros((halo, Wc), F32)
    zp_ref[halo:halo + L, :] = u_ref[:, 0:Wc] * jax.nn.sigmoid(u_ref[:, Wc:2 * Wc])
    first = halo - CONF_KERNEL // 2
    span = sh_ref.shape[1]
    for s in range(SUBLANES):
        sh_ref[s] = zp_ref[first + s:first + s + span, :]
    R = CONF_ROWS
    for c in range(L // R):
        acc = jnp.broadcast_to(b_ref[...], (R, Wc))
        for k in range(CONF_KERNEL):
            a, s = divmod(k, SUBLANES)
            acc = acc + w_ref[k:k + 1, :] * sh_ref[s, c * R + SUBLANES * a:c * R + SUBLANES * a + R, :]
        mu = jnp.mean(acc, axis=-1, keepdims=True)
        xc = acc - mu
        z = xc * lax.rsqrt(jnp.mean(xc * xc, axis=-1, keepdims=True) + EPS) * g_ref[...] + be_ref[...]
        y_ref[c * R:(c + 1) * R, :] = _silu(z)


def _conformer(u_conf, lw, B, L):
    ws = (lw["conf_dw_w"], lw["conf_dw_b"], lw["conf_ln_g"], lw["conf_ln_b"])
    return pl.pallas_call(
        functools.partial(_conf_kernel, L=L),
        grid=(B,),
        in_specs=[pl.BlockSpec((L, CONF_COLS), lambda b: (b, 0))] + [_per_layer(w, lw["layer"]) for w in ws],
        out_specs=pl.BlockSpec((L, CONF_WIDTH), lambda b: (b, 0)),
        out_shape=jax.ShapeDtypeStruct((B * L, CONF_WIDTH), F32),
        scratch_shapes=[pltpu.VMEM((L + 2 * CONF_HALO, CONF_WIDTH), F32),
                        pltpu.VMEM((SUBLANES, L + SUBLANES * ((CONF_KERNEL - 1) // SUBLANES), CONF_WIDTH), F32)],
        compiler_params=_cparams(("arbitrary",), 48),
        name="conformer",
    )(u_conf, *ws)


def _merge_kernel(x_ref, mod_ref, g1_ref, yhy_ref, ymla_ref, yret_ref, yconf_ref,
                  gw_ref, gb_ref, why_ref, wmla_ref, wret_ref, wconf_ref, wo_ref, o_ref, *, seqs_per_tile):
    x = x_ref[...]
    h = _rms(x, g1_ref[...]) * (1.0 + mod_ref[1:2, :]) + mod_ref[0:1, :]
    hb = h.astype(BF16)
    if seqs_per_tile == 1:
        yhy = yhy_ref[...]
    else:
        yhy = jnp.concatenate(
            [yhy_ref[:, s * HY_WIDTH:(s + 1) * HY_WIDTH] for s in range(seqs_per_tile)], axis=0)
    branches = ((yhy, why_ref), (ymla_ref[...], wmla_ref), (yret_ref[...], wret_ref), (yconf_ref[...], wconf_ref))
    D = D_MODEL
    merged = None
    for i, (y, w_ref) in enumerate(branches):
        gate = jax.nn.sigmoid(_dot(hb, gw_ref[:, i * D:(i + 1) * D]) + gb_ref[:, i * D:(i + 1) * D])
        term = gate * _dot(y.astype(BF16), w_ref[...])
        merged = term if merged is None else merged + term
    o_ref[...] = x + mod_ref[2:3, :] * _dot(merged.astype(BF16), wo_ref[...])


def _merge(x, mod, lw, y_hy, y_mla, y_ret, y_conf, B, L, latent):
    T = B * L
    TM = WIDE_TOKEN_TILE
    if L >= TM:
        tiles_per_seq, seqs_per_tile = L // TM, 1
        hy_spec = pl.BlockSpec((TM, HY_WIDTH), lambda i: (i % tiles_per_seq, i // tiles_per_seq))
    else:
        tiles_per_seq, seqs_per_tile = 1, TM // L
        hy_spec = pl.BlockSpec((L, seqs_per_tile * HY_WIDTH), lambda i: (0, i))
    once = lambda a: _per_layer(a, lw["layer"], single_buffer=True)
    row = lambda w: pl.BlockSpec((TM, w), lambda i: (i, 0))
    ws = (lw["gate_w"], lw["gate_b"], lw["hy_out"], lw["mla_out"], lw["ret_out"], lw["conf_out"], lw["w_o"])
    return pl.pallas_call(
        functools.partial(_merge_kernel, seqs_per_tile=seqs_per_tile),
        grid=(T // TM,),
        in_specs=[row(D_MODEL), _mod_spec(lw["layer"], tiles_per_seq, latent),
                  once(lw["norm1_g"]), hy_spec, row(256), row(256), row(256)] + [once(w) for w in ws],
        out_specs=row(D_MODEL),
        out_shape=jax.ShapeDtypeStruct((T, D_MODEL), F32),
        compiler_params=_cparams(("arbitrary",), 56),
        name="merge",
    )(x, mod, lw["norm1_g"], y_hy, y_mla, y_ret, y_conf, *ws)


def _ffn_kernel(x_ref, mod_ref, g2_ref, w1_ref, w2_ref, fg_ref, o_ref, *, final):
    x = x_ref[...]
    h2 = (_rms(x, g2_ref[...]) * (1.0 + mod_ref[4:5, :]) + mod_ref[3:4, :]).astype(BF16)
    acc = None
    for c0 in range(0, D_FF, FFN_CHUNK):
        c1 = min(c0 + FFN_CHUNK, D_FF)
        a = _dot(h2, w1_ref[:, c0:c1])
        b = _dot(h2, w1_ref[:, D_FF + c0:D_FF + c1])
        part = _dot((_silu(a) * b).astype(BF16), w2_ref[c0:c1, :])
        acc = part if acc is None else acc + part
    out = x + mod_ref[5:6, :] * acc
    if final:
        out = _rms(out, fg_ref[...])
    o_ref[...] = out


def _ffn(x, mod, lw, final_g, B, L, latent, final):
    T = B * L
    TM = WIDE_TOKEN_TILE
    tiles_per_seq = max(L // TM, 1)
    once = lambda a: _per_layer(a, lw["layer"], single_buffer=True)
    row = pl.BlockSpec((TM, D_MODEL), lambda i: (i, 0))
    ws = (lw["norm2_g"], lw["ffn_w1"], lw["ffn_w2"])
    return pl.pallas_call(
        functools.partial(_ffn_kernel, final=final),
        grid=(T // TM,),
        in_specs=[row, _mod_spec(lw["layer"], tiles_per_seq, latent)]
        + [once(w) for w in ws] + [pl.BlockSpec(final_g.shape, lambda i: (0, 0))],
        out_specs=row,
        out_shape=jax.ShapeDtypeStruct((T, D_MODEL), F32),
        compiler_params=_cparams(("arbitrary",), 56),
        name="ffn",
    )(x, mod, *ws, final_g)


def _trunk_layer(x, mod, lw, final_g, tables, B, L, ctx, state, final):
    latent = ctx is not None
    hv, hx1, hx2, qh, kh, vh, u_ret, u_conf, kt_ret, *cache = _inproj(x, mod, lw, B, L, latent)
    ckv, kr = cache if cache else (None, None)
    spectra = _filter_spectra(lw, L, tables)
    y_hy = _hyena(hv, hx1, hx2, lw, spectra, tables, B, L)
    y_mla = _mla(qh, kh, vh, lw, B, L, ctx)
    y_ret, S = _retention(u_ret, kt_ret, lw, B, L, state, emit_state=not latent)
    y_conf = _conformer(u_conf, lw, B, L)
    x = _merge(x, mod, lw, y_hy, y_mla, y_ret, y_conf, B, L, latent)
    x = _ffn(x, mod, lw, final_g, B, L, latent, final)
    return x, ckv, kr, S


def _stacked_weights(w_in, p):
    depth = w_in.shape[0]
    hy, mla = HY_COLS, HY_COLS + MLA_COLS
    ret = mla + RET_COLS
    w = w_in.astype(BF16)
    lanes_after = HEAD_LANES - MLA_NOPE - MLA_ROPE
    ck = jnp.concatenate([w[:, :, hy + MLA_Q_LORA:hy + MLA_Q_LORA + MLA_KV_LORA],
                          jnp.pad(w[:, :, mla - MLA_ROPE:mla], ((0, 0), (0, 0), (MLA_NOPE, lanes_after)))],
                         axis=2)
    dq = MLA_NOPE + MLA_ROPE
    head_pad = lambda a: jnp.pad(a, ((0, 0),) * 3 + ((0, HEAD_LANES - a.shape[3]),)).reshape(
        depth, a.shape[1], MLA_HEADS * HEAD_LANES)
    w_uq = head_pad(p["mla_w_uq"].reshape(depth, MLA_Q_LORA, MLA_HEADS, dq))
    w_ukv = p["mla_w_ukv"].reshape(depth, MLA_KV_LORA, MLA_HEADS, MLA_NOPE + MLA_V)
    w_ukv = jnp.concatenate([head_pad(w_ukv[..., :MLA_NOPE]), head_pad(w_ukv[..., MLA_NOPE:])], axis=2)
    row = lambda name: p[name].reshape(depth, 1, -1)
    rows = ("norm1_g", "norm2_g", "mla_q_norm", "mla_kv_norm", "hy_conv_b", "hy_b1", "hy_b2",
            "conf_dw_b", "conf_ln_g", "conf_ln_b", "gate_b")
    as_is = ("hy_conv_w", "hy_w2", "hy_w3", "hy_bias", "ret_decay", "conf_dw_w")
    bf16 = ("gate_w", "hy_out", "mla_out", "ret_out", "conf_out", "w_o", "ffn_w1", "ffn_w2")
    return {
        "w_hy": w[:, :, :hy], "w_cq": w[:, :, hy:hy + MLA_Q_LORA], "w_ck": ck,
        "w_ret": w[:, :, mla:ret], "w_conf": w[:, :, ret:],
        "mla_w_uq": w_uq.astype(BF16), "mla_w_ukv": w_ukv.astype(BF16),
        "hy_w1": jnp.pad(p["hy_w1"], ((0, 0), (0, 128 - HY_EMB), (0, 0))),
        **{name: row(name) for name in rows},
        **{name: p[name] for name in as_is},
        **{name: p[name].astype(BF16) for name in bf16},
    }


def kernel(x_prompt, x_sample, cache_mla_ckv, cache_mla_krope, state_ret, c, c_ctx, ada_w, ada_b, norm1_g, w_in, hy_conv_w, hy_conv_b, hy_w1, hy_b1, hy_w2, hy_b2, hy_w3, hy_bias, hy_out, mla_q_norm, mla_w_uq, mla_kv_norm, mla_w_ukv, mla_out, ret_decay, ret_out, conf_dw_w, conf_dw_b, conf_ln_g, conf_ln_b, conf_out, gate_w, gate_b, w_o, norm2_g, ffn_w1, ffn_w2, final_norm_g):
    p = dict(norm1_g=norm1_g, hy_conv_w=hy_conv_w, hy_conv_b=hy_conv_b, hy_w1=hy_w1, hy_b1=hy_b1,
             hy_w2=hy_w2, hy_b2=hy_b2, hy_w3=hy_w3, hy_bias=hy_bias, hy_out=hy_out,
             mla_q_norm=mla_q_norm, mla_w_uq=mla_w_uq, mla_kv_norm=mla_kv_norm, mla_w_ukv=mla_w_ukv,
             mla_out=mla_out, ret_decay=ret_decay, ret_out=ret_out, conf_dw_w=conf_dw_w,
             conf_dw_b=conf_dw_b, conf_ln_g=conf_ln_g, conf_ln_b=conf_ln_b, conf_out=conf_out,
             gate_w=gate_w, gate_b=gate_b, w_o=w_o, norm2_g=norm2_g, ffn_w1=ffn_w1, ffn_w2=ffn_w2)
    Bp, Lp, D = x_prompt.shape
    Bs, Ls, _ = x_sample.shape
    depth = w_in.shape[0]

    cond = jnp.concatenate([c_ctx[None, :], c, jnp.zeros((MOD_ROWS - 1 - Bs, D), F32)], axis=0)
    mod = _modulation(cond, ada_w, ada_b)
    tables_p = _dft_tables(Lp)
    tables_s = _dft_tables(Ls)
    final_g = final_norm_g.reshape(1, D)

    xp = x_prompt.reshape(Bp * Lp, D)
    xs = x_sample.reshape(Bs * Ls, D)
    ckvs, kropes, rets = [], [], []
    weights = _stacked_weights(w_in, p)
    for l in range(depth):
        lw = dict(weights, layer=l)
        final = l == depth - 1
        xp, ckv, kr, S = _trunk_layer(xp, mod, lw, final_g, tables_p, Bp, Lp, None, None, final)
        ckvs.append(ckv.reshape(Bp, Lp, MLA_KV_LORA))
        kropes.append(kr.reshape(Bp, Lp, MLA_ROPE))
        rets.append(S)
        xs, _, _, _ = _trunk_layer(xs, mod, lw, final_g, tables_s, Bs, Ls,
                                   (cache_mla_ckv, cache_mla_krope), state_ret, final)
    return (xp.reshape(Bp, Lp, D), xs.reshape(Bs, Ls, D),
            jnp.stack(ckvs, axis=1), jnp.stack(kropes, axis=1), jnp.stack(rets, axis=1))
```

```python
import functools
import math

import jax
import jax.numpy as jnp
from jax import lax
from jax.experimental import pallas as pl
from jax.experimental.pallas import tpu as pltpu

F32 = jnp.float32
BF16 = jnp.bfloat16

D_MODEL = 1024
DEPTH = 2
PAST_LEN = 256
EPS = 1e-6
GRID_W = 64

HY_WIDTH = 256
HY_EMB = 33
HY_BANDS = (HY_EMB - 1) // 2
HY_FFN = 64
HY_FAST_DECAY = 0.3
HY_SLOW_DECAY = 1.5
HY_TARGET = 1e-2

MLA_HEADS = 4
MLA_Q_LORA = 256
MLA_KV_LORA = 128
MLA_NOPE = 64
MLA_ROPE = 32
MLA_V = 64
ROPE_BASE = 10000.0

RET_HEADS = 4
RET_DK = 64
RET_DV = 64
RET_BLOCK = 256

CONF_WIDTH = 256
CONF_KERNEL = 31

D_FF = ((8 * D_MODEL // 3 + 255) // 256) * 256
N_BRANCH = 4

HY_COLS = 3 * HY_WIDTH
MLA_COLS = MLA_Q_LORA + MLA_KV_LORA + MLA_ROPE
RET_COLS = 2 * RET_HEADS * RET_DK + 2 * RET_HEADS * RET_DV
CONF_COLS = 2 * CONF_WIDTH

VMEM_BYTES_V7X = 64 * 1024 * 1024
SUBLANES = 8
TOKEN_TILE = 512
WIDE_TOKEN_TILE = 1024
Q_TILE = 512
MLA_SOFTMAX_ROWS = 16
DFT_TILE = 512
SHORT_SEQ_ROWS = 1024
HYENA_GROUP_ROWS = 1024
HYENA_ROWS = 1024
CONF_ROWS = 128
CONF_HALO = 16
MXU_DIM_V7X = 256
FFN_CHUNK = 4 * MXU_DIM_V7X
MOD_ROWS = 8


def _cparams(sem, vmem_mb):
    return pltpu.CompilerParams(dimension_semantics=sem, vmem_limit_bytes=vmem_mb * 1024 * 1024)


def _dot(a, b):
    return jnp.dot(a, b, preferred_element_type=F32)


def _dot_nt(a, b):
    return lax.dot_general(a, b, (((1,), (1,)), ((), ())), preferred_element_type=F32)


def _dot_exact(a, b):
    return jnp.dot(a, b, preferred_element_type=F32, precision=lax.Precision.HIGHEST)


def _dot_split(a, b):
    a_hi, b_hi = a.astype(BF16), b.astype(BF16)
    a_lo = (a - a_hi.astype(F32)).astype(BF16)
    b_lo = (b - b_hi.astype(F32)).astype(BF16)
    return _dot(a_hi, b_hi) + (_dot(a_lo, b_hi) + _dot(a_hi, b_lo))


def _rms(x, g):
    return x * lax.rsqrt(jnp.mean(x * x, axis=-1, keepdims=True) + EPS) * g


def _silu(x):
    return x * jax.nn.sigmoid(x)


def _mod_kernel(c_ref, w_ref, b_ref, o_ref):
    s = _silu(c_ref[...]).astype(BF16)
    o_ref[...] = _dot(s, w_ref[...].astype(BF16)) + b_ref[...]


def _modulation(cond, ada_w, ada_b):
    depth, d, cols = ada_w.shape
    blk = 1024
    out = pl.pallas_call(
        _mod_kernel,
        grid=(depth, cols // blk),
        in_specs=[
            pl.BlockSpec((MOD_ROWS, d), lambda l, j: (0, 0)),
            pl.BlockSpec((None, d, blk), lambda l, j: (l, 0, j)),
            pl.BlockSpec((None, 1, blk), lambda l, j: (l, 0, j)),
        ],
        out_specs=pl.BlockSpec((None, MOD_ROWS, blk), lambda l, j: (l, 0, j)),
        out_shape=jax.ShapeDtypeStruct((depth, MOD_ROWS, cols), F32),
        compiler_params=_cparams(("arbitrary", "arbitrary"), 32),
        name="modulation",
    )(cond, ada_w, ada_b.reshape(depth, 1, cols))
    return out.reshape(depth, MOD_ROWS, 6, d)


def _seqs_per_step(B, L):
    return max(1, min(B, SHORT_SEQ_ROWS // L))


def _per_layer(a, layer, single_buffer=False):
    mode = dict(pipeline_mode=pl.Buffered(1)) if single_buffer else {}
    return pl.BlockSpec((None,) + a.shape[1:], lambda *_: (layer,) + (0,) * (a.ndim - 1), **mode)


def _mod_spec(layer, tiles_per_seq, latent):
    group = (lambda i: 1 + i // tiles_per_seq) if latent else (lambda i: 0)
    return pl.BlockSpec((None, None, 6, D_MODEL), lambda i: (layer, group(i), 0, 0))


HEAD_LANES = 128


def _rope_lanes(x, cos, sa, sb):
    reps = x.shape[1] // HEAD_LANES
    wide = lambda t: jnp.concatenate([t] * reps, axis=1) if reps > 1 else t
    half = MLA_ROPE // 2
    return (x * wide(cos) + pltpu.roll(x, x.shape[1] - half, 1) * wide(sa)
            + pltpu.roll(x, half, 1) * wide(sb))


def _inproj_kernel(*refs, seqs_per_tile, seq_len, latent):
    (x_ref, mod_ref, g1_ref, why_ref, wcq_ref, wck_ref, wret_ref, wconf_ref,
     qn_ref, kvn_ref, wuq_ref, wukv_ref) = refs[:12]
    refs = refs[12:]
    if latent:
        cos_ref, sa_ref, sb_ref = refs[:3]
        refs = refs[3:]
    hv_ref, hx1_ref, hx2_ref, q_ref, kh_ref, vh_ref, ret_ref, conf_ref, rkt_ref = refs[:9]
    if not latent:
        ckv_ref, kr_ref = refs[9:]
    x = x_ref[...]
    h = _rms(x, g1_ref[...]) * (1.0 + mod_ref[1:2, :]) + mod_ref[0:1, :]
    hb = h.astype(BF16)

    u = _dot(hb, why_ref[...])
    for p, o_ref in enumerate((hv_ref, hx1_ref, hx2_ref)):
        part = u[:, p * HY_WIDTH:(p + 1) * HY_WIDTH]
        if seqs_per_tile == 1:
            o_ref[...] = part
        else:
            for s in range(seqs_per_tile):
                o_ref[:, s * HY_WIDTH:(s + 1) * HY_WIDTH] = part[s * seq_len:(s + 1) * seq_len]

    heads_w = MLA_HEADS * HEAD_LANES
    cq = _dot(hb, wcq_ref[...])
    q = _dot(_rms(cq, qn_ref[...]).astype(BF16), wuq_ref[...])
    ck = _dot(hb, wck_ref[...])
    ckv = _rms(ck[:, :MLA_KV_LORA], kvn_ref[...])
    kr_block = ck[:, MLA_KV_LORA:MLA_KV_LORA + HEAD_LANES]
    if latent:
        q = _rope_lanes(q, cos_ref[...], sa_ref[...], sb_ref[...])
        kr_block = _rope_lanes(kr_block, cos_ref[...], sa_ref[...], sb_ref[...])
    else:
        ckv_ref[...] = ckv
        kr_ref[...] = kr_block[:, MLA_NOPE:MLA_NOPE + MLA_ROPE]
    q_ref[...] = (q * ((MLA_NOPE + MLA_ROPE) ** -0.5 * math.log2(math.e))).astype(BF16)
    kv = _dot(ckv.astype(BF16), wukv_ref[...])
    kh_ref[...] = (kv[:, :heads_w] + jnp.concatenate([kr_block] * MLA_HEADS, axis=1)).astype(BF16)
    lane = lax.broadcasted_iota(jnp.int32, (1, heads_w), 1)
    ones_lane = jnp.where(lane % HEAD_LANES == MLA_V, 1.0, 0.0)
    vh_ref[...] = (kv[:, heads_w:] + ones_lane).astype(BF16)
    ret = _dot(hb, wret_ref[...])
    ret_ref[...] = ret
    qk = RET_HEADS * RET_DK
    rkt_ref[...] = ret[:, qk:2 * qk].T
    conf_ref[...] = _dot(hb, wconf_ref[...])


def _inproj(x, mod, lw, B, L, latent):
    T = B * L
    TM = TOKEN_TILE
    nt = T // TM
    if L >= TM:
        tiles_per_seq, seqs_per_tile = L // TM, 1
        hy_block = (TM, HY_WIDTH)
        hy_map = lambda i: (i % tiles_per_seq, i // tiles_per_seq)
    else:
        tiles_per_seq, seqs_per_tile = 1, TM // L
        hy_block = (L, seqs_per_tile * HY_WIDTH)
        hy_map = lambda i: (0, i)
    row = lambda w: pl.BlockSpec((TM, w), lambda i: (i, 0))
    weights = (lw["norm1_g"], lw["w_hy"], lw["w_cq"], lw["w_ck"], lw["w_ret"], lw["w_conf"],
               lw["mla_q_norm"], lw["mla_kv_norm"], lw["mla_w_uq"], lw["mla_w_ukv"])
    hy_shape = jax.ShapeDtypeStruct((L, B * HY_WIDTH), F32)
    qk = RET_HEADS * RET_DK
    heads_w = MLA_HEADS * HEAD_LANES
    ins = [x, mod, *weights]
    in_specs = ([row(D_MODEL), _mod_spec(lw["layer"], tiles_per_seq, latent)]
                + [_per_layer(w, lw["layer"]) for w in weights])
    out_specs = ([pl.BlockSpec(hy_block, hy_map)] * 3
                 + [row(heads_w)] * 3 + [row(RET_COLS), row(CONF_COLS), pl.BlockSpec((qk, TM), lambda i: (0, i))])
    out_shape = ([hy_shape] * 3 + [jax.ShapeDtypeStruct((T, heads_w), BF16)] * 3
                 + [jax.ShapeDtypeStruct((T, RET_COLS), F32), jax.ShapeDtypeStruct((T, CONF_COLS), F32),
                    jax.ShapeDtypeStruct((qk, T), F32)])
    if latent:
        ins += list(_rope_lane_tables(L))
        in_specs += [pl.BlockSpec((TM, HEAD_LANES), lambda i: (i % tiles_per_seq, 0))] * 3
    else:
        out_specs += [row(MLA_KV_LORA), row(MLA_ROPE)]
        out_shape += [jax.ShapeDtypeStruct((T, MLA_KV_LORA), F32), jax.ShapeDtypeStruct((T, MLA_ROPE), F32)]
    return pl.pallas_call(
        functools.partial(_inproj_kernel, seqs_per_tile=seqs_per_tile, seq_len=L, latent=latent),
        grid=(nt,),
        in_specs=in_specs,
        out_specs=out_specs,
        out_shape=out_shape,
        compiler_params=_cparams(("arbitrary",), 48),
        name="inproj",
    )(*ins)


def _dft_tables(L):
    N = 2 * L
    k_lo = min(L, 32)
    k_hi = L // k_lo
    t = jnp.arange(L, dtype=jnp.int32)[None, :]

    def cs(k):
        m = (k[:, None] * t) % N
        ang = m.astype(F32) * (2.0 * math.pi / N)
        return jnp.cos(ang), jnp.sin(ang)

    ca, sa = cs(jnp.arange(k_hi, dtype=jnp.int32) * k_lo)
    cb, sb = cs(jnp.arange(k_lo, dtype=jnp.int32))
    cos = (ca[:, None, :] * cb[None, :, :] - sa[:, None, :] * sb[None, :, :]).reshape(L, L)
    sin = (sa[:, None, :] * cb[None, :, :] + ca[:, None, :] * sb[None, :, :]).reshape(L, L)
    return cos.astype(BF16), (-sin).astype(BF16)


def _filter_features(L):
    t = jnp.linspace(0.0, 1.0, L, dtype=F32)[:, None]
    w = 2.0 * math.pi * jnp.arange(L, dtype=F32)[:, None] / L
    f = jnp.linspace(1e-4, HY_BANDS - 1, HY_BANDS, dtype=F32)[None, :]
    z = jnp.concatenate([t, jnp.cos(f * w), -jnp.sin(f * w)], axis=-1)
    z = jnp.pad(z, ((0, 0), (0, 128 - HY_EMB)))
    max_decay = math.log(HY_TARGET) / HY_FAST_DECAY
    min_decay = math.log(HY_TARGET) / HY_SLOW_DECAY
    deltas = jnp.abs(jnp.linspace(min_decay, max_decay, HY_WIDTH, dtype=F32))
    decay = jnp.exp(-t * deltas[None, :])
    return z, decay


def _alternating(rows):
    t = lax.broadcasted_iota(jnp.int32, (rows, 1), 0)
    return (1 - 2 * (t & 1)).astype(F32)


def _filter_kernel(z_ref, dec_ref, w1_ref, b1_ref, w2_ref, b2_ref, w3_ref, cos_ref, sin_ref,
                   kr_ref, ki_ref, kn_ref, h_ref, *, L):
    j = pl.program_id(0)
    W = HY_WIDTH
    N = 2 * L

    @pl.when(j == 0)
    def _():
        h = jnp.sin(_dot_exact(z_ref[...], w1_ref[...]) + b1_ref[...])
        h = jnp.sin(_dot_exact(h, w2_ref[...]) + b2_ref[...])
        h = _dot_split(h, w3_ref[...]) * jnp.concatenate([dec_ref[...]] * 4, axis=1)
        cs = jnp.sum(jnp.abs(h), axis=0, keepdims=True)
        s0 = cs[:, 0:W] + cs[:, W:2 * W]
        s1 = cs[:, 2 * W:3 * W] + cs[:, 3 * W:4 * W]
        h = h / jnp.concatenate([s0, s0, s1, s1], axis=1)
        row = lax.broadcasted_iota(jnp.int32, h.shape, 0)
        col = lax.broadcasted_iota(jnp.int32, h.shape, 1)
        backward = (col // W) % 2 == 1
        h = jnp.where(backward & (row == 0), 0.0, h)
        h_ref[...] = h.astype(BF16)
        nyq = jnp.sum(h * _alternating(L), axis=0, keepdims=True) * (1.0 / N)
        for o in range(2):
            c = 2 * o * W
            kn_ref[o] = nyq[:, c:c + W] + nyq[:, c + W:c + 2 * W]

    hb = h_ref[...]
    sr = _dot(cos_ref[...], hb)
    si = _dot(sin_ref[...], hb)
    row = lax.broadcasted_iota(jnp.int32, (sr.shape[0], W), 0)
    scale = jnp.where((row == 0) & (j == 0), 1.0 / N, 2.0 / N)
    for o in range(2):
        c = 2 * o * W
        kr_ref[o] = (sr[:, c:c + W] + sr[:, c + W:c + 2 * W]) * scale
        ki_ref[o] = (si[:, c:c + W] - si[:, c + W:c + 2 * W]) * scale


def _filter_spectra(lw, L, tables):
    z, decay = _filter_features(L)
    cos, msin = tables
    Tk = min(DFT_TILE, L)
    full = lambda a: pl.BlockSpec(a.shape, lambda j: (0,) * a.ndim)
    mlp = (lw["hy_w1"], lw["hy_b1"], lw["hy_w2"], lw["hy_b2"], lw["hy_w3"])
    ins = (z, decay, *mlp)
    tile = pl.BlockSpec((Tk, L), lambda j: (j, 0))
    spec = pl.BlockSpec((2, Tk, HY_WIDTH), lambda j: (0, j, 0))
    return pl.pallas_call(
        functools.partial(_filter_kernel, L=L),
        grid=(L // Tk,),
        in_specs=[full(z), full(decay)] + [_per_layer(a, lw["layer"]) for a in mlp] + [tile, tile],
        out_specs=[spec, spec, pl.BlockSpec((2, 1, HY_WIDTH), lambda j: (0, 0, 0))],
        out_shape=[jax.ShapeDtypeStruct((2, L, HY_WIDTH), F32)] * 2
        + [jax.ShapeDtypeStruct((2, 1, HY_WIDTH), F32)],
        scratch_shapes=[pltpu.VMEM((L, 4 * HY_WIDTH), BF16)],
        compiler_params=_cparams(("arbitrary",), 48),
        name="hyena_filter",
    )(*ins, cos, msin)


def _hyena_kernel(v_ref, x1_ref, x2_ref, cw_ref, cb_ref, bias_ref, kr_ref, ki_ref, kn_ref,
                  cos_ref, sin_ref, y_ref, cur_ref, curb_ref, yr_ref, yi_ref, gate_ref,
                  *, L, W, T):
    reps = W // HY_WIDTH
    tiled = lambda a: jnp.concatenate([a] * reps, axis=1) if reps > 1 else a
    alt = _alternating(L)

    def short_conv(u_ref, p):
        u = u_ref[...]
        row = lax.broadcasted_iota(jnp.int32, (L, W), 0)
        prev = jnp.where(row == 0, 0.0, pltpu.roll(u, 1, 0))
        nxt = jnp.where(row == L - 1, 0.0, pltpu.roll(u, L - 1, 0))
        cols = slice(p * HY_WIDTH, (p + 1) * HY_WIDTH)
        w = [tiled(cw_ref[k:k + 1, cols]) for k in range(3)]
        return prev * w[0] + u * w[1] + nxt * w[2] + tiled(cb_ref[:, cols])

    cur_ref[...] = short_conv(v_ref, 0)
    for o, x_ref in enumerate((x1_ref, x2_ref)):
        cur = cur_ref[...]
        curb_ref[...] = cur.astype(BF16)
        gate_ref[...] = short_conv(x_ref, o + 1)
        nyq = jnp.sum(cur * alt, axis=0, keepdims=True) * tiled(kn_ref[o])
        for f in range(L // T):
            rows = slice(f * T, (f + 1) * T)
            cb = curb_ref[...]
            xr = _dot(cos_ref[rows, :], cb)
            xi = _dot(sin_ref[rows, :], cb)
            kr, ki = tiled(kr_ref[o, rows, :]), tiled(ki_ref[o, rows, :])
            yr_ref[rows, :] = (xr * kr - xi * ki).astype(BF16)
            yi_ref[rows, :] = (xr * ki + xi * kr).astype(BF16)
        for t in range(L // T):
            rows = slice(t * T, (t + 1) * T)
            conv = (_dot(cos_ref[rows, :], yr_ref[...]) + _dot(sin_ref[rows, :], yi_ref[...])
                    + alt[rows] * nyq + cur_ref[rows, :] * tiled(bias_ref[o:o + 1, :]))
            out = gate_ref[rows, :] * conv
            if o == 0:
                cur_ref[rows, :] = out
            else:
                y_ref[rows, :] = out


def _hyena(hv, hx1, hx2, lw, spectra, tables, B, L):
    kr, ki, kn = spectra
    cos, msin = tables
    T = min(HYENA_ROWS, L)
    W = HY_WIDTH * max(1, min(B, HYENA_GROUP_ROWS // L))
    ng = (B * HY_WIDTH) // W
    col = pl.BlockSpec((L, W), lambda g: (0, g))
    once = lambda a: pl.BlockSpec(a.shape, lambda g: (0,) * a.ndim, pipeline_mode=pl.Buffered(1))
    params = (lw["hy_conv_w"], lw["hy_conv_b"], lw["hy_bias"])
    consts = (kr, ki, kn, cos, msin)
    return pl.pallas_call(
        functools.partial(_hyena_kernel, L=L, W=W, T=T),
        grid=(ng,),
        in_specs=[col, col, col] + [_per_layer(a, lw["layer"]) for a in params] + [once(a) for a in consts],
        out_specs=col,
        out_shape=jax.ShapeDtypeStruct((L, B * HY_WIDTH), F32),
        scratch_shapes=[pltpu.VMEM((L, W), F32), pltpu.VMEM((L, W), BF16), pltpu.VMEM((L, W), BF16),
                        pltpu.VMEM((L, W), BF16), pltpu.VMEM((L, W), F32)],
        compiler_params=_cparams(("arbitrary",), 60),
        name="hyena_conv",
    )(hv, hx1, hx2, *params, *consts)


def _mla_kernel(*refs, L, latent, seqs):
    if latent:
        q_ref, kh_ref, vh_ref, cckv_ref, ckr_ref, wukv_ref, o_ref, s_ref, p_ref, kctx_ref, vctx_ref = refs
    else:
        q_ref, kh_ref, vh_ref, o_ref, s_ref, p_ref = refs
    TQ = q_ref.shape[0] // seqs
    heads_w = MLA_HEADS * HEAD_LANES
    block = lambda h: slice(h * HEAD_LANES, (h + 1) * HEAD_LANES)

    if latent:
        @pl.when(pl.program_id(1) == 0)
        def _():
            kv = _dot(cckv_ref[...].astype(BF16), wukv_ref[...])
            zeros = lambda w: jnp.zeros((PAST_LEN, w), F32)
            kr_block = jnp.concatenate(
                [zeros(MLA_NOPE), ckr_ref[...], zeros(HEAD_LANES - MLA_NOPE - MLA_ROPE)], axis=1)
            kctx_ref[...] = (kv[:, :heads_w] + jnp.concatenate([kr_block] * MLA_HEADS, axis=1)).astype(BF16)
            lane = lax.broadcasted_iota(jnp.int32, (1, heads_w), 1)
            vctx_ref[...] = (kv[:, heads_w:] + jnp.where(lane % HEAD_LANES == MLA_V, 1.0, 0.0)).astype(BF16)

    def attend(q_rows, k_rows):
        def scores(h):
            qh = q_ref[q_rows, block(h)]
            s_ref[h, :, 0:L] = _dot_nt(qh, kh_ref[k_rows, block(h)])
            if latent:
                s_ref[h, :, L:L + PAST_LEN] = _dot_nt(qh, kctx_ref[:, block(h)])

        outs = []
        scores(0)
        for h in range(MLA_HEADS):
            if h + 1 < MLA_HEADS:
                scores(h + 1)
            for r in range(TQ // MLA_SOFTMAX_ROWS):
                rows = slice(r * MLA_SOFTMAX_ROWS, (r + 1) * MLA_SOFTMAX_ROWS)
                s = s_ref[h, rows, :]
                p_ref[h, rows, :] = jnp.exp2(s - jnp.max(s, axis=-1, keepdims=True)).astype(BF16)
            pv = _dot(p_ref[h, :, 0:L], vh_ref[k_rows, block(h)])
            if latent:
                pv = pv + _dot(p_ref[h, :, L:L + PAST_LEN], vctx_ref[:, block(h)])
            outs.append(pv[:, :MLA_V] / pv[:, MLA_V:MLA_V + 1])
        o_ref[q_rows, :] = jnp.concatenate(outs, axis=-1)

    for sq in range(seqs):
        attend(slice(sq * TQ, (sq + 1) * TQ), slice(sq * L, (sq + 1) * L))


def _rope_tables(L):
    rows = L // GRID_W
    row = jnp.repeat(jnp.arange(rows), GRID_W).astype(F32)
    col = jnp.tile(jnp.arange(GRID_W), rows).astype(F32)
    per_axis = MLA_ROPE // 4
    inv = ROPE_BASE ** (-jnp.arange(per_axis, dtype=F32) / per_axis)
    ang = jnp.concatenate([row[:, None] * inv, col[:, None] * inv], axis=-1)
    return jnp.cos(ang), jnp.sin(ang)


def _rope_lane_tables(L):
    cos, sin = _rope_tables(L)
    pad = HEAD_LANES - MLA_NOPE - MLA_ROPE
    one, zero = jnp.ones((L, MLA_NOPE), F32), jnp.zeros((L, MLA_NOPE), F32)
    half0 = jnp.zeros_like(sin)
    cos_t = jnp.concatenate([one, cos, cos, one[:, :pad]], axis=1)
    sa_t = jnp.concatenate([zero, -sin, half0, zero[:, :pad]], axis=1)
    sb_t = jnp.concatenate([zero, half0, sin, zero[:, :pad]], axis=1)
    return cos_t, sa_t, sb_t


def _mla(qh, kh, vh, lw, B, L, ctx):
    latent = ctx is not None
    TQ = min(Q_TILE, L)
    nq = L // TQ
    seqs = 1 if latent or nq > 1 else _seqs_per_step(B, L)
    Lk = L + PAST_LEN if latent else L
    heads_w = MLA_HEADS * HEAD_LANES
    seq = pl.BlockSpec((seqs * L, heads_w), lambda b, i: (b, 0))
    ins = [qh, kh, vh]
    specs = [pl.BlockSpec((seqs * TQ, heads_w), lambda b, i: (b * nq + i, 0)), seq, seq]
    scratch = [pltpu.VMEM((MLA_HEADS, TQ, Lk), F32), pltpu.VMEM((MLA_HEADS, TQ, Lk), BF16)]
    if latent:
        ins += [ctx[0], ctx[1], lw["mla_w_ukv"]]
        layer = lw["layer"]
        specs += [pl.BlockSpec((None, None, PAST_LEN, MLA_KV_LORA), lambda b, i: (b, layer, 0, 0)),
                  pl.BlockSpec((None, None, PAST_LEN, MLA_ROPE), lambda b, i: (b, layer, 0, 0)),
                  _per_layer(lw["mla_w_ukv"], lw["layer"])]
        scratch += [pltpu.VMEM((PAST_LEN, heads_w), BF16)] * 2
    return pl.pallas_call(
        functools.partial(_mla_kernel, L=L, latent=latent, seqs=seqs),
        grid=(B // seqs, nq),
        in_specs=specs,
        out_specs=pl.BlockSpec((seqs * TQ, MLA_HEADS * MLA_V), lambda b, i: (b * nq + i, 0)),
        out_shape=jax.ShapeDtypeStruct((B * L, MLA_HEADS * MLA_V), F32),
        scratch_shapes=scratch,
        compiler_params=_cparams(("arbitrary", "arbitrary"), 48),
        name="mla_attention",
    )(*ins)


def _ret_kernel(*refs, L, seqs, has_state, emit_state):
    refs = list(refs)
    u_ref, kt_ref, dl_ref = refs[:3]
    pos = 3
    s0_ref = None
    if has_state:
        s0_ref = refs[pos]
        pos += 1
    y_ref = refs[pos]
    pos += 1
    sout_ref = None
    if emit_state:
        sout_ref = refs[pos]
        pos += 1
    of_ref, kv_ref, sin_ref, dcomb_ref, dtab_ref, hmask_ref, avg_ref = refs[pos:]

    C = RET_BLOCK
    n = L // C
    H, DK, DV = RET_HEADS, RET_DK, RET_DV
    qk = H * DK
    DK_F, DK_B, DQ_F, DQ_B, DC_F, DC_B, MASK = range(7)
    use_cross = has_state or n > 1

    @pl.when(pl.program_id(0) == 0)
    def _():
        x = dl_ref[...]
        log_g = jnp.minimum(x, 0.0) - jnp.log1p(jnp.exp(-jnp.abs(x)))
        gf = [log_g[0:1, h:h + 1] for h in range(H)]
        gb = [log_g[1:2, h:h + 1] for h in range(H)]
        diff = (lax.broadcasted_iota(jnp.int32, (C, C), 0)
                - lax.broadcasted_iota(jnp.int32, (C, C), 1)).astype(F32)
        idx = lax.broadcasted_iota(jnp.int32, (C, 1), 0).astype(F32)
        for h in range(H):
            dcomb_ref[h] = (jnp.where(diff >= 0, jnp.exp(jnp.maximum(diff, 0.0) * gf[h]), 0.0)
                            + jnp.where(diff <= 0, jnp.exp(jnp.maximum(-diff, 0.0) * gb[h]), 0.0))

        def head_cols(lag, g):
            return jnp.concatenate([jnp.broadcast_to(jnp.exp(lag * g[h]), (C, DK)) for h in range(H)], axis=1)

        def head_rows(lag, g):
            return jnp.concatenate(
                [jnp.broadcast_to(jnp.exp(lag * g[h]) * (DK ** -0.5), (DK, C)) for h in range(H)], axis=0)

        tok = lax.broadcasted_iota(jnp.int32, (1, C), 1).astype(F32)
        dtab_ref[DK_F] = head_rows(C - 1.0 - tok, gf)
        dtab_ref[DK_B] = head_rows(tok, gb)
        dtab_ref[DQ_F] = head_cols(idx + 1.0, gf)
        dtab_ref[DQ_B] = head_cols(C - idx, gb)
        row_head = lax.broadcasted_iota(jnp.int32, (qk, qk), 0) // DK
        col_head = lax.broadcasted_iota(jnp.int32, (qk, qk), 1) // DV
        same_head = row_head == col_head
        for slot, g in ((DC_F, gf), (DC_B, gb)):
            dc = jnp.zeros((qk, qk), F32)
            for h in range(H):
                dc = jnp.where(same_head & (row_head == h), jnp.exp(C * g[h]), dc)
            dtab_ref[slot] = dc
        dtab_ref[MASK] = same_head.astype(F32)
        avg_ref[...] = jnp.where(same_head, 1.0 / DV, 0.0).astype(BF16)
        lane_head = lax.broadcasted_iota(jnp.int32, (C, qk), 1) // DK
        for h in range(H):
            hmask_ref[h] = (lane_head == h).astype(F32)

    def head_mean(a):
        total = None
        for _ in range(3):
            part = a.astype(BF16)
            a = a - part.astype(F32)
            term = _dot(part, avg_ref[...])
            total = term if total is None else total + term
        return total

    for sq in range(seqs):
        chunk = lambda c: slice(sq * L + c * C, sq * L + (c + 1) * C)

        for c in range(n):
            rows = chunk(c)
            q = u_ref[rows, 0:qk]
            kb = (u_ref[rows, qk:2 * qk] * (DK ** -0.5)).astype(BF16)
            v = u_ref[rows, 2 * qk:2 * qk + H * DV]
            att = [(_dot_nt((q * hmask_ref[h]).astype(BF16), kb) * dcomb_ref[h]).astype(BF16)
                   for h in range(H)]
            v_heads = jnp.concatenate([(v * hmask_ref[h]).astype(BF16) for h in range(H)], axis=0)
            of_ref[c * C:(c + 1) * C, :] = _dot(jnp.concatenate(att, axis=1), v_heads)
            vb = v.astype(BF16)
            kt = kt_ref[:, rows]
            for d, slot in ((0, DK_F), (1, DK_B)):
                kv_ref[d, c] = _dot((kt * dtab_ref[slot]).astype(BF16), vb)

        finals = []
        for d, slot in ((0, DC_F), (1, DC_B)):
            if has_state:
                zero = jnp.zeros((DK, DV), F32)
                S = jnp.concatenate(
                    [jnp.concatenate([s0_ref[sq, d, h] if g == h else zero for g in range(H)], axis=1)
                     for h in range(H)], axis=0)
            else:
                S = jnp.zeros((qk, H * DV), F32)
            for c in (range(n) if d == 0 else reversed(range(n))):
                if use_cross:
                    sin_ref[d, c] = S.astype(BF16)
                S = S * dtab_ref[slot] + kv_ref[d, c] * dtab_ref[MASK]
            finals.append(S)
        if emit_state:
            for d in range(2):
                for h in range(H):
                    sout_ref[sq, d, h] = finals[d][h * DK:(h + 1) * DK, h * DV:(h + 1) * DV]

        for c in range(n):
            rows = chunk(c)
            tot = of_ref[c * C:(c + 1) * C, :]
            if use_cross:
                qb = u_ref[rows, 0:qk].astype(BF16)
                tot = (tot + _dot(qb, sin_ref[0, c]) * dtab_ref[DQ_F]
                       + _dot(qb, sin_ref[1, c]) * dtab_ref[DQ_B])
            xc = tot - head_mean(tot)
            normed = xc * lax.rsqrt(head_mean(xc * xc) + EPS)
            gate = u_ref[rows, 2 * qk + H * DV:2 * qk + 2 * H * DV]
            y_ref[rows, :] = _silu(gate) * normed


def _retention(u_ret, kt_ret, lw, B, L, state, emit_state):
    has_state = state is not None
    seqs = _seqs_per_step(B, L)
    st_dims = (2, RET_HEADS, RET_DK, RET_DV)
    ins = [u_ret, kt_ret, lw["ret_decay"]]
    specs = [pl.BlockSpec((seqs * L, RET_COLS), lambda b: (b, 0)),
             pl.BlockSpec((RET_HEADS * RET_DK, seqs * L), lambda b: (0, b)),
             _per_layer(lw["ret_decay"], lw["layer"])]
    if has_state:
        layer = lw["layer"]
        ins.append(state)
        specs.append(pl.BlockSpec((seqs, None) + st_dims, lambda b: (b, layer, 0, 0, 0, 0)))
    vd = RET_HEADS * RET_DV
    out_specs = [pl.BlockSpec((seqs * L, vd), lambda b: (b, 0))]
    out_shape = [jax.ShapeDtypeStruct((B * L, vd), F32)]
    if emit_state:
        out_specs.append(pl.BlockSpec((seqs,) + st_dims, lambda b: (b, 0, 0, 0, 0)))
        out_shape.append(jax.ShapeDtypeStruct((B,) + st_dims, F32))
    res = pl.pallas_call(
        functools.partial(_ret_kernel, L=L, seqs=seqs, has_state=has_state, emit_state=emit_state),
        grid=(B // seqs,),
        in_specs=specs,
        out_specs=out_specs,
        out_shape=out_shape,
        scratch_shapes=[pltpu.VMEM((L, vd), F32),
                        pltpu.VMEM((2, L // RET_BLOCK, vd, vd), F32),
                        pltpu.VMEM((2, L // RET_BLOCK, vd, vd), BF16),
                        pltpu.VMEM((RET_HEADS, RET_BLOCK, RET_BLOCK), F32),
                        pltpu.VMEM((7, RET_BLOCK, vd), F32),
                        pltpu.VMEM((RET_HEADS, RET_BLOCK, vd), F32),
                        pltpu.VMEM((vd, vd), BF16)],
        compiler_params=_cparams(("arbitrary",), 48),
        name="retention",
    )(*ins)
    return (res[0], res[1]) if emit_state else (res[0], None)


def _conf_kernel(u_ref, w_ref, b_ref, g_ref, be_ref, y_ref, zp_ref, sh_ref, *, L, seqs):
    Wc = CONF_WIDTH
    halo = CONF_HALO
    zp_ref[0:halo, :] = jnp.zeros((halo, Wc), F32)
    zp_ref[halo + L:2 * halo + L, :] = jnp.zeros((halo, Wc), F32)
    first = halo - CONF_KERNEL // 2
    span = sh_ref.shape[1]
    R = CONF_ROWS
    for sq in range(seqs):
        r0 = sq * L
        zp_ref[halo:halo + L, :] = u_ref[r0:r0 + L, 0:Wc] * jax.nn.sigmoid(u_ref[r0:r0 + L, Wc:2 * Wc])
        for s in range(SUBLANES):
            sh_ref[s] = zp_ref[first + s:first + s + span, :]
        for c in range(L // R):
            acc = jnp.broadcast_to(b_ref[...], (R, Wc))
            for k in range(CONF_KERNEL):
                a, s = divmod(k, SUBLANES)
                acc = acc + w_ref[k:k + 1, :] * sh_ref[s, c * R + SUBLANES * a:c * R + SUBLANES * a + R, :]
            mu = jnp.mean(acc, axis=-1, keepdims=True)
            xc = acc - mu
            z = xc * lax.rsqrt(jnp.mean(xc * xc, axis=-1, keepdims=True) + EPS) * g_ref[...] + be_ref[...]
            y_ref[r0 + c * R:r0 + (c + 1) * R, :] = _silu(z)


def _conformer(u_conf, lw, B, L):
    ws = (lw["conf_dw_w"], lw["conf_dw_b"], lw["conf_ln_g"], lw["conf_ln_b"])
    seqs = _seqs_per_step(B, L)
    return pl.pallas_call(
        functools.partial(_conf_kernel, L=L, seqs=seqs),
        grid=(B // seqs,),
        in_specs=[pl.BlockSpec((seqs * L, CONF_COLS), lambda b: (b, 0))]
        + [_per_layer(w, lw["layer"]) for w in ws],
        out_specs=pl.BlockSpec((seqs * L, CONF_WIDTH), lambda b: (b, 0)),
        out_shape=jax.ShapeDtypeStruct((B * L, CONF_WIDTH), F32),
        scratch_shapes=[pltpu.VMEM((L + 2 * CONF_HALO, CONF_WIDTH), F32),
                        pltpu.VMEM((SUBLANES, L + SUBLANES * ((CONF_KERNEL - 1) // SUBLANES), CONF_WIDTH), F32)],
        compiler_params=_cparams(("arbitrary",), 48),
        name="conformer",
    )(u_conf, *ws)


def _merge_kernel(x_ref, mod_ref, g1_ref, yhy_ref, ymla_ref, yret_ref, yconf_ref,
                  gw_ref, gb_ref, why_ref, wmla_ref, wret_ref, wconf_ref, wo_ref, o_ref, *, seqs_per_tile):
    x = x_ref[...]
    h = _rms(x, g1_ref[...]) * (1.0 + mod_ref[1:2, :]) + mod_ref[0:1, :]
    hb = h.astype(BF16)
    if seqs_per_tile == 1:
        yhy = yhy_ref[...]
    else:
        yhy = jnp.concatenate(
            [yhy_ref[:, s * HY_WIDTH:(s + 1) * HY_WIDTH] for s in range(seqs_per_tile)], axis=0)
    branches = ((yhy, why_ref), (ymla_ref[...], wmla_ref), (yret_ref[...], wret_ref), (yconf_ref[...], wconf_ref))
    D = D_MODEL
    merged = None
    for i, (y, w_ref) in enumerate(branches):
        gate = jax.nn.sigmoid(_dot(hb, gw_ref[:, i * D:(i + 1) * D]) + gb_ref[:, i * D:(i + 1) * D])
        term = gate * _dot(y.astype(BF16), w_ref[...])
        merged = term if merged is None else merged + term
    o_ref[...] = x + mod_ref[2:3, :] * _dot(merged.astype(BF16), wo_ref[...])


def _merge(x, mod, lw, y_hy, y_mla, y_ret, y_conf, B, L, latent):
    T = B * L
    TM = WIDE_TOKEN_TILE
    if L >= TM:
        tiles_per_seq, seqs_per_tile = L // TM, 1
        hy_spec = pl.BlockSpec((TM, HY_WIDTH), lambda i: (i % tiles_per_seq, i // tiles_per_seq))
    else:
        tiles_per_seq, seqs_per_tile = 1, TM // L
        hy_spec = pl.BlockSpec((L, seqs_per_tile * HY_WIDTH), lambda i: (0, i))
    once = lambda a: _per_layer(a, lw["layer"], single_buffer=True)
    row = lambda w: pl.BlockSpec((TM, w), lambda i: (i, 0))
    ws = (lw["gate_w"], lw["gate_b"], lw["hy_out"], lw["mla_out"], lw["ret_out"], lw["conf_out"], lw["w_o"])
    return pl.pallas_call(
        functools.partial(_merge_kernel, seqs_per_tile=seqs_per_tile),
        grid=(T // TM,),
        in_specs=[row(D_MODEL), _mod_spec(lw["layer"], tiles_per_seq, latent),
                  once(lw["norm1_g"]), hy_spec, row(256), row(256), row(256)] + [once(w) for w in ws],
        out_specs=row(D_MODEL),
        out_shape=jax.ShapeDtypeStruct((T, D_MODEL), F32),
        compiler_params=_cparams(("arbitrary",), 56),
        name="merge",
    )(x, mod, lw["norm1_g"], y_hy, y_mla, y_ret, y_conf, *ws)


def _ffn_kernel(x_ref, mod_ref, g2_ref, w1_ref, w2_ref, fg_ref, o_ref, *, final):
    x = x_ref[...]
    h2 = (_rms(x, g2_ref[...]) * (1.0 + mod_ref[4:5, :]) + mod_ref[3:4, :]).astype(BF16)
    acc = None
    for c0 in range(0, D_FF, FFN_CHUNK):
        c1 = min(c0 + FFN_CHUNK, D_FF)
        a = _dot(h2, w1_ref[:, c0:c1])
        b = _dot(h2, w1_ref[:, D_FF + c0:D_FF + c1])
        part = _dot((_silu(a) * b).astype(BF16), w2_ref[c0:c1, :])
        acc = part if acc is None else acc + part
    out = x + mod_ref[5:6, :] * acc
    if final:
        out = _rms(out, fg_ref[...])
    o_ref[...] = out


def _ffn(x, mod, lw, final_g, B, L, latent, final):
    T = B * L
    TM = WIDE_TOKEN_TILE
    tiles_per_seq = max(L // TM, 1)
    once = lambda a: _per_layer(a, lw["layer"], single_buffer=True)
    row = pl.BlockSpec((TM, D_MODEL), lambda i: (i, 0))
    ws = (lw["norm2_g"], lw["ffn_w1"], lw["ffn_w2"])
    return pl.pallas_call(
        functools.partial(_ffn_kernel, final=final),
        grid=(T // TM,),
        in_specs=[row, _mod_spec(lw["layer"], tiles_per_seq, latent)]
        + [once(w) for w in ws] + [pl.BlockSpec(final_g.shape, lambda i: (0, 0))],
        out_specs=row,
        out_shape=jax.ShapeDtypeStruct((T, D_MODEL), F32),
        compiler_params=_cparams(("arbitrary",), 56),
        name="ffn",
    )(x, mod, *ws, final_g)


def _trunk_layer(x, mod, lw, final_g, tables, B, L, ctx, state, final):
    latent = ctx is not None
    hv, hx1, hx2, qh, kh, vh, u_ret, u_conf, kt_ret, *cache = _inproj(x, mod, lw, B, L, latent)
    ckv, kr = cache if cache else (None, None)
    spectra = _filter_spectra(lw, L, tables)
    y_hy = _hyena(hv, hx1, hx2, lw, spectra, tables, B, L)
    y_mla = _mla(qh, kh, vh, lw, B, L, ctx)
    y_ret, S = _retention(u_ret, kt_ret, lw, B, L, state, emit_state=not latent)
    y_conf = _conformer(u_conf, lw, B, L)
    x = _merge(x, mod, lw, y_hy, y_mla, y_ret, y_conf, B, L, latent)
    x = _ffn(x, mod, lw, final_g, B, L, latent, final)
    return x, ckv, kr, S


def _stacked_weights(w_in, p):
    depth = w_in.shape[0]
    hy, mla = HY_COLS, HY_COLS + MLA_COLS
    ret = mla + RET_COLS
    w = w_in.astype(BF16)
    lanes_after = HEAD_LANES - MLA_NOPE - MLA_ROPE
    ck = jnp.concatenate([w[:, :, hy + MLA_Q_LORA:hy + MLA_Q_LORA + MLA_KV_LORA],
                          jnp.pad(w[:, :, mla - MLA_ROPE:mla], ((0, 0), (0, 0), (MLA_NOPE, lanes_after)))],
                         axis=2)
    dq = MLA_NOPE + MLA_ROPE
    head_pad = lambda a: jnp.pad(a, ((0, 0),) * 3 + ((0, HEAD_LANES - a.shape[3]),)).reshape(
        depth, a.shape[1], MLA_HEADS * HEAD_LANES)
    w_uq = head_pad(p["mla_w_uq"].reshape(depth, MLA_Q_LORA, MLA_HEADS, dq))
    w_ukv = p["mla_w_ukv"].reshape(depth, MLA_KV_LORA, MLA_HEADS, MLA_NOPE + MLA_V)
    w_ukv = jnp.concatenate([head_pad(w_ukv[..., :MLA_NOPE]), head_pad(w_ukv[..., MLA_NOPE:])], axis=2)
    row = lambda name: p[name].reshape(depth, 1, -1)
    rows = ("norm1_g", "norm2_g", "mla_q_norm", "mla_kv_norm", "hy_conv_b", "hy_b1", "hy_b2",
            "conf_dw_b", "conf_ln_g", "conf_ln_b", "gate_b")
    as_is = ("hy_conv_w", "hy_w2", "hy_w3", "hy_bias", "ret_decay", "conf_dw_w")
    bf16 = ("gate_w", "hy_out", "mla_out", "ret_out", "conf_out", "w_o", "ffn_w1", "ffn_w2")
    return {
        "w_hy": w[:, :, :hy], "w_cq": w[:, :, hy:hy + MLA_Q_LORA], "w_ck": ck,
        "w_ret": w[:, :, mla:ret], "w_conf": w[:, :, ret:],
        "mla_w_uq": w_uq.astype(BF16), "mla_w_ukv": w_ukv.astype(BF16),
        "hy_w1": jnp.pad(p["hy_w1"], ((0, 0), (0, 128 - HY_EMB), (0, 0))),
        **{name: row(name) for name in rows},
        **{name: p[name] for name in as_is},
        **{name: p[name].astype(BF16) for name in bf16},
    }


def kernel(x_prompt, x_sample, cache_mla_ckv, cache_mla_krope, state_ret, c, c_ctx, ada_w, ada_b, norm1_g, w_in, hy_conv_w, hy_conv_b, hy_w1, hy_b1, hy_w2, hy_b2, hy_w3, hy_bias, hy_out, mla_q_norm, mla_w_uq, mla_kv_norm, mla_w_ukv, mla_out, ret_decay, ret_out, conf_dw_w, conf_dw_b, conf_ln_g, conf_ln_b, conf_out, gate_w, gate_b, w_o, norm2_g, ffn_w1, ffn_w2, final_norm_g):
    p = dict(norm1_g=norm1_g, hy_conv_w=hy_conv_w, hy_conv_b=hy_conv_b, hy_w1=hy_w1, hy_b1=hy_b1,
             hy_w2=hy_w2, hy_b2=hy_b2, hy_w3=hy_w3, hy_bias=hy_bias, hy_out=hy_out,
             mla_q_norm=mla_q_norm, mla_w_uq=mla_w_uq, mla_kv_norm=mla_kv_norm, mla_w_ukv=mla_w_ukv,
             mla_out=mla_out, ret_decay=ret_decay, ret_out=ret_out, conf_dw_w=conf_dw_w,
             conf_dw_b=conf_dw_b, conf_ln_g=conf_ln_g, conf_ln_b=conf_ln_b, conf_out=conf_out,
             gate_w=gate_w, gate_b=gate_b, w_o=w_o, norm2_g=norm2_g, ffn_w1=ffn_w1, ffn_w2=ffn_w2)
    Bp, Lp, D = x_prompt.shape
    Bs, Ls, _ = x_sample.shape
    depth = w_in.shape[0]

    cond = jnp.concatenate([c_ctx[None, :], c, jnp.zeros((MOD_ROWS - 1 - Bs, D), F32)], axis=0)
    mod = _modulation(cond, ada_w, ada_b)
    tables_p = _dft_tables(Lp)
    tables_s = _dft_tables(Ls)
    final_g = final_norm_g.reshape(1, D)

    xp = x_prompt.reshape(Bp * Lp, D)
    xs = x_sample.reshape(Bs * Ls, D)
    ckvs, kropes, rets = [], [], []
    weights = _stacked_weights(w_in, p)
    for l in range(depth):
        lw = dict(weights, layer=l)
        final = l == depth - 1
        xp, ckv, kr, S = _trunk_layer(xp, mod, lw, final_g, tables_p, Bp, Lp, None, None, final)
        ckvs.append(ckv.reshape(Bp, Lp, MLA_KV_LORA))
        kropes.append(kr.reshape(Bp, Lp, MLA_ROPE))
        rets.append(S)
        xs, _, _, _ = _trunk_layer(xs, mod, lw, final_g, tables_s, Bs, Ls,
                                   (cache_mla_ckv, cache_mla_krope), state_ret, final)
    return (xp.reshape(Bp, Lp, D), xs.reshape(Bs, Ls, D),
            jnp.stack(ckvs, axis=1), jnp.stack(kropes, axis=1), jnp.stack(rets, axis=1))
```

```python
import functools
import math

import jax
import jax.numpy as jnp
from jax import lax
from jax.experimental import pallas as pl
from jax.experimental.pallas import tpu as pltpu

F32 = jnp.float32
BF16 = jnp.bfloat16

D_MODEL = 1024
DEPTH = 2
PAST_LEN = 256
EPS = 1e-6
GRID_W = 64

HY_WIDTH = 256
HY_EMB = 33
HY_BANDS = (HY_EMB - 1) // 2
HY_FFN = 64
HY_FAST_DECAY = 0.3
HY_SLOW_DECAY = 1.5
HY_TARGET = 1e-2

MLA_HEADS = 4
MLA_Q_LORA = 256
MLA_KV_LORA = 128
MLA_NOPE = 64
MLA_ROPE = 32
MLA_V = 64
ROPE_BASE = 10000.0

RET_HEADS = 4
RET_DK = 64
RET_DV = 64
RET_BLOCK = 256

CONF_WIDTH = 256
CONF_KERNEL = 31

D_FF = ((8 * D_MODEL // 3 + 255) // 256) * 256
N_BRANCH = 4

HY_COLS = 3 * HY_WIDTH
MLA_COLS = MLA_Q_LORA + MLA_KV_LORA + MLA_ROPE
RET_COLS = 2 * RET_HEADS * RET_DK + 2 * RET_HEADS * RET_DV
CONF_COLS = 2 * CONF_WIDTH

VMEM_BYTES_V7X = 64 * 1024 * 1024
SUBLANES = 8
TOKEN_TILE = 512
WIDE_TOKEN_TILE = 1024
Q_TILE = 512
MLA_SOFTMAX_ROWS = 16
DFT_TILE = 512
SHORT_SEQ_ROWS = 1024
HYENA_GROUP_ROWS = 1024
HYENA_ROWS = 1024
CONF_ROWS = 128
CONF_HALO = 16
MXU_DIM_V7X = 256
FFN_CHUNK = 4 * MXU_DIM_V7X
MOD_ROWS = 8


def _cparams(sem, vmem_mb):
    return pltpu.CompilerParams(dimension_semantics=sem, vmem_limit_bytes=vmem_mb * 1024 * 1024)


def _dot(a, b):
    return jnp.dot(a, b, preferred_element_type=F32)


def _dot_nt(a, b):
    return lax.dot_general(a, b, (((1,), (1,)), ((), ())), preferred_element_type=F32)


def _dot_exact(a, b):
    return jnp.dot(a, b, preferred_element_type=F32, precision=lax.Precision.HIGHEST)


def _dot_split(a, b):
    a_hi, b_hi = a.astype(BF16), b.astype(BF16)
    a_lo = (a - a_hi.astype(F32)).astype(BF16)
    b_lo = (b - b_hi.astype(F32)).astype(BF16)
    return _dot(a_hi, b_hi) + (_dot(a_lo, b_hi) + _dot(a_hi, b_lo))


def _rms(x, g):
    return x * lax.rsqrt(jnp.mean(x * x, axis=-1, keepdims=True) + EPS) * g


def _silu(x):
    return x * jax.nn.sigmoid(x)


def _mod_kernel(c_ref, w_ref, b_ref, o_ref):
    s = _silu(c_ref[...]).astype(BF16)
    o_ref[...] = _dot(s, w_ref[...].astype(BF16)) + b_ref[...]


def _modulation(cond, ada_w, ada_b):
    depth, d, cols = ada_w.shape
    blk = 1024
    out = pl.pallas_call(
        _mod_kernel,
        grid=(depth, cols // blk),
        in_specs=[
            pl.BlockSpec((MOD_ROWS, d), lambda l, j: (0, 0)),
            pl.BlockSpec((None, d, blk), lambda l, j: (l, 0, j)),
            pl.BlockSpec((None, 1, blk), lambda l, j: (l, 0, j)),
        ],
        out_specs=pl.BlockSpec((None, MOD_ROWS, blk), lambda l, j: (l, 0, j)),
        out_shape=jax.ShapeDtypeStruct((depth, MOD_ROWS, cols), F32),
        compiler_params=_cparams(("arbitrary", "arbitrary"), 32),
        name="modulation",
    )(cond, ada_w, ada_b.reshape(depth, 1, cols))
    return out.reshape(depth, MOD_ROWS, 6, d)


def _seqs_per_step(B, L):
    return max(1, min(B, SHORT_SEQ_ROWS // L))


def _per_layer(a, layer, single_buffer=False):
    mode = dict(pipeline_mode=pl.Buffered(1)) if single_buffer else {}
    return pl.BlockSpec((None,) + a.shape[1:], lambda *_: (layer,) + (0,) * (a.ndim - 1), **mode)


def _mod_spec(layer, tiles_per_seq, latent):
    group = (lambda i: 1 + i // tiles_per_seq) if latent else (lambda i: 0)
    return pl.BlockSpec((None, None, 6, D_MODEL), lambda i: (layer, group(i), 0, 0))


HEAD_LANES = 128


def _rope_lanes(x, cos, sa, sb):
    reps = x.shape[1] // HEAD_LANES
    wide = lambda t: jnp.concatenate([t] * reps, axis=1) if reps > 1 else t
    half = MLA_ROPE // 2
    return (x * wide(cos) + pltpu.roll(x, x.shape[1] - half, 1) * wide(sa)
            + pltpu.roll(x, half, 1) * wide(sb))


def _inproj_kernel(*refs, seqs_per_tile, seq_len, latent):
    x_ref, mod_ref, g1_ref, win_ref, qn_ref, kvn_ref, wuq_ref, wukv_ref = refs[:8]
    refs = refs[8:]
    c_cq = HY_COLS
    c_ck = c_cq + MLA_Q_LORA
    c_ret = c_ck + MLA_KV_LORA + HEAD_LANES
    c_conf = c_ret + RET_COLS
    why_ref, wcq_ref, wck_ref, wret_ref, wconf_ref = (
        win_ref.at[:, a:b] for a, b in ((0, c_cq), (c_cq, c_ck), (c_ck, c_ret), (c_ret, c_conf),
                                        (c_conf, c_conf + CONF_COLS)))
    if latent:
        cos_ref, sa_ref, sb_ref = refs[:3]
        refs = refs[3:]
    hv_ref, hx1_ref, hx2_ref, q_ref, kh_ref, vh_ref, ret_ref, conf_ref, rkt_ref = refs[:9]
    if not latent:
        ckv_ref, kr_ref = refs[9:]
    x = x_ref[...]
    h = _rms(x, g1_ref[...]) * (1.0 + mod_ref[1:2, :]) + mod_ref[0:1, :]
    hb = h.astype(BF16)

    u = _dot(hb, why_ref[...])
    for p, o_ref in enumerate((hv_ref, hx1_ref, hx2_ref)):
        part = u[:, p * HY_WIDTH:(p + 1) * HY_WIDTH]
        if seqs_per_tile == 1:
            o_ref[...] = part
        else:
            for s in range(seqs_per_tile):
                o_ref[:, s * HY_WIDTH:(s + 1) * HY_WIDTH] = part[s * seq_len:(s + 1) * seq_len]

    heads_w = MLA_HEADS * HEAD_LANES
    cq = _dot(hb, wcq_ref[...])
    q = _dot(_rms(cq, qn_ref[...]).astype(BF16), wuq_ref[...])
    ck = _dot(hb, wck_ref[...])
    ckv = _rms(ck[:, :MLA_KV_LORA], kvn_ref[...])
    kr_block = ck[:, MLA_KV_LORA:MLA_KV_LORA + HEAD_LANES]
    if latent:
        q = _rope_lanes(q, cos_ref[...], sa_ref[...], sb_ref[...])
        kr_block = _rope_lanes(kr_block, cos_ref[...], sa_ref[...], sb_ref[...])
    else:
        ckv_ref[...] = ckv
        kr_ref[...] = kr_block[:, MLA_NOPE:MLA_NOPE + MLA_ROPE]
    q_ref[...] = (q * ((MLA_NOPE + MLA_ROPE) ** -0.5 * math.log2(math.e))).astype(BF16)
    kv = _dot(ckv.astype(BF16), wukv_ref[...])
    kh_ref[...] = (kv[:, :heads_w] + jnp.concatenate([kr_block] * MLA_HEADS, axis=1)).astype(BF16)
    lane = lax.broadcasted_iota(jnp.int32, (1, heads_w), 1)
    ones_lane = jnp.where(lane % HEAD_LANES == MLA_V, 1.0, 0.0)
    vh_ref[...] = (kv[:, heads_w:] + ones_lane).astype(BF16)
    ret = _dot(hb, wret_ref[...])
    ret_ref[...] = ret
    qk = RET_HEADS * RET_DK
    rkt_ref[...] = ret[:, qk:2 * qk].T
    conf_ref[...] = _dot(hb, wconf_ref[...])


def _inproj(x, mod, lw, B, L, latent):
    T = B * L
    TM = TOKEN_TILE
    nt = T // TM
    if L >= TM:
        tiles_per_seq, seqs_per_tile = L // TM, 1
        hy_block = (TM, HY_WIDTH)
        hy_map = lambda i: (i % tiles_per_seq, i // tiles_per_seq)
    else:
        tiles_per_seq, seqs_per_tile = 1, TM // L
        hy_block = (L, seqs_per_tile * HY_WIDTH)
        hy_map = lambda i: (0, i)
    row = lambda w: pl.BlockSpec((TM, w), lambda i: (i, 0))
    weights = (lw["norm1_g"], lw["w_in"], lw["mla_q_norm"], lw["mla_kv_norm"], lw["mla_w_uq"], lw["mla_w_ukv"])
    hy_shape = jax.ShapeDtypeStruct((L, B * HY_WIDTH), F32)
    qk = RET_HEADS * RET_DK
    heads_w = MLA_HEADS * HEAD_LANES
    ins = [x, mod, *weights]
    in_specs = ([row(D_MODEL), _mod_spec(lw["layer"], tiles_per_seq, latent)]
                + [_per_layer(w, lw["layer"]) for w in weights])
    out_specs = ([pl.BlockSpec(hy_block, hy_map)] * 3
                 + [row(heads_w)] * 3 + [row(RET_COLS), row(CONF_COLS), pl.BlockSpec((qk, TM), lambda i: (0, i))])
    out_shape = ([hy_shape] * 3 + [jax.ShapeDtypeStruct((T, heads_w), BF16)] * 3
                 + [jax.ShapeDtypeStruct((T, RET_COLS), F32), jax.ShapeDtypeStruct((T, CONF_COLS), F32),
                    jax.ShapeDtypeStruct((qk, T), F32)])
    if latent:
        ins += list(_rope_lane_tables(L))
        in_specs += [pl.BlockSpec((TM, HEAD_LANES), lambda i: (i % tiles_per_seq, 0))] * 3
    else:
        out_specs += [row(MLA_KV_LORA), row(MLA_ROPE)]
        out_shape += [jax.ShapeDtypeStruct((T, MLA_KV_LORA), F32), jax.ShapeDtypeStruct((T, MLA_ROPE), F32)]
    return pl.pallas_call(
        functools.partial(_inproj_kernel, seqs_per_tile=seqs_per_tile, seq_len=L, latent=latent),
        grid=(nt,),
        in_specs=in_specs,
        out_specs=out_specs,
        out_shape=out_shape,
        compiler_params=_cparams(("arbitrary",), 48),
        name="inproj",
    )(*ins)


def _dft_tables(L):
    N = 2 * L
    k_lo = min(L, 32)
    k_hi = L // k_lo
    t = jnp.arange(L, dtype=jnp.int32)[None, :]

    def cs(k):
        m = (k[:, None] * t) % N
        ang = m.astype(F32) * (2.0 * math.pi / N)
        return jnp.cos(ang), jnp.sin(ang)

    ca, sa = cs(jnp.arange(k_hi, dtype=jnp.int32) * k_lo)
    cb, sb = cs(jnp.arange(k_lo, dtype=jnp.int32))
    cos = (ca[:, None, :] * cb[None, :, :] - sa[:, None, :] * sb[None, :, :]).reshape(L, L)
    sin = (sa[:, None, :] * cb[None, :, :] + ca[:, None, :] * sb[None, :, :]).reshape(L, L)
    return cos.astype(BF16), (-sin).astype(BF16)


def _filter_features(L):
    t = jnp.linspace(0.0, 1.0, L, dtype=F32)[:, None]
    w = 2.0 * math.pi * jnp.arange(L, dtype=F32)[:, None] / L
    f = jnp.linspace(1e-4, HY_BANDS - 1, HY_BANDS, dtype=F32)[None, :]
    z = jnp.concatenate([t, jnp.cos(f * w), -jnp.sin(f * w)], axis=-1)
    z = jnp.pad(z, ((0, 0), (0, 128 - HY_EMB)))
    max_decay = math.log(HY_TARGET) / HY_FAST_DECAY
    min_decay = math.log(HY_TARGET) / HY_SLOW_DECAY
    deltas = jnp.abs(jnp.linspace(min_decay, max_decay, HY_WIDTH, dtype=F32))
    decay = jnp.exp(-t * deltas[None, :])
    return z, decay


def _alternating(rows):
    t = lax.broadcasted_iota(jnp.int32, (rows, 1), 0)
    return (1 - 2 * (t & 1)).astype(F32)


def _filter_kernel(z_ref, dec_ref, w1_ref, b1_ref, w2_ref, b2_ref, w3_ref, cos_ref, sin_ref,
                   kr_ref, ki_ref, kn_ref, h_ref, *, L):
    j = pl.program_id(0)
    W = HY_WIDTH
    N = 2 * L

    @pl.when(j == 0)
    def _():
        h = jnp.sin(_dot_exact(z_ref[...], w1_ref[...]) + b1_ref[...])
        h = jnp.sin(_dot_exact(h, w2_ref[...]) + b2_ref[...])
        h = _dot_split(h, w3_ref[...]) * jnp.concatenate([dec_ref[...]] * 4, axis=1)
        cs = jnp.sum(jnp.abs(h), axis=0, keepdims=True)
        s0 = cs[:, 0:W] + cs[:, W:2 * W]
        s1 = cs[:, 2 * W:3 * W] + cs[:, 3 * W:4 * W]
        h = h / jnp.concatenate([s0, s0, s1, s1], axis=1)
        row = lax.broadcasted_iota(jnp.int32, h.shape, 0)
        col = lax.broadcasted_iota(jnp.int32, h.shape, 1)
        backward = (col // W) % 2 == 1
        h = jnp.where(backward & (row == 0), 0.0, h)
        h_ref[...] = h.astype(BF16)
        nyq = jnp.sum(h * _alternating(L), axis=0, keepdims=True) * (1.0 / N)
        for o in range(2):
            c = 2 * o * W
            kn_ref[o] = nyq[:, c:c + W] + nyq[:, c + W:c + 2 * W]

    hb = h_ref[...]
    sr = _dot(cos_ref[...], hb)
    si = _dot(sin_ref[...], hb)
    row = lax.broadcasted_iota(jnp.int32, (sr.shape[0], W), 0)
    scale = jnp.where((row == 0) & (j == 0), 1.0 / N, 2.0 / N)
    for o in range(2):
        c = 2 * o * W
        kr_ref[o] = (sr[:, c:c + W] + sr[:, c + W:c + 2 * W]) * scale
        ki_ref[o] = (si[:, c:c + W] - si[:, c + W:c + 2 * W]) * scale


def _filter_spectra(lw, L, tables):
    z, decay = _filter_features(L)
    cos, msin = tables
    Tk = min(DFT_TILE, L)
    full = lambda a: pl.BlockSpec(a.shape, lambda j: (0,) * a.ndim)
    mlp = (lw["hy_w1"], lw["hy_b1"], lw["hy_w2"], lw["hy_b2"], lw["hy_w3"])
    ins = (z, decay, *mlp)
    tile = pl.BlockSpec((Tk, L), lambda j: (j, 0))
    spec = pl.BlockSpec((2, Tk, HY_WIDTH), lambda j: (0, j, 0))
    return pl.pallas_call(
        functools.partial(_filter_kernel, L=L),
        grid=(L // Tk,),
        in_specs=[full(z), full(decay)] + [_per_layer(a, lw["layer"]) for a in mlp] + [tile, tile],
        out_specs=[spec, spec, pl.BlockSpec((2, 1, HY_WIDTH), lambda j: (0, 0, 0))],
        out_shape=[jax.ShapeDtypeStruct((2, L, HY_WIDTH), F32)] * 2
        + [jax.ShapeDtypeStruct((2, 1, HY_WIDTH), F32)],
        scratch_shapes=[pltpu.VMEM((L, 4 * HY_WIDTH), BF16)],
        compiler_params=_cparams(("arbitrary",), 48),
        name="hyena_filter",
    )(*ins, cos, msin)


def _hyena_kernel(v_ref, x1_ref, x2_ref, cw_ref, cb_ref, bias_ref, kr_ref, ki_ref, kn_ref,
                  cos_ref, sin_ref, y_ref, cur_ref, curb_ref, yr_ref, yi_ref, gate_ref,
                  *, L, W, T):
    reps = W // HY_WIDTH
    tiled = lambda a: jnp.concatenate([a] * reps, axis=1) if reps > 1 else a
    alt = _alternating(L)

    def short_conv(u_ref, p):
        u = u_ref[...]
        row = lax.broadcasted_iota(jnp.int32, (L, W), 0)
        prev = jnp.where(row == 0, 0.0, pltpu.roll(u, 1, 0))
        nxt = jnp.where(row == L - 1, 0.0, pltpu.roll(u, L - 1, 0))
        cols = slice(p * HY_WIDTH, (p + 1) * HY_WIDTH)
        w = [tiled(cw_ref[k:k + 1, cols]) for k in range(3)]
        return prev * w[0] + u * w[1] + nxt * w[2] + tiled(cb_ref[:, cols])

    cur_ref[...] = short_conv(v_ref, 0)
    for o, x_ref in enumerate((x1_ref, x2_ref)):
        cur = cur_ref[...]
        curb_ref[...] = cur.astype(BF16)
        gate_ref[...] = short_conv(x_ref, o + 1)
        nyq = jnp.sum(cur * alt, axis=0, keepdims=True) * tiled(kn_ref[o])
        for f in range(L // T):
            rows = slice(f * T, (f + 1) * T)
            cb = curb_ref[...]
            xr = _dot(cos_ref[rows, :], cb)
            xi = _dot(sin_ref[rows, :], cb)
            kr, ki = tiled(kr_ref[o, rows, :]), tiled(ki_ref[o, rows, :])
            yr_ref[rows, :] = (xr * kr - xi * ki).astype(BF16)
            yi_ref[rows, :] = (xr * ki + xi * kr).astype(BF16)
        for t in range(L // T):
            rows = slice(t * T, (t + 1) * T)
            conv = (_dot(cos_ref[rows, :], yr_ref[...]) + _dot(sin_ref[rows, :], yi_ref[...])
                    + alt[rows] * nyq + cur_ref[rows, :] * tiled(bias_ref[o:o + 1, :]))
            out = gate_ref[rows, :] * conv
            if o == 0:
                cur_ref[rows, :] = out
            else:
                y_ref[rows, :] = out


def _hyena(hv, hx1, hx2, lw, spectra, tables, B, L):
    kr, ki, kn = spectra
    cos, msin = tables
    T = min(HYENA_ROWS, L)
    W = HY_WIDTH * max(1, min(B, HYENA_GROUP_ROWS // L))
    ng = (B * HY_WIDTH) // W
    col = pl.BlockSpec((L, W), lambda g: (0, g))
    once = lambda a: pl.BlockSpec(a.shape, lambda g: (0,) * a.ndim, pipeline_mode=pl.Buffered(1))
    params = (lw["hy_conv_w"], lw["hy_conv_b"], lw["hy_bias"])
    consts = (kr, ki, kn, cos, msin)
    return pl.pallas_call(
        functools.partial(_hyena_kernel, L=L, W=W, T=T),
        grid=(ng,),
        in_specs=[col, col, col] + [_per_layer(a, lw["layer"]) for a in params] + [once(a) for a in consts],
        out_specs=col,
        out_shape=jax.ShapeDtypeStruct((L, B * HY_WIDTH), F32),
        scratch_shapes=[pltpu.VMEM((L, W), F32), pltpu.VMEM((L, W), BF16), pltpu.VMEM((L, W), BF16),
                        pltpu.VMEM((L, W), BF16), pltpu.VMEM((L, W), F32)],
        compiler_params=_cparams(("arbitrary",), 60),
        name="hyena_conv",
    )(hv, hx1, hx2, *params, *consts)


def _mla_kernel(*refs, L, latent, seqs):
    if latent:
        q_ref, kh_ref, vh_ref, cckv_ref, ckr_ref, wukv_ref, o_ref, s_ref, p_ref, kctx_ref, vctx_ref = refs
    else:
        q_ref, kh_ref, vh_ref, o_ref, s_ref, p_ref = refs
    TQ = q_ref.shape[0] // seqs
    heads_w = MLA_HEADS * HEAD_LANES
    block = lambda h: slice(h * HEAD_LANES, (h + 1) * HEAD_LANES)

    if latent:
        @pl.when(pl.program_id(1) == 0)
        def _():
            kv = _dot(cckv_ref[...].astype(BF16), wukv_ref[...])
            zeros = lambda w: jnp.zeros((PAST_LEN, w), F32)
            kr_block = jnp.concatenate(
                [zeros(MLA_NOPE), ckr_ref[...], zeros(HEAD_LANES - MLA_NOPE - MLA_ROPE)], axis=1)
            kctx_ref[...] = (kv[:, :heads_w] + jnp.concatenate([kr_block] * MLA_HEADS, axis=1)).astype(BF16)
            lane = lax.broadcasted_iota(jnp.int32, (1, heads_w), 1)
            vctx_ref[...] = (kv[:, heads_w:] + jnp.where(lane % HEAD_LANES == MLA_V, 1.0, 0.0)).astype(BF16)

    def attend(q_rows, k_rows):
        def scores(h):
            qh = q_ref[q_rows, block(h)]
            s_ref[h, :, 0:L] = _dot_nt(qh, kh_ref[k_rows, block(h)])
            if latent:
                s_ref[h, :, L:L + PAST_LEN] = _dot_nt(qh, kctx_ref[:, block(h)])

        outs = []
        scores(0)
        for h in range(MLA_HEADS):
            if h + 1 < MLA_HEADS:
                scores(h + 1)
            for r in range(TQ // MLA_SOFTMAX_ROWS):
                rows = slice(r * MLA_SOFTMAX_ROWS, (r + 1) * MLA_SOFTMAX_ROWS)
                s = s_ref[h, rows, :]
                p_ref[h, rows, :] = jnp.exp2(s - jnp.max(s, axis=-1, keepdims=True)).astype(BF16)
            pv = _dot(p_ref[h, :, 0:L], vh_ref[k_rows, block(h)])
            if latent:
                pv = pv + _dot(p_ref[h, :, L:L + PAST_LEN], vctx_ref[:, block(h)])
            outs.append(pv[:, :MLA_V] / pv[:, MLA_V:MLA_V + 1])
        o_ref[q_rows, :] = jnp.concatenate(outs, axis=-1)

    for sq in range(seqs):
        attend(slice(sq * TQ, (sq + 1) * TQ), slice(sq * L, (sq + 1) * L))


def _rope_tables(L):
    rows = L // GRID_W
    row = jnp.repeat(jnp.arange(rows), GRID_W).astype(F32)
    col = jnp.tile(jnp.arange(GRID_W), rows).astype(F32)
    per_axis = MLA_ROPE // 4
    inv = ROPE_BASE ** (-jnp.arange(per_axis, dtype=F32) / per_axis)
    ang = jnp.concatenate([row[:, None] * inv, col[:, None] * inv], axis=-1)
    return jnp.cos(ang), jnp.sin(ang)


def _rope_lane_tables(L):
    cos, sin = _rope_tables(L)
    pad = HEAD_LANES - MLA_NOPE - MLA_ROPE
    one, zero = jnp.ones((L, MLA_NOPE), F32), jnp.zeros((L, MLA_NOPE), F32)
    half0 = jnp.zeros_like(sin)
    cos_t = jnp.concatenate([one, cos, cos, one[:, :pad]], axis=1)
    sa_t = jnp.concatenate([zero, -sin, half0, zero[:, :pad]], axis=1)
    sb_t = jnp.concatenate([zero, half0, sin, zero[:, :pad]], axis=1)
    return cos_t, sa_t, sb_t


def _mla(qh, kh, vh, lw, B, L, ctx):
    latent = ctx is not None
    TQ = min(Q_TILE, L)
    nq = L // TQ
    seqs = 1 if latent or nq > 1 else _seqs_per_step(B, L)
    Lk = L + PAST_LEN if latent else L
    heads_w = MLA_HEADS * HEAD_LANES
    seq = pl.BlockSpec((seqs * L, heads_w), lambda b, i: (b, 0))
    ins = [qh, kh, vh]
    specs = [pl.BlockSpec((seqs * TQ, heads_w), lambda b, i: (b * nq + i, 0)), seq, seq]
    scratch = [pltpu.VMEM((MLA_HEADS, TQ, Lk), F32), pltpu.VMEM((MLA_HEADS, TQ, Lk), BF16)]
    if latent:
        ins += [ctx[0], ctx[1], lw["mla_w_ukv"]]
        layer = lw["layer"]
        specs += [pl.BlockSpec((None, None, PAST_LEN, MLA_KV_LORA), lambda b, i: (b, layer, 0, 0)),
                  pl.BlockSpec((None, None, PAST_LEN, MLA_ROPE), lambda b, i: (b, layer, 0, 0)),
                  _per_layer(lw["mla_w_ukv"], lw["layer"])]
        scratch += [pltpu.VMEM((PAST_LEN, heads_w), BF16)] * 2
    return pl.pallas_call(
        functools.partial(_mla_kernel, L=L, latent=latent, seqs=seqs),
        grid=(B // seqs, nq),
        in_specs=specs,
        out_specs=pl.BlockSpec((seqs * TQ, MLA_HEADS * MLA_V), lambda b, i: (b * nq + i, 0)),
        out_shape=jax.ShapeDtypeStruct((B * L, MLA_HEADS * MLA_V), F32),
        scratch_shapes=scratch,
        compiler_params=_cparams(("arbitrary", "arbitrary"), 48),
        name="mla_attention",
    )(*ins)


def _ret_kernel(*refs, L, seqs, has_state, emit_state):
    refs = list(refs)
    u_ref, kt_ref, dl_ref = refs[:3]
    pos = 3
    s0_ref = None
    if has_state:
        s0_ref = refs[pos]
        pos += 1
    y_ref = refs[pos]
    pos += 1
    sout_ref = None
    if emit_state:
        sout_ref = refs[pos]
        pos += 1
    of_ref, kv_ref, sin_ref, dcomb_ref, dtab_ref, hmask_ref, avg_ref = refs[pos:]

    C = RET_BLOCK
    n = L // C
    H, DK, DV = RET_HEADS, RET_DK, RET_DV
    qk = H * DK
    DK_F, DK_B, DQ_F, DQ_B, DC_F, DC_B, MASK = range(7)
    use_cross = has_state or n > 1

    @pl.when(pl.program_id(0) == 0)
    def _():
        x = dl_ref[...]
        log_g = jnp.minimum(x, 0.0) - jnp.log1p(jnp.exp(-jnp.abs(x)))
        gf = [log_g[0:1, h:h + 1] for h in range(H)]
        gb = [log_g[1:2, h:h + 1] for h in range(H)]
        diff = (lax.broadcasted_iota(jnp.int32, (C, C), 0)
                - lax.broadcasted_iota(jnp.int32, (C, C), 1)).astype(F32)
        idx = lax.broadcasted_iota(jnp.int32, (C, 1), 0).astype(F32)
        for h in range(H):
            dcomb_ref[h] = (jnp.where(diff >= 0, jnp.exp(jnp.maximum(diff, 0.0) * gf[h]), 0.0)
                            + jnp.where(diff <= 0, jnp.exp(jnp.maximum(-diff, 0.0) * gb[h]), 0.0))

        def head_cols(lag, g):
            return jnp.concatenate([jnp.broadcast_to(jnp.exp(lag * g[h]), (C, DK)) for h in range(H)], axis=1)

        def head_rows(lag, g):
            return jnp.concatenate(
                [jnp.broadcast_to(jnp.exp(lag * g[h]) * (DK ** -0.5), (DK, C)) for h in range(H)], axis=0)

        tok = lax.broadcasted_iota(jnp.int32, (1, C), 1).astype(F32)
        dtab_ref[DK_F] = head_rows(C - 1.0 - tok, gf)
        dtab_ref[DK_B] = head_rows(tok, gb)
        dtab_ref[DQ_F] = head_cols(idx + 1.0, gf)
        dtab_ref[DQ_B] = head_cols(C - idx, gb)
        row_head = lax.broadcasted_iota(jnp.int32, (qk, qk), 0) // DK
        col_head = lax.broadcasted_iota(jnp.int32, (qk, qk), 1) // DV
        same_head = row_head == col_head
        for slot, g in ((DC_F, gf), (DC_B, gb)):
            dc = jnp.zeros((qk, qk), F32)
            for h in range(H):
                dc = jnp.where(same_head & (row_head == h), jnp.exp(C * g[h]), dc)
            dtab_ref[slot] = dc
        dtab_ref[MASK] = same_head.astype(F32)
        avg_ref[...] = jnp.where(same_head, 1.0 / DV, 0.0).astype(BF16)
        lane_head = lax.broadcasted_iota(jnp.int32, (C, qk), 1) // DK
        for h in range(H):
            hmask_ref[h] = (lane_head == h).astype(F32)

    def head_mean(a):
        total = None
        for _ in range(3):
            part = a.astype(BF16)
            a = a - part.astype(F32)
            term = _dot(part, avg_ref[...])
            total = term if total is None else total + term
        return total

    for sq in range(seqs):
        chunk = lambda c: slice(sq * L + c * C, sq * L + (c + 1) * C)

        for c in range(n):
            rows = chunk(c)
            q = u_ref[rows, 0:qk]
            kb = (u_ref[rows, qk:2 * qk] * (DK ** -0.5)).astype(BF16)
            v = u_ref[rows, 2 * qk:2 * qk + H * DV]
            att = [(_dot_nt((q * hmask_ref[h]).astype(BF16), kb) * dcomb_ref[h]).astype(BF16)
                   for h in range(H)]
            v_heads = jnp.concatenate([(v * hmask_ref[h]).astype(BF16) for h in range(H)], axis=0)
            of_ref[c * C:(c + 1) * C, :] = _dot(jnp.concatenate(att, axis=1), v_heads)
            vb = v.astype(BF16)
            kt = kt_ref[:, rows]
            for d, slot in ((0, DK_F), (1, DK_B)):
                kv_ref[d, c] = _dot((kt * dtab_ref[slot]).astype(BF16), vb)

        finals = []
        for d, slot in ((0, DC_F), (1, DC_B)):
            if has_state:
                zero = jnp.zeros((DK, DV), F32)
                S = jnp.concatenate(
                    [jnp.concatenate([s0_ref[sq, d, h] if g == h else zero for g in range(H)], axis=1)
                     for h in range(H)], axis=0)
            else:
                S = jnp.zeros((qk, H * DV), F32)
            for c in (range(n) if d == 0 else reversed(range(n))):
                if use_cross:
                    sin_ref[d, c] = S.astype(BF16)
                S = S * dtab_ref[slot] + kv_ref[d, c] * dtab_ref[MASK]
            finals.append(S)
        if emit_state:
            for d in range(2):
                for h in range(H):
                    sout_ref[sq, d, h] = finals[d][h * DK:(h + 1) * DK, h * DV:(h + 1) * DV]

        for c in range(n):
            rows = chunk(c)
            tot = of_ref[c * C:(c + 1) * C, :]
            if use_cross:
                qb = u_ref[rows, 0:qk].astype(BF16)
                tot = (tot + _dot(qb, sin_ref[0, c]) * dtab_ref[DQ_F]
                       + _dot(qb, sin_ref[1, c]) * dtab_ref[DQ_B])
            xc = tot - head_mean(tot)
            normed = xc * lax.rsqrt(head_mean(xc * xc) + EPS)
            gate = u_ref[rows, 2 * qk + H * DV:2 * qk + 2 * H * DV]
            y_ref[rows, :] = _silu(gate) * normed


def _retention(u_ret, kt_ret, lw, B, L, state, emit_state):
    has_state = state is not None
    seqs = _seqs_per_step(B, L)
    st_dims = (2, RET_HEADS, RET_DK, RET_DV)
    ins = [u_ret, kt_ret, lw["ret_decay"]]
    specs = [pl.BlockSpec((seqs * L, RET_COLS), lambda b: (b, 0)),
             pl.BlockSpec((RET_HEADS * RET_DK, seqs * L), lambda b: (0, b)),
             _per_layer(lw["ret_decay"], lw["layer"])]
    if has_state:
        layer = lw["layer"]
        ins.append(state)
        specs.append(pl.BlockSpec((seqs, None) + st_dims, lambda b: (b, layer, 0, 0, 0, 0)))
    vd = RET_HEADS * RET_DV
    out_specs = [pl.BlockSpec((seqs * L, vd), lambda b: (b, 0))]
    out_shape = [jax.ShapeDtypeStruct((B * L, vd), F32)]
    if emit_state:
        out_specs.append(pl.BlockSpec((seqs,) + st_dims, lambda b: (b, 0, 0, 0, 0)))
        out_shape.append(jax.ShapeDtypeStruct((B,) + st_dims, F32))
    res = pl.pallas_call(
        functools.partial(_ret_kernel, L=L, seqs=seqs, has_state=has_state, emit_state=emit_state),
        grid=(B // seqs,),
        in_specs=specs,
        out_specs=out_specs,
        out_shape=out_shape,
        scratch_shapes=[pltpu.VMEM((L, vd), F32),
                        pltpu.VMEM((2, L // RET_BLOCK, vd, vd), F32),
                        pltpu.VMEM((2, L // RET_BLOCK, vd, vd), BF16),
                        pltpu.VMEM((RET_HEADS, RET_BLOCK, RET_BLOCK), F32),
                        pltpu.VMEM((7, RET_BLOCK, vd), F32),
                        pltpu.VMEM((RET_HEADS, RET_BLOCK, vd), F32),
                        pltpu.VMEM((vd, vd), BF16)],
        compiler_params=_cparams(("arbitrary",), 48),
        name="retention",
    )(*ins)
    return (res[0], res[1]) if emit_state else (res[0], None)


def _conf_kernel(u_ref, w_ref, b_ref, g_ref, be_ref, y_ref, zp_ref, sh_ref, *, L, seqs):
    Wc = CONF_WIDTH
    halo = CONF_HALO
    zp_ref[0:halo, :] = jnp.zeros((halo, Wc), F32)
    zp_ref[halo + L:2 * halo + L, :] = jnp.zeros((halo, Wc), F32)
    first = halo - CONF_KERNEL // 2
    span = sh_ref.shape[1]
    R = CONF_ROWS
    for sq in range(seqs):
        r0 = sq * L
        zp_ref[halo:halo + L, :] = u_ref[r0:r0 + L, 0:Wc] * jax.nn.sigmoid(u_ref[r0:r0 + L, Wc:2 * Wc])
        for s in range(SUBLANES):
            sh_ref[s] = zp_ref[first + s:first + s + span, :]
        for c in range(L // R):
            acc = jnp.broadcast_to(b_ref[...], (R, Wc))
            for k in range(CONF_KERNEL):
                a, s = divmod(k, SUBLANES)
                acc = acc + w_ref[k:k + 1, :] * sh_ref[s, c * R + SUBLANES * a:c * R + SUBLANES * a + R, :]
            mu = jnp.mean(acc, axis=-1, keepdims=True)
            xc = acc - mu
            z = xc * lax.rsqrt(jnp.mean(xc * xc, axis=-1, keepdims=True) + EPS) * g_ref[...] + be_ref[...]
            y_ref[r0 + c * R:r0 + (c + 1) * R, :] = _silu(z)


def _conformer(u_conf, lw, B, L):
    ws = (lw["conf_dw_w"], lw["conf_dw_b"], lw["conf_ln_g"], lw["conf_ln_b"])
    seqs = _seqs_per_step(B, L)
    return pl.pallas_call(
        functools.partial(_conf_kernel, L=L, seqs=seqs),
        grid=(B // seqs,),
        in_specs=[pl.BlockSpec((seqs * L, CONF_COLS), lambda b: (b, 0))]
        + [_per_layer(w, lw["layer"]) for w in ws],
        out_specs=pl.BlockSpec((seqs * L, CONF_WIDTH), lambda b: (b, 0)),
        out_shape=jax.ShapeDtypeStruct((B * L, CONF_WIDTH), F32),
        scratch_shapes=[pltpu.VMEM((L + 2 * CONF_HALO, CONF_WIDTH), F32),
                        pltpu.VMEM((SUBLANES, L + SUBLANES * ((CONF_KERNEL - 1) // SUBLANES), CONF_WIDTH), F32)],
        compiler_params=_cparams(("arbitrary",), 48),
        name="conformer",
    )(u_conf, *ws)


def _merge_kernel(x_ref, mod_ref, g1_ref, yhy_ref, ymla_ref, yret_ref, yconf_ref,
                  gw_ref, gb_ref, why_ref, wmla_ref, wret_ref, wconf_ref, wo_ref, o_ref, *, seqs_per_tile):
    x = x_ref[...]
    h = _rms(x, g1_ref[...]) * (1.0 + mod_ref[1:2, :]) + mod_ref[0:1, :]
    hb = h.astype(BF16)
    if seqs_per_tile == 1:
        yhy = yhy_ref[...]
    else:
        yhy = jnp.concatenate(
            [yhy_ref[:, s * HY_WIDTH:(s + 1) * HY_WIDTH] for s in range(seqs_per_tile)], axis=0)
    branches = ((yhy, why_ref), (ymla_ref[...], wmla_ref), (yret_ref[...], wret_ref), (yconf_ref[...], wconf_ref))
    D = D_MODEL
    merged = None
    for i, (y, w_ref) in enumerate(branches):
        gate = jax.nn.sigmoid(_dot(hb, gw_ref[:, i * D:(i + 1) * D]) + gb_ref[:, i * D:(i + 1) * D])
        term = gate * _dot(y.astype(BF16), w_ref[...])
        merged = term if merged is None else merged + term
    o_ref[...] = x + mod_ref[2:3, :] * _dot(merged.astype(BF16), wo_ref[...])


def _merge(x, mod, lw, y_hy, y_mla, y_ret, y_conf, B, L, latent):
    T = B * L
    TM = WIDE_TOKEN_TILE
    if L >= TM:
        tiles_per_seq, seqs_per_tile = L // TM, 1
        hy_spec = pl.BlockSpec((TM, HY_WIDTH), lambda i: (i % tiles_per_seq, i // tiles_per_seq))
    else:
        tiles_per_seq, seqs_per_tile = 1, TM // L
        hy_spec = pl.BlockSpec((L, seqs_per_tile * HY_WIDTH), lambda i: (0, i))
    once = lambda a: _per_layer(a, lw["layer"], single_buffer=True)
    row = lambda w: pl.BlockSpec((TM, w), lambda i: (i, 0))
    ws = (lw["gate_w"], lw["gate_b"], lw["hy_out"], lw["mla_out"], lw["ret_out"], lw["conf_out"], lw["w_o"])
    return pl.pallas_call(
        functools.partial(_merge_kernel, seqs_per_tile=seqs_per_tile),
        grid=(T // TM,),
        in_specs=[row(D_MODEL), _mod_spec(lw["layer"], tiles_per_seq, latent),
                  once(lw["norm1_g"]), hy_spec, row(256), row(256), row(256)] + [once(w) for w in ws],
        out_specs=row(D_MODEL),
        out_shape=jax.ShapeDtypeStruct((T, D_MODEL), F32),
        compiler_params=_cparams(("arbitrary",), 56),
        name="merge",
    )(x, mod, lw["norm1_g"], y_hy, y_mla, y_ret, y_conf, *ws)


def _ffn_kernel(x_ref, mod_ref, g2_ref, w1_ref, w2_ref, fg_ref, o_ref, *, final):
    x = x_ref[...]
    h2 = (_rms(x, g2_ref[...]) * (1.0 + mod_ref[4:5, :]) + mod_ref[3:4, :]).astype(BF16)
    acc = None
    for c0 in range(0, D_FF, FFN_CHUNK):
        c1 = min(c0 + FFN_CHUNK, D_FF)
        a = _dot(h2, w1_ref[:, c0:c1])
        b = _dot(h2, w1_ref[:, D_FF + c0:D_FF + c1])
        part = _dot((_silu(a) * b).astype(BF16), w2_ref[c0:c1, :])
        acc = part if acc is None else acc + part
    out = x + mod_ref[5:6, :] * acc
    if final:
        out = _rms(out, fg_ref[...])
    o_ref[...] = out


def _ffn(x, mod, lw, final_g, B, L, latent, final):
    T = B * L
    TM = WIDE_TOKEN_TILE
    tiles_per_seq = max(L // TM, 1)
    once = lambda a: _per_layer(a, lw["layer"], single_buffer=True)
    row = pl.BlockSpec((TM, D_MODEL), lambda i: (i, 0))
    ws = (lw["norm2_g"], lw["ffn_w1"], lw["ffn_w2"])
    return pl.pallas_call(
        functools.partial(_ffn_kernel, final=final),
        grid=(T // TM,),
        in_specs=[row, _mod_spec(lw["layer"], tiles_per_seq, latent)]
        + [once(w) for w in ws] + [pl.BlockSpec(final_g.shape, lambda i: (0, 0))],
        out_specs=row,
        out_shape=jax.ShapeDtypeStruct((T, D_MODEL), F32),
        compiler_params=_cparams(("arbitrary",), 56),
        name="ffn",
    )(x, mod, *ws, final_g)


def _trunk_layer(x, mod, lw, final_g, tables, B, L, ctx, state, final):
    latent = ctx is not None
    hv, hx1, hx2, qh, kh, vh, u_ret, u_conf, kt_ret, *cache = _inproj(x, mod, lw, B, L, latent)
    ckv, kr = cache if cache else (None, None)
    spectra = _filter_spectra(lw, L, tables)
    y_hy = _hyena(hv, hx1, hx2, lw, spectra, tables, B, L)
    y_mla = _mla(qh, kh, vh, lw, B, L, ctx)
    y_ret, S = _retention(u_ret, kt_ret, lw, B, L, state, emit_state=not latent)
    y_conf = _conformer(u_conf, lw, B, L)
    x = _merge(x, mod, lw, y_hy, y_mla, y_ret, y_conf, B, L, latent)
    x = _ffn(x, mod, lw, final_g, B, L, latent, final)
    return x, ckv, kr, S


def _stacked_weights(w_in, p):
    depth = w_in.shape[0]
    mla = HY_COLS + MLA_COLS
    lanes_after = HEAD_LANES - MLA_NOPE - MLA_ROPE
    w_in_all = jnp.concatenate(
        [w_in[:, :, :mla - MLA_ROPE],
         jnp.pad(w_in[:, :, mla - MLA_ROPE:mla], ((0, 0), (0, 0), (MLA_NOPE, lanes_after))),
         w_in[:, :, mla:]], axis=2).astype(BF16)
    dq = MLA_NOPE + MLA_ROPE
    head_pad = lambda a: jnp.pad(a, ((0, 0),) * 3 + ((0, HEAD_LANES - a.shape[3]),)).reshape(
        depth, a.shape[1], MLA_HEADS * HEAD_LANES)
    w_uq = head_pad(p["mla_w_uq"].reshape(depth, MLA_Q_LORA, MLA_HEADS, dq))
    w_ukv = p["mla_w_ukv"].reshape(depth, MLA_KV_LORA, MLA_HEADS, MLA_NOPE + MLA_V)
    w_ukv = jnp.concatenate([head_pad(w_ukv[..., :MLA_NOPE]), head_pad(w_ukv[..., MLA_NOPE:])], axis=2)
    row = lambda name: p[name].reshape(depth, 1, -1)
    rows = ("norm1_g", "norm2_g", "mla_q_norm", "mla_kv_norm", "hy_conv_b", "hy_b1", "hy_b2",
            "conf_dw_b", "conf_ln_g", "conf_ln_b", "gate_b")
    as_is = ("hy_conv_w", "hy_w2", "hy_w3", "hy_bias", "ret_decay", "conf_dw_w")
    bf16 = ("gate_w", "hy_out", "mla_out", "ret_out", "conf_out", "w_o", "ffn_w1", "ffn_w2")
    return {
        "w_in": w_in_all,
        "mla_w_uq": w_uq.astype(BF16), "mla_w_ukv": w_ukv.astype(BF16),
        "hy_w1": jnp.pad(p["hy_w1"], ((0, 0), (0, 128 - HY_EMB), (0, 0))),
        **{name: row(name) for name in rows},
        **{name: p[name] for name in as_is},
        **{name: p[name].astype(BF16) for name in bf16},
    }


def kernel(x_prompt, x_sample, cache_mla_ckv, cache_mla_krope, state_ret, c, c_ctx, ada_w, ada_b, norm1_g, w_in, hy_conv_w, hy_conv_b, hy_w1, hy_b1, hy_w2, hy_b2, hy_w3, hy_bias, hy_out, mla_q_norm, mla_w_uq, mla_kv_norm, mla_w_ukv, mla_out, ret_decay, ret_out, conf_dw_w, conf_dw_b, conf_ln_g, conf_ln_b, conf_out, gate_w, gate_b, w_o, norm2_g, ffn_w1, ffn_w2, final_norm_g):
    p = dict(norm1_g=norm1_g, hy_conv_w=hy_conv_w, hy_conv_b=hy_conv_b, hy_w1=hy_w1, hy_b1=hy_b1,
             hy_w2=hy_w2, hy_b2=hy_b2, hy_w3=hy_w3, hy_bias=hy_bias, hy_out=hy_out,
             mla_q_norm=mla_q_norm, mla_w_uq=mla_w_uq, mla_kv_norm=mla_kv_norm, mla_w_ukv=mla_w_ukv,
             mla_out=mla_out, ret_decay=ret_decay, ret_out=ret_out, conf_dw_w=conf_dw_w,
             conf_dw_b=conf_dw_b, conf_ln_g=conf_ln_g, conf_ln_b=conf_ln_b, conf_out=conf_out,
             gate_w=gate_w, gate_b=gate_b, w_o=w_o, norm2_g=norm2_g, ffn_w1=ffn_w1, ffn_w2=ffn_w2)
    Bp, Lp, D = x_prompt.shape
    Bs, Ls, _ = x_sample.shape
    depth = w_in.shape[0]

    cond = jnp.concatenate([c_ctx[None, :], c, jnp.zeros((MOD_ROWS - 1 - Bs, D), F32)], axis=0)
    mod = _modulation(cond, ada_w, ada_b)
    tables_p = _dft_tables(Lp)
    tables_s = _dft_tables(Ls)
    final_g = final_norm_g.reshape(1, D)

    xp = x_prompt.reshape(Bp * Lp, D)
    xs = x_sample.reshape(Bs * Ls, D)
    ckvs, kropes, rets = [], [], []
    weights = _stacked_weights(w_in, p)
    for l in range(depth):
        lw = dict(weights, layer=l)
        final = l == depth - 1
        xp, ckv, kr, S = _trunk_layer(xp, mod, lw, final_g, tables_p, Bp, Lp, None, None, final)
        ckvs.append(ckv.reshape(Bp, Lp, MLA_KV_LORA))
        kropes.append(kr.reshape(Bp, Lp, MLA_ROPE))
        rets.append(S)
        xs, _, _, _ = _trunk_layer(xs, mod, lw, final_g, tables_s, Bs, Ls,
                                   (cache_mla_ckv, cache_mla_krope), state_ret, final)
    return (xp.reshape(Bp, Lp, D), xs.reshape(Bs, Ls, D),
            jnp.stack(ckvs, axis=1), jnp.stack(kropes, axis=1), jnp.stack(rets, axis=1))
```

```python
import functools
import math

import jax
import jax.numpy as jnp
from jax import lax
from jax.experimental import pallas as pl
from jax.experimental.pallas import tpu as pltpu

F32 = jnp.float32
BF16 = jnp.bfloat16

D_MODEL = 1024
DEPTH = 2
PAST_LEN = 256
EPS = 1e-6
GRID_W = 64

HY_WIDTH = 256
HY_EMB = 33
HY_BANDS = (HY_EMB - 1) // 2
HY_FFN = 64
HY_FAST_DECAY = 0.3
HY_SLOW_DECAY = 1.5
HY_TARGET = 1e-2

MLA_HEADS = 4
MLA_Q_LORA = 256
MLA_KV_LORA = 128
MLA_NOPE = 64
MLA_ROPE = 32
MLA_V = 64
ROPE_BASE = 10000.0

RET_HEADS = 4
RET_DK = 64
RET_DV = 64
RET_BLOCK = 256

CONF_WIDTH = 256
CONF_KERNEL = 31

D_FF = ((8 * D_MODEL // 3 + 255) // 256) * 256
N_BRANCH = 4

HY_COLS = 3 * HY_WIDTH
MLA_COLS = MLA_Q_LORA + MLA_KV_LORA + MLA_ROPE
RET_COLS = 2 * RET_HEADS * RET_DK + 2 * RET_HEADS * RET_DV
CONF_COLS = 2 * CONF_WIDTH

VMEM_BYTES_V7X = 64 * 1024 * 1024
SUBLANES = 8
TOKEN_TILE = 512
WIDE_TOKEN_TILE = 1024
Q_TILE = 512
MLA_SOFTMAX_ROWS = 16
DFT_TILE = 512
SHORT_SEQ_ROWS = 1024
HYENA_GROUP_ROWS = 1024
HYENA_ROWS = 1024
CONF_ROWS = 128
CONF_HALO = 16
MXU_DIM_V7X = 256
FFN_CHUNK = 4 * MXU_DIM_V7X
MOD_ROWS = 8


def _cparams(sem, vmem_mb):
    return pltpu.CompilerParams(dimension_semantics=sem, vmem_limit_bytes=vmem_mb * 1024 * 1024)


def _dot(a, b):
    return jnp.dot(a, b, preferred_element_type=F32)


def _dot_nt(a, b):
    return lax.dot_general(a, b, (((1,), (1,)), ((), ())), preferred_element_type=F32)


def _dot_exact(a, b):
    return jnp.dot(a, b, preferred_element_type=F32, precision=lax.Precision.HIGHEST)


def _dot_split(a, b):
    a_hi, b_hi = a.astype(BF16), b.astype(BF16)
    a_lo = (a - a_hi.astype(F32)).astype(BF16)
    b_lo = (b - b_hi.astype(F32)).astype(BF16)
    return _dot(a_hi, b_hi) + (_dot(a_lo, b_hi) + _dot(a_hi, b_lo))


def _rms(x, g):
    return x * lax.rsqrt(jnp.mean(x * x, axis=-1, keepdims=True) + EPS) * g


def _sigmoid(x):
    return 0.5 * jnp.tanh(0.5 * x) + 0.5


def _silu(x):
    return x * _sigmoid(x)


def _mod_kernel(c_ref, w_ref, b_ref, o_ref):
    s = _silu(c_ref[...]).astype(BF16)
    o_ref[...] = _dot(s, w_ref[...].astype(BF16)) + b_ref[...]


def _modulation(cond, ada_w, ada_b):
    depth, d, cols = ada_w.shape
    blk = 1024
    out = pl.pallas_call(
        _mod_kernel,
        grid=(depth, cols // blk),
        in_specs=[
            pl.BlockSpec((MOD_ROWS, d), lambda l, j: (0, 0)),
            pl.BlockSpec((None, d, blk), lambda l, j: (l, 0, j)),
            pl.BlockSpec((None, 1, blk), lambda l, j: (l, 0, j)),
        ],
        out_specs=pl.BlockSpec((None, MOD_ROWS, blk), lambda l, j: (l, 0, j)),
        out_shape=jax.ShapeDtypeStruct((depth, MOD_ROWS, cols), F32),
        compiler_params=_cparams(("arbitrary", "arbitrary"), 32),
        name="modulation",
    )(cond, ada_w, ada_b.reshape(depth, 1, cols))
    return out.reshape(depth, MOD_ROWS, 6, d)


def _seqs_per_step(B, L):
    return max(1, min(B, SHORT_SEQ_ROWS // L))


def _per_layer(a, layer, single_buffer=False):
    mode = dict(pipeline_mode=pl.Buffered(1)) if single_buffer else {}
    return pl.BlockSpec((None,) + a.shape[1:], lambda *_: (layer,) + (0,) * (a.ndim - 1), **mode)


def _mod_spec(layer, tiles_per_seq, latent):
    group = (lambda i: 1 + i // tiles_per_seq) if latent else (lambda i: 0)
    return pl.BlockSpec((None, None, 6, D_MODEL), lambda i: (layer, group(i), 0, 0))


HEAD_LANES = 128


def _rope_lanes(x, cos, sa, sb):
    reps = x.shape[1] // HEAD_LANES
    wide = lambda t: jnp.concatenate([t] * reps, axis=1) if reps > 1 else t
    half = MLA_ROPE // 2
    return (x * wide(cos) + pltpu.roll(x, x.shape[1] - half, 1) * wide(sa)
            + pltpu.roll(x, half, 1) * wide(sb))


def _inproj_kernel(*refs, seqs_per_tile, seq_len, latent):
    x_ref, mod_ref, g1_ref, win_ref, qn_ref, kvn_ref, wuq_ref, wukv_ref = refs[:8]
    refs = refs[8:]
    c_cq = HY_COLS
    c_ck = c_cq + MLA_Q_LORA
    c_ret = c_ck + MLA_KV_LORA + HEAD_LANES
    c_conf = c_ret + RET_COLS
    why_ref, wcq_ref, wck_ref, wret_ref, wconf_ref = (
        win_ref.at[:, a:b] for a, b in ((0, c_cq), (c_cq, c_ck), (c_ck, c_ret), (c_ret, c_conf),
                                        (c_conf, c_conf + CONF_COLS)))
    if latent:
        cos_ref, sa_ref, sb_ref = refs[:3]
        refs = refs[3:]
    hv_ref, hx1_ref, hx2_ref, q_ref, kh_ref, vh_ref, ret_ref, conf_ref, rkt_ref = refs[:9]
    if not latent:
        ckv_ref, kr_ref = refs[9:]
    x = x_ref[...]
    h = _rms(x, g1_ref[...]) * (1.0 + mod_ref[1:2, :]) + mod_ref[0:1, :]
    hb = h.astype(BF16)

    u = _dot(hb, why_ref[...])
    for p, o_ref in enumerate((hv_ref, hx1_ref, hx2_ref)):
        part = u[:, p * HY_WIDTH:(p + 1) * HY_WIDTH]
        if seqs_per_tile == 1:
            o_ref[...] = part
        else:
            for s in range(seqs_per_tile):
                o_ref[:, s * HY_WIDTH:(s + 1) * HY_WIDTH] = part[s * seq_len:(s + 1) * seq_len]

    heads_w = MLA_HEADS * HEAD_LANES
    cq = _dot(hb, wcq_ref[...])
    q = _dot(_rms(cq, qn_ref[...]).astype(BF16), wuq_ref[...])
    ck = _dot(hb, wck_ref[...])
    ckv = _rms(ck[:, :MLA_KV_LORA], kvn_ref[...])
    kr_block = ck[:, MLA_KV_LORA:MLA_KV_LORA + HEAD_LANES]
    if latent:
        q = _rope_lanes(q, cos_ref[...], sa_ref[...], sb_ref[...])
        kr_block = _rope_lanes(kr_block, cos_ref[...], sa_ref[...], sb_ref[...])
    else:
        ckv_ref[...] = ckv
        kr_ref[...] = kr_block[:, MLA_NOPE:MLA_NOPE + MLA_ROPE]
    q_ref[...] = (q * ((MLA_NOPE + MLA_ROPE) ** -0.5 * math.log2(math.e))).astype(BF16)
    kv = _dot(ckv.astype(BF16), wukv_ref[...])
    kh_ref[...] = (kv[:, :heads_w] + jnp.concatenate([kr_block] * MLA_HEADS, axis=1)).astype(BF16)
    lane = lax.broadcasted_iota(jnp.int32, (1, heads_w), 1)
    ones_lane = jnp.where(lane % HEAD_LANES == MLA_V, 1.0, 0.0)
    vh_ref[...] = (kv[:, heads_w:] + ones_lane).astype(BF16)
    ret = _dot(hb, wret_ref[...])
    ret_ref[...] = ret
    qk = RET_HEADS * RET_DK
    rkt_ref[...] = ret[:, qk:2 * qk].T
    conf_ref[...] = _dot(hb, wconf_ref[...])


def _inproj(x, mod, lw, B, L, latent):
    T = B * L
    TM = TOKEN_TILE
    nt = T // TM
    if L >= TM:
        tiles_per_seq, seqs_per_tile = L // TM, 1
        hy_block = (TM, HY_WIDTH)
        hy_map = lambda i: (i % tiles_per_seq, i // tiles_per_seq)
    else:
        tiles_per_seq, seqs_per_tile = 1, TM // L
        hy_block = (L, seqs_per_tile * HY_WIDTH)
        hy_map = lambda i: (0, i)
    row = lambda w: pl.BlockSpec((TM, w), lambda i: (i, 0))
    weights = (lw["norm1_g"], lw["w_in"], lw["mla_q_norm"], lw["mla_kv_norm"], lw["mla_w_uq"], lw["mla_w_ukv"])
    hy_shape = jax.ShapeDtypeStruct((L, B * HY_WIDTH), F32)
    qk = RET_HEADS * RET_DK
    heads_w = MLA_HEADS * HEAD_LANES
    ins = [x, mod, *weights]
    in_specs = ([row(D_MODEL), _mod_spec(lw["layer"], tiles_per_seq, latent)]
                + [_per_layer(w, lw["layer"]) for w in weights])
    out_specs = ([pl.BlockSpec(hy_block, hy_map)] * 3
                 + [row(heads_w)] * 3 + [row(RET_COLS), row(CONF_COLS), pl.BlockSpec((qk, TM), lambda i: (0, i))])
    out_shape = ([hy_shape] * 3 + [jax.ShapeDtypeStruct((T, heads_w), BF16)] * 3
                 + [jax.ShapeDtypeStruct((T, RET_COLS), F32), jax.ShapeDtypeStruct((T, CONF_COLS), F32),
                    jax.ShapeDtypeStruct((qk, T), F32)])
    if latent:
        ins += list(_rope_lane_tables(L))
        in_specs += [pl.BlockSpec((TM, HEAD_LANES), lambda i: (i % tiles_per_seq, 0))] * 3
    else:
        out_specs += [row(MLA_KV_LORA), row(MLA_ROPE)]
        out_shape += [jax.ShapeDtypeStruct((T, MLA_KV_LORA), F32), jax.ShapeDtypeStruct((T, MLA_ROPE), F32)]
    return pl.pallas_call(
        functools.partial(_inproj_kernel, seqs_per_tile=seqs_per_tile, seq_len=L, latent=latent),
        grid=(nt,),
        in_specs=in_specs,
        out_specs=out_specs,
        out_shape=out_shape,
        compiler_params=_cparams(("arbitrary",), 48),
        name="inproj",
    )(*ins)


def _dft_tables(L):
    N = 2 * L
    k_lo = min(L, 32)
    k_hi = L // k_lo
    t = jnp.arange(L, dtype=jnp.int32)[None, :]

    def cs(k):
        m = (k[:, None] * t) % N
        ang = m.astype(F32) * (2.0 * math.pi / N)
        return jnp.cos(ang), jnp.sin(ang)

    ca, sa = cs(jnp.arange(k_hi, dtype=jnp.int32) * k_lo)
    cb, sb = cs(jnp.arange(k_lo, dtype=jnp.int32))
    cos = (ca[:, None, :] * cb[None, :, :] - sa[:, None, :] * sb[None, :, :]).reshape(L, L)
    sin = (sa[:, None, :] * cb[None, :, :] + ca[:, None, :] * sb[None, :, :]).reshape(L, L)
    return cos.astype(BF16), (-sin).astype(BF16)


def _filter_features(L):
    t = jnp.linspace(0.0, 1.0, L, dtype=F32)[:, None]
    w = 2.0 * math.pi * jnp.arange(L, dtype=F32)[:, None] / L
    f = jnp.linspace(1e-4, HY_BANDS - 1, HY_BANDS, dtype=F32)[None, :]
    z = jnp.concatenate([t, jnp.cos(f * w), -jnp.sin(f * w)], axis=-1)
    z = jnp.pad(z, ((0, 0), (0, 128 - HY_EMB)))
    max_decay = math.log(HY_TARGET) / HY_FAST_DECAY
    min_decay = math.log(HY_TARGET) / HY_SLOW_DECAY
    deltas = jnp.abs(jnp.linspace(min_decay, max_decay, HY_WIDTH, dtype=F32))
    decay = jnp.exp(-t * deltas[None, :])
    return z, decay


def _alternating(rows):
    t = lax.broadcasted_iota(jnp.int32, (rows, 1), 0)
    return (1 - 2 * (t & 1)).astype(F32)


def _filter_kernel(z_ref, dec_ref, w1_ref, b1_ref, w2_ref, b2_ref, w3_ref, cos_ref, sin_ref,
                   kr_ref, ki_ref, kn_ref, h_ref, *, L):
    j = pl.program_id(0)
    W = HY_WIDTH
    N = 2 * L

    @pl.when(j == 0)
    def _():
        h = jnp.sin(_dot_exact(z_ref[...], w1_ref[...]) + b1_ref[...])
        h = jnp.sin(_dot_exact(h, w2_ref[...]) + b2_ref[...])
        h = _dot_split(h, w3_ref[...]) * jnp.concatenate([dec_ref[...]] * 4, axis=1)
        cs = jnp.sum(jnp.abs(h), axis=0, keepdims=True)
        s0 = cs[:, 0:W] + cs[:, W:2 * W]
        s1 = cs[:, 2 * W:3 * W] + cs[:, 3 * W:4 * W]
        h = h / jnp.concatenate([s0, s0, s1, s1], axis=1)
        row = lax.broadcasted_iota(jnp.int32, h.shape, 0)
        col = lax.broadcasted_iota(jnp.int32, h.shape, 1)
        backward = (col // W) % 2 == 1
        h = jnp.where(backward & (row == 0), 0.0, h)
        h_ref[...] = h.astype(BF16)
        nyq = jnp.sum(h * _alternating(L), axis=0, keepdims=True) * (1.0 / N)
        for o in range(2):
            c = 2 * o * W
            kn_ref[o] = nyq[:, c:c + W] + nyq[:, c + W:c + 2 * W]

    hb = h_ref[...]
    sr = _dot(cos_ref[...], hb)
    si = _dot(sin_ref[...], hb)
    row = lax.broadcasted_iota(jnp.int32, (sr.shape[0], W), 0)
    scale = jnp.where((row == 0) & (j == 0), 1.0 / N, 2.0 / N)
    for o in range(2):
        c = 2 * o * W
        kr_ref[o] = (sr[:, c:c + W] + sr[:, c + W:c + 2 * W]) * scale
        ki_ref[o] = (si[:, c:c + W] - si[:, c + W:c + 2 * W]) * scale


def _filter_spectra(lw, L, tables):
    z, decay = _filter_features(L)
    cos, msin = tables
    Tk = min(DFT_TILE, L)
    full = lambda a: pl.BlockSpec(a.shape, lambda j: (0,) * a.ndim)
    mlp = (lw["hy_w1"], lw["hy_b1"], lw["hy_w2"], lw["hy_b2"], lw["hy_w3"])
    ins = (z, decay, *mlp)
    tile = pl.BlockSpec((Tk, L), lambda j: (j, 0))
    spec = pl.BlockSpec((2, Tk, HY_WIDTH), lambda j: (0, j, 0))
    return pl.pallas_call(
        functools.partial(_filter_kernel, L=L),
        grid=(L // Tk,),
        in_specs=[full(z), full(decay)] + [_per_layer(a, lw["layer"]) for a in mlp] + [tile, tile],
        out_specs=[spec, spec, pl.BlockSpec((2, 1, HY_WIDTH), lambda j: (0, 0, 0))],
        out_shape=[jax.ShapeDtypeStruct((2, L, HY_WIDTH), F32)] * 2
        + [jax.ShapeDtypeStruct((2, 1, HY_WIDTH), F32)],
        scratch_shapes=[pltpu.VMEM((L, 4 * HY_WIDTH), BF16)],
        compiler_params=_cparams(("arbitrary",), 48),
        name="hyena_filter",
    )(*ins, cos, msin)


def _hyena_kernel(v_ref, x1_ref, x2_ref, cw_ref, cb_ref, bias_ref, kr_ref, ki_ref, kn_ref,
                  cos_ref, sin_ref, y_ref, cur_ref, curb_ref, yr_ref, yi_ref, gate_ref,
                  *, L, W, T):
    reps = W // HY_WIDTH
    tiled = lambda a: jnp.concatenate([a] * reps, axis=1) if reps > 1 else a
    alt = _alternating(L)

    def short_conv(u_ref, p):
        u = u_ref[...]
        row = lax.broadcasted_iota(jnp.int32, (L, W), 0)
        prev = jnp.where(row == 0, 0.0, pltpu.roll(u, 1, 0))
        nxt = jnp.where(row == L - 1, 0.0, pltpu.roll(u, L - 1, 0))
        cols = slice(p * HY_WIDTH, (p + 1) * HY_WIDTH)
        w = [tiled(cw_ref[k:k + 1, cols]) for k in range(3)]
        return prev * w[0] + u * w[1] + nxt * w[2] + tiled(cb_ref[:, cols])

    cur_ref[...] = short_conv(v_ref, 0)
    for o, x_ref in enumerate((x1_ref, x2_ref)):
        cur = cur_ref[...]
        curb_ref[...] = cur.astype(BF16)
        gate_ref[...] = short_conv(x_ref, o + 1)
        nyq = jnp.sum(cur * alt, axis=0, keepdims=True) * tiled(kn_ref[o])
        for f in range(L // T):
            rows = slice(f * T, (f + 1) * T)
            cb = curb_ref[...]
            xr = _dot(cos_ref[rows, :], cb)
            xi = _dot(sin_ref[rows, :], cb)
            kr, ki = tiled(kr_ref[o, rows, :]), tiled(ki_ref[o, rows, :])
            yr_ref[rows, :] = (xr * kr - xi * ki).astype(BF16)
            yi_ref[rows, :] = (xr * ki + xi * kr).astype(BF16)
        for t in range(L // T):
            rows = slice(t * T, (t + 1) * T)
            conv = (_dot(cos_ref[rows, :], yr_ref[...]) + _dot(sin_ref[rows, :], yi_ref[...])
                    + alt[rows] * nyq + cur_ref[rows, :] * tiled(bias_ref[o:o + 1, :]))
            out = gate_ref[rows, :] * conv
            if o == 0:
                cur_ref[rows, :] = out
            else:
                y_ref[rows, :] = out


def _hyena(hv, hx1, hx2, lw, spectra, tables, B, L):
    kr, ki, kn = spectra
    cos, msin = tables
    T = min(HYENA_ROWS, L)
    W = HY_WIDTH * max(1, min(B, HYENA_GROUP_ROWS // L))
    ng = (B * HY_WIDTH) // W
    col = pl.BlockSpec((L, W), lambda g: (0, g))
    once = lambda a: pl.BlockSpec(a.shape, lambda g: (0,) * a.ndim, pipeline_mode=pl.Buffered(1))
    params = (lw["hy_conv_w"], lw["hy_conv_b"], lw["hy_bias"])
    consts = (kr, ki, kn, cos, msin)
    return pl.pallas_call(
        functools.partial(_hyena_kernel, L=L, W=W, T=T),
        grid=(ng,),
        in_specs=[col, col, col] + [_per_layer(a, lw["layer"]) for a in params] + [once(a) for a in consts],
        out_specs=col,
        out_shape=jax.ShapeDtypeStruct((L, B * HY_WIDTH), F32),
        scratch_shapes=[pltpu.VMEM((L, W), F32), pltpu.VMEM((L, W), BF16), pltpu.VMEM((L, W), BF16),
                        pltpu.VMEM((L, W), BF16), pltpu.VMEM((L, W), F32)],
        compiler_params=_cparams(("arbitrary",), 60),
        name="hyena_conv",
    )(hv, hx1, hx2, *params, *consts)


def _mla_kernel(*refs, L, latent, seqs):
    if latent:
        q_ref, kh_ref, vh_ref, cckv_ref, ckr_ref, wukv_ref, o_ref, s_ref, p_ref, kctx_ref, vctx_ref = refs
    else:
        q_ref, kh_ref, vh_ref, o_ref, s_ref, p_ref = refs
    TQ = q_ref.shape[0] // seqs
    heads_w = MLA_HEADS * HEAD_LANES
    block = lambda h: slice(h * HEAD_LANES, (h + 1) * HEAD_LANES)

    if latent:
        @pl.when(pl.program_id(1) == 0)
        def _():
            kv = _dot(cckv_ref[...].astype(BF16), wukv_ref[...])
            zeros = lambda w: jnp.zeros((PAST_LEN, w), F32)
            kr_block = jnp.concatenate(
                [zeros(MLA_NOPE), ckr_ref[...], zeros(HEAD_LANES - MLA_NOPE - MLA_ROPE)], axis=1)
            kctx_ref[...] = (kv[:, :heads_w] + jnp.concatenate([kr_block] * MLA_HEADS, axis=1)).astype(BF16)
            lane = lax.broadcasted_iota(jnp.int32, (1, heads_w), 1)
            vctx_ref[...] = (kv[:, heads_w:] + jnp.where(lane % HEAD_LANES == MLA_V, 1.0, 0.0)).astype(BF16)

    def attend(q_rows, k_rows):
        def scores(h):
            qh = q_ref[q_rows, block(h)]
            s_ref[h, :, 0:L] = _dot_nt(qh, kh_ref[k_rows, block(h)])
            if latent:
                s_ref[h, :, L:L + PAST_LEN] = _dot_nt(qh, kctx_ref[:, block(h)])

        outs = []
        scores(0)
        for h in range(MLA_HEADS):
            if h + 1 < MLA_HEADS:
                scores(h + 1)
            for r in range(TQ // MLA_SOFTMAX_ROWS):
                rows = slice(r * MLA_SOFTMAX_ROWS, (r + 1) * MLA_SOFTMAX_ROWS)
                s = s_ref[h, rows, :]
                p_ref[h, rows, :] = jnp.exp2(s - jnp.max(s, axis=-1, keepdims=True)).astype(BF16)
            pv = _dot(p_ref[h, :, 0:L], vh_ref[k_rows, block(h)])
            if latent:
                pv = pv + _dot(p_ref[h, :, L:L + PAST_LEN], vctx_ref[:, block(h)])
            outs.append(pv[:, :MLA_V] / pv[:, MLA_V:MLA_V + 1])
        o_ref[q_rows, :] = jnp.concatenate(outs, axis=-1)

    for sq in range(seqs):
        attend(slice(sq * TQ, (sq + 1) * TQ), slice(sq * L, (sq + 1) * L))


def _rope_tables(L):
    rows = L // GRID_W
    row = jnp.repeat(jnp.arange(rows), GRID_W).astype(F32)
    col = jnp.tile(jnp.arange(GRID_W), rows).astype(F32)
    per_axis = MLA_ROPE // 4
    inv = ROPE_BASE ** (-jnp.arange(per_axis, dtype=F32) / per_axis)
    ang = jnp.concatenate([row[:, None] * inv, col[:, None] * inv], axis=-1)
    return jnp.cos(ang), jnp.sin(ang)


def _rope_lane_tables(L):
    cos, sin = _rope_tables(L)
    pad = HEAD_LANES - MLA_NOPE - MLA_ROPE
    one, zero = jnp.ones((L, MLA_NOPE), F32), jnp.zeros((L, MLA_NOPE), F32)
    half0 = jnp.zeros_like(sin)
    cos_t = jnp.concatenate([one, cos, cos, one[:, :pad]], axis=1)
    sa_t = jnp.concatenate([zero, -sin, half0, zero[:, :pad]], axis=1)
    sb_t = jnp.concatenate([zero, half0, sin, zero[:, :pad]], axis=1)
    return cos_t, sa_t, sb_t


def _mla(qh, kh, vh, lw, B, L, ctx):
    latent = ctx is not None
    TQ = min(Q_TILE, L)
    nq = L // TQ
    seqs = 1 if latent or nq > 1 else _seqs_per_step(B, L)
    Lk = L + PAST_LEN if latent else L
    heads_w = MLA_HEADS * HEAD_LANES
    seq = pl.BlockSpec((seqs * L, heads_w), lambda b, i: (b, 0))
    ins = [qh, kh, vh]
    specs = [pl.BlockSpec((seqs * TQ, heads_w), lambda b, i: (b * nq + i, 0)), seq, seq]
    scratch = [pltpu.VMEM((MLA_HEADS, TQ, Lk), F32), pltpu.VMEM((MLA_HEADS, TQ, Lk), BF16)]
    if latent:
        ins += [ctx[0], ctx[1], lw["mla_w_ukv"]]
        layer = lw["layer"]
        specs += [pl.BlockSpec((None, None, PAST_LEN, MLA_KV_LORA), lambda b, i: (b, layer, 0, 0)),
                  pl.BlockSpec((None, None, PAST_LEN, MLA_ROPE), lambda b, i: (b, layer, 0, 0)),
                  _per_layer(lw["mla_w_ukv"], lw["layer"])]
        scratch += [pltpu.VMEM((PAST_LEN, heads_w), BF16)] * 2
    return pl.pallas_call(
        functools.partial(_mla_kernel, L=L, latent=latent, seqs=seqs),
        grid=(B // seqs, nq),
        in_specs=specs,
        out_specs=pl.BlockSpec((seqs * TQ, MLA_HEADS * MLA_V), lambda b, i: (b * nq + i, 0)),
        out_shape=jax.ShapeDtypeStruct((B * L, MLA_HEADS * MLA_V), F32),
        scratch_shapes=scratch,
        compiler_params=_cparams(("arbitrary", "arbitrary"), 48),
        name="mla_attention",
    )(*ins)


def _ret_kernel(*refs, L, seqs, has_state, emit_state):
    refs = list(refs)
    u_ref, kt_ref, dl_ref = refs[:3]
    pos = 3
    s0_ref = None
    if has_state:
        s0_ref = refs[pos]
        pos += 1
    y_ref = refs[pos]
    pos += 1
    sout_ref = None
    if emit_state:
        sout_ref = refs[pos]
        pos += 1
    of_ref, kv_ref, sin_ref, dcomb_ref, dtab_ref, hmask_ref, avg_ref = refs[pos:]

    C = RET_BLOCK
    n = L // C
    H, DK, DV = RET_HEADS, RET_DK, RET_DV
    qk = H * DK
    DK_F, DK_B, DQ_F, DQ_B, DC_F, DC_B, MASK = range(7)
    use_cross = has_state or n > 1

    @pl.when(pl.program_id(0) == 0)
    def _():
        x = dl_ref[...]
        log_g = jnp.minimum(x, 0.0) - jnp.log1p(jnp.exp(-jnp.abs(x)))
        gf = [log_g[0:1, h:h + 1] for h in range(H)]
        gb = [log_g[1:2, h:h + 1] for h in range(H)]
        diff = (lax.broadcasted_iota(jnp.int32, (C, C), 0)
                - lax.broadcasted_iota(jnp.int32, (C, C), 1)).astype(F32)
        idx = lax.broadcasted_iota(jnp.int32, (C, 1), 0).astype(F32)
        for h in range(H):
            dcomb_ref[h] = (jnp.where(diff >= 0, jnp.exp(jnp.maximum(diff, 0.0) * gf[h]), 0.0)
                            + jnp.where(diff <= 0, jnp.exp(jnp.maximum(-diff, 0.0) * gb[h]), 0.0))

        def head_cols(lag, g):
            return jnp.concatenate([jnp.broadcast_to(jnp.exp(lag * g[h]), (C, DK)) for h in range(H)], axis=1)

        def head_rows(lag, g):
            return jnp.concatenate(
                [jnp.broadcast_to(jnp.exp(lag * g[h]) * (DK ** -0.5), (DK, C)) for h in range(H)], axis=0)

        tok = lax.broadcasted_iota(jnp.int32, (1, C), 1).astype(F32)
        dtab_ref[DK_F] = head_rows(C - 1.0 - tok, gf)
        dtab_ref[DK_B] = head_rows(tok, gb)
        dtab_ref[DQ_F] = head_cols(idx + 1.0, gf)
        dtab_ref[DQ_B] = head_cols(C - idx, gb)
        row_head = lax.broadcasted_iota(jnp.int32, (qk, qk), 0) // DK
        col_head = lax.broadcasted_iota(jnp.int32, (qk, qk), 1) // DV
        same_head = row_head == col_head
        for slot, g in ((DC_F, gf), (DC_B, gb)):
            dc = jnp.zeros((qk, qk), F32)
            for h in range(H):
                dc = jnp.where(same_head & (row_head == h), jnp.exp(C * g[h]), dc)
            dtab_ref[slot] = dc
        dtab_ref[MASK] = same_head.astype(F32)
        avg_ref[...] = jnp.where(same_head, 1.0 / DV, 0.0).astype(BF16)
        lane_head = lax.broadcasted_iota(jnp.int32, (C, qk), 1) // DK
        for h in range(H):
            hmask_ref[h] = (lane_head == h).astype(F32)

    def head_mean(a):
        total = None
        for _ in range(3):
            part = a.astype(BF16)
            a = a - part.astype(F32)
            term = _dot(part, avg_ref[...])
            total = term if total is None else total + term
        return total

    for sq in range(seqs):
        chunk = lambda c: slice(sq * L + c * C, sq * L + (c + 1) * C)

        for c in range(n):
            rows = chunk(c)
            q = u_ref[rows, 0:qk]
            kb = (u_ref[rows, qk:2 * qk] * (DK ** -0.5)).astype(BF16)
            v = u_ref[rows, 2 * qk:2 * qk + H * DV]
            att = [(_dot_nt((q * hmask_ref[h]).astype(BF16), kb) * dcomb_ref[h]).astype(BF16)
                   for h in range(H)]
            v_heads = jnp.concatenate([(v * hmask_ref[h]).astype(BF16) for h in range(H)], axis=0)
            of_ref[c * C:(c + 1) * C, :] = _dot(jnp.concatenate(att, axis=1), v_heads)
            vb = v.astype(BF16)
            kt = kt_ref[:, rows]
            for d, slot in ((0, DK_F), (1, DK_B)):
                kv_ref[d, c] = _dot((kt * dtab_ref[slot]).astype(BF16), vb)

        finals = []
        for d, slot in ((0, DC_F), (1, DC_B)):
            if has_state:
                zero = jnp.zeros((DK, DV), F32)
                S = jnp.concatenate(
                    [jnp.concatenate([s0_ref[sq, d, h] if g == h else zero for g in range(H)], axis=1)
                     for h in range(H)], axis=0)
            else:
                S = jnp.zeros((qk, H * DV), F32)
            for c in (range(n) if d == 0 else reversed(range(n))):
                if use_cross:
                    sin_ref[d, c] = S.astype(BF16)
                S = S * dtab_ref[slot] + kv_ref[d, c] * dtab_ref[MASK]
            finals.append(S)
        if emit_state:
            for d in range(2):
                for h in range(H):
                    sout_ref[sq, d, h] = finals[d][h * DK:(h + 1) * DK, h * DV:(h + 1) * DV]

        for c in range(n):
            rows = chunk(c)
            tot = of_ref[c * C:(c + 1) * C, :]
            if use_cross:
                qb = u_ref[rows, 0:qk].astype(BF16)
                tot = (tot + _dot(qb, sin_ref[0, c]) * dtab_ref[DQ_F]
                       + _dot(qb, sin_ref[1, c]) * dtab_ref[DQ_B])
            xc = tot - head_mean(tot)
            normed = xc * lax.rsqrt(head_mean(xc * xc) + EPS)
            gate = u_ref[rows, 2 * qk + H * DV:2 * qk + 2 * H * DV]
            y_ref[rows, :] = _silu(gate) * normed


def _retention(u_ret, kt_ret, lw, B, L, state, emit_state):
    has_state = state is not None
    seqs = _seqs_per_step(B, L)
    st_dims = (2, RET_HEADS, RET_DK, RET_DV)
    ins = [u_ret, kt_ret, lw["ret_decay"]]
    specs = [pl.BlockSpec((seqs * L, RET_COLS), lambda b: (b, 0)),
             pl.BlockSpec((RET_HEADS * RET_DK, seqs * L), lambda b: (0, b)),
             _per_layer(lw["ret_decay"], lw["layer"])]
    if has_state:
        layer = lw["layer"]
        ins.append(state)
        specs.append(pl.BlockSpec((seqs, None) + st_dims, lambda b: (b, layer, 0, 0, 0, 0)))
    vd = RET_HEADS * RET_DV
    out_specs = [pl.BlockSpec((seqs * L, vd), lambda b: (b, 0))]
    out_shape = [jax.ShapeDtypeStruct((B * L, vd), F32)]
    if emit_state:
        out_specs.append(pl.BlockSpec((seqs,) + st_dims, lambda b: (b, 0, 0, 0, 0)))
        out_shape.append(jax.ShapeDtypeStruct((B,) + st_dims, F32))
    res = pl.pallas_call(
        functools.partial(_ret_kernel, L=L, seqs=seqs, has_state=has_state, emit_state=emit_state),
        grid=(B // seqs,),
        in_specs=specs,
        out_specs=out_specs,
        out_shape=out_shape,
        scratch_shapes=[pltpu.VMEM((L, vd), F32),
                        pltpu.VMEM((2, L // RET_BLOCK, vd, vd), F32),
                        pltpu.VMEM((2, L // RET_BLOCK, vd, vd), BF16),
                        pltpu.VMEM((RET_HEADS, RET_BLOCK, RET_BLOCK), F32),
                        pltpu.VMEM((7, RET_BLOCK, vd), F32),
                        pltpu.VMEM((RET_HEADS, RET_BLOCK, vd), F32),
                        pltpu.VMEM((vd, vd), BF16)],
        compiler_params=_cparams(("arbitrary",), 48),
        name="retention",
    )(*ins)
    return (res[0], res[1]) if emit_state else (res[0], None)


def _conf_kernel(u_ref, w_ref, b_ref, g_ref, be_ref, y_ref, zp_ref, sh_ref, *, L, seqs):
    Wc = CONF_WIDTH
    halo = CONF_HALO
    zp_ref[0:halo, :] = jnp.zeros((halo, Wc), F32)
    zp_ref[halo + L:2 * halo + L, :] = jnp.zeros((halo, Wc), F32)
    first = halo - CONF_KERNEL // 2
    span = sh_ref.shape[1]
    R = CONF_ROWS
    for sq in range(seqs):
        r0 = sq * L
        zp_ref[halo:halo + L, :] = u_ref[r0:r0 + L, 0:Wc] * _sigmoid(u_ref[r0:r0 + L, Wc:2 * Wc])
        for s in range(SUBLANES):
            sh_ref[s] = zp_ref[first + s:first + s + span, :]
        for c in range(L // R):
            acc = jnp.broadcast_to(b_ref[...], (R, Wc))
            for k in range(CONF_KERNEL):
                a, s = divmod(k, SUBLANES)
                acc = acc + w_ref[k:k + 1, :] * sh_ref[s, c * R + SUBLANES * a:c * R + SUBLANES * a + R, :]
            mu = jnp.mean(acc, axis=-1, keepdims=True)
            xc = acc - mu
            z = xc * lax.rsqrt(jnp.mean(xc * xc, axis=-1, keepdims=True) + EPS) * g_ref[...] + be_ref[...]
            y_ref[r0 + c * R:r0 + (c + 1) * R, :] = _silu(z)


def _conformer(u_conf, lw, B, L):
    ws = (lw["conf_dw_w"], lw["conf_dw_b"], lw["conf_ln_g"], lw["conf_ln_b"])
    seqs = _seqs_per_step(B, L)
    return pl.pallas_call(
        functools.partial(_conf_kernel, L=L, seqs=seqs),
        grid=(B // seqs,),
        in_specs=[pl.BlockSpec((seqs * L, CONF_COLS), lambda b: (b, 0))]
        + [_per_layer(w, lw["layer"]) for w in ws],
        out_specs=pl.BlockSpec((seqs * L, CONF_WIDTH), lambda b: (b, 0)),
        out_shape=jax.ShapeDtypeStruct((B * L, CONF_WIDTH), F32),
        scratch_shapes=[pltpu.VMEM((L + 2 * CONF_HALO, CONF_WIDTH), F32),
                        pltpu.VMEM((SUBLANES, L + SUBLANES * ((CONF_KERNEL - 1) // SUBLANES), CONF_WIDTH), F32)],
        compiler_params=_cparams(("arbitrary",), 48),
        name="conformer",
    )(u_conf, *ws)


def _merge_kernel(x_ref, mod_ref, g1_ref, yhy_ref, ymla_ref, yret_ref, yconf_ref,
                  gw_ref, gb_ref, why_ref, wmla_ref, wret_ref, wconf_ref, wo_ref, o_ref, *, seqs_per_tile):
    x = x_ref[...]
    h = _rms(x, g1_ref[...]) * (1.0 + mod_ref[1:2, :]) + mod_ref[0:1, :]
    hb = h.astype(BF16)
    if seqs_per_tile == 1:
        yhy = yhy_ref[...]
    else:
        yhy = jnp.concatenate(
            [yhy_ref[:, s * HY_WIDTH:(s + 1) * HY_WIDTH] for s in range(seqs_per_tile)], axis=0)
    branches = ((yhy, why_ref), (ymla_ref[...], wmla_ref), (yret_ref[...], wret_ref), (yconf_ref[...], wconf_ref))
    D = D_MODEL
    merged = None
    for i, (y, w_ref) in enumerate(branches):
        gate = _sigmoid(_dot(hb, gw_ref[:, i * D:(i + 1) * D]) + gb_ref[:, i * D:(i + 1) * D])
        term = gate * _dot(y.astype(BF16), w_ref[...])
        merged = term if merged is None else merged + term
    o_ref[...] = x + mod_ref[2:3, :] * _dot(merged.astype(BF16), wo_ref[...])


def _merge(x, mod, lw, y_hy, y_mla, y_ret, y_conf, B, L, latent):
    T = B * L
    TM = WIDE_TOKEN_TILE
    if L >= TM:
        tiles_per_seq, seqs_per_tile = L // TM, 1
        hy_spec = pl.BlockSpec((TM, HY_WIDTH), lambda i: (i % tiles_per_seq, i // tiles_per_seq))
    else:
        tiles_per_seq, seqs_per_tile = 1, TM // L
        hy_spec = pl.BlockSpec((L, seqs_per_tile * HY_WIDTH), lambda i: (0, i))
    once = lambda a: _per_layer(a, lw["layer"], single_buffer=True)
    row = lambda w: pl.BlockSpec((TM, w), lambda i: (i, 0))
    ws = (lw["gate_w"], lw["gate_b"], lw["hy_out"], lw["mla_out"], lw["ret_out"], lw["conf_out"], lw["w_o"])
    return pl.pallas_call(
        functools.partial(_merge_kernel, seqs_per_tile=seqs_per_tile),
        grid=(T // TM,),
        in_specs=[row(D_MODEL), _mod_spec(lw["layer"], tiles_per_seq, latent),
                  once(lw["norm1_g"]), hy_spec, row(256), row(256), row(256)] + [once(w) for w in ws],
        out_specs=row(D_MODEL),
        out_shape=jax.ShapeDtypeStruct((T, D_MODEL), F32),
        compiler_params=_cparams(("arbitrary",), 56),
        name="merge",
    )(x, mod, lw["norm1_g"], y_hy, y_mla, y_ret, y_conf, *ws)


def _ffn_kernel(x_ref, mod_ref, g2_ref, w1_ref, w2_ref, fg_ref, o_ref, *, final):
    x = x_ref[...]
    h2 = (_rms(x, g2_ref[...]) * (1.0 + mod_ref[4:5, :]) + mod_ref[3:4, :]).astype(BF16)
    acc = None
    for c0 in range(0, D_FF, FFN_CHUNK):
        c1 = min(c0 + FFN_CHUNK, D_FF)
        a = _dot(h2, w1_ref[:, c0:c1])
        b = _dot(h2, w1_ref[:, D_FF + c0:D_FF + c1])
        part = _dot((_silu(a) * b).astype(BF16), w2_ref[c0:c1, :])
        acc = part if acc is None else acc + part
    out = x + mod_ref[5:6, :] * acc
    if final:
        out = _rms(out, fg_ref[...])
    o_ref[...] = out


def _ffn(x, mod, lw, final_g, B, L, latent, final):
    T = B * L
    TM = WIDE_TOKEN_TILE
    tiles_per_seq = max(L // TM, 1)
    once = lambda a: _per_layer(a, lw["layer"], single_buffer=True)
    row = pl.BlockSpec((TM, D_MODEL), lambda i: (i, 0))
    ws = (lw["norm2_g"], lw["ffn_w1"], lw["ffn_w2"])
    return pl.pallas_call(
        functools.partial(_ffn_kernel, final=final),
        grid=(T // TM,),
        in_specs=[row, _mod_spec(lw["layer"], tiles_per_seq, latent)]
        + [once(w) for w in ws] + [pl.BlockSpec(final_g.shape, lambda i: (0, 0))],
        out_specs=row,
        out_shape=jax.ShapeDtypeStruct((T, D_MODEL), F32),
        compiler_params=_cparams(("arbitrary",), 56),
        name="ffn",
    )(x, mod, *ws, final_g)


def _trunk_layer(x, mod, lw, final_g, tables, B, L, ctx, state, final):
    latent = ctx is not None
    hv, hx1, hx2, qh, kh, vh, u_ret, u_conf, kt_ret, *cache = _inproj(x, mod, lw, B, L, latent)
    ckv, kr = cache if cache else (None, None)
    spectra = _filter_spectra(lw, L, tables)
    y_hy = _hyena(hv, hx1, hx2, lw, spectra, tables, B, L)
    y_mla = _mla(qh, kh, vh, lw, B, L, ctx)
    y_ret, S = _retention(u_ret, kt_ret, lw, B, L, state, emit_state=not latent)
    y_conf = _conformer(u_conf, lw, B, L)
    x = _merge(x, mod, lw, y_hy, y_mla, y_ret, y_conf, B, L, latent)
    x = _ffn(x, mod, lw, final_g, B, L, latent, final)
    return x, ckv, kr, S


def _regroup_kernel(w_ref, o_ref):
    w = w_ref[...]
    rows = w.shape[0]
    rope_key = HY_COLS + MLA_COLS - MLA_ROPE
    zeros = lambda n: jnp.zeros((rows, n), F32)
    o_ref[...] = jnp.concatenate(
        [w[:, :rope_key], zeros(MLA_NOPE), w[:, rope_key:rope_key + MLA_ROPE],
         zeros(HEAD_LANES - MLA_NOPE - MLA_ROPE), w[:, rope_key + MLA_ROPE:]], axis=1).astype(BF16)


def _regroup_w_in(w_in):
    depth, d, cols = w_in.shape
    rows = 256
    out_cols = cols - MLA_ROPE + HEAD_LANES
    return pl.pallas_call(
        _regroup_kernel,
        grid=(depth, d // rows),
        in_specs=[pl.BlockSpec((None, rows, cols), lambda l, i: (l, i, 0))],
        out_specs=pl.BlockSpec((None, rows, out_cols), lambda l, i: (l, i, 0)),
        out_shape=jax.ShapeDtypeStruct((depth, d, out_cols), BF16),
        compiler_params=_cparams(("arbitrary", "arbitrary"), 32),
        name="regroup_w_in",
    )(w_in)


def _stacked_weights(w_in, p):
    depth = w_in.shape[0]
    w_in_all = _regroup_w_in(w_in)
    dq = MLA_NOPE + MLA_ROPE
    head_pad = lambda a: jnp.pad(a, ((0, 0),) * 3 + ((0, HEAD_LANES - a.shape[3]),)).reshape(
        depth, a.shape[1], MLA_HEADS * HEAD_LANES)
    w_uq = head_pad(p["mla_w_uq"].reshape(depth, MLA_Q_LORA, MLA_HEADS, dq))
    w_ukv = p["mla_w_ukv"].reshape(depth, MLA_KV_LORA, MLA_HEADS, MLA_NOPE + MLA_V)
    w_ukv = jnp.concatenate([head_pad(w_ukv[..., :MLA_NOPE]), head_pad(w_ukv[..., MLA_NOPE:])], axis=2)
    row = lambda name: p[name].reshape(depth, 1, -1)
    rows = ("norm1_g", "norm2_g", "mla_q_norm", "mla_kv_norm", "hy_conv_b", "hy_b1", "hy_b2",
            "conf_dw_b", "conf_ln_g", "conf_ln_b", "gate_b")
    as_is = ("hy_conv_w", "hy_w2", "hy_w3", "hy_bias", "ret_decay", "conf_dw_w")
    bf16 = ("gate_w", "hy_out", "mla_out", "ret_out", "conf_out", "w_o", "ffn_w1", "ffn_w2")
    return {
        "w_in": w_in_all,
        "mla_w_uq": w_uq.astype(BF16), "mla_w_ukv": w_ukv.astype(BF16),
        "hy_w1": jnp.pad(p["hy_w1"], ((0, 0), (0, 128 - HY_EMB), (0, 0))),
        **{name: row(name) for name in rows},
        **{name: p[name] for name in as_is},
        **{name: p[name].astype(BF16) for name in bf16},
    }


def kernel(x_prompt, x_sample, cache_mla_ckv, cache_mla_krope, state_ret, c, c_ctx, ada_w, ada_b, norm1_g, w_in, hy_conv_w, hy_conv_b, hy_w1, hy_b1, hy_w2, hy_b2, hy_w3, hy_bias, hy_out, mla_q_norm, mla_w_uq, mla_kv_norm, mla_w_ukv, mla_out, ret_decay, ret_out, conf_dw_w, conf_dw_b, conf_ln_g, conf_ln_b, conf_out, gate_w, gate_b, w_o, norm2_g, ffn_w1, ffn_w2, final_norm_g):
    p = dict(norm1_g=norm1_g, hy_conv_w=hy_conv_w, hy_conv_b=hy_conv_b, hy_w1=hy_w1, hy_b1=hy_b1,
             hy_w2=hy_w2, hy_b2=hy_b2, hy_w3=hy_w3, hy_bias=hy_bias, hy_out=hy_out,
             mla_q_norm=mla_q_norm, mla_w_uq=mla_w_uq, mla_kv_norm=mla_kv_norm, mla_w_ukv=mla_w_ukv,
             mla_out=mla_out, ret_decay=ret_decay, ret_out=ret_out, conf_dw_w=conf_dw_w,
             conf_dw_b=conf_dw_b, conf_ln_g=conf_ln_g, conf_ln_b=conf_ln_b, conf_out=conf_out,
             gate_w=gate_w, gate_b=gate_b, w_o=w_o, norm2_g=norm2_g, ffn_w1=ffn_w1, ffn_w2=ffn_w2)
    Bp, Lp, D = x_prompt.shape
    Bs, Ls, _ = x_sample.shape
    depth = w_in.shape[0]

    cond = jnp.concatenate([c_ctx[None, :], c, jnp.zeros((MOD_ROWS - 1 - Bs, D), F32)], axis=0)
    mod = _modulation(cond, ada_w, ada_b)
    tables_p = _dft_tables(Lp)
    tables_s = _dft_tables(Ls)
    final_g = final_norm_g.reshape(1, D)

    xp = x_prompt.reshape(Bp * Lp, D)
    xs = x_sample.reshape(Bs * Ls, D)
    ckvs, kropes, rets = [], [], []
    weights = _stacked_weights(w_in, p)
    for l in range(depth):
        lw = dict(weights, layer=l)
        final = l == depth - 1
        xp, ckv, kr, S = _trunk_layer(xp, mod, lw, final_g, tables_p, Bp, Lp, None, None, final)
        ckvs.append(ckv.reshape(Bp, Lp, MLA_KV_LORA))
        kropes.append(kr.reshape(Bp, Lp, MLA_ROPE))
        rets.append(S)
        xs, _, _, _ = _trunk_layer(xs, mod, lw, final_g, tables_s, Bs, Ls,
                                   (cache_mla_ckv, cache_mla_krope), state_ret, final)
    return (xp.reshape(Bp, Lp, D), xs.reshape(Bs, Ls, D),
            jnp.stack(ckvs, axis=1), jnp.stack(kropes, axis=1), jnp.stack(rets, axis=1))
```

```python
import functools
import math

import jax
import jax.numpy as jnp
from jax import lax
from jax.experimental import pallas as pl
from jax.experimental.pallas import tpu as pltpu

F32 = jnp.float32
BF16 = jnp.bfloat16

D_MODEL = 1024
DEPTH = 2
PAST_LEN = 256
EPS = 1e-6
GRID_W = 64

HY_WIDTH = 256
HY_EMB = 33
HY_BANDS = (HY_EMB - 1) // 2
HY_FFN = 64
HY_FAST_DECAY = 0.3
HY_SLOW_DECAY = 1.5
HY_TARGET = 1e-2

MLA_HEADS = 4
MLA_Q_LORA = 256
MLA_KV_LORA = 128
MLA_NOPE = 64
MLA_ROPE = 32
MLA_V = 64
ROPE_BASE = 10000.0

RET_HEADS = 4
RET_DK = 64
RET_DV = 64
RET_BLOCK = 256

CONF_WIDTH = 256
CONF_KERNEL = 31

D_FF = ((8 * D_MODEL // 3 + 255) // 256) * 256

HY_COLS = 3 * HY_WIDTH
MLA_COLS = MLA_Q_LORA + MLA_KV_LORA + MLA_ROPE
RET_COLS = 2 * RET_HEADS * RET_DK + 2 * RET_HEADS * RET_DV
CONF_COLS = 2 * CONF_WIDTH

VMEM_BYTES_V7X = 64 * 1024 * 1024
SUBLANES = 8
LANES = 128
TOKEN_TILE = 512
WIDE_TOKEN_TILE = 1024
Q_TILE = 512
MLA_SOFTMAX_ROWS = 16
DFT_TILE = 512
SHORT_SEQ_ROWS = 1024
HYENA_GROUP_ROWS = 1024
HYENA_ROWS = 1024
CONF_ROWS = 128
REGROUP_ROWS = 256
CONF_HALO = 16
MXU_DIM_V7X = 256
FFN_CHUNK = 4 * MXU_DIM_V7X
MOD_ROWS = 8


def _cparams(sem, vmem_mb):
    vmem_bytes = vmem_mb * 1024 * 1024
    assert vmem_bytes < VMEM_BYTES_V7X
    return pltpu.CompilerParams(dimension_semantics=sem, vmem_limit_bytes=vmem_bytes)


def _dot(a, b):
    return jnp.dot(a, b, preferred_element_type=F32)


def _dot_nt(a, b):
    return lax.dot_general(a, b, (((1,), (1,)), ((), ())), preferred_element_type=F32)


def _dot_exact(a, b):
    return jnp.dot(a, b, preferred_element_type=F32, precision=lax.Precision.HIGHEST)


def _dot_split(a, b):
    a_hi, b_hi = a.astype(BF16), b.astype(BF16)
    a_lo = (a - a_hi.astype(F32)).astype(BF16)
    b_lo = (b - b_hi.astype(F32)).astype(BF16)
    return _dot(a_hi, b_hi) + (_dot(a_lo, b_hi) + _dot(a_hi, b_lo))


def _rms(x, g):
    return x * lax.rsqrt(jnp.mean(x * x, axis=-1, keepdims=True) + EPS) * g


def _sigmoid(x):
    return 0.5 * jnp.tanh(0.5 * x) + 0.5


def _silu(x):
    return x * _sigmoid(x)


def _mod_kernel(c_ref, w_ref, b_ref, o_ref):
    s = _silu(c_ref[...]).astype(BF16)
    o_ref[...] = _dot(s, w_ref[...].astype(BF16)) + b_ref[...]


def _modulation(cond, ada_w, ada_b):
    depth, d, cols = ada_w.shape
    blk = 1024
    out = pl.pallas_call(
        _mod_kernel,
        grid=(depth, cols // blk),
        in_specs=[
            pl.BlockSpec((MOD_ROWS, d), lambda l, j: (0, 0)),
            pl.BlockSpec((None, d, blk), lambda l, j: (l, 0, j)),
            pl.BlockSpec((None, 1, blk), lambda l, j: (l, 0, j)),
        ],
        out_specs=pl.BlockSpec((None, MOD_ROWS, blk), lambda l, j: (l, 0, j)),
        out_shape=jax.ShapeDtypeStruct((depth, MOD_ROWS, cols), F32),
        compiler_params=_cparams(("arbitrary", "arbitrary"), 32),
        name="modulation",
    )(cond, ada_w, ada_b.reshape(depth, 1, cols))
    return out.reshape(depth, MOD_ROWS, 6, d)


def _seqs_per_step(B, L):
    return max(1, min(B, SHORT_SEQ_ROWS // L))


def _per_layer(a, layer, single_buffer=False):
    mode = dict(pipeline_mode=pl.Buffered(1)) if single_buffer else {}
    return pl.BlockSpec((None,) + a.shape[1:], lambda *_: (layer,) + (0,) * (a.ndim - 1), **mode)


def _mod_spec(layer, tiles_per_seq, latent):
    group = (lambda i: 1 + i // tiles_per_seq) if latent else (lambda i: 0)
    return pl.BlockSpec((None, None, 6, D_MODEL), lambda i: (layer, group(i), 0, 0))


HEAD_LANES = LANES


def _rope_lanes(x, cos, sa, sb):
    reps = x.shape[1] // HEAD_LANES
    wide = lambda t: jnp.concatenate([t] * reps, axis=1) if reps > 1 else t
    half = MLA_ROPE // 2
    return (x * wide(cos) + pltpu.roll(x, x.shape[1] - half, 1) * wide(sa)
            + pltpu.roll(x, half, 1) * wide(sb))


def _inproj_kernel(*refs, seqs_per_tile, seq_len, latent):
    x_ref, mod_ref, g1_ref, win_ref, qn_ref, kvn_ref, wuq_ref, wukv_ref = refs[:8]
    refs = refs[8:]
    c_cq = HY_COLS
    c_ck = c_cq + MLA_Q_LORA
    c_ret = c_ck + MLA_KV_LORA + HEAD_LANES
    c_conf = c_ret + RET_COLS
    why_ref, wcq_ref, wck_ref, wret_ref, wconf_ref = (
        win_ref.at[:, a:b] for a, b in ((0, c_cq), (c_cq, c_ck), (c_ck, c_ret), (c_ret, c_conf),
                                        (c_conf, c_conf + CONF_COLS)))
    if latent:
        cos_ref, sa_ref, sb_ref = refs[:3]
        refs = refs[3:]
    hv_ref, hx1_ref, hx2_ref, q_ref, kh_ref, vh_ref, ret_ref, conf_ref, rkt_ref = refs[:9]
    if not latent:
        ckv_ref, kr_ref = refs[9:]
    x = x_ref[...]
    h = _rms(x, g1_ref[...]) * (1.0 + mod_ref[1:2, :]) + mod_ref[0:1, :]
    hb = h.astype(BF16)

    u = _dot(hb, why_ref[...])
    for p, o_ref in enumerate((hv_ref, hx1_ref, hx2_ref)):
        part = u[:, p * HY_WIDTH:(p + 1) * HY_WIDTH]
        if seqs_per_tile == 1:
            o_ref[...] = part
        else:
            for s in range(seqs_per_tile):
                o_ref[:, s * HY_WIDTH:(s + 1) * HY_WIDTH] = part[s * seq_len:(s + 1) * seq_len]

    heads_w = MLA_HEADS * HEAD_LANES
    cq = _dot(hb, wcq_ref[...])
    q = _dot(_rms(cq, qn_ref[...]).astype(BF16), wuq_ref[...])
    ck = _dot(hb, wck_ref[...])
    ckv = _rms(ck[:, :MLA_KV_LORA], kvn_ref[...])
    kr_block = ck[:, MLA_KV_LORA:MLA_KV_LORA + HEAD_LANES]
    if latent:
        q = _rope_lanes(q, cos_ref[...], sa_ref[...], sb_ref[...])
        kr_block = _rope_lanes(kr_block, cos_ref[...], sa_ref[...], sb_ref[...])
    else:
        ckv_ref[...] = ckv
        kr_ref[...] = kr_block[:, MLA_NOPE:MLA_NOPE + MLA_ROPE]
    q_ref[...] = (q * ((MLA_NOPE + MLA_ROPE) ** -0.5 * math.log2(math.e))).astype(BF16)
    kv = _dot(ckv.astype(BF16), wukv_ref[...])
    kh_ref[...] = (kv[:, :heads_w] + jnp.concatenate([kr_block] * MLA_HEADS, axis=1)).astype(BF16)
    lane = lax.broadcasted_iota(jnp.int32, (1, heads_w), 1)
    ones_lane = jnp.where(lane % HEAD_LANES == MLA_V, 1.0, 0.0)
    vh_ref[...] = (kv[:, heads_w:] + ones_lane).astype(BF16)
    ret = _dot(hb, wret_ref[...])
    ret_ref[...] = ret
    qk = RET_HEADS * RET_DK
    rkt_ref[...] = ret[:, qk:2 * qk].T
    conf_ref[...] = _dot(hb, wconf_ref[...])


def _inproj(x, mod, lw, B, L, latent):
    T = B * L
    TM = TOKEN_TILE
    nt = T // TM
    if L >= TM:
        tiles_per_seq, seqs_per_tile = L // TM, 1
        hy_block = (TM, HY_WIDTH)
        hy_map = lambda i: (i % tiles_per_seq, i // tiles_per_seq)
    else:
        tiles_per_seq, seqs_per_tile = 1, TM // L
        hy_block = (L, seqs_per_tile * HY_WIDTH)
        hy_map = lambda i: (0, i)
    row = lambda w: pl.BlockSpec((TM, w), lambda i: (i, 0))
    weights = (lw["norm1_g"], lw["w_in"], lw["mla_q_norm"], lw["mla_kv_norm"], lw["mla_w_uq"], lw["mla_w_ukv"])
    hy_shape = jax.ShapeDtypeStruct((L, B * HY_WIDTH), F32)
    qk = RET_HEADS * RET_DK
    heads_w = MLA_HEADS * HEAD_LANES
    ins = [x, mod, *weights]
    in_specs = ([row(D_MODEL), _mod_spec(lw["layer"], tiles_per_seq, latent)]
                + [_per_layer(w, lw["layer"]) for w in weights])
    out_specs = ([pl.BlockSpec(hy_block, hy_map)] * 3
                 + [row(heads_w)] * 3 + [row(RET_COLS), row(CONF_COLS), pl.BlockSpec((qk, TM), lambda i: (0, i))])
    out_shape = ([hy_shape] * 3 + [jax.ShapeDtypeStruct((T, heads_w), BF16)] * 3
                 + [jax.ShapeDtypeStruct((T, RET_COLS), F32), jax.ShapeDtypeStruct((T, CONF_COLS), F32),
                    jax.ShapeDtypeStruct((qk, T), F32)])
    if latent:
        ins += list(_rope_lane_tables(L))
        in_specs += [pl.BlockSpec((TM, HEAD_LANES), lambda i: (i % tiles_per_seq, 0))] * 3
    else:
        out_specs += [row(MLA_KV_LORA), row(MLA_ROPE)]
        out_shape += [jax.ShapeDtypeStruct((T, MLA_KV_LORA), F32), jax.ShapeDtypeStruct((T, MLA_ROPE), F32)]
    return pl.pallas_call(
        functools.partial(_inproj_kernel, seqs_per_tile=seqs_per_tile, seq_len=L, latent=latent),
        grid=(nt,),
        in_specs=in_specs,
        out_specs=out_specs,
        out_shape=out_shape,
        compiler_params=_cparams(("arbitrary",), 48),
        name="inproj",
    )(*ins)


def _dft_tables(L):
    N = 2 * L
    k_lo = min(L, 32)
    k_hi = L // k_lo
    t = jnp.arange(L, dtype=jnp.int32)[None, :]

    def cs(k):
        m = (k[:, None] * t) % N
        ang = m.astype(F32) * (2.0 * math.pi / N)
        return jnp.cos(ang), jnp.sin(ang)

    ca, sa = cs(jnp.arange(k_hi, dtype=jnp.int32) * k_lo)
    cb, sb = cs(jnp.arange(k_lo, dtype=jnp.int32))
    cos = (ca[:, None, :] * cb[None, :, :] - sa[:, None, :] * sb[None, :, :]).reshape(L, L)
    sin = (sa[:, None, :] * cb[None, :, :] + ca[:, None, :] * sb[None, :, :]).reshape(L, L)
    return cos.astype(BF16), (-sin).astype(BF16)


def _filter_features(L):
    t = jnp.linspace(0.0, 1.0, L, dtype=F32)[:, None]
    w = 2.0 * math.pi * jnp.arange(L, dtype=F32)[:, None] / L
    f = jnp.linspace(1e-4, HY_BANDS - 1, HY_BANDS, dtype=F32)[None, :]
    z = jnp.concatenate([t, jnp.cos(f * w), -jnp.sin(f * w)], axis=-1)
    z = jnp.pad(z, ((0, 0), (0, LANES - HY_EMB)))
    max_decay = math.log(HY_TARGET) / HY_FAST_DECAY
    min_decay = math.log(HY_TARGET) / HY_SLOW_DECAY
    deltas = jnp.abs(jnp.linspace(min_decay, max_decay, HY_WIDTH, dtype=F32))
    decay = jnp.exp(-t * deltas[None, :])
    return z, decay


def _alternating(rows):
    t = lax.broadcasted_iota(jnp.int32, (rows, 1), 0)
    return (1 - 2 * (t & 1)).astype(F32)


def _filter_kernel(z_ref, dec_ref, w1_ref, b1_ref, w2_ref, b2_ref, w3_ref, cos_ref, sin_ref,
                   kr_ref, ki_ref, kn_ref, hsum_ref, hdiff_ref, *, L):
    j = pl.program_id(0)
    W = HY_WIDTH
    N = 2 * L

    @pl.when(j == 0)
    def _():
        h = jnp.sin(_dot_exact(z_ref[...], w1_ref[...]) + b1_ref[...])
        h = jnp.sin(_dot_exact(h, w2_ref[...]) + b2_ref[...])
        h = _dot_split(h, w3_ref[...]) * jnp.concatenate([dec_ref[...]] * 4, axis=1)
        cs = jnp.sum(jnp.abs(h), axis=0, keepdims=True)
        s0 = cs[:, 0:W] + cs[:, W:2 * W]
        s1 = cs[:, 2 * W:3 * W] + cs[:, 3 * W:4 * W]
        h = h / jnp.concatenate([s0, s0, s1, s1], axis=1)
        row = lax.broadcasted_iota(jnp.int32, h.shape, 0)
        col = lax.broadcasted_iota(jnp.int32, h.shape, 1)
        backward = (col // W) % 2 == 1
        h = jnp.where(backward & (row == 0), 0.0, h)
        fwd = jnp.concatenate([h[:, 0:W], h[:, 2 * W:3 * W]], axis=1)
        bwd = jnp.concatenate([h[:, W:2 * W], h[:, 3 * W:4 * W]], axis=1)
        hsum_ref[...] = (fwd + bwd).astype(BF16)
        hdiff_ref[...] = (fwd - bwd).astype(BF16)
        nyq = jnp.sum((fwd + bwd) * _alternating(L), axis=0, keepdims=True) * (1.0 / N)
        for o in range(2):
            kn_ref[o] = nyq[:, o * W:(o + 1) * W]

    sr = _dot(cos_ref[...], hsum_ref[...])
    si = _dot(sin_ref[...], hdiff_ref[...])
    row = lax.broadcasted_iota(jnp.int32, (sr.shape[0], W), 0)
    scale = jnp.where((row == 0) & (j == 0), 1.0 / N, 2.0 / N)
    for o in range(2):
        kr_ref[o] = sr[:, o * W:(o + 1) * W] * scale
        ki_ref[o] = si[:, o * W:(o + 1) * W] * scale


def _filter_spectra(lw, L, tables):
    z, decay = _filter_features(L)
    cos, msin = tables
    Tk = min(DFT_TILE, L)
    full = lambda a: pl.BlockSpec(a.shape, lambda j: (0,) * a.ndim)
    mlp = (lw["hy_w1"], lw["hy_b1"], lw["hy_w2"], lw["hy_b2"], lw["hy_w3"])
    ins = (z, decay, *mlp)
    tile = pl.BlockSpec((Tk, L), lambda j: (j, 0))
    spec = pl.BlockSpec((2, Tk, HY_WIDTH), lambda j: (0, j, 0))
    return pl.pallas_call(
        functools.partial(_filter_kernel, L=L),
        grid=(L // Tk,),
        in_specs=[full(z), full(decay)] + [_per_layer(a, lw["layer"]) for a in mlp] + [tile, tile],
        out_specs=[spec, spec, pl.BlockSpec((2, 1, HY_WIDTH), lambda j: (0, 0, 0))],
        out_shape=[jax.ShapeDtypeStruct((2, L, HY_WIDTH), F32)] * 2
        + [jax.ShapeDtypeStruct((2, 1, HY_WIDTH), F32)],
        scratch_shapes=[pltpu.VMEM((L, 2 * HY_WIDTH), BF16)] * 2,
        compiler_params=_cparams(("arbitrary",), 48),
        name="hyena_filter",
    )(*ins, cos, msin)


def _hyena_kernel(v_ref, x1_ref, x2_ref, cw_ref, cb_ref, bias_ref, kr_ref, ki_ref, kn_ref,
                  cos_ref, sin_ref, y_ref, cur_ref, curb_ref, yr_ref, yi_ref, gate_ref,
                  *, L, W, T):
    reps = W // HY_WIDTH
    tiled = lambda a: jnp.concatenate([a] * reps, axis=1) if reps > 1 else a
    alt = _alternating(L)

    def short_conv(u_ref, p):
        u = u_ref[...]
        row = lax.broadcasted_iota(jnp.int32, (L, W), 0)
        prev = jnp.where(row == 0, 0.0, pltpu.roll(u, 1, 0))
        nxt = jnp.where(row == L - 1, 0.0, pltpu.roll(u, L - 1, 0))
        cols = slice(p * HY_WIDTH, (p + 1) * HY_WIDTH)
        w = [tiled(cw_ref[k:k + 1, cols]) for k in range(3)]
        return prev * w[0] + u * w[1] + nxt * w[2] + tiled(cb_ref[:, cols])

    cur_ref[...] = short_conv(v_ref, 0)
    for o, x_ref in enumerate((x1_ref, x2_ref)):
        cur = cur_ref[...]
        curb_ref[...] = cur.astype(BF16)
        gate_ref[...] = short_conv(x_ref, o + 1)
        nyq = jnp.sum(cur * alt, axis=0, keepdims=True) * tiled(kn_ref[o])
        for f in range(L // T):
            rows = slice(f * T, (f + 1) * T)
            cb = curb_ref[...]
            xr = _dot(cos_ref[rows, :], cb)
            xi = _dot(sin_ref[rows, :], cb)
            kr, ki = tiled(kr_ref[o, rows, :]), tiled(ki_ref[o, rows, :])
            yr_ref[rows, :] = (xr * kr - xi * ki).astype(BF16)
            yi_ref[rows, :] = (xr * ki + xi * kr).astype(BF16)
        for t in range(L // T):
            rows = slice(t * T, (t + 1) * T)
            conv = (_dot(cos_ref[rows, :], yr_ref[...]) + _dot(sin_ref[rows, :], yi_ref[...])
                    + alt[rows] * nyq + cur_ref[rows, :] * tiled(bias_ref[o:o + 1, :]))
            out = gate_ref[rows, :] * conv
            if o == 0:
                cur_ref[rows, :] = out
            else:
                y_ref[rows, :] = out


def _hyena(hv, hx1, hx2, lw, spectra, tables, B, L):
    kr, ki, kn = spectra
    cos, msin = tables
    T = min(HYENA_ROWS, L)
    W = HY_WIDTH * max(1, min(B, HYENA_GROUP_ROWS // L))
    ng = (B * HY_WIDTH) // W
    col = pl.BlockSpec((L, W), lambda g: (0, g))
    once = lambda a: pl.BlockSpec(a.shape, lambda g: (0,) * a.ndim, pipeline_mode=pl.Buffered(1))
    params = (lw["hy_conv_w"], lw["hy_conv_b"], lw["hy_bias"])
    consts = (kr, ki, kn, cos, msin)
    return pl.pallas_call(
        functools.partial(_hyena_kernel, L=L, W=W, T=T),
        grid=(ng,),
        in_specs=[col, col, col] + [_per_layer(a, lw["layer"]) for a in params] + [once(a) for a in consts],
        out_specs=col,
        out_shape=jax.ShapeDtypeStruct((L, B * HY_WIDTH), F32),
        scratch_shapes=[pltpu.VMEM((L, W), F32), pltpu.VMEM((L, W), BF16), pltpu.VMEM((L, W), BF16),
                        pltpu.VMEM((L, W), BF16), pltpu.VMEM((L, W), F32)],
        compiler_params=_cparams(("arbitrary",), 60),
        name="hyena_conv",
    )(hv, hx1, hx2, *params, *consts)


def _mla_kernel(*refs, L, latent, seqs):
    if latent:
        q_ref, kh_ref, vh_ref, cckv_ref, ckr_ref, wukv_ref, o_ref, s_ref, p_ref, kctx_ref, vctx_ref = refs
    else:
        q_ref, kh_ref, vh_ref, o_ref, s_ref, p_ref = refs
    TQ = q_ref.shape[0] // seqs
    heads_w = MLA_HEADS * HEAD_LANES
    block = lambda h: slice(h * HEAD_LANES, (h + 1) * HEAD_LANES)

    if latent:
        @pl.when(pl.program_id(1) == 0)
        def _():
            kv = _dot(cckv_ref[...].astype(BF16), wukv_ref[...])
            zeros = lambda w: jnp.zeros((PAST_LEN, w), F32)
            kr_block = jnp.concatenate(
                [zeros(MLA_NOPE), ckr_ref[...], zeros(HEAD_LANES - MLA_NOPE - MLA_ROPE)], axis=1)
            kctx_ref[...] = (kv[:, :heads_w] + jnp.concatenate([kr_block] * MLA_HEADS, axis=1)).astype(BF16)
            lane = lax.broadcasted_iota(jnp.int32, (1, heads_w), 1)
            vctx_ref[...] = (kv[:, heads_w:] + jnp.where(lane % HEAD_LANES == MLA_V, 1.0, 0.0)).astype(BF16)

    def attend(q_rows, k_rows):
        def scores(h):
            qh = q_ref[q_rows, block(h)]
            s_ref[h, :, 0:L] = _dot_nt(qh, kh_ref[k_rows, block(h)])
            if latent:
                s_ref[h, :, L:L + PAST_LEN] = _dot_nt(qh, kctx_ref[:, block(h)])

        outs = []
        scores(0)
        for h in range(MLA_HEADS):
            if h + 1 < MLA_HEADS:
                scores(h + 1)
            for r in range(TQ // MLA_SOFTMAX_ROWS):
                rows = slice(r * MLA_SOFTMAX_ROWS, (r + 1) * MLA_SOFTMAX_ROWS)
                s = s_ref[h, rows, :]
                p_ref[h, rows, :] = jnp.exp2(s - jnp.max(s, axis=-1, keepdims=True)).astype(BF16)
            pv = _dot(p_ref[h, :, 0:L], vh_ref[k_rows, block(h)])
            if latent:
                pv = pv + _dot(p_ref[h, :, L:L + PAST_LEN], vctx_ref[:, block(h)])
            outs.append(pv[:, :MLA_V] / pv[:, MLA_V:MLA_V + 1])
        o_ref[q_rows, :] = jnp.concatenate(outs, axis=-1)

    for sq in range(seqs):
        attend(slice(sq * TQ, (sq + 1) * TQ), slice(sq * L, (sq + 1) * L))


def _rope_tables(L):
    rows = L // GRID_W
    row = jnp.repeat(jnp.arange(rows), GRID_W).astype(F32)
    col = jnp.tile(jnp.arange(GRID_W), rows).astype(F32)
    per_axis = MLA_ROPE // 4
    inv = ROPE_BASE ** (-jnp.arange(per_axis, dtype=F32) / per_axis)
    ang = jnp.concatenate([row[:, None] * inv, col[:, None] * inv], axis=-1)
    return jnp.cos(ang), jnp.sin(ang)


def _rope_lane_tables(L):
    cos, sin = _rope_tables(L)
    pad = HEAD_LANES - MLA_NOPE - MLA_ROPE
    one, zero = jnp.ones((L, MLA_NOPE), F32), jnp.zeros((L, MLA_NOPE), F32)
    half0 = jnp.zeros_like(sin)
    cos_t = jnp.concatenate([one, cos, cos, one[:, :pad]], axis=1)
    sa_t = jnp.concatenate([zero, -sin, half0, zero[:, :pad]], axis=1)
    sb_t = jnp.concatenate([zero, half0, sin, zero[:, :pad]], axis=1)
    return cos_t, sa_t, sb_t


def _mla(qh, kh, vh, lw, B, L, ctx):
    latent = ctx is not None
    TQ = min(Q_TILE, L)
    nq = L // TQ
    seqs = 1 if latent or nq > 1 else _seqs_per_step(B, L)
    Lk = L + PAST_LEN if latent else L
    heads_w = MLA_HEADS * HEAD_LANES
    seq = pl.BlockSpec((seqs * L, heads_w), lambda b, i: (b, 0))
    ins = [qh, kh, vh]
    specs = [pl.BlockSpec((seqs * TQ, heads_w), lambda b, i: (b * nq + i, 0)), seq, seq]
    scratch = [pltpu.VMEM((MLA_HEADS, TQ, Lk), F32), pltpu.VMEM((MLA_HEADS, TQ, Lk), BF16)]
    if latent:
        ins += [ctx[0], ctx[1], lw["mla_w_ukv"]]
        layer = lw["layer"]
        specs += [pl.BlockSpec((None, None, PAST_LEN, MLA_KV_LORA), lambda b, i: (b, layer, 0, 0)),
                  pl.BlockSpec((None, None, PAST_LEN, MLA_ROPE), lambda b, i: (b, layer, 0, 0)),
                  _per_layer(lw["mla_w_ukv"], lw["layer"])]
        scratch += [pltpu.VMEM((PAST_LEN, heads_w), BF16)] * 2
    return pl.pallas_call(
        functools.partial(_mla_kernel, L=L, latent=latent, seqs=seqs),
        grid=(B // seqs, nq),
        in_specs=specs,
        out_specs=pl.BlockSpec((seqs * TQ, MLA_HEADS * MLA_V), lambda b, i: (b * nq + i, 0)),
        out_shape=jax.ShapeDtypeStruct((B * L, MLA_HEADS * MLA_V), F32),
        scratch_shapes=scratch,
        compiler_params=_cparams(("arbitrary", "arbitrary"), 48),
        name="mla_attention",
    )(*ins)


def _ret_kernel(*refs, L, seqs, has_state, emit_state):
    refs = list(refs)
    u_ref, kt_ref, dl_ref = refs[:3]
    pos = 3
    s0_ref = None
    if has_state:
        s0_ref = refs[pos]
        pos += 1
    y_ref = refs[pos]
    pos += 1
    sout_ref = None
    if emit_state:
        sout_ref = refs[pos]
        pos += 1
    of_ref, kv_ref, sin_ref, dcomb_ref, dtab_ref, hmask_ref, avg_ref = refs[pos:]

    C = RET_BLOCK
    n = L // C
    H, DK, DV = RET_HEADS, RET_DK, RET_DV
    qk = H * DK
    DK_F, DK_B, DQ_F, DQ_B, DC_F, DC_B, MASK = range(7)
    use_cross = has_state or n > 1

    @pl.when(pl.program_id(0) == 0)
    def _():
        x = dl_ref[...]
        log_g = jnp.minimum(x, 0.0) - jnp.log1p(jnp.exp(-jnp.abs(x)))
        gf = [log_g[0:1, h:h + 1] for h in range(H)]
        gb = [log_g[1:2, h:h + 1] for h in range(H)]
        diff = (lax.broadcasted_iota(jnp.int32, (C, C), 0)
                - lax.broadcasted_iota(jnp.int32, (C, C), 1)).astype(F32)
        idx = lax.broadcasted_iota(jnp.int32, (C, 1), 0).astype(F32)
        for h in range(H):
            dcomb_ref[h] = (jnp.where(diff >= 0, jnp.exp(jnp.maximum(diff, 0.0) * gf[h]), 0.0)
                            + jnp.where(diff <= 0, jnp.exp(jnp.maximum(-diff, 0.0) * gb[h]), 0.0))

        def head_cols(lag, g):
            return jnp.concatenate([jnp.broadcast_to(jnp.exp(lag * g[h]), (C, DK)) for h in range(H)], axis=1)

        def head_rows(lag, g):
            return jnp.concatenate(
                [jnp.broadcast_to(jnp.exp(lag * g[h]) * (DK ** -0.5), (DK, C)) for h in range(H)], axis=0)

        tok = lax.broadcasted_iota(jnp.int32, (1, C), 1).astype(F32)
        dtab_ref[DK_F] = head_rows(C - 1.0 - tok, gf)
        dtab_ref[DK_B] = head_rows(tok, gb)
        dtab_ref[DQ_F] = head_cols(idx + 1.0, gf)
        dtab_ref[DQ_B] = head_cols(C - idx, gb)
        row_head = lax.broadcasted_iota(jnp.int32, (qk, qk), 0) // DK
        col_head = lax.broadcasted_iota(jnp.int32, (qk, qk), 1) // DV
        same_head = row_head == col_head
        for slot, g in ((DC_F, gf), (DC_B, gb)):
            dc = jnp.zeros((qk, qk), F32)
            for h in range(H):
                dc = jnp.where(same_head & (row_head == h), jnp.exp(C * g[h]), dc)
            dtab_ref[slot] = dc
        dtab_ref[MASK] = same_head.astype(F32)
        avg_ref[...] = jnp.where(same_head, 1.0 / DV, 0.0).astype(BF16)
        lane_head = lax.broadcasted_iota(jnp.int32, (C, qk), 1) // DK
        for h in range(H):
            hmask_ref[h] = (lane_head == h).astype(F32)

    def head_mean(a):
        total = None
        for _ in range(3):
            part = a.astype(BF16)
            a = a - part.astype(F32)
            term = _dot(part, avg_ref[...])
            total = term if total is None else total + term
        return total

    for sq in range(seqs):
        chunk = lambda c: slice(sq * L + c * C, sq * L + (c + 1) * C)

        for c in range(n):
            rows = chunk(c)
            q = u_ref[rows, 0:qk]
            kb = (u_ref[rows, qk:2 * qk] * (DK ** -0.5)).astype(BF16)
            v = u_ref[rows, 2 * qk:2 * qk + H * DV]
            att = [(_dot_nt((q * hmask_ref[h]).astype(BF16), kb) * dcomb_ref[h]).astype(BF16)
                   for h in range(H)]
            v_heads = jnp.concatenate([(v * hmask_ref[h]).astype(BF16) for h in range(H)], axis=0)
            of_ref[c * C:(c + 1) * C, :] = _dot(jnp.concatenate(att, axis=1), v_heads)
            vb = v.astype(BF16)
            kt = kt_ref[:, rows]
            for d, slot in ((0, DK_F), (1, DK_B)):
                kv_ref[d, c] = _dot((kt * dtab_ref[slot]).astype(BF16), vb)

        finals = []
        for d, slot in ((0, DC_F), (1, DC_B)):
            if has_state:
                zero = jnp.zeros((DK, DV), F32)
                S = jnp.concatenate(
                    [jnp.concatenate([s0_ref[sq, d, h] if g == h else zero for g in range(H)], axis=1)
                     for h in range(H)], axis=0)
            else:
                S = jnp.zeros((qk, H * DV), F32)
            for c in (range(n) if d == 0 else reversed(range(n))):
                if use_cross:
                    sin_ref[d, c] = S.astype(BF16)
                S = S * dtab_ref[slot] + kv_ref[d, c] * dtab_ref[MASK]
            finals.append(S)
        if emit_state:
            for d in range(2):
                for h in range(H):
                    sout_ref[sq, d, h] = finals[d][h * DK:(h + 1) * DK, h * DV:(h + 1) * DV]

        for c in range(n):
            rows = chunk(c)
            tot = of_ref[c * C:(c + 1) * C, :]
            if use_cross:
                qb = u_ref[rows, 0:qk].astype(BF16)
                tot = (tot + _dot(qb, sin_ref[0, c]) * dtab_ref[DQ_F]
                       + _dot(qb, sin_ref[1, c]) * dtab_ref[DQ_B])
            xc = tot - head_mean(tot)
            normed = xc * lax.rsqrt(head_mean(xc * xc) + EPS)
            gate = u_ref[rows, 2 * qk + H * DV:2 * qk + 2 * H * DV]
            y_ref[rows, :] = _silu(gate) * normed


def _retention(u_ret, kt_ret, lw, B, L, state, emit_state):
    has_state = state is not None
    seqs = _seqs_per_step(B, L)
    st_dims = (2, RET_HEADS, RET_DK, RET_DV)
    ins = [u_ret, kt_ret, lw["ret_decay"]]
    specs = [pl.BlockSpec((seqs * L, RET_COLS), lambda b: (b, 0)),
             pl.BlockSpec((RET_HEADS * RET_DK, seqs * L), lambda b: (0, b)),
             _per_layer(lw["ret_decay"], lw["layer"])]
    if has_state:
        layer = lw["layer"]
        ins.append(state)
        specs.append(pl.BlockSpec((seqs, None) + st_dims, lambda b: (b, layer, 0, 0, 0, 0)))
    vd = RET_HEADS * RET_DV
    out_specs = [pl.BlockSpec((seqs * L, vd), lambda b: (b, 0))]
    out_shape = [jax.ShapeDtypeStruct((B * L, vd), F32)]
    if emit_state:
        out_specs.append(pl.BlockSpec((seqs,) + st_dims, lambda b: (b, 0, 0, 0, 0)))
        out_shape.append(jax.ShapeDtypeStruct((B,) + st_dims, F32))
    res = pl.pallas_call(
        functools.partial(_ret_kernel, L=L, seqs=seqs, has_state=has_state, emit_state=emit_state),
        grid=(B // seqs,),
        in_specs=specs,
        out_specs=out_specs,
        out_shape=out_shape,
        scratch_shapes=[pltpu.VMEM((L, vd), F32),
                        pltpu.VMEM((2, L // RET_BLOCK, vd, vd), F32),
                        pltpu.VMEM((2, L // RET_BLOCK, vd, vd), BF16),
                        pltpu.VMEM((RET_HEADS, RET_BLOCK, RET_BLOCK), F32),
                        pltpu.VMEM((7, RET_BLOCK, vd), F32),
                        pltpu.VMEM((RET_HEADS, RET_BLOCK, vd), F32),
                        pltpu.VMEM((vd, vd), BF16)],
        compiler_params=_cparams(("arbitrary",), 48),
        name="retention",
    )(*ins)
    return (res[0], res[1]) if emit_state else (res[0], None)


def _conf_kernel(u_ref, w_ref, b_ref, g_ref, be_ref, y_ref, zp_ref, sh_ref, *, L, seqs):
    Wc = CONF_WIDTH
    halo = CONF_HALO
    zp_ref[0:halo, :] = jnp.zeros((halo, Wc), F32)
    zp_ref[halo + L:2 * halo + L, :] = jnp.zeros((halo, Wc), F32)
    first = halo - CONF_KERNEL // 2
    span = sh_ref.shape[1]
    R = CONF_ROWS
    for sq in range(seqs):
        r0 = sq * L
        zp_ref[halo:halo + L, :] = u_ref[r0:r0 + L, 0:Wc] * _sigmoid(u_ref[r0:r0 + L, Wc:2 * Wc])
        for s in range(SUBLANES):
            sh_ref[s] = zp_ref[first + s:first + s + span, :]
        for c in range(L // R):
            acc = jnp.broadcast_to(b_ref[...], (R, Wc))
            for k in range(CONF_KERNEL):
                a, s = divmod(k, SUBLANES)
                acc = acc + w_ref[k:k + 1, :] * sh_ref[s, c * R + SUBLANES * a:c * R + SUBLANES * a + R, :]
            mu = jnp.mean(acc, axis=-1, keepdims=True)
            xc = acc - mu
            z = xc * lax.rsqrt(jnp.mean(xc * xc, axis=-1, keepdims=True) + EPS) * g_ref[...] + be_ref[...]
            y_ref[r0 + c * R:r0 + (c + 1) * R, :] = _silu(z)


def _conformer(u_conf, lw, B, L):
    ws = (lw["conf_dw_w"], lw["conf_dw_b"], lw["conf_ln_g"], lw["conf_ln_b"])
    seqs = _seqs_per_step(B, L)
    return pl.pallas_call(
        functools.partial(_conf_kernel, L=L, seqs=seqs),
        grid=(B // seqs,),
        in_specs=[pl.BlockSpec((seqs * L, CONF_COLS), lambda b: (b, 0))]
        + [_per_layer(w, lw["layer"]) for w in ws],
        out_specs=pl.BlockSpec((seqs * L, CONF_WIDTH), lambda b: (b, 0)),
        out_shape=jax.ShapeDtypeStruct((B * L, CONF_WIDTH), F32),
        scratch_shapes=[pltpu.VMEM((L + 2 * CONF_HALO, CONF_WIDTH), F32),
                        pltpu.VMEM((SUBLANES, L + SUBLANES * ((CONF_KERNEL - 1) // SUBLANES), CONF_WIDTH), F32)],
        compiler_params=_cparams(("arbitrary",), 48),
        name="conformer",
    )(u_conf, *ws)


def _merge_kernel(x_ref, mod_ref, g1_ref, yhy_ref, ymla_ref, yret_ref, yconf_ref,
                  gw_ref, gb_ref, why_ref, wmla_ref, wret_ref, wconf_ref, wo_ref, o_ref, *, seqs_per_tile):
    x = x_ref[...]
    h = _rms(x, g1_ref[...]) * (1.0 + mod_ref[1:2, :]) + mod_ref[0:1, :]
    hb = h.astype(BF16)
    if seqs_per_tile == 1:
        yhy = yhy_ref[...]
    else:
        yhy = jnp.concatenate(
            [yhy_ref[:, s * HY_WIDTH:(s + 1) * HY_WIDTH] for s in range(seqs_per_tile)], axis=0)
    branches = ((yhy, why_ref), (ymla_ref[...], wmla_ref), (yret_ref[...], wret_ref), (yconf_ref[...], wconf_ref))
    D = D_MODEL
    merged = None
    for i, (y, w_ref) in enumerate(branches):
        gate = _sigmoid(_dot(hb, gw_ref[:, i * D:(i + 1) * D]) + gb_ref[:, i * D:(i + 1) * D])
        term = gate * _dot(y.astype(BF16), w_ref[...])
        merged = term if merged is None else merged + term
    o_ref[...] = x + mod_ref[2:3, :] * _dot(merged.astype(BF16), wo_ref[...])


def _merge(x, mod, lw, y_hy, y_mla, y_ret, y_conf, B, L, latent):
    T = B * L
    TM = WIDE_TOKEN_TILE
    if L >= TM:
        tiles_per_seq, seqs_per_tile = L // TM, 1
        hy_spec = pl.BlockSpec((TM, HY_WIDTH), lambda i: (i % tiles_per_seq, i // tiles_per_seq))
    else:
        tiles_per_seq, seqs_per_tile = 1, TM // L
        hy_spec = pl.BlockSpec((L, seqs_per_tile * HY_WIDTH), lambda i: (0, i))
    once = lambda a: _per_layer(a, lw["layer"], single_buffer=True)
    row = lambda w: pl.BlockSpec((TM, w), lambda i: (i, 0))
    ws = (lw["gate_w"], lw["gate_b"], lw["hy_out"], lw["mla_out"], lw["ret_out"], lw["conf_out"], lw["w_o"])
    return pl.pallas_call(
        functools.partial(_merge_kernel, seqs_per_tile=seqs_per_tile),
        grid=(T // TM,),
        in_specs=[row(D_MODEL), _mod_spec(lw["layer"], tiles_per_seq, latent),
                  once(lw["norm1_g"]), hy_spec, row(MLA_HEADS * MLA_V), row(RET_HEADS * RET_DV),
                  row(CONF_WIDTH)] + [once(w) for w in ws],
        out_specs=row(D_MODEL),
        out_shape=jax.ShapeDtypeStruct((T, D_MODEL), F32),
        compiler_params=_cparams(("arbitrary",), 56),
        name="merge",
    )(x, mod, lw["norm1_g"], y_hy, y_mla, y_ret, y_conf, *ws)


def _ffn_kernel(x_ref, mod_ref, g2_ref, w1_ref, w2_ref, fg_ref, o_ref, *, final):
    x = x_ref[...]
    h2 = (_rms(x, g2_ref[...]) * (1.0 + mod_ref[4:5, :]) + mod_ref[3:4, :]).astype(BF16)
    acc = None
    for c0 in range(0, D_FF, FFN_CHUNK):
        c1 = min(c0 + FFN_CHUNK, D_FF)
        a = _dot(h2, w1_ref[:, c0:c1])
        b = _dot(h2, w1_ref[:, D_FF + c0:D_FF + c1])
        part = _dot((_silu(a) * b).astype(BF16), w2_ref[c0:c1, :])
        acc = part if acc is None else acc + part
    out = x + mod_ref[5:6, :] * acc
    if final:
        out = _rms(out, fg_ref[...])
    o_ref[...] = out


def _ffn(x, mod, lw, final_g, B, L, latent, final):
    T = B * L
    TM = WIDE_TOKEN_TILE
    tiles_per_seq = max(L // TM, 1)
    once = lambda a: _per_layer(a, lw["layer"], single_buffer=True)
    row = pl.BlockSpec((TM, D_MODEL), lambda i: (i, 0))
    ws = (lw["norm2_g"], lw["ffn_w1"], lw["ffn_w2"])
    return pl.pallas_call(
        functools.partial(_ffn_kernel, final=final),
        grid=(T // TM,),
        in_specs=[row, _mod_spec(lw["layer"], tiles_per_seq, latent)]
        + [once(w) for w in ws] + [pl.BlockSpec(final_g.shape, lambda i: (0, 0))],
        out_specs=row,
        out_shape=jax.ShapeDtypeStruct((T, D_MODEL), F32),
        compiler_params=_cparams(("arbitrary",), 56),
        name="ffn",
    )(x, mod, *ws, final_g)


def _trunk_layer(x, mod, lw, final_g, tables, B, L, ctx, state, final):
    latent = ctx is not None
    hv, hx1, hx2, qh, kh, vh, u_ret, u_conf, kt_ret, *cache = _inproj(x, mod, lw, B, L, latent)
    ckv, kr = cache if cache else (None, None)
    spectra = _filter_spectra(lw, L, tables)
    y_hy = _hyena(hv, hx1, hx2, lw, spectra, tables, B, L)
    y_mla = _mla(qh, kh, vh, lw, B, L, ctx)
    y_ret, S = _retention(u_ret, kt_ret, lw, B, L, state, emit_state=not latent)
    y_conf = _conformer(u_conf, lw, B, L)
    x = _merge(x, mod, lw, y_hy, y_mla, y_ret, y_conf, B, L, latent)
    x = _ffn(x, mod, lw, final_g, B, L, latent, final)
    return x, ckv, kr, S


def _regroup_kernel(w_ref, o_ref):
    w = w_ref[...]
    rows = w.shape[0]
    rope_key = HY_COLS + MLA_COLS - MLA_ROPE
    zeros = lambda n: jnp.zeros((rows, n), F32)
    o_ref[...] = jnp.concatenate(
        [w[:, :rope_key], zeros(MLA_NOPE), w[:, rope_key:rope_key + MLA_ROPE],
         zeros(HEAD_LANES - MLA_NOPE - MLA_ROPE), w[:, rope_key + MLA_ROPE:]], axis=1).astype(BF16)


def _regroup_w_in(w_in):
    depth, d, cols = w_in.shape
    rows = REGROUP_ROWS
    out_cols = cols - MLA_ROPE + HEAD_LANES
    return pl.pallas_call(
        _regroup_kernel,
        grid=(depth, d // rows),
        in_specs=[pl.BlockSpec((None, rows, cols), lambda l, i: (l, i, 0))],
        out_specs=pl.BlockSpec((None, rows, out_cols), lambda l, i: (l, i, 0)),
        out_shape=jax.ShapeDtypeStruct((depth, d, out_cols), BF16),
        compiler_params=_cparams(("arbitrary", "arbitrary"), 32),
        name="regroup_w_in",
    )(w_in)


def _stacked_weights(w_in, p):
    depth = w_in.shape[0]
    w_in_all = _regroup_w_in(w_in)
    dq = MLA_NOPE + MLA_ROPE
    head_pad = lambda a: jnp.pad(a, ((0, 0),) * 3 + ((0, HEAD_LANES - a.shape[3]),)).reshape(
        depth, a.shape[1], MLA_HEADS * HEAD_LANES)
    w_uq = head_pad(p["mla_w_uq"].reshape(depth, MLA_Q_LORA, MLA_HEADS, dq))
    w_ukv = p["mla_w_ukv"].reshape(depth, MLA_KV_LORA, MLA_HEADS, MLA_NOPE + MLA_V)
    w_ukv = jnp.concatenate([head_pad(w_ukv[..., :MLA_NOPE]), head_pad(w_ukv[..., MLA_NOPE:])], axis=2)
    row = lambda name: p[name].reshape(depth, 1, -1)
    rows = ("norm1_g", "norm2_g", "mla_q_norm", "mla_kv_norm", "hy_conv_b", "hy_b1", "hy_b2",
            "conf_dw_b", "conf_ln_g", "conf_ln_b", "gate_b")
    as_is = ("hy_conv_w", "hy_w2", "hy_w3", "hy_bias", "ret_decay", "conf_dw_w")
    bf16 = ("gate_w", "hy_out", "mla_out", "ret_out", "conf_out", "w_o", "ffn_w1", "ffn_w2")
    return {
        "w_in": w_in_all,
        "mla_w_uq": w_uq.astype(BF16), "mla_w_ukv": w_ukv.astype(BF16),
        "hy_w1": jnp.pad(p["hy_w1"], ((0, 0), (0, LANES - HY_EMB), (0, 0))),
        **{name: row(name) for name in rows},
        **{name: p[name] for name in as_is},
        **{name: p[name].astype(BF16) for name in bf16},
    }


def kernel(x_prompt, x_sample, cache_mla_ckv, cache_mla_krope, state_ret, c, c_ctx, ada_w, ada_b, norm1_g, w_in, hy_conv_w, hy_conv_b, hy_w1, hy_b1, hy_w2, hy_b2, hy_w3, hy_bias, hy_out, mla_q_norm, mla_w_uq, mla_kv_norm, mla_w_ukv, mla_out, ret_decay, ret_out, conf_dw_w, conf_dw_b, conf_ln_g, conf_ln_b, conf_out, gate_w, gate_b, w_o, norm2_g, ffn_w1, ffn_w2, final_norm_g):
    p = dict(norm1_g=norm1_g, hy_conv_w=hy_conv_w, hy_conv_b=hy_conv_b, hy_w1=hy_w1, hy_b1=hy_b1,
             hy_w2=hy_w2, hy_b2=hy_b2, hy_w3=hy_w3, hy_bias=hy_bias, hy_out=hy_out,
             mla_q_norm=mla_q_norm, mla_w_uq=mla_w_uq, mla_kv_norm=mla_kv_norm, mla_w_ukv=mla_w_ukv,
             mla_out=mla_out, ret_decay=ret_decay, ret_out=ret_out, conf_dw_w=conf_dw_w,
             conf_dw_b=conf_dw_b, conf_ln_g=conf_ln_g, conf_ln_b=conf_ln_b, conf_out=conf_out,
             gate_w=gate_w, gate_b=gate_b, w_o=w_o, norm2_g=norm2_g, ffn_w1=ffn_w1, ffn_w2=ffn_w2)
    Bp, Lp, D = x_prompt.shape
    Bs, Ls, _ = x_sample.shape
    depth = w_in.shape[0]

    cond = jnp.concatenate([c_ctx[None, :], c, jnp.zeros((MOD_ROWS - 1 - Bs, D), F32)], axis=0)
    mod = _modulation(cond, ada_w, ada_b)
    tables_p = _dft_tables(Lp)
    tables_s = _dft_tables(Ls)
    final_g = final_norm_g.reshape(1, D)

    xp = x_prompt.reshape(Bp * Lp, D)
    xs = x_sample.reshape(Bs * Ls, D)
    ckvs, kropes, rets = [], [], []
    weights = _stacked_weights(w_in, p)
    for l in range(depth):
        lw = dict(weights, layer=l)
        final = l == depth - 1
        xp, ckv, kr, S = _trunk_layer(xp, mod, lw, final_g, tables_p, Bp, Lp, None, None, final)
        ckvs.append(ckv.reshape(Bp, Lp, MLA_KV_LORA))
        kropes.append(kr.reshape(Bp, Lp, MLA_ROPE))
        rets.append(S)
        xs, _, _, _ = _trunk_layer(xs, mod, lw, final_g, tables_s, Bs, Ls,
                                   (cache_mla_ckv, cache_mla_krope), state_ret, final)
    return (xp.reshape(Bp, Lp, D), xs.reshape(Bs, Ls, D),
            jnp.stack(ckvs, axis=1), jnp.stack(kropes, axis=1), jnp.stack(rets, axis=1))
```

```python
import functools
import math

import jax
import jax.numpy as jnp
from jax import lax
from jax.experimental import pallas as pl
from jax.experimental.pallas import tpu as pltpu

F32 = jnp.float32
BF16 = jnp.bfloat16

D_MODEL = 1024
DEPTH = 2
PAST_LEN = 256
EPS = 1e-6
GRID_W = 64

HY_WIDTH = 256
HY_EMB = 33
HY_BANDS = (HY_EMB - 1) // 2
HY_FFN = 64
HY_FAST_DECAY = 0.3
HY_SLOW_DECAY = 1.5
HY_TARGET = 1e-2

MLA_HEADS = 4
MLA_Q_LORA = 256
MLA_KV_LORA = 128
MLA_NOPE = 64
MLA_ROPE = 32
MLA_V = 64
ROPE_BASE = 10000.0

RET_HEADS = 4
RET_DK = 64
RET_DV = 64
RET_BLOCK = 256

CONF_WIDTH = 256
CONF_KERNEL = 31

D_FF = ((8 * D_MODEL // 3 + 255) // 256) * 256

HY_COLS = 3 * HY_WIDTH
MLA_COLS = MLA_Q_LORA + MLA_KV_LORA + MLA_ROPE
RET_COLS = 2 * RET_HEADS * RET_DK + 2 * RET_HEADS * RET_DV
CONF_COLS = 2 * CONF_WIDTH

VMEM_BYTES_V7X = 64 * 1024 * 1024
SUBLANES = 8
LANES = 128
TOKEN_TILE = 512
WIDE_TOKEN_TILE = 1024
Q_TILE = 512
MLA_SOFTMAX_ROWS = 16
DFT_TILE = 512
SHORT_SEQ_ROWS = 1024
HYENA_GROUP_ROWS = 1024
HYENA_ROWS = 1024
CONF_ROWS = 128
REGROUP_ROWS = 256
CONF_HALO = 16
MXU_DIM_V7X = 256
FFN_CHUNK = 4 * MXU_DIM_V7X
MOD_ROWS = 8


def _cparams(sem, vmem_mb):
    vmem_bytes = vmem_mb * 1024 * 1024
    assert vmem_bytes < VMEM_BYTES_V7X
    return pltpu.CompilerParams(dimension_semantics=sem, vmem_limit_bytes=vmem_bytes)


def _dot(a, b):
    return jnp.dot(a, b, preferred_element_type=F32)


def _dot_nt(a, b):
    return lax.dot_general(a, b, (((1,), (1,)), ((), ())), preferred_element_type=F32)


def _dot_exact(a, b):
    return jnp.dot(a, b, preferred_element_type=F32, precision=lax.Precision.HIGHEST)


def _dot_split(a, b):
    a_hi, b_hi = a.astype(BF16), b.astype(BF16)
    a_lo = (a - a_hi.astype(F32)).astype(BF16)
    b_lo = (b - b_hi.astype(F32)).astype(BF16)
    return _dot(a_hi, b_hi) + (_dot(a_lo, b_hi) + _dot(a_hi, b_lo))


def _rms(x, g):
    return x * lax.rsqrt(jnp.mean(x * x, axis=-1, keepdims=True) + EPS) * g


def _sigmoid(x):
    return 0.5 * jnp.tanh(0.5 * x) + 0.5


def _silu(x):
    return x * _sigmoid(x)


def _mod_kernel(c_ref, w_ref, b_ref, o_ref):
    s = _silu(c_ref[...]).astype(BF16)
    o_ref[...] = _dot(s, w_ref[...].astype(BF16)) + b_ref[...]


def _modulation(cond, ada_w, ada_b):
    depth, d, cols = ada_w.shape
    blk = 1024
    out = pl.pallas_call(
        _mod_kernel,
        grid=(depth, cols // blk),
        in_specs=[
            pl.BlockSpec((MOD_ROWS, d), lambda l, j: (0, 0)),
            pl.BlockSpec((None, d, blk), lambda l, j: (l, 0, j)),
            pl.BlockSpec((None, 1, blk), lambda l, j: (l, 0, j)),
        ],
        out_specs=pl.BlockSpec((None, MOD_ROWS, blk), lambda l, j: (l, 0, j)),
        out_shape=jax.ShapeDtypeStruct((depth, MOD_ROWS, cols), F32),
        compiler_params=_cparams(("arbitrary", "arbitrary"), 32),
        name="modulation",
    )(cond, ada_w, ada_b.reshape(depth, 1, cols))
    return out.reshape(depth, MOD_ROWS, 6, d)


def _seqs_per_step(B, L):
    return max(1, min(B, SHORT_SEQ_ROWS // L))


def _per_layer(a, layer, single_buffer=False):
    mode = dict(pipeline_mode=pl.Buffered(1)) if single_buffer else {}
    return pl.BlockSpec((None,) + a.shape[1:], lambda *_: (layer,) + (0,) * (a.ndim - 1), **mode)


def _mod_spec(layer, tiles_per_seq, latent):
    group = (lambda i: 1 + i // tiles_per_seq) if latent else (lambda i: 0)
    return pl.BlockSpec((None, None, 6, D_MODEL), lambda i: (layer, group(i), 0, 0))


HEAD_LANES = LANES


def _rope_lanes(x, cos, sa, sb):
    reps = x.shape[1] // HEAD_LANES
    wide = lambda t: jnp.concatenate([t] * reps, axis=1) if reps > 1 else t
    half = MLA_ROPE // 2
    return (x * wide(cos) + pltpu.roll(x, x.shape[1] - half, 1) * wide(sa)
            + pltpu.roll(x, half, 1) * wide(sb))


def _inproj_kernel(*refs, seqs_per_tile, seq_len, latent, cache_layers, aliased):
    x_ref, mod_ref, g1_ref, win_ref, qn_ref, kvn_ref, wuq_ref, wukv_ref = refs[:8]
    refs = refs[8:]
    c_cq = HY_COLS
    c_ck = c_cq + MLA_Q_LORA
    c_ret = c_ck + MLA_KV_LORA + HEAD_LANES
    c_conf = c_ret + RET_COLS
    why_ref, wcq_ref, wck_ref, wret_ref, wconf_ref = (
        win_ref.at[:, a:b] for a, b in ((0, c_cq), (c_cq, c_ck), (c_ck, c_ret), (c_ret, c_conf),
                                        (c_conf, c_conf + CONF_COLS)))
    if latent:
        cos_ref, sa_ref, sb_ref = refs[:3]
        refs = refs[3:]
    if aliased:
        refs = refs[2:]
    hv_ref, hx1_ref, hx2_ref, q_ref, kh_ref, vh_ref, ret_ref, conf_ref, rkt_ref = refs[:9]
    if not latent:
        ckv_ref, kr_ref = refs[9:]
    x = x_ref[...]
    h = _rms(x, g1_ref[...]) * (1.0 + mod_ref[1:2, :]) + mod_ref[0:1, :]
    hb = h.astype(BF16)

    u = _dot(hb, why_ref[...])
    for p, o_ref in enumerate((hv_ref, hx1_ref, hx2_ref)):
        part = u[:, p * HY_WIDTH:(p + 1) * HY_WIDTH]
        if seqs_per_tile == 1:
            o_ref[...] = part
        else:
            for s in range(seqs_per_tile):
                o_ref[:, s * HY_WIDTH:(s + 1) * HY_WIDTH] = part[s * seq_len:(s + 1) * seq_len]

    heads_w = MLA_HEADS * HEAD_LANES
    cq = _dot(hb, wcq_ref[...])
    q = _dot(_rms(cq, qn_ref[...]).astype(BF16), wuq_ref[...])
    ck = _dot(hb, wck_ref[...])
    ckv = _rms(ck[:, :MLA_KV_LORA], kvn_ref[...])
    kr_block = ck[:, MLA_KV_LORA:MLA_KV_LORA + HEAD_LANES]
    if latent:
        q = _rope_lanes(q, cos_ref[...], sa_ref[...], sb_ref[...])
        kr_block = _rope_lanes(kr_block, cos_ref[...], sa_ref[...], sb_ref[...])
    else:
        kr = kr_block[:, MLA_NOPE:MLA_NOPE + MLA_ROPE]
        for s in range(seqs_per_tile):
            rows = slice(s * seq_len, (s + 1) * seq_len)
            for out_ref, val in ((ckv_ref, ckv), (kr_ref, kr)):
                mine = out_ref.at[s, 0] if cache_layers else out_ref.at[s]
                mine[...] = val[rows]
                for other in range(1, cache_layers):
                    out_ref[s, other] = jnp.zeros(out_ref.shape[2:], F32)
    q_ref[...] = (q * ((MLA_NOPE + MLA_ROPE) ** -0.5 * math.log2(math.e))).astype(BF16)
    kv = _dot(ckv.astype(BF16), wukv_ref[...])
    kh_ref[...] = (kv[:, :heads_w] + jnp.concatenate([kr_block] * MLA_HEADS, axis=1)).astype(BF16)
    lane = lax.broadcasted_iota(jnp.int32, (1, heads_w), 1)
    ones_lane = jnp.where(lane % HEAD_LANES == MLA_V, 1.0, 0.0)
    vh_ref[...] = (kv[:, heads_w:] + ones_lane).astype(BF16)
    ret = _dot(hb, wret_ref[...])
    ret_ref[...] = ret
    qk = RET_HEADS * RET_DK
    rkt_ref[...] = ret[:, qk:2 * qk].T
    conf_ref[...] = _dot(hb, wconf_ref[...])


def _inproj(x, mod, lw, B, L, latent, caches_out=None):
    T = B * L
    TM = TOKEN_TILE
    nt = T // TM
    if L >= TM:
        tiles_per_seq, seqs_per_tile = L // TM, 1
        hy_block = (TM, HY_WIDTH)
        hy_map = lambda i: (i % tiles_per_seq, i // tiles_per_seq)
    else:
        tiles_per_seq, seqs_per_tile = 1, TM // L
        hy_block = (L, seqs_per_tile * HY_WIDTH)
        hy_map = lambda i: (0, i)
    row = lambda w: pl.BlockSpec((TM, w), lambda i: (i, 0))
    weights = (lw["norm1_g"], lw["w_in"], lw["mla_q_norm"], lw["mla_kv_norm"], lw["mla_w_uq"], lw["mla_w_ukv"])
    hy_shape = jax.ShapeDtypeStruct((L, B * HY_WIDTH), F32)
    qk = RET_HEADS * RET_DK
    heads_w = MLA_HEADS * HEAD_LANES
    ins = [x, mod, *weights]
    in_specs = ([row(D_MODEL), _mod_spec(lw["layer"], tiles_per_seq, latent)]
                + [_per_layer(w, lw["layer"]) for w in weights])
    out_specs = ([pl.BlockSpec(hy_block, hy_map)] * 3
                 + [row(heads_w)] * 3 + [row(RET_COLS), row(CONF_COLS), pl.BlockSpec((qk, TM), lambda i: (0, i))])
    out_shape = ([hy_shape] * 3 + [jax.ShapeDtypeStruct((T, heads_w), BF16)] * 3
                 + [jax.ShapeDtypeStruct((T, RET_COLS), F32), jax.ShapeDtypeStruct((T, CONF_COLS), F32),
                    jax.ShapeDtypeStruct((qk, T), F32)])
    if latent:
        ins += list(_rope_lane_tables(L))
        in_specs += [pl.BlockSpec((TM, HEAD_LANES), lambda i: (i % tiles_per_seq, 0))] * 3
    aliases = {}
    cache_layers = 0
    if not latent:
        assert tiles_per_seq == 1
        layer, depth = lw["layer"], lw["w_in"].shape[0]
        widths = (MLA_KV_LORA, MLA_ROPE)
        if caches_out is None:
            cache_layers = depth
            out_specs += [pl.BlockSpec((seqs_per_tile, depth, L, w), lambda i: (i, 0, 0, 0)) for w in widths]
        else:
            aliases = {len(ins) + k: len(out_shape) + k for k in range(len(widths))}
            ins += list(caches_out)
            in_specs += [pl.BlockSpec(memory_space=pl.ANY)] * len(widths)
            out_specs += [pl.BlockSpec((seqs_per_tile, None, L, w), lambda i: (i, layer, 0, 0)) for w in widths]
        out_shape += [jax.ShapeDtypeStruct((B, depth, L, w), F32) for w in widths]
    return pl.pallas_call(
        functools.partial(_inproj_kernel, seqs_per_tile=seqs_per_tile, seq_len=L, latent=latent,
                          cache_layers=cache_layers, aliased=bool(aliases)),
        grid=(nt,),
        in_specs=in_specs,
        out_specs=out_specs,
        out_shape=out_shape,
        input_output_aliases=aliases,
        compiler_params=_cparams(("arbitrary",), 48),
        name="inproj",
    )(*ins)


def _dft_tables(L):
    N = 2 * L
    k_lo = min(L, 32)
    k_hi = L // k_lo
    t = jnp.arange(L, dtype=jnp.int32)[None, :]

    def cs(k):
        m = (k[:, None] * t) % N
        ang = m.astype(F32) * (2.0 * math.pi / N)
        return jnp.cos(ang), jnp.sin(ang)

    ca, sa = cs(jnp.arange(k_hi, dtype=jnp.int32) * k_lo)
    cb, sb = cs(jnp.arange(k_lo, dtype=jnp.int32))
    cos = (ca[:, None, :] * cb[None, :, :] - sa[:, None, :] * sb[None, :, :]).reshape(L, L)
    sin = (sa[:, None, :] * cb[None, :, :] + ca[:, None, :] * sb[None, :, :]).reshape(L, L)
    return cos.astype(BF16), (-sin).astype(BF16)


def _filter_features(L):
    t = jnp.linspace(0.0, 1.0, L, dtype=F32)[:, None]
    w = 2.0 * math.pi * jnp.arange(L, dtype=F32)[:, None] / L
    f = jnp.linspace(1e-4, HY_BANDS - 1, HY_BANDS, dtype=F32)[None, :]
    z = jnp.concatenate([t, jnp.cos(f * w), -jnp.sin(f * w)], axis=-1)
    z = jnp.pad(z, ((0, 0), (0, LANES - HY_EMB)))
    max_decay = math.log(HY_TARGET) / HY_FAST_DECAY
    min_decay = math.log(HY_TARGET) / HY_SLOW_DECAY
    deltas = jnp.abs(jnp.linspace(min_decay, max_decay, HY_WIDTH, dtype=F32))
    decay = jnp.exp(-t * deltas[None, :])
    return z, decay


def _alternating(rows):
    t = lax.broadcasted_iota(jnp.int32, (rows, 1), 0)
    return (1 - 2 * (t & 1)).astype(F32)


def _filter_kernel(z_ref, dec_ref, w1_ref, b1_ref, w2_ref, b2_ref, w3_ref, cos_ref, sin_ref,
                   kr_ref, ki_ref, kn_ref, hsum_ref, hdiff_ref, *, L):
    j = pl.program_id(0)
    W = HY_WIDTH
    N = 2 * L

    @pl.when(j == 0)
    def _():
        h = jnp.sin(_dot_exact(z_ref[...], w1_ref[...]) + b1_ref[...])
        h = jnp.sin(_dot_exact(h, w2_ref[...]) + b2_ref[...])
        h = _dot_split(h, w3_ref[...]) * jnp.concatenate([dec_ref[...]] * 4, axis=1)
        cs = jnp.sum(jnp.abs(h), axis=0, keepdims=True)
        s0 = cs[:, 0:W] + cs[:, W:2 * W]
        s1 = cs[:, 2 * W:3 * W] + cs[:, 3 * W:4 * W]
        h = h / jnp.concatenate([s0, s0, s1, s1], axis=1)
        row = lax.broadcasted_iota(jnp.int32, h.shape, 0)
        col = lax.broadcasted_iota(jnp.int32, h.shape, 1)
        backward = (col // W) % 2 == 1
        h = jnp.where(backward & (row == 0), 0.0, h)
        fwd = jnp.concatenate([h[:, 0:W], h[:, 2 * W:3 * W]], axis=1)
        bwd = jnp.concatenate([h[:, W:2 * W], h[:, 3 * W:4 * W]], axis=1)
        hsum_ref[...] = (fwd + bwd).astype(BF16)
        hdiff_ref[...] = (fwd - bwd).astype(BF16)
        nyq = jnp.sum((fwd + bwd) * _alternating(L), axis=0, keepdims=True) * (1.0 / N)
        for o in range(2):
            kn_ref[o] = nyq[:, o * W:(o + 1) * W]

    sr = _dot(cos_ref[...], hsum_ref[...])
    si = _dot(sin_ref[...], hdiff_ref[...])
    row = lax.broadcasted_iota(jnp.int32, (sr.shape[0], W), 0)
    scale = jnp.where((row == 0) & (j == 0), 1.0 / N, 2.0 / N)
    for o in range(2):
        kr_ref[o] = sr[:, o * W:(o + 1) * W] * scale
        ki_ref[o] = si[:, o * W:(o + 1) * W] * scale


def _filter_spectra(lw, L, tables):
    z, decay = _filter_features(L)
    cos, msin = tables
    Tk = min(DFT_TILE, L)
    full = lambda a: pl.BlockSpec(a.shape, lambda j: (0,) * a.ndim)
    mlp = (lw["hy_w1"], lw["hy_b1"], lw["hy_w2"], lw["hy_b2"], lw["hy_w3"])
    ins = (z, decay, *mlp)
    tile = pl.BlockSpec((Tk, L), lambda j: (j, 0))
    spec = pl.BlockSpec((2, Tk, HY_WIDTH), lambda j: (0, j, 0))
    return pl.pallas_call(
        functools.partial(_filter_kernel, L=L),
        grid=(L // Tk,),
        in_specs=[full(z), full(decay)] + [_per_layer(a, lw["layer"]) for a in mlp] + [tile, tile],
        out_specs=[spec, spec, pl.BlockSpec((2, 1, HY_WIDTH), lambda j: (0, 0, 0))],
        out_shape=[jax.ShapeDtypeStruct((2, L, HY_WIDTH), F32)] * 2
        + [jax.ShapeDtypeStruct((2, 1, HY_WIDTH), F32)],
        scratch_shapes=[pltpu.VMEM((L, 2 * HY_WIDTH), BF16)] * 2,
        compiler_params=_cparams(("arbitrary",), 48),
        name="hyena_filter",
    )(*ins, cos, msin)


def _hyena_kernel(v_ref, x1_ref, x2_ref, cw_ref, cb_ref, bias_ref, kr_ref, ki_ref, kn_ref,
                  cos_ref, sin_ref, y_ref, cur_ref, curb_ref, yr_ref, yi_ref, gate_ref,
                  *, L, W, T):
    reps = W // HY_WIDTH
    tiled = lambda a: jnp.concatenate([a] * reps, axis=1) if reps > 1 else a
    alt = _alternating(L)

    def short_conv(u_ref, p):
        u = u_ref[...]
        row = lax.broadcasted_iota(jnp.int32, (L, W), 0)
        prev = jnp.where(row == 0, 0.0, pltpu.roll(u, 1, 0))
        nxt = jnp.where(row == L - 1, 0.0, pltpu.roll(u, L - 1, 0))
        cols = slice(p * HY_WIDTH, (p + 1) * HY_WIDTH)
        w = [tiled(cw_ref[k:k + 1, cols]) for k in range(3)]
        return prev * w[0] + u * w[1] + nxt * w[2] + tiled(cb_ref[:, cols])

    cur_ref[...] = short_conv(v_ref, 0)
    for o, x_ref in enumerate((x1_ref, x2_ref)):
        cur = cur_ref[...]
        curb_ref[...] = cur.astype(BF16)
        gate_ref[...] = short_conv(x_ref, o + 1)
        nyq = jnp.sum(cur * alt, axis=0, keepdims=True) * tiled(kn_ref[o])
        for f in range(L // T):
            rows = slice(f * T, (f + 1) * T)
            cb = curb_ref[...]
            xr = _dot(cos_ref[rows, :], cb)
            xi = _dot(sin_ref[rows, :], cb)
            kr, ki = tiled(kr_ref[o, rows, :]), tiled(ki_ref[o, rows, :])
            yr_ref[rows, :] = (xr * kr - xi * ki).astype(BF16)
            yi_ref[rows, :] = (xr * ki + xi * kr).astype(BF16)
        for t in range(L // T):
            rows = slice(t * T, (t + 1) * T)
            conv = (_dot(cos_ref[rows, :], yr_ref[...]) + _dot(sin_ref[rows, :], yi_ref[...])
                    + alt[rows] * nyq + cur_ref[rows, :] * tiled(bias_ref[o:o + 1, :]))
            out = gate_ref[rows, :] * conv
            if o == 0:
                cur_ref[rows, :] = out
            else:
                y_ref[rows, :] = out


def _hyena(hv, hx1, hx2, lw, spectra, tables, B, L):
    kr, ki, kn = spectra
    cos, msin = tables
    T = min(HYENA_ROWS, L)
    W = HY_WIDTH * max(1, min(B, HYENA_GROUP_ROWS // L))
    ng = (B * HY_WIDTH) // W
    col = pl.BlockSpec((L, W), lambda g: (0, g))
    once = lambda a: pl.BlockSpec(a.shape, lambda g: (0,) * a.ndim, pipeline_mode=pl.Buffered(1))
    params = (lw["hy_conv_w"], lw["hy_conv_b"], lw["hy_bias"])
    consts = (kr, ki, kn, cos, msin)
    return pl.pallas_call(
        functools.partial(_hyena_kernel, L=L, W=W, T=T),
        grid=(ng,),
        in_specs=[col, col, col] + [_per_layer(a, lw["layer"]) for a in params] + [once(a) for a in consts],
        out_specs=col,
        out_shape=jax.ShapeDtypeStruct((L, B * HY_WIDTH), F32),
        scratch_shapes=[pltpu.VMEM((L, W), F32), pltpu.VMEM((L, W), BF16), pltpu.VMEM((L, W), BF16),
                        pltpu.VMEM((L, W), BF16), pltpu.VMEM((L, W), F32)],
        compiler_params=_cparams(("arbitrary",), 60),
        name="hyena_conv",
    )(hv, hx1, hx2, *params, *consts)


def _mla_kernel(*refs, L, latent, seqs):
    if latent:
        q_ref, kh_ref, vh_ref, cckv_ref, ckr_ref, wukv_ref, o_ref, s_ref, p_ref, kctx_ref, vctx_ref = refs
    else:
        q_ref, kh_ref, vh_ref, o_ref, s_ref, p_ref = refs
    TQ = q_ref.shape[0] // seqs
    heads_w = MLA_HEADS * HEAD_LANES
    block = lambda h: slice(h * HEAD_LANES, (h + 1) * HEAD_LANES)

    if latent:
        @pl.when(pl.program_id(1) == 0)
        def _():
            kv = _dot(cckv_ref[...].astype(BF16), wukv_ref[...])
            zeros = lambda w: jnp.zeros((PAST_LEN, w), F32)
            kr_block = jnp.concatenate(
                [zeros(MLA_NOPE), ckr_ref[...], zeros(HEAD_LANES - MLA_NOPE - MLA_ROPE)], axis=1)
            kctx_ref[...] = (kv[:, :heads_w] + jnp.concatenate([kr_block] * MLA_HEADS, axis=1)).astype(BF16)
            lane = lax.broadcasted_iota(jnp.int32, (1, heads_w), 1)
            vctx_ref[...] = (kv[:, heads_w:] + jnp.where(lane % HEAD_LANES == MLA_V, 1.0, 0.0)).astype(BF16)

    def attend(q_rows, k_rows):
        def scores(h):
            qh = q_ref[q_rows, block(h)]
            s_ref[h, :, 0:L] = _dot_nt(qh, kh_ref[k_rows, block(h)])
            if latent:
                s_ref[h, :, L:L + PAST_LEN] = _dot_nt(qh, kctx_ref[:, block(h)])

        outs = []
        scores(0)
        for h in range(MLA_HEADS):
            if h + 1 < MLA_HEADS:
                scores(h + 1)
            for r in range(TQ // MLA_SOFTMAX_ROWS):
                rows = slice(r * MLA_SOFTMAX_ROWS, (r + 1) * MLA_SOFTMAX_ROWS)
                s = s_ref[h, rows, :]
                p_ref[h, rows, :] = jnp.exp2(s - jnp.max(s, axis=-1, keepdims=True)).astype(BF16)
            pv = _dot(p_ref[h, :, 0:L], vh_ref[k_rows, block(h)])
            if latent:
                pv = pv + _dot(p_ref[h, :, L:L + PAST_LEN], vctx_ref[:, block(h)])
            outs.append(pv[:, :MLA_V] / pv[:, MLA_V:MLA_V + 1])
        o_ref[q_rows, :] = jnp.concatenate(outs, axis=-1)

    for sq in range(seqs):
        attend(slice(sq * TQ, (sq + 1) * TQ), slice(sq * L, (sq + 1) * L))


def _rope_tables(L):
    rows = L // GRID_W
    row = jnp.repeat(jnp.arange(rows), GRID_W).astype(F32)
    col = jnp.tile(jnp.arange(GRID_W), rows).astype(F32)
    per_axis = MLA_ROPE // 4
    inv = ROPE_BASE ** (-jnp.arange(per_axis, dtype=F32) / per_axis)
    ang = jnp.concatenate([row[:, None] * inv, col[:, None] * inv], axis=-1)
    return jnp.cos(ang), jnp.sin(ang)


def _rope_lane_tables(L):
    cos, sin = _rope_tables(L)
    pad = HEAD_LANES - MLA_NOPE - MLA_ROPE
    one, zero = jnp.ones((L, MLA_NOPE), F32), jnp.zeros((L, MLA_NOPE), F32)
    half0 = jnp.zeros_like(sin)
    cos_t = jnp.concatenate([one, cos, cos, one[:, :pad]], axis=1)
    sa_t = jnp.concatenate([zero, -sin, half0, zero[:, :pad]], axis=1)
    sb_t = jnp.concatenate([zero, half0, sin, zero[:, :pad]], axis=1)
    return cos_t, sa_t, sb_t


def _mla(qh, kh, vh, lw, B, L, ctx):
    latent = ctx is not None
    TQ = min(Q_TILE, L)
    nq = L // TQ
    seqs = 1 if latent or nq > 1 else _seqs_per_step(B, L)
    Lk = L + PAST_LEN if latent else L
    heads_w = MLA_HEADS * HEAD_LANES
    seq = pl.BlockSpec((seqs * L, heads_w), lambda b, i: (b, 0))
    ins = [qh, kh, vh]
    specs = [pl.BlockSpec((seqs * TQ, heads_w), lambda b, i: (b * nq + i, 0)), seq, seq]
    scratch = [pltpu.VMEM((MLA_HEADS, TQ, Lk), F32), pltpu.VMEM((MLA_HEADS, TQ, Lk), BF16)]
    if latent:
        ins += [ctx[0], ctx[1], lw["mla_w_ukv"]]
        layer = lw["layer"]
        specs += [pl.BlockSpec((None, None, PAST_LEN, MLA_KV_LORA), lambda b, i: (b, layer, 0, 0)),
                  pl.BlockSpec((None, None, PAST_LEN, MLA_ROPE), lambda b, i: (b, layer, 0, 0)),
                  _per_layer(lw["mla_w_ukv"], lw["layer"])]
        scratch += [pltpu.VMEM((PAST_LEN, heads_w), BF16)] * 2
    return pl.pallas_call(
        functools.partial(_mla_kernel, L=L, latent=latent, seqs=seqs),
        grid=(B // seqs, nq),
        in_specs=specs,
        out_specs=pl.BlockSpec((seqs * TQ, MLA_HEADS * MLA_V), lambda b, i: (b * nq + i, 0)),
        out_shape=jax.ShapeDtypeStruct((B * L, MLA_HEADS * MLA_V), F32),
        scratch_shapes=scratch,
        compiler_params=_cparams(("arbitrary", "arbitrary"), 48),
        name="mla_attention",
    )(*ins)


def _ret_kernel(*refs, L, seqs, has_state, emit_state, state_layers, aliased):
    refs = list(refs)
    u_ref, kt_ref, dl_ref = refs[:3]
    pos = 3
    s0_ref = None
    if has_state:
        s0_ref = refs[pos]
        pos += 1
    if aliased:
        pos += 1
    y_ref = refs[pos]
    pos += 1
    sout_ref = None
    if emit_state:
        sout_ref = refs[pos]
        pos += 1
    of_ref, kv_ref, sin_ref, dcomb_ref, dtab_ref, hmask_ref, avg_ref = refs[pos:]

    C = RET_BLOCK
    n = L // C
    H, DK, DV = RET_HEADS, RET_DK, RET_DV
    qk = H * DK
    DK_F, DK_B, DQ_F, DQ_B, DC_F, DC_B, MASK = range(7)
    use_cross = has_state or n > 1

    @pl.when(pl.program_id(0) == 0)
    def _():
        x = dl_ref[...]
        log_g = jnp.minimum(x, 0.0) - jnp.log1p(jnp.exp(-jnp.abs(x)))
        gf = [log_g[0:1, h:h + 1] for h in range(H)]
        gb = [log_g[1:2, h:h + 1] for h in range(H)]
        diff = (lax.broadcasted_iota(jnp.int32, (C, C), 0)
                - lax.broadcasted_iota(jnp.int32, (C, C), 1)).astype(F32)
        idx = lax.broadcasted_iota(jnp.int32, (C, 1), 0).astype(F32)
        for h in range(H):
            dcomb_ref[h] = (jnp.where(diff >= 0, jnp.exp(jnp.maximum(diff, 0.0) * gf[h]), 0.0)
                            + jnp.where(diff <= 0, jnp.exp(jnp.maximum(-diff, 0.0) * gb[h]), 0.0))

        def head_cols(lag, g):
            return jnp.concatenate([jnp.broadcast_to(jnp.exp(lag * g[h]), (C, DK)) for h in range(H)], axis=1)

        def head_rows(lag, g):
            return jnp.concatenate(
                [jnp.broadcast_to(jnp.exp(lag * g[h]) * (DK ** -0.5), (DK, C)) for h in range(H)], axis=0)

        tok = lax.broadcasted_iota(jnp.int32, (1, C), 1).astype(F32)
        dtab_ref[DK_F] = head_rows(C - 1.0 - tok, gf)
        dtab_ref[DK_B] = head_rows(tok, gb)
        dtab_ref[DQ_F] = head_cols(idx + 1.0, gf)
        dtab_ref[DQ_B] = head_cols(C - idx, gb)
        row_head = lax.broadcasted_iota(jnp.int32, (qk, qk), 0) // DK
        col_head = lax.broadcasted_iota(jnp.int32, (qk, qk), 1) // DV
        same_head = row_head == col_head
        for slot, g in ((DC_F, gf), (DC_B, gb)):
            dc = jnp.zeros((qk, qk), F32)
            for h in range(H):
                dc = jnp.where(same_head & (row_head == h), jnp.exp(C * g[h]), dc)
            dtab_ref[slot] = dc
        dtab_ref[MASK] = same_head.astype(F32)
        avg_ref[...] = jnp.where(same_head, 1.0 / DV, 0.0).astype(BF16)
        lane_head = lax.broadcasted_iota(jnp.int32, (C, qk), 1) // DK
        for h in range(H):
            hmask_ref[h] = (lane_head == h).astype(F32)

    def head_mean(a):
        total = None
        for _ in range(3):
            part = a.astype(BF16)
            a = a - part.astype(F32)
            term = _dot(part, avg_ref[...])
            total = term if total is None else total + term
        return total

    for sq in range(seqs):
        chunk = lambda c: slice(sq * L + c * C, sq * L + (c + 1) * C)

        for c in range(n):
            rows = chunk(c)
            q = u_ref[rows, 0:qk]
            kb = (u_ref[rows, qk:2 * qk] * (DK ** -0.5)).astype(BF16)
            v = u_ref[rows, 2 * qk:2 * qk + H * DV]
            att = [(_dot_nt((q * hmask_ref[h]).astype(BF16), kb) * dcomb_ref[h]).astype(BF16)
                   for h in range(H)]
            v_heads = jnp.concatenate([(v * hmask_ref[h]).astype(BF16) for h in range(H)], axis=0)
            of_ref[c * C:(c + 1) * C, :] = _dot(jnp.concatenate(att, axis=1), v_heads)
            vb = v.astype(BF16)
            kt = kt_ref[:, rows]
            for d, slot in ((0, DK_F), (1, DK_B)):
                kv_ref[d, c] = _dot((kt * dtab_ref[slot]).astype(BF16), vb)

        finals = []
        for d, slot in ((0, DC_F), (1, DC_B)):
            if has_state:
                zero = jnp.zeros((DK, DV), F32)
                S = jnp.concatenate(
                    [jnp.concatenate([s0_ref[sq, d, h] if g == h else zero for g in range(H)], axis=1)
                     for h in range(H)], axis=0)
            else:
                S = jnp.zeros((qk, H * DV), F32)
            for c in (range(n) if d == 0 else reversed(range(n))):
                if use_cross:
                    sin_ref[d, c] = S.astype(BF16)
                S = S * dtab_ref[slot] + kv_ref[d, c] * dtab_ref[MASK]
            finals.append(S)
        if emit_state:
            mine = sout_ref.at[sq, 0] if state_layers else sout_ref.at[sq]
            for d in range(2):
                for h in range(H):
                    mine[d, h] = finals[d][h * DK:(h + 1) * DK, h * DV:(h + 1) * DV]
            for other in range(1, state_layers):
                sout_ref[sq, other] = jnp.zeros(sout_ref.shape[2:], F32)

        for c in range(n):
            rows = chunk(c)
            tot = of_ref[c * C:(c + 1) * C, :]
            if use_cross:
                qb = u_ref[rows, 0:qk].astype(BF16)
                tot = (tot + _dot(qb, sin_ref[0, c]) * dtab_ref[DQ_F]
                       + _dot(qb, sin_ref[1, c]) * dtab_ref[DQ_B])
            xc = tot - head_mean(tot)
            normed = xc * lax.rsqrt(head_mean(xc * xc) + EPS)
            gate = u_ref[rows, 2 * qk + H * DV:2 * qk + 2 * H * DV]
            y_ref[rows, :] = _silu(gate) * normed


def _retention(u_ret, kt_ret, lw, B, L, state, emit_state, states_out=None):
    has_state = state is not None
    seqs = _seqs_per_step(B, L)
    st_dims = (2, RET_HEADS, RET_DK, RET_DV)
    ins = [u_ret, kt_ret, lw["ret_decay"]]
    specs = [pl.BlockSpec((seqs * L, RET_COLS), lambda b: (b, 0)),
             pl.BlockSpec((RET_HEADS * RET_DK, seqs * L), lambda b: (0, b)),
             _per_layer(lw["ret_decay"], lw["layer"])]
    if has_state:
        layer = lw["layer"]
        ins.append(state)
        specs.append(pl.BlockSpec((seqs, None) + st_dims, lambda b: (b, layer, 0, 0, 0, 0)))
    vd = RET_HEADS * RET_DV
    out_specs = [pl.BlockSpec((seqs * L, vd), lambda b: (b, 0))]
    out_shape = [jax.ShapeDtypeStruct((B * L, vd), F32)]
    aliases = {}
    state_layers = 0
    if emit_state:
        layer, depth = lw["layer"], lw["ret_decay"].shape[0]
        if states_out is None:
            state_layers = depth
            out_specs.append(pl.BlockSpec((seqs, depth) + st_dims, lambda b: (b, 0, 0, 0, 0, 0)))
        else:
            aliases = {len(ins): 1}
            ins.append(states_out)
            specs.append(pl.BlockSpec(memory_space=pl.ANY))
            out_specs.append(pl.BlockSpec((seqs, None) + st_dims, lambda b: (b, layer, 0, 0, 0, 0)))
        out_shape.append(jax.ShapeDtypeStruct((B, depth) + st_dims, F32))
    res = pl.pallas_call(
        functools.partial(_ret_kernel, L=L, seqs=seqs, has_state=has_state, emit_state=emit_state,
                          state_layers=state_layers, aliased=bool(aliases)),
        grid=(B // seqs,),
        in_specs=specs,
        out_specs=out_specs,
        out_shape=out_shape,
        input_output_aliases=aliases,
        scratch_shapes=[pltpu.VMEM((L, vd), F32),
                        pltpu.VMEM((2, L // RET_BLOCK, vd, vd), F32),
                        pltpu.VMEM((2, L // RET_BLOCK, vd, vd), BF16),
                        pltpu.VMEM((RET_HEADS, RET_BLOCK, RET_BLOCK), F32),
                        pltpu.VMEM((7, RET_BLOCK, vd), F32),
                        pltpu.VMEM((RET_HEADS, RET_BLOCK, vd), F32),
                        pltpu.VMEM((vd, vd), BF16)],
        compiler_params=_cparams(("arbitrary",), 48),
        name="retention",
    )(*ins)
    return (res[0], res[1]) if emit_state else (res[0], None)


def _conf_kernel(u_ref, w_ref, b_ref, g_ref, be_ref, y_ref, zp_ref, sh_ref, *, L, seqs):
    Wc = CONF_WIDTH
    halo = CONF_HALO
    zp_ref[0:halo, :] = jnp.zeros((halo, Wc), F32)
    zp_ref[halo + L:2 * halo + L, :] = jnp.zeros((halo, Wc), F32)
    first = halo - CONF_KERNEL // 2
    span = sh_ref.shape[1]
    R = CONF_ROWS
    for sq in range(seqs):
        r0 = sq * L
        zp_ref[halo:halo + L, :] = u_ref[r0:r0 + L, 0:Wc] * _sigmoid(u_ref[r0:r0 + L, Wc:2 * Wc])
        for s in range(SUBLANES):
            sh_ref[s] = zp_ref[first + s:first + s + span, :]
        for c in range(L // R):
            acc = jnp.broadcast_to(b_ref[...], (R, Wc))
            for k in range(CONF_KERNEL):
                a, s = divmod(k, SUBLANES)
                acc = acc + w_ref[k:k + 1, :] * sh_ref[s, c * R + SUBLANES * a:c * R + SUBLANES * a + R, :]
            mu = jnp.mean(acc, axis=-1, keepdims=True)
            xc = acc - mu
            z = xc * lax.rsqrt(jnp.mean(xc * xc, axis=-1, keepdims=True) + EPS) * g_ref[...] + be_ref[...]
            y_ref[r0 + c * R:r0 + (c + 1) * R, :] = _silu(z)


def _conformer(u_conf, lw, B, L):
    ws = (lw["conf_dw_w"], lw["conf_dw_b"], lw["conf_ln_g"], lw["conf_ln_b"])
    seqs = _seqs_per_step(B, L)
    return pl.pallas_call(
        functools.partial(_conf_kernel, L=L, seqs=seqs),
        grid=(B // seqs,),
        in_specs=[pl.BlockSpec((seqs * L, CONF_COLS), lambda b: (b, 0))]
        + [_per_layer(w, lw["layer"]) for w in ws],
        out_specs=pl.BlockSpec((seqs * L, CONF_WIDTH), lambda b: (b, 0)),
        out_shape=jax.ShapeDtypeStruct((B * L, CONF_WIDTH), F32),
        scratch_shapes=[pltpu.VMEM((L + 2 * CONF_HALO, CONF_WIDTH), F32),
                        pltpu.VMEM((SUBLANES, L + SUBLANES * ((CONF_KERNEL - 1) // SUBLANES), CONF_WIDTH), F32)],
        compiler_params=_cparams(("arbitrary",), 48),
        name="conformer",
    )(u_conf, *ws)


def _merge_kernel(x_ref, mod_ref, g1_ref, yhy_ref, ymla_ref, yret_ref, yconf_ref,
                  gw_ref, gb_ref, why_ref, wmla_ref, wret_ref, wconf_ref, wo_ref, o_ref, *, seqs_per_tile):
    x = x_ref[...]
    h = _rms(x, g1_ref[...]) * (1.0 + mod_ref[1:2, :]) + mod_ref[0:1, :]
    hb = h.astype(BF16)
    if seqs_per_tile == 1:
        yhy = yhy_ref[...]
    else:
        yhy = jnp.concatenate(
            [yhy_ref[:, s * HY_WIDTH:(s + 1) * HY_WIDTH] for s in range(seqs_per_tile)], axis=0)
    branches = ((yhy, why_ref), (ymla_ref[...], wmla_ref), (yret_ref[...], wret_ref), (yconf_ref[...], wconf_ref))
    D = D_MODEL
    merged = None
    for i, (y, w_ref) in enumerate(branches):
        gate = _sigmoid(_dot(hb, gw_ref[:, i * D:(i + 1) * D]) + gb_ref[:, i * D:(i + 1) * D])
        term = gate * _dot(y.astype(BF16), w_ref[...])
        merged = term if merged is None else merged + term
    o_ref[...] = x + mod_ref[2:3, :] * _dot(merged.astype(BF16), wo_ref[...])


def _merge(x, mod, lw, y_hy, y_mla, y_ret, y_conf, B, L, latent):
    T = B * L
    TM = WIDE_TOKEN_TILE
    if L >= TM:
        tiles_per_seq, seqs_per_tile = L // TM, 1
        hy_spec = pl.BlockSpec((TM, HY_WIDTH), lambda i: (i % tiles_per_seq, i // tiles_per_seq))
    else:
        tiles_per_seq, seqs_per_tile = 1, TM // L
        hy_spec = pl.BlockSpec((L, seqs_per_tile * HY_WIDTH), lambda i: (0, i))
    once = lambda a: _per_layer(a, lw["layer"], single_buffer=True)
    row = lambda w: pl.BlockSpec((TM, w), lambda i: (i, 0))
    ws = (lw["gate_w"], lw["gate_b"], lw["hy_out"], lw["mla_out"], lw["ret_out"], lw["conf_out"], lw["w_o"])
    return pl.pallas_call(
        functools.partial(_merge_kernel, seqs_per_tile=seqs_per_tile),
        grid=(T // TM,),
        in_specs=[row(D_MODEL), _mod_spec(lw["layer"], tiles_per_seq, latent),
                  once(lw["norm1_g"]), hy_spec, row(MLA_HEADS * MLA_V), row(RET_HEADS * RET_DV),
                  row(CONF_WIDTH)] + [once(w) for w in ws],
        out_specs=row(D_MODEL),
        out_shape=jax.ShapeDtypeStruct((T, D_MODEL), F32),
        compiler_params=_cparams(("arbitrary",), 56),
        name="merge",
    )(x, mod, lw["norm1_g"], y_hy, y_mla, y_ret, y_conf, *ws)


def _ffn_kernel(x_ref, mod_ref, g2_ref, w1_ref, w2_ref, fg_ref, o_ref, *, final):
    x = x_ref[...]
    h2 = (_rms(x, g2_ref[...]) * (1.0 + mod_ref[4:5, :]) + mod_ref[3:4, :]).astype(BF16)
    acc = None
    for c0 in range(0, D_FF, FFN_CHUNK):
        c1 = min(c0 + FFN_CHUNK, D_FF)
        a = _dot(h2, w1_ref[:, c0:c1])
        b = _dot(h2, w1_ref[:, D_FF + c0:D_FF + c1])
        part = _dot((_silu(a) * b).astype(BF16), w2_ref[c0:c1, :])
        acc = part if acc is None else acc + part
    out = x + mod_ref[5:6, :] * acc
    if final:
        out = _rms(out, fg_ref[...])
    o_ref[...] = out


def _ffn(x, mod, lw, final_g, B, L, latent, final):
    T = B * L
    TM = WIDE_TOKEN_TILE
    tiles_per_seq = max(L // TM, 1)
    once = lambda a: _per_layer(a, lw["layer"], single_buffer=True)
    row = pl.BlockSpec((TM, D_MODEL), lambda i: (i, 0))
    ws = (lw["norm2_g"], lw["ffn_w1"], lw["ffn_w2"])
    return pl.pallas_call(
        functools.partial(_ffn_kernel, final=final),
        grid=(T // TM,),
        in_specs=[row, _mod_spec(lw["layer"], tiles_per_seq, latent)]
        + [once(w) for w in ws] + [pl.BlockSpec(final_g.shape, lambda i: (0, 0))],
        out_specs=row,
        out_shape=jax.ShapeDtypeStruct((T, D_MODEL), F32),
        compiler_params=_cparams(("arbitrary",), 56),
        name="ffn",
    )(x, mod, *ws, final_g)


def _trunk_layer(x, mod, lw, final_g, tables, B, L, ctx, state, final, caches_out=None, states_out=None):
    latent = ctx is not None
    hv, hx1, hx2, qh, kh, vh, u_ret, u_conf, kt_ret, *cache = _inproj(x, mod, lw, B, L, latent, caches_out)
    ckv, kr = cache if cache else (None, None)
    spectra = _filter_spectra(lw, L, tables)
    y_hy = _hyena(hv, hx1, hx2, lw, spectra, tables, B, L)
    y_mla = _mla(qh, kh, vh, lw, B, L, ctx)
    y_ret, S = _retention(u_ret, kt_ret, lw, B, L, state, emit_state=not latent, states_out=states_out)
    y_conf = _conformer(u_conf, lw, B, L)
    x = _merge(x, mod, lw, y_hy, y_mla, y_ret, y_conf, B, L, latent)
    x = _ffn(x, mod, lw, final_g, B, L, latent, final)
    return x, ckv, kr, S


def _regroup_kernel(w_ref, o_ref):
    w = w_ref[...]
    rows = w.shape[0]
    rope_key = HY_COLS + MLA_COLS - MLA_ROPE
    zeros = lambda n: jnp.zeros((rows, n), F32)
    o_ref[...] = jnp.concatenate(
        [w[:, :rope_key], zeros(MLA_NOPE), w[:, rope_key:rope_key + MLA_ROPE],
         zeros(HEAD_LANES - MLA_NOPE - MLA_ROPE), w[:, rope_key + MLA_ROPE:]], axis=1).astype(BF16)


def _regroup_w_in(w_in):
    depth, d, cols = w_in.shape
    rows = REGROUP_ROWS
    out_cols = cols - MLA_ROPE + HEAD_LANES
    return pl.pallas_call(
        _regroup_kernel,
        grid=(depth, d // rows),
        in_specs=[pl.BlockSpec((None, rows, cols), lambda l, i: (l, i, 0))],
        out_specs=pl.BlockSpec((None, rows, out_cols), lambda l, i: (l, i, 0)),
        out_shape=jax.ShapeDtypeStruct((depth, d, out_cols), BF16),
        compiler_params=_cparams(("arbitrary", "arbitrary"), 32),
        name="regroup_w_in",
    )(w_in)


def _stacked_weights(w_in, p):
    depth = w_in.shape[0]
    w_in_all = _regroup_w_in(w_in)
    dq = MLA_NOPE + MLA_ROPE
    head_pad = lambda a: jnp.pad(a, ((0, 0),) * 3 + ((0, HEAD_LANES - a.shape[3]),)).reshape(
        depth, a.shape[1], MLA_HEADS * HEAD_LANES)
    w_uq = head_pad(p["mla_w_uq"].reshape(depth, MLA_Q_LORA, MLA_HEADS, dq))
    w_ukv = p["mla_w_ukv"].reshape(depth, MLA_KV_LORA, MLA_HEADS, MLA_NOPE + MLA_V)
    w_ukv = jnp.concatenate([head_pad(w_ukv[..., :MLA_NOPE]), head_pad(w_ukv[..., MLA_NOPE:])], axis=2)
    row = lambda name: p[name].reshape(depth, 1, -1)
    rows = ("norm1_g", "norm2_g", "mla_q_norm", "mla_kv_norm", "hy_conv_b", "hy_b1", "hy_b2",
            "conf_dw_b", "conf_ln_g", "conf_ln_b", "gate_b")
    as_is = ("hy_conv_w", "hy_w2", "hy_w3", "hy_bias", "ret_decay", "conf_dw_w")
    bf16 = ("gate_w", "hy_out", "mla_out", "ret_out", "conf_out", "w_o", "ffn_w1", "ffn_w2")
    return {
        "w_in": w_in_all,
        "mla_w_uq": w_uq.astype(BF16), "mla_w_ukv": w_ukv.astype(BF16),
        "hy_w1": jnp.pad(p["hy_w1"], ((0, 0), (0, LANES - HY_EMB), (0, 0))),
        **{name: row(name) for name in rows},
        **{name: p[name] for name in as_is},
        **{name: p[name].astype(BF16) for name in bf16},
    }


def kernel(x_prompt, x_sample, cache_mla_ckv, cache_mla_krope, state_ret, c, c_ctx, ada_w, ada_b, norm1_g, w_in, hy_conv_w, hy_conv_b, hy_w1, hy_b1, hy_w2, hy_b2, hy_w3, hy_bias, hy_out, mla_q_norm, mla_w_uq, mla_kv_norm, mla_w_ukv, mla_out, ret_decay, ret_out, conf_dw_w, conf_dw_b, conf_ln_g, conf_ln_b, conf_out, gate_w, gate_b, w_o, norm2_g, ffn_w1, ffn_w2, final_norm_g):
    p = dict(norm1_g=norm1_g, hy_conv_w=hy_conv_w, hy_conv_b=hy_conv_b, hy_w1=hy_w1, hy_b1=hy_b1,
             hy_w2=hy_w2, hy_b2=hy_b2, hy_w3=hy_w3, hy_bias=hy_bias, hy_out=hy_out,
             mla_q_norm=mla_q_norm, mla_w_uq=mla_w_uq, mla_kv_norm=mla_kv_norm, mla_w_ukv=mla_w_ukv,
             mla_out=mla_out, ret_decay=ret_decay, ret_out=ret_out, conf_dw_w=conf_dw_w,
             conf_dw_b=conf_dw_b, conf_ln_g=conf_ln_g, conf_ln_b=conf_ln_b, conf_out=conf_out,
             gate_w=gate_w, gate_b=gate_b, w_o=w_o, norm2_g=norm2_g, ffn_w1=ffn_w1, ffn_w2=ffn_w2)
    Bp, Lp, D = x_prompt.shape
    Bs, Ls, _ = x_sample.shape
    depth = w_in.shape[0]

    cond = jnp.concatenate([c_ctx[None, :], c, jnp.zeros((MOD_ROWS - 1 - Bs, D), F32)], axis=0)
    mod = _modulation(cond, ada_w, ada_b)
    tables_p = _dft_tables(Lp)
    tables_s = _dft_tables(Ls)
    final_g = final_norm_g.reshape(1, D)

    xp = x_prompt.reshape(Bp * Lp, D)
    xs = x_sample.reshape(Bs * Ls, D)
    caches, states = None, None
    weights = _stacked_weights(w_in, p)
    for l in range(depth):
        lw = dict(weights, layer=l)
        final = l == depth - 1
        xp, ckv, kr, states = _trunk_layer(xp, mod, lw, final_g, tables_p, Bp, Lp, None, None, final,
                                           caches, states)
        caches = (ckv, kr)
        xs, _, _, _ = _trunk_layer(xs, mod, lw, final_g, tables_s, Bs, Ls,
                                   (cache_mla_ckv, cache_mla_krope), state_ret, final)
    return (xp.reshape(Bp, Lp, D), xs.reshape(Bs, Ls, D), *caches, states)
```

```python
import functools
import math

import jax
import jax.numpy as jnp
from jax import lax
from jax.experimental import pallas as pl
from jax.experimental.pallas import tpu as pltpu

F32 = jnp.float32
BF16 = jnp.bfloat16

D_MODEL = 1024
DEPTH = 2
PAST_LEN = 256
EPS = 1e-6
GRID_W = 64

HY_WIDTH = 256
HY_EMB = 33
HY_BANDS = (HY_EMB - 1) // 2
HY_FFN = 64
HY_FAST_DECAY = 0.3
HY_SLOW_DECAY = 1.5
HY_TARGET = 1e-2

MLA_HEADS = 4
MLA_Q_LORA = 256
MLA_KV_LORA = 128
MLA_NOPE = 64
MLA_ROPE = 32
MLA_V = 64
ROPE_BASE = 10000.0

RET_HEADS = 4
RET_DK = 64
RET_DV = 64
RET_BLOCK = 256

CONF_WIDTH = 256
CONF_KERNEL = 31

D_FF = ((8 * D_MODEL // 3 + 255) // 256) * 256

HY_COLS = 3 * HY_WIDTH
MLA_COLS = MLA_Q_LORA + MLA_KV_LORA + MLA_ROPE
RET_COLS = 2 * RET_HEADS * RET_DK + 2 * RET_HEADS * RET_DV
CONF_COLS = 2 * CONF_WIDTH

VMEM_BYTES_V7X = 64 * 1024 * 1024
SUBLANES = 8
LANES = 128
TOKEN_TILE = 512
WIDE_TOKEN_TILE = 1024
Q_TILE = 512
MLA_SOFTMAX_ROWS = 16
DFT_TILE = 512
SHORT_SEQ_ROWS = 1024
HYENA_GROUP_ROWS = 1024
HYENA_ROWS = 1024
CONF_ROWS = 128
REGROUP_COLS = 128
CONF_HALO = 16
MXU_DIM_V7X = 256
FFN_CHUNK = 4 * MXU_DIM_V7X
MOD_ROWS = 8


def _cparams(sem, vmem_mb):
    vmem_bytes = vmem_mb * 1024 * 1024
    assert vmem_bytes < VMEM_BYTES_V7X
    return pltpu.CompilerParams(dimension_semantics=sem, vmem_limit_bytes=vmem_bytes)


def _dot(a, b):
    return jnp.dot(a, b, preferred_element_type=F32)


def _dot_nt(a, b):
    return lax.dot_general(a, b, (((1,), (1,)), ((), ())), preferred_element_type=F32)


def _dot_exact(a, b):
    return jnp.dot(a, b, preferred_element_type=F32, precision=lax.Precision.HIGHEST)


def _dot_split(a, b):
    a_hi, b_hi = a.astype(BF16), b.astype(BF16)
    a_lo = (a - a_hi.astype(F32)).astype(BF16)
    b_lo = (b - b_hi.astype(F32)).astype(BF16)
    return _dot(a_hi, b_hi) + (_dot(a_lo, b_hi) + _dot(a_hi, b_lo))


def _rms(x, g):
    return x * lax.rsqrt(jnp.mean(x * x, axis=-1, keepdims=True) + EPS) * g


def _sigmoid(x):
    return 0.5 * jnp.tanh(0.5 * x) + 0.5


def _silu(x):
    return x * _sigmoid(x)


def _mod_kernel(c_ref, w_ref, b_ref, o_ref):
    s = _silu(c_ref[...]).astype(BF16)
    o_ref[...] = _dot(s, w_ref[...].astype(BF16)) + b_ref[...]


def _modulation(cond, ada_w, ada_b):
    depth, d, cols = ada_w.shape
    blk = 1024
    out = pl.pallas_call(
        _mod_kernel,
        grid=(depth, cols // blk),
        in_specs=[
            pl.BlockSpec((MOD_ROWS, d), lambda l, j: (0, 0)),
            pl.BlockSpec((None, d, blk), lambda l, j: (l, 0, j)),
            pl.BlockSpec((None, 1, blk), lambda l, j: (l, 0, j)),
        ],
        out_specs=pl.BlockSpec((None, MOD_ROWS, blk), lambda l, j: (l, 0, j)),
        out_shape=jax.ShapeDtypeStruct((depth, MOD_ROWS, cols), F32),
        compiler_params=_cparams(("arbitrary", "arbitrary"), 32),
        name="modulation",
    )(cond, ada_w, ada_b.reshape(depth, 1, cols))
    return out.reshape(depth, MOD_ROWS, 6, d)


def _seqs_per_step(B, L):
    return max(1, min(B, SHORT_SEQ_ROWS // L))


def _per_layer(a, layer, single_buffer=False):
    mode = dict(pipeline_mode=pl.Buffered(1)) if single_buffer else {}
    return pl.BlockSpec((None,) + a.shape[1:], lambda *_: (layer,) + (0,) * (a.ndim - 1), **mode)


def _mod_spec(layer, tiles_per_seq, latent):
    group = (lambda i: 1 + i // tiles_per_seq) if latent else (lambda i: 0)
    return pl.BlockSpec((None, None, 6, D_MODEL), lambda i: (layer, group(i), 0, 0))


HEAD_LANES = LANES


def _rope_lanes(x, cos, sa, sb):
    reps = x.shape[1] // HEAD_LANES
    wide = lambda t: jnp.concatenate([t] * reps, axis=1) if reps > 1 else t
    half = MLA_ROPE // 2
    return (x * wide(cos) + pltpu.roll(x, x.shape[1] - half, 1) * wide(sa)
            + pltpu.roll(x, half, 1) * wide(sb))


def _inproj_kernel(*refs, seqs_per_tile, seq_len, latent, cache_layers, aliased):
    x_ref, mod_ref, g1_ref, win_ref, qn_ref, kvn_ref, wuq_ref, wukv_ref = refs[:8]
    refs = refs[8:]
    c_cq = HY_COLS
    c_ck = c_cq + MLA_Q_LORA
    c_ret = c_ck + MLA_KV_LORA + HEAD_LANES
    c_conf = c_ret + RET_COLS
    why_ref, wcq_ref, wck_ref, wret_ref, wconf_ref = (
        win_ref.at[:, a:b] for a, b in ((0, c_cq), (c_cq, c_ck), (c_ck, c_ret), (c_ret, c_conf),
                                        (c_conf, c_conf + CONF_COLS)))
    if latent:
        cos_ref, sa_ref, sb_ref = refs[:3]
        refs = refs[3:]
    if aliased:
        refs = refs[2:]
    hv_ref, hx1_ref, hx2_ref, q_ref, kh_ref, vh_ref, ret_ref, conf_ref, rkt_ref = refs[:9]
    if not latent:
        ckv_ref, kr_ref = refs[9:]
    x = x_ref[...]
    h = _rms(x, g1_ref[...]) * (1.0 + mod_ref[1:2, :]) + mod_ref[0:1, :]
    hb = h.astype(BF16)

    u = _dot(hb, why_ref[...])
    for p, o_ref in enumerate((hv_ref, hx1_ref, hx2_ref)):
        part = u[:, p * HY_WIDTH:(p + 1) * HY_WIDTH]
        if seqs_per_tile == 1:
            o_ref[...] = part
        else:
            for s in range(seqs_per_tile):
                o_ref[:, s * HY_WIDTH:(s + 1) * HY_WIDTH] = part[s * seq_len:(s + 1) * seq_len]

    heads_w = MLA_HEADS * HEAD_LANES
    cq = _dot(hb, wcq_ref[...])
    q = _dot(_rms(cq, qn_ref[...]).astype(BF16), wuq_ref[...])
    ck = _dot(hb, wck_ref[...])
    ckv = _rms(ck[:, :MLA_KV_LORA], kvn_ref[...])
    kr_block = ck[:, MLA_KV_LORA:MLA_KV_LORA + HEAD_LANES]
    if latent:
        q = _rope_lanes(q, cos_ref[...], sa_ref[...], sb_ref[...])
        kr_block = _rope_lanes(kr_block, cos_ref[...], sa_ref[...], sb_ref[...])
    else:
        kr = kr_block[:, MLA_NOPE:MLA_NOPE + MLA_ROPE]
        for s in range(seqs_per_tile):
            rows = slice(s * seq_len, (s + 1) * seq_len)
            for out_ref, val in ((ckv_ref, ckv), (kr_ref, kr)):
                mine = out_ref.at[s, 0] if cache_layers else out_ref.at[s]
                mine[...] = val[rows]
                for other in range(1, cache_layers):
                    out_ref[s, other] = jnp.zeros(out_ref.shape[2:], F32)
    q_ref[...] = (q * ((MLA_NOPE + MLA_ROPE) ** -0.5 * math.log2(math.e))).astype(BF16)
    kv = _dot(ckv.astype(BF16), wukv_ref[...])
    kh_ref[...] = (kv[:, :heads_w] + jnp.concatenate([kr_block] * MLA_HEADS, axis=1)).astype(BF16)
    lane = lax.broadcasted_iota(jnp.int32, (1, heads_w), 1)
    ones_lane = jnp.where(lane % HEAD_LANES == MLA_V, 1.0, 0.0)
    vh_ref[...] = (kv[:, heads_w:] + ones_lane).astype(BF16)
    ret = _dot(hb, wret_ref[...])
    ret_ref[...] = ret
    qk = RET_HEADS * RET_DK
    rkt_ref[...] = ret[:, qk:2 * qk].T
    conf_ref[...] = _dot(hb, wconf_ref[...])


def _inproj(x, mod, lw, B, L, latent, caches_out=None):
    T = B * L
    TM = TOKEN_TILE
    nt = T // TM
    if L >= TM:
        tiles_per_seq, seqs_per_tile = L // TM, 1
        hy_block = (TM, HY_WIDTH)
        hy_map = lambda i: (i % tiles_per_seq, i // tiles_per_seq)
    else:
        tiles_per_seq, seqs_per_tile = 1, TM // L
        hy_block = (L, seqs_per_tile * HY_WIDTH)
        hy_map = lambda i: (0, i)
    row = lambda w: pl.BlockSpec((TM, w), lambda i: (i, 0))
    weights = (lw["norm1_g"], lw["w_in"], lw["mla_q_norm"], lw["mla_kv_norm"], lw["mla_w_uq"], lw["mla_w_ukv"])
    hy_shape = jax.ShapeDtypeStruct((L, B * HY_WIDTH), F32)
    qk = RET_HEADS * RET_DK
    heads_w = MLA_HEADS * HEAD_LANES
    ins = [x, mod, *weights]
    in_specs = ([row(D_MODEL), _mod_spec(lw["layer"], tiles_per_seq, latent)]
                + [_per_layer(w, lw["layer"]) for w in weights])
    out_specs = ([pl.BlockSpec(hy_block, hy_map)] * 3
                 + [row(heads_w)] * 3 + [row(RET_COLS), row(CONF_COLS), pl.BlockSpec((qk, TM), lambda i: (0, i))])
    out_shape = ([hy_shape] * 3 + [jax.ShapeDtypeStruct((T, heads_w), BF16)] * 3
                 + [jax.ShapeDtypeStruct((T, RET_COLS), F32), jax.ShapeDtypeStruct((T, CONF_COLS), F32),
                    jax.ShapeDtypeStruct((qk, T), F32)])
    if latent:
        ins += list(_rope_lane_tables(L))
        in_specs += [pl.BlockSpec((TM, HEAD_LANES), lambda i: (i % tiles_per_seq, 0))] * 3
    aliases = {}
    cache_layers = 0
    if not latent:
        assert tiles_per_seq == 1
        layer, depth = lw["layer"], lw["w_in"].shape[0]
        widths = (MLA_KV_LORA, MLA_ROPE)
        if caches_out is None:
            cache_layers = depth
            out_specs += [pl.BlockSpec((seqs_per_tile, depth, L, w), lambda i: (i, 0, 0, 0)) for w in widths]
        else:
            aliases = {len(ins) + k: len(out_shape) + k for k in range(len(widths))}
            ins += list(caches_out)
            in_specs += [pl.BlockSpec(memory_space=pl.ANY)] * len(widths)
            out_specs += [pl.BlockSpec((seqs_per_tile, None, L, w), lambda i: (i, layer, 0, 0)) for w in widths]
        out_shape += [jax.ShapeDtypeStruct((B, depth, L, w), F32) for w in widths]
    return pl.pallas_call(
        functools.partial(_inproj_kernel, seqs_per_tile=seqs_per_tile, seq_len=L, latent=latent,
                          cache_layers=cache_layers, aliased=bool(aliases)),
        grid=(nt,),
        in_specs=in_specs,
        out_specs=out_specs,
        out_shape=out_shape,
        input_output_aliases=aliases,
        compiler_params=_cparams(("arbitrary",), 48),
        name="inproj",
    )(*ins)


def _dft_tables(L):
    N = 2 * L
    k_lo = min(L, 32)
    k_hi = L // k_lo
    t = jnp.arange(L, dtype=jnp.int32)[None, :]

    def cs(k):
        m = (k[:, None] * t) % N
        ang = m.astype(F32) * (2.0 * math.pi / N)
        return jnp.cos(ang), jnp.sin(ang)

    ca, sa = cs(jnp.arange(k_hi, dtype=jnp.int32) * k_lo)
    cb, sb = cs(jnp.arange(k_lo, dtype=jnp.int32))
    cos = (ca[:, None, :] * cb[None, :, :] - sa[:, None, :] * sb[None, :, :]).reshape(L, L)
    sin = (sa[:, None, :] * cb[None, :, :] + ca[:, None, :] * sb[None, :, :]).reshape(L, L)
    return cos.astype(BF16), (-sin).astype(BF16)


def _filter_features(L):
    t = jnp.linspace(0.0, 1.0, L, dtype=F32)[:, None]
    w = 2.0 * math.pi * jnp.arange(L, dtype=F32)[:, None] / L
    f = jnp.linspace(1e-4, HY_BANDS - 1, HY_BANDS, dtype=F32)[None, :]
    z = jnp.concatenate([t, jnp.cos(f * w), -jnp.sin(f * w)], axis=-1)
    z = jnp.pad(z, ((0, 0), (0, LANES - HY_EMB)))
    max_decay = math.log(HY_TARGET) / HY_FAST_DECAY
    min_decay = math.log(HY_TARGET) / HY_SLOW_DECAY
    deltas = jnp.abs(jnp.linspace(min_decay, max_decay, HY_WIDTH, dtype=F32))
    decay = jnp.exp(-t * deltas[None, :])
    return z, decay


def _alternating(rows):
    t = lax.broadcasted_iota(jnp.int32, (rows, 1), 0)
    return (1 - 2 * (t & 1)).astype(F32)


def _filter_kernel(z_ref, dec_ref, w1_ref, b1_ref, w2_ref, b2_ref, w3_ref, cos_ref, sin_ref,
                   kr_ref, ki_ref, kn_ref, hsum_ref, hdiff_ref, *, L):
    j = pl.program_id(0)
    W = HY_WIDTH
    N = 2 * L

    @pl.when(j == 0)
    def _():
        h = jnp.sin(_dot_exact(z_ref[...], w1_ref[...]) + b1_ref[...])
        h = jnp.sin(_dot_exact(h, w2_ref[...]) + b2_ref[...])
        h = _dot_split(h, w3_ref[...]) * jnp.concatenate([dec_ref[...]] * 4, axis=1)
        cs = jnp.sum(jnp.abs(h), axis=0, keepdims=True)
        s0 = cs[:, 0:W] + cs[:, W:2 * W]
        s1 = cs[:, 2 * W:3 * W] + cs[:, 3 * W:4 * W]
        h = h / jnp.concatenate([s0, s0, s1, s1], axis=1)
        row = lax.broadcasted_iota(jnp.int32, h.shape, 0)
        col = lax.broadcasted_iota(jnp.int32, h.shape, 1)
        backward = (col // W) % 2 == 1
        h = jnp.where(backward & (row == 0), 0.0, h)
        fwd = jnp.concatenate([h[:, 0:W], h[:, 2 * W:3 * W]], axis=1)
        bwd = jnp.concatenate([h[:, W:2 * W], h[:, 3 * W:4 * W]], axis=1)
        hsum_ref[...] = (fwd + bwd).astype(BF16)
        hdiff_ref[...] = (fwd - bwd).astype(BF16)
        nyq = jnp.sum((fwd + bwd) * _alternating(L), axis=0, keepdims=True) * (1.0 / N)
        for o in range(2):
            kn_ref[o] = nyq[:, o * W:(o + 1) * W]

    sr = _dot(cos_ref[...], hsum_ref[...])
    si = _dot(sin_ref[...], hdiff_ref[...])
    row = lax.broadcasted_iota(jnp.int32, (sr.shape[0], W), 0)
    scale = jnp.where((row == 0) & (j == 0), 1.0 / N, 2.0 / N)
    for o in range(2):
        kr_ref[o] = sr[:, o * W:(o + 1) * W] * scale
        ki_ref[o] = si[:, o * W:(o + 1) * W] * scale


def _filter_spectra(lw, L, tables):
    z, decay = _filter_features(L)
    cos, msin = tables
    Tk = min(DFT_TILE, L)
    full = lambda a: pl.BlockSpec(a.shape, lambda j: (0,) * a.ndim)
    mlp = (lw["hy_w1"], lw["hy_b1"], lw["hy_w2"], lw["hy_b2"], lw["hy_w3"])
    ins = (z, decay, *mlp)
    tile = pl.BlockSpec((Tk, L), lambda j: (j, 0))
    spec = pl.BlockSpec((2, Tk, HY_WIDTH), lambda j: (0, j, 0))
    return pl.pallas_call(
        functools.partial(_filter_kernel, L=L),
        grid=(L // Tk,),
        in_specs=[full(z), full(decay)] + [_per_layer(a, lw["layer"]) for a in mlp] + [tile, tile],
        out_specs=[spec, spec, pl.BlockSpec((2, 1, HY_WIDTH), lambda j: (0, 0, 0))],
        out_shape=[jax.ShapeDtypeStruct((2, L, HY_WIDTH), F32)] * 2
        + [jax.ShapeDtypeStruct((2, 1, HY_WIDTH), F32)],
        scratch_shapes=[pltpu.VMEM((L, 2 * HY_WIDTH), BF16)] * 2,
        compiler_params=_cparams(("arbitrary",), 48),
        name="hyena_filter",
    )(*ins, cos, msin)


def _hyena_kernel(v_ref, x1_ref, x2_ref, cw_ref, cb_ref, bias_ref, kr_ref, ki_ref, kn_ref,
                  cos_ref, sin_ref, y_ref, cur_ref, curb_ref, yr_ref, yi_ref, gate_ref,
                  *, L, W, T):
    reps = W // HY_WIDTH
    tiled = lambda a: jnp.concatenate([a] * reps, axis=1) if reps > 1 else a
    alt = _alternating(L)

    def short_conv(u_ref, p):
        u = u_ref[...]
        row = lax.broadcasted_iota(jnp.int32, (L, W), 0)
        prev = jnp.where(row == 0, 0.0, pltpu.roll(u, 1, 0))
        nxt = jnp.where(row == L - 1, 0.0, pltpu.roll(u, L - 1, 0))
        cols = slice(p * HY_WIDTH, (p + 1) * HY_WIDTH)
        w = [tiled(cw_ref[k:k + 1, cols]) for k in range(3)]
        return prev * w[0] + u * w[1] + nxt * w[2] + tiled(cb_ref[:, cols])

    cur_ref[...] = short_conv(v_ref, 0)
    for o, x_ref in enumerate((x1_ref, x2_ref)):
        cur = cur_ref[...]
        curb_ref[...] = cur.astype(BF16)
        gate_ref[...] = short_conv(x_ref, o + 1)
        nyq = jnp.sum(cur * alt, axis=0, keepdims=True) * tiled(kn_ref[o])
        for f in range(L // T):
            rows = slice(f * T, (f + 1) * T)
            cb = curb_ref[...]
            xr = _dot(cos_ref[rows, :], cb)
            xi = _dot(sin_ref[rows, :], cb)
            kr, ki = tiled(kr_ref[o, rows, :]), tiled(ki_ref[o, rows, :])
            yr_ref[rows, :] = (xr * kr - xi * ki).astype(BF16)
            yi_ref[rows, :] = (xr * ki + xi * kr).astype(BF16)
        for t in range(L // T):
            rows = slice(t * T, (t + 1) * T)
            conv = (_dot(cos_ref[rows, :], yr_ref[...]) + _dot(sin_ref[rows, :], yi_ref[...])
                    + alt[rows] * nyq + cur_ref[rows, :] * tiled(bias_ref[o:o + 1, :]))
            out = gate_ref[rows, :] * conv
            if o == 0:
                cur_ref[rows, :] = out
            else:
                y_ref[rows, :] = out


def _hyena(hv, hx1, hx2, lw, spectra, tables, B, L):
    kr, ki, kn = spectra
    cos, msin = tables
    T = min(HYENA_ROWS, L)
    W = HY_WIDTH * max(1, min(B, HYENA_GROUP_ROWS // L))
    ng = (B * HY_WIDTH) // W
    col = pl.BlockSpec((L, W), lambda g: (0, g))
    once = lambda a: pl.BlockSpec(a.shape, lambda g: (0,) * a.ndim, pipeline_mode=pl.Buffered(1))
    params = (lw["hy_conv_w"], lw["hy_conv_b"], lw["hy_bias"])
    consts = (kr, ki, kn, cos, msin)
    return pl.pallas_call(
        functools.partial(_hyena_kernel, L=L, W=W, T=T),
        grid=(ng,),
        in_specs=[col, col, col] + [_per_layer(a, lw["layer"]) for a in params] + [once(a) for a in consts],
        out_specs=col,
        out_shape=jax.ShapeDtypeStruct((L, B * HY_WIDTH), F32),
        scratch_shapes=[pltpu.VMEM((L, W), F32), pltpu.VMEM((L, W), BF16), pltpu.VMEM((L, W), BF16),
                        pltpu.VMEM((L, W), BF16), pltpu.VMEM((L, W), F32)],
        compiler_params=_cparams(("arbitrary",), 60),
        name="hyena_conv",
    )(hv, hx1, hx2, *params, *consts)


def _mla_kernel(*refs, L, latent, seqs):
    if latent:
        q_ref, kh_ref, vh_ref, cckv_ref, ckr_ref, wukv_ref, o_ref, s_ref, p_ref, kctx_ref, vctx_ref = refs
    else:
        q_ref, kh_ref, vh_ref, o_ref, s_ref, p_ref = refs
    TQ = q_ref.shape[0] // seqs
    heads_w = MLA_HEADS * HEAD_LANES
    block = lambda h: slice(h * HEAD_LANES, (h + 1) * HEAD_LANES)

    if latent:
        @pl.when(pl.program_id(1) == 0)
        def _():
            kv = _dot(cckv_ref[...].astype(BF16), wukv_ref[...])
            zeros = lambda w: jnp.zeros((PAST_LEN, w), F32)
            kr_block = jnp.concatenate(
                [zeros(MLA_NOPE), ckr_ref[...], zeros(HEAD_LANES - MLA_NOPE - MLA_ROPE)], axis=1)
            kctx_ref[...] = (kv[:, :heads_w] + jnp.concatenate([kr_block] * MLA_HEADS, axis=1)).astype(BF16)
            lane = lax.broadcasted_iota(jnp.int32, (1, heads_w), 1)
            vctx_ref[...] = (kv[:, heads_w:] + jnp.where(lane % HEAD_LANES == MLA_V, 1.0, 0.0)).astype(BF16)

    def attend(q_rows, k_rows):
        def scores(h):
            qh = q_ref[q_rows, block(h)]
            s_ref[h, :, 0:L] = _dot_nt(qh, kh_ref[k_rows, block(h)])
            if latent:
                s_ref[h, :, L:L + PAST_LEN] = _dot_nt(qh, kctx_ref[:, block(h)])

        outs = []
        scores(0)
        for h in range(MLA_HEADS):
            if h + 1 < MLA_HEADS:
                scores(h + 1)
            for r in range(TQ // MLA_SOFTMAX_ROWS):
                rows = slice(r * MLA_SOFTMAX_ROWS, (r + 1) * MLA_SOFTMAX_ROWS)
                s = s_ref[h, rows, :]
                p_ref[h, rows, :] = jnp.exp2(s - jnp.max(s, axis=-1, keepdims=True)).astype(BF16)
            pv = _dot(p_ref[h, :, 0:L], vh_ref[k_rows, block(h)])
            if latent:
                pv = pv + _dot(p_ref[h, :, L:L + PAST_LEN], vctx_ref[:, block(h)])
            outs.append(pv[:, :MLA_V] / pv[:, MLA_V:MLA_V + 1])
        o_ref[q_rows, :] = jnp.concatenate(outs, axis=-1)

    for sq in range(seqs):
        attend(slice(sq * TQ, (sq + 1) * TQ), slice(sq * L, (sq + 1) * L))


def _rope_tables(L):
    rows = L // GRID_W
    row = jnp.repeat(jnp.arange(rows), GRID_W).astype(F32)
    col = jnp.tile(jnp.arange(GRID_W), rows).astype(F32)
    per_axis = MLA_ROPE // 4
    inv = ROPE_BASE ** (-jnp.arange(per_axis, dtype=F32) / per_axis)
    ang = jnp.concatenate([row[:, None] * inv, col[:, None] * inv], axis=-1)
    return jnp.cos(ang), jnp.sin(ang)


def _rope_lane_tables(L):
    cos, sin = _rope_tables(L)
    pad = HEAD_LANES - MLA_NOPE - MLA_ROPE
    one, zero = jnp.ones((L, MLA_NOPE), F32), jnp.zeros((L, MLA_NOPE), F32)
    half0 = jnp.zeros_like(sin)
    cos_t = jnp.concatenate([one, cos, cos, one[:, :pad]], axis=1)
    sa_t = jnp.concatenate([zero, -sin, half0, zero[:, :pad]], axis=1)
    sb_t = jnp.concatenate([zero, half0, sin, zero[:, :pad]], axis=1)
    return cos_t, sa_t, sb_t


def _mla(qh, kh, vh, lw, B, L, ctx):
    latent = ctx is not None
    TQ = min(Q_TILE, L)
    nq = L // TQ
    seqs = 1 if latent or nq > 1 else _seqs_per_step(B, L)
    Lk = L + PAST_LEN if latent else L
    heads_w = MLA_HEADS * HEAD_LANES
    seq = pl.BlockSpec((seqs * L, heads_w), lambda b, i: (b, 0))
    ins = [qh, kh, vh]
    specs = [pl.BlockSpec((seqs * TQ, heads_w), lambda b, i: (b * nq + i, 0)), seq, seq]
    scratch = [pltpu.VMEM((MLA_HEADS, TQ, Lk), F32), pltpu.VMEM((MLA_HEADS, TQ, Lk), BF16)]
    if latent:
        ins += [ctx[0], ctx[1], lw["mla_w_ukv"]]
        layer = lw["layer"]
        specs += [pl.BlockSpec((None, None, PAST_LEN, MLA_KV_LORA), lambda b, i: (b, layer, 0, 0)),
                  pl.BlockSpec((None, None, PAST_LEN, MLA_ROPE), lambda b, i: (b, layer, 0, 0)),
                  _per_layer(lw["mla_w_ukv"], lw["layer"])]
        scratch += [pltpu.VMEM((PAST_LEN, heads_w), BF16)] * 2
    return pl.pallas_call(
        functools.partial(_mla_kernel, L=L, latent=latent, seqs=seqs),
        grid=(B // seqs, nq),
        in_specs=specs,
        out_specs=pl.BlockSpec((seqs * TQ, MLA_HEADS * MLA_V), lambda b, i: (b * nq + i, 0)),
        out_shape=jax.ShapeDtypeStruct((B * L, MLA_HEADS * MLA_V), F32),
        scratch_shapes=scratch,
        compiler_params=_cparams(("arbitrary", "arbitrary"), 48),
        name="mla_attention",
    )(*ins)


def _ret_kernel(*refs, L, seqs, has_state, emit_state, state_layers, aliased):
    refs = list(refs)
    u_ref, kt_ref, dl_ref = refs[:3]
    pos = 3
    s0_ref = None
    if has_state:
        s0_ref = refs[pos]
        pos += 1
    if aliased:
        pos += 1
    y_ref = refs[pos]
    pos += 1
    sout_ref = None
    if emit_state:
        sout_ref = refs[pos]
        pos += 1
    of_ref, kv_ref, sin_ref, dcomb_ref, dtab_ref, hmask_ref, avg_ref = refs[pos:]

    C = RET_BLOCK
    n = L // C
    H, DK, DV = RET_HEADS, RET_DK, RET_DV
    qk = H * DK
    DK_F, DK_B, DQ_F, DQ_B, DC_F, DC_B, MASK = range(7)
    use_cross = has_state or n > 1

    @pl.when(pl.program_id(0) == 0)
    def _():
        x = dl_ref[...]
        log_g = jnp.minimum(x, 0.0) - jnp.log1p(jnp.exp(-jnp.abs(x)))
        gf = [log_g[0:1, h:h + 1] for h in range(H)]
        gb = [log_g[1:2, h:h + 1] for h in range(H)]
        diff = (lax.broadcasted_iota(jnp.int32, (C, C), 0)
                - lax.broadcasted_iota(jnp.int32, (C, C), 1)).astype(F32)
        idx = lax.broadcasted_iota(jnp.int32, (C, 1), 0).astype(F32)
        for h in range(H):
            dcomb_ref[h] = (jnp.where(diff >= 0, jnp.exp(jnp.maximum(diff, 0.0) * gf[h]), 0.0)
                            + jnp.where(diff <= 0, jnp.exp(jnp.maximum(-diff, 0.0) * gb[h]), 0.0))

        def head_cols(lag, g):
            return jnp.concatenate([jnp.broadcast_to(jnp.exp(lag * g[h]), (C, DK)) for h in range(H)], axis=1)

        def head_rows(lag, g):
            return jnp.concatenate(
                [jnp.broadcast_to(jnp.exp(lag * g[h]) * (DK ** -0.5), (DK, C)) for h in range(H)], axis=0)

        tok = lax.broadcasted_iota(jnp.int32, (1, C), 1).astype(F32)
        dtab_ref[DK_F] = head_rows(C - 1.0 - tok, gf)
        dtab_ref[DK_B] = head_rows(tok, gb)
        dtab_ref[DQ_F] = head_cols(idx + 1.0, gf)
        dtab_ref[DQ_B] = head_cols(C - idx, gb)
        row_head = lax.broadcasted_iota(jnp.int32, (qk, qk), 0) // DK
        col_head = lax.broadcasted_iota(jnp.int32, (qk, qk), 1) // DV
        same_head = row_head == col_head
        for slot, g in ((DC_F, gf), (DC_B, gb)):
            dc = jnp.zeros((qk, qk), F32)
            for h in range(H):
                dc = jnp.where(same_head & (row_head == h), jnp.exp(C * g[h]), dc)
            dtab_ref[slot] = dc
        dtab_ref[MASK] = same_head.astype(F32)
        avg_ref[...] = jnp.where(same_head, 1.0 / DV, 0.0).astype(BF16)
        lane_head = lax.broadcasted_iota(jnp.int32, (C, qk), 1) // DK
        for h in range(H):
            hmask_ref[h] = (lane_head == h).astype(F32)

    def head_mean(a):
        total = None
        for _ in range(3):
            part = a.astype(BF16)
            a = a - part.astype(F32)
            term = _dot(part, avg_ref[...])
            total = term if total is None else total + term
        return total

    for sq in range(seqs):
        chunk = lambda c: slice(sq * L + c * C, sq * L + (c + 1) * C)

        for c in range(n):
            rows = chunk(c)
            q = u_ref[rows, 0:qk]
            kb = (u_ref[rows, qk:2 * qk] * (DK ** -0.5)).astype(BF16)
            v = u_ref[rows, 2 * qk:2 * qk + H * DV]
            att = [(_dot_nt((q * hmask_ref[h]).astype(BF16), kb) * dcomb_ref[h]).astype(BF16)
                   for h in range(H)]
            v_heads = jnp.concatenate([(v * hmask_ref[h]).astype(BF16) for h in range(H)], axis=0)
            of_ref[c * C:(c + 1) * C, :] = _dot(jnp.concatenate(att, axis=1), v_heads)
            vb = v.astype(BF16)
            kt = kt_ref[:, rows]
            for d, slot in ((0, DK_F), (1, DK_B)):
                kv_ref[d, c] = _dot((kt * dtab_ref[slot]).astype(BF16), vb)

        finals = []
        for d, slot in ((0, DC_F), (1, DC_B)):
            if has_state:
                zero = jnp.zeros((DK, DV), F32)
                S = jnp.concatenate(
                    [jnp.concatenate([s0_ref[sq, d, h] if g == h else zero for g in range(H)], axis=1)
                     for h in range(H)], axis=0)
            else:
                S = jnp.zeros((qk, H * DV), F32)
            for c in (range(n) if d == 0 else reversed(range(n))):
                if use_cross:
                    sin_ref[d, c] = S.astype(BF16)
                S = S * dtab_ref[slot] + kv_ref[d, c] * dtab_ref[MASK]
            finals.append(S)
        if emit_state:
            mine = sout_ref.at[sq, 0] if state_layers else sout_ref.at[sq]
            for d in range(2):
                for h in range(H):
                    mine[d, h] = finals[d][h * DK:(h + 1) * DK, h * DV:(h + 1) * DV]
            for other in range(1, state_layers):
                sout_ref[sq, other] = jnp.zeros(sout_ref.shape[2:], F32)

        for c in range(n):
            rows = chunk(c)
            tot = of_ref[c * C:(c + 1) * C, :]
            if use_cross:
                qb = u_ref[rows, 0:qk].astype(BF16)
                tot = (tot + _dot(qb, sin_ref[0, c]) * dtab_ref[DQ_F]
                       + _dot(qb, sin_ref[1, c]) * dtab_ref[DQ_B])
            xc = tot - head_mean(tot)
            normed = xc * lax.rsqrt(head_mean(xc * xc) + EPS)
            gate = u_ref[rows, 2 * qk + H * DV:2 * qk + 2 * H * DV]
            y_ref[rows, :] = _silu(gate) * normed


def _retention(u_ret, kt_ret, lw, B, L, state, emit_state, states_out=None):
    has_state = state is not None
    seqs = _seqs_per_step(B, L)
    st_dims = (2, RET_HEADS, RET_DK, RET_DV)
    ins = [u_ret, kt_ret, lw["ret_decay"]]
    specs = [pl.BlockSpec((seqs * L, RET_COLS), lambda b: (b, 0)),
             pl.BlockSpec((RET_HEADS * RET_DK, seqs * L), lambda b: (0, b)),
             _per_layer(lw["ret_decay"], lw["layer"])]
    if has_state:
        layer = lw["layer"]
        ins.append(state)
        specs.append(pl.BlockSpec((seqs, None) + st_dims, lambda b: (b, layer, 0, 0, 0, 0)))
    vd = RET_HEADS * RET_DV
    out_specs = [pl.BlockSpec((seqs * L, vd), lambda b: (b, 0))]
    out_shape = [jax.ShapeDtypeStruct((B * L, vd), F32)]
    aliases = {}
    state_layers = 0
    if emit_state:
        layer, depth = lw["layer"], lw["ret_decay"].shape[0]
        if states_out is None:
            state_layers = depth
            out_specs.append(pl.BlockSpec((seqs, depth) + st_dims, lambda b: (b, 0, 0, 0, 0, 0)))
        else:
            aliases = {len(ins): 1}
            ins.append(states_out)
            specs.append(pl.BlockSpec(memory_space=pl.ANY))
            out_specs.append(pl.BlockSpec((seqs, None) + st_dims, lambda b: (b, layer, 0, 0, 0, 0)))
        out_shape.append(jax.ShapeDtypeStruct((B, depth) + st_dims, F32))
    res = pl.pallas_call(
        functools.partial(_ret_kernel, L=L, seqs=seqs, has_state=has_state, emit_state=emit_state,
                          state_layers=state_layers, aliased=bool(aliases)),
        grid=(B // seqs,),
        in_specs=specs,
        out_specs=out_specs,
        out_shape=out_shape,
        input_output_aliases=aliases,
        scratch_shapes=[pltpu.VMEM((L, vd), F32),
                        pltpu.VMEM((2, L // RET_BLOCK, vd, vd), F32),
                        pltpu.VMEM((2, L // RET_BLOCK, vd, vd), BF16),
                        pltpu.VMEM((RET_HEADS, RET_BLOCK, RET_BLOCK), F32),
                        pltpu.VMEM((7, RET_BLOCK, vd), F32),
                        pltpu.VMEM((RET_HEADS, RET_BLOCK, vd), F32),
                        pltpu.VMEM((vd, vd), BF16)],
        compiler_params=_cparams(("arbitrary",), 48),
        name="retention",
    )(*ins)
    return (res[0], res[1]) if emit_state else (res[0], None)


def _conf_kernel(u_ref, w_ref, b_ref, g_ref, be_ref, y_ref, zp_ref, sh_ref, *, L, seqs):
    Wc = CONF_WIDTH
    halo = CONF_HALO
    zp_ref[0:halo, :] = jnp.zeros((halo, Wc), F32)
    zp_ref[halo + L:2 * halo + L, :] = jnp.zeros((halo, Wc), F32)
    first = halo - CONF_KERNEL // 2
    span = sh_ref.shape[1]
    R = CONF_ROWS
    for sq in range(seqs):
        r0 = sq * L
        zp_ref[halo:halo + L, :] = u_ref[r0:r0 + L, 0:Wc] * _sigmoid(u_ref[r0:r0 + L, Wc:2 * Wc])
        for s in range(SUBLANES):
            sh_ref[s] = zp_ref[first + s:first + s + span, :]
        for c in range(L // R):
            acc = jnp.broadcast_to(b_ref[...], (R, Wc))
            for k in range(CONF_KERNEL):
                a, s = divmod(k, SUBLANES)
                acc = acc + w_ref[k:k + 1, :] * sh_ref[s, c * R + SUBLANES * a:c * R + SUBLANES * a + R, :]
            mu = jnp.mean(acc, axis=-1, keepdims=True)
            xc = acc - mu
            z = xc * lax.rsqrt(jnp.mean(xc * xc, axis=-1, keepdims=True) + EPS) * g_ref[...] + be_ref[...]
            y_ref[r0 + c * R:r0 + (c + 1) * R, :] = _silu(z)


def _conformer(u_conf, lw, B, L):
    ws = (lw["conf_dw_w"], lw["conf_dw_b"], lw["conf_ln_g"], lw["conf_ln_b"])
    seqs = _seqs_per_step(B, L)
    return pl.pallas_call(
        functools.partial(_conf_kernel, L=L, seqs=seqs),
        grid=(B // seqs,),
        in_specs=[pl.BlockSpec((seqs * L, CONF_COLS), lambda b: (b, 0))]
        + [_per_layer(w, lw["layer"]) for w in ws],
        out_specs=pl.BlockSpec((seqs * L, CONF_WIDTH), lambda b: (b, 0)),
        out_shape=jax.ShapeDtypeStruct((B * L, CONF_WIDTH), F32),
        scratch_shapes=[pltpu.VMEM((L + 2 * CONF_HALO, CONF_WIDTH), F32),
                        pltpu.VMEM((SUBLANES, L + SUBLANES * ((CONF_KERNEL - 1) // SUBLANES), CONF_WIDTH), F32)],
        compiler_params=_cparams(("arbitrary",), 48),
        name="conformer",
    )(u_conf, *ws)


def _merge_kernel(x_ref, mod_ref, g1_ref, yhy_ref, ymla_ref, yret_ref, yconf_ref,
                  gw_ref, gb_ref, why_ref, wmla_ref, wret_ref, wconf_ref, wo_ref, o_ref, *, seqs_per_tile):
    x = x_ref[...]
    h = _rms(x, g1_ref[...]) * (1.0 + mod_ref[1:2, :]) + mod_ref[0:1, :]
    hb = h.astype(BF16)
    if seqs_per_tile == 1:
        yhy = yhy_ref[...]
    else:
        yhy = jnp.concatenate(
            [yhy_ref[:, s * HY_WIDTH:(s + 1) * HY_WIDTH] for s in range(seqs_per_tile)], axis=0)
    branches = ((yhy, why_ref), (ymla_ref[...], wmla_ref), (yret_ref[...], wret_ref), (yconf_ref[...], wconf_ref))
    D = D_MODEL
    merged = None
    for i, (y, w_ref) in enumerate(branches):
        gate = _sigmoid(_dot(hb, gw_ref[:, i * D:(i + 1) * D]) + gb_ref[:, i * D:(i + 1) * D])
        term = gate * _dot(y.astype(BF16), w_ref[...])
        merged = term if merged is None else merged + term
    o_ref[...] = x + mod_ref[2:3, :] * _dot(merged.astype(BF16), wo_ref[...])


def _merge(x, mod, lw, y_hy, y_mla, y_ret, y_conf, B, L, latent):
    T = B * L
    TM = WIDE_TOKEN_TILE
    if L >= TM:
        tiles_per_seq, seqs_per_tile = L // TM, 1
        hy_spec = pl.BlockSpec((TM, HY_WIDTH), lambda i: (i % tiles_per_seq, i // tiles_per_seq))
    else:
        tiles_per_seq, seqs_per_tile = 1, TM // L
        hy_spec = pl.BlockSpec((L, seqs_per_tile * HY_WIDTH), lambda i: (0, i))
    once = lambda a: _per_layer(a, lw["layer"], single_buffer=True)
    row = lambda w: pl.BlockSpec((TM, w), lambda i: (i, 0))
    ws = (lw["gate_w"], lw["gate_b"], lw["hy_out"], lw["mla_out"], lw["ret_out"], lw["conf_out"], lw["w_o"])
    return pl.pallas_call(
        functools.partial(_merge_kernel, seqs_per_tile=seqs_per_tile),
        grid=(T // TM,),
        in_specs=[row(D_MODEL), _mod_spec(lw["layer"], tiles_per_seq, latent),
                  once(lw["norm1_g"]), hy_spec, row(MLA_HEADS * MLA_V), row(RET_HEADS * RET_DV),
                  row(CONF_WIDTH)] + [once(w) for w in ws],
        out_specs=row(D_MODEL),
        out_shape=jax.ShapeDtypeStruct((T, D_MODEL), F32),
        compiler_params=_cparams(("arbitrary",), 56),
        name="merge",
    )(x, mod, lw["norm1_g"], y_hy, y_mla, y_ret, y_conf, *ws)


def _ffn_kernel(x_ref, mod_ref, g2_ref, w1_ref, w2_ref, fg_ref, o_ref, *, final):
    x = x_ref[...]
    h2 = (_rms(x, g2_ref[...]) * (1.0 + mod_ref[4:5, :]) + mod_ref[3:4, :]).astype(BF16)
    acc = None
    for c0 in range(0, D_FF, FFN_CHUNK):
        c1 = min(c0 + FFN_CHUNK, D_FF)
        a = _dot(h2, w1_ref[:, c0:c1])
        b = _dot(h2, w1_ref[:, D_FF + c0:D_FF + c1])
        part = _dot((_silu(a) * b).astype(BF16), w2_ref[c0:c1, :])
        acc = part if acc is None else acc + part
    out = x + mod_ref[5:6, :] * acc
    if final:
        out = _rms(out, fg_ref[...])
    o_ref[...] = out


def _ffn(x, mod, lw, final_g, B, L, latent, final):
    T = B * L
    TM = WIDE_TOKEN_TILE
    tiles_per_seq = max(L // TM, 1)
    once = lambda a: _per_layer(a, lw["layer"], single_buffer=True)
    row = pl.BlockSpec((TM, D_MODEL), lambda i: (i, 0))
    ws = (lw["norm2_g"], lw["ffn_w1"], lw["ffn_w2"])
    return pl.pallas_call(
        functools.partial(_ffn_kernel, final=final),
        grid=(T // TM,),
        in_specs=[row, _mod_spec(lw["layer"], tiles_per_seq, latent)]
        + [once(w) for w in ws] + [pl.BlockSpec(final_g.shape, lambda i: (0, 0))],
        out_specs=row,
        out_shape=jax.ShapeDtypeStruct((T, D_MODEL), F32),
        compiler_params=_cparams(("arbitrary",), 56),
        name="ffn",
    )(x, mod, *ws, final_g)


def _trunk_layer(x, mod, lw, final_g, tables, B, L, ctx, state, final, caches_out=None, states_out=None):
    latent = ctx is not None
    hv, hx1, hx2, qh, kh, vh, u_ret, u_conf, kt_ret, *cache = _inproj(x, mod, lw, B, L, latent, caches_out)
    ckv, kr = cache if cache else (None, None)
    spectra = _filter_spectra(lw, L, tables)
    y_hy = _hyena(hv, hx1, hx2, lw, spectra, tables, B, L)
    y_mla = _mla(qh, kh, vh, lw, B, L, ctx)
    y_ret, S = _retention(u_ret, kt_ret, lw, B, L, state, emit_state=not latent, states_out=states_out)
    y_conf = _conformer(u_conf, lw, B, L)
    x = _merge(x, mod, lw, y_hy, y_mla, y_ret, y_conf, B, L, latent)
    x = _ffn(x, mod, lw, final_g, B, L, latent, final)
    return x, ckv, kr, S


def _regroup_kernel(wt_ref, o_ref):
    d = wt_ref.shape[1]
    rope_key = HY_COLS + MLA_COLS - MLA_ROPE

    def move(dst, src, n):
        for c in range(0, n, REGROUP_COLS):
            m = min(REGROUP_COLS, n - c)
            o_ref[:, dst + c:dst + c + m] = wt_ref[src + c:src + c + m, :].T.astype(BF16)

    move(0, 0, rope_key)
    slab = pltpu.roll(wt_ref[rope_key:rope_key + HEAD_LANES, :].T, MLA_NOPE, 1)
    lane = lax.broadcasted_iota(jnp.int32, (d, HEAD_LANES), 1)
    rope_lanes = (lane >= MLA_NOPE) & (lane < MLA_NOPE + MLA_ROPE)
    o_ref[:, rope_key:rope_key + HEAD_LANES] = jnp.where(rope_lanes, slab, 0.0).astype(BF16)
    move(rope_key + HEAD_LANES, rope_key + MLA_ROPE, wt_ref.shape[0] - rope_key - MLA_ROPE)


def _regroup_w_in(w_in):
    depth, d, cols = w_in.shape
    out_cols = cols - MLA_ROPE + HEAD_LANES
    return pl.pallas_call(
        _regroup_kernel,
        grid=(depth,),
        in_specs=[pl.BlockSpec((None, cols, d), lambda l: (l, 0, 0))],
        out_specs=pl.BlockSpec((None, d, out_cols), lambda l: (l, 0, 0)),
        out_shape=jax.ShapeDtypeStruct((depth, d, out_cols), BF16),
        compiler_params=_cparams(("arbitrary",), 48),
        name="regroup_w_in",
    )(jnp.swapaxes(w_in, 1, 2))


def _stacked_weights(w_in, p):
    depth = w_in.shape[0]
    w_in_all = _regroup_w_in(w_in)
    dq = MLA_NOPE + MLA_ROPE
    head_pad = lambda a: jnp.pad(a, ((0, 0),) * 3 + ((0, HEAD_LANES - a.shape[3]),)).reshape(
        depth, a.shape[1], MLA_HEADS * HEAD_LANES)
    w_uq = head_pad(p["mla_w_uq"].reshape(depth, MLA_Q_LORA, MLA_HEADS, dq))
    w_ukv = p["mla_w_ukv"].reshape(depth, MLA_KV_LORA, MLA_HEADS, MLA_NOPE + MLA_V)
    w_ukv = jnp.concatenate([head_pad(w_ukv[..., :MLA_NOPE]), head_pad(w_ukv[..., MLA_NOPE:])], axis=2)
    row = lambda name: p[name].reshape(depth, 1, -1)
    rows = ("norm1_g", "norm2_g", "mla_q_norm", "mla_kv_norm", "hy_conv_b", "hy_b1", "hy_b2",
            "conf_dw_b", "conf_ln_g", "conf_ln_b", "gate_b")
    as_is = ("hy_conv_w", "hy_w2", "hy_w3", "hy_bias", "ret_decay", "conf_dw_w")
    bf16 = ("gate_w", "hy_out", "mla_out", "ret_out", "conf_out", "w_o", "ffn_w1", "ffn_w2")
    return {
        "w_in": w_in_all,
        "mla_w_uq": w_uq.astype(BF16), "mla_w_ukv": w_ukv.astype(BF16),
        "hy_w1": jnp.pad(p["hy_w1"], ((0, 0), (0, LANES - HY_EMB), (0, 0))),
        **{name: row(name) for name in rows},
        **{name: p[name] for name in as_is},
        **{name: p[name].astype(BF16) for name in bf16},
    }


def kernel(x_prompt, x_sample, cache_mla_ckv, cache_mla_krope, state_ret, c, c_ctx, ada_w, ada_b, norm1_g, w_in, hy_conv_w, hy_conv_b, hy_w1, hy_b1, hy_w2, hy_b2, hy_w3, hy_bias, hy_out, mla_q_norm, mla_w_uq, mla_kv_norm, mla_w_ukv, mla_out, ret_decay, ret_out, conf_dw_w, conf_dw_b, conf_ln_g, conf_ln_b, conf_out, gate_w, gate_b, w_o, norm2_g, ffn_w1, ffn_w2, final_norm_g):
    p = dict(norm1_g=norm1_g, hy_conv_w=hy_conv_w, hy_conv_b=hy_conv_b, hy_w1=hy_w1, hy_b1=hy_b1,
             hy_w2=hy_w2, hy_b2=hy_b2, hy_w3=hy_w3, hy_bias=hy_bias, hy_out=hy_out,
             mla_q_norm=mla_q_norm, mla_w_uq=mla_w_uq, mla_kv_norm=mla_kv_norm, mla_w_ukv=mla_w_ukv,
             mla_out=mla_out, ret_decay=ret_decay, ret_out=ret_out, conf_dw_w=conf_dw_w,
             conf_dw_b=conf_dw_b, conf_ln_g=conf_ln_g, conf_ln_b=conf_ln_b, conf_out=conf_out,
             gate_w=gate_w, gate_b=gate_b, w_o=w_o, norm2_g=norm2_g, ffn_w1=ffn_w1, ffn_w2=ffn_w2)
    Bp, Lp, D = x_prompt.shape
    Bs, Ls, _ = x_sample.shape
    depth = w_in.shape[0]

    cond = jnp.concatenate([c_ctx[None, :], c, jnp.zeros((MOD_ROWS - 1 - Bs, D), F32)], axis=0)
    mod = _modulation(cond, ada_w, ada_b)
    tables_p = _dft_tables(Lp)
    tables_s = _dft_tables(Ls)
    final_g = final_norm_g.reshape(1, D)

    xp = x_prompt.reshape(Bp * Lp, D)
    xs = x_sample.reshape(Bs * Ls, D)
    caches, states = None, None
    weights = _stacked_weights(w_in, p)
    for l in range(depth):
        lw = dict(weights, layer=l)
        final = l == depth - 1
        xp, ckv, kr, states = _trunk_layer(xp, mod, lw, final_g, tables_p, Bp, Lp, None, None, final,
                                           caches, states)
        caches = (ckv, kr)
        xs, _, _, _ = _trunk_layer(xs, mod, lw, final_g, tables_s, Bs, Ls,
                                   (cache_mla_ckv, cache_mla_krope), state_ret, final)
    return (xp.reshape(Bp, Lp, D), xs.reshape(Bs, Ls, D), *caches, states)
```

```python
import functools
import math

import jax
import jax.numpy as jnp
from jax import lax
from jax.experimental import pallas as pl
from jax.experimental.pallas import tpu as pltpu

F32 = jnp.float32
BF16 = jnp.bfloat16

D_MODEL = 1024
DEPTH = 2
PAST_LEN = 256
EPS = 1e-6
GRID_W = 64

HY_WIDTH = 256
HY_EMB = 33
HY_BANDS = (HY_EMB - 1) // 2
HY_FFN = 64
HY_FAST_DECAY = 0.3
HY_SLOW_DECAY = 1.5
HY_TARGET = 1e-2

MLA_HEADS = 4
MLA_Q_LORA = 256
MLA_KV_LORA = 128
MLA_NOPE = 64
MLA_ROPE = 32
MLA_V = 64
ROPE_BASE = 10000.0

RET_HEADS = 4
RET_DK = 64
RET_DV = 64
RET_BLOCK = 256

CONF_WIDTH = 256
CONF_KERNEL = 31

D_FF = ((8 * D_MODEL // 3 + 255) // 256) * 256

HY_COLS = 3 * HY_WIDTH
MLA_COLS = MLA_Q_LORA + MLA_KV_LORA + MLA_ROPE
RET_COLS = 2 * RET_HEADS * RET_DK + 2 * RET_HEADS * RET_DV
CONF_COLS = 2 * CONF_WIDTH

VMEM_BYTES_V7X = 64 * 1024 * 1024
SUBLANES = 8
LANES = 128
TOKEN_TILE = 512
WIDE_TOKEN_TILE = 1024
Q_TILE = 512
MLA_SOFTMAX_ROWS = 16
DFT_TILE = 512
SHORT_SEQ_ROWS = 1024
HYENA_GROUP_ROWS = 1024
HYENA_ROWS = 1024
CONF_ROWS = 128
REGROUP_COLS = 128
CONF_HALO = 16
MXU_DIM_V7X = 256
FFN_CHUNK = 4 * MXU_DIM_V7X
MOD_ROWS = 8


def _cparams(sem, vmem_mb):
    vmem_bytes = vmem_mb * 1024 * 1024
    assert vmem_bytes < VMEM_BYTES_V7X
    return pltpu.CompilerParams(dimension_semantics=sem, vmem_limit_bytes=vmem_bytes)


def _dot(a, b):
    return jnp.dot(a, b, preferred_element_type=F32)


def _dot_nt(a, b):
    return lax.dot_general(a, b, (((1,), (1,)), ((), ())), preferred_element_type=F32)


def _dot_exact(a, b):
    return jnp.dot(a, b, preferred_element_type=F32, precision=lax.Precision.HIGHEST)


def _dot_split(a, b):
    a_hi, b_hi = a.astype(BF16), b.astype(BF16)
    a_lo = (a - a_hi.astype(F32)).astype(BF16)
    b_lo = (b - b_hi.astype(F32)).astype(BF16)
    return _dot(a_hi, b_hi) + (_dot(a_lo, b_hi) + _dot(a_hi, b_lo))


def _rms(x, g):
    return x * lax.rsqrt(jnp.mean(x * x, axis=-1, keepdims=True) + EPS) * g


def _sigmoid(x):
    return 0.5 * jnp.tanh(0.5 * x) + 0.5


def _silu(x):
    return x * _sigmoid(x)


def _mod_kernel(c_ref, w_ref, b_ref, o_ref):
    s = _silu(c_ref[...]).astype(BF16)
    o_ref[...] = _dot(s, w_ref[...].astype(BF16)) + b_ref[...]


def _modulation(cond, ada_w, ada_b):
    depth, d, cols = ada_w.shape
    blk = 1024
    out = pl.pallas_call(
        _mod_kernel,
        grid=(depth, cols // blk),
        in_specs=[
            pl.BlockSpec((MOD_ROWS, d), lambda l, j: (0, 0)),
            pl.BlockSpec((None, d, blk), lambda l, j: (l, 0, j)),
            pl.BlockSpec((None, 1, blk), lambda l, j: (l, 0, j)),
        ],
        out_specs=pl.BlockSpec((None, MOD_ROWS, blk), lambda l, j: (l, 0, j)),
        out_shape=jax.ShapeDtypeStruct((depth, MOD_ROWS, cols), F32),
        compiler_params=_cparams(("arbitrary", "arbitrary"), 32),
        name="modulation",
    )(cond, ada_w, ada_b.reshape(depth, 1, cols))
    return out.reshape(depth, MOD_ROWS, 6, d)


def _seqs_per_step(B, L):
    return max(1, min(B, SHORT_SEQ_ROWS // L))


def _per_layer(a, layer, single_buffer=False):
    mode = dict(pipeline_mode=pl.Buffered(1)) if single_buffer else {}
    return pl.BlockSpec((None,) + a.shape[1:], lambda *_: (layer,) + (0,) * (a.ndim - 1), **mode)


def _mod_spec(layer, tiles_per_seq, latent):
    group = (lambda i: 1 + i // tiles_per_seq) if latent else (lambda i: 0)
    return pl.BlockSpec((None, None, 6, D_MODEL), lambda i: (layer, group(i), 0, 0))


HEAD_LANES = LANES


def _rope_lanes(x, cos, sa, sb):
    reps = x.shape[1] // HEAD_LANES
    wide = lambda t: jnp.concatenate([t] * reps, axis=1) if reps > 1 else t
    half = MLA_ROPE // 2
    return (x * wide(cos) + pltpu.roll(x, x.shape[1] - half, 1) * wide(sa)
            + pltpu.roll(x, half, 1) * wide(sb))


def _inproj_kernel(*refs, seqs_per_tile, seq_len, latent, cache_layers, aliased):
    x_ref, mod_ref, g1_ref, win_ref, qn_ref, kvn_ref, wuq_ref, wukv_ref = refs[:8]
    refs = refs[8:]
    c_cq = HY_COLS
    c_ck = c_cq + MLA_Q_LORA
    c_ret = c_ck + MLA_KV_LORA + HEAD_LANES
    c_conf = c_ret + RET_COLS
    why_ref, wcq_ref, wck_ref, wret_ref, wconf_ref = (
        win_ref.at[:, a:b] for a, b in ((0, c_cq), (c_cq, c_ck), (c_ck, c_ret), (c_ret, c_conf),
                                        (c_conf, c_conf + CONF_COLS)))
    if latent:
        cos_ref, sa_ref, sb_ref = refs[:3]
        refs = refs[3:]
    if aliased:
        refs = refs[2:]
    hv_ref, hx1_ref, hx2_ref, q_ref, kh_ref, vh_ref, ret_ref, conf_ref, rkt_ref = refs[:9]
    if not latent:
        ckv_ref, kr_ref = refs[9:]
    x = x_ref[...]
    h = _rms(x, g1_ref[...]) * (1.0 + mod_ref[1:2, :]) + mod_ref[0:1, :]
    hb = h.astype(BF16)

    u = _dot(hb, why_ref[...])
    for p, o_ref in enumerate((hv_ref, hx1_ref, hx2_ref)):
        part = u[:, p * HY_WIDTH:(p + 1) * HY_WIDTH]
        if seqs_per_tile == 1:
            o_ref[...] = part
        else:
            for s in range(seqs_per_tile):
                o_ref[:, s * HY_WIDTH:(s + 1) * HY_WIDTH] = part[s * seq_len:(s + 1) * seq_len]

    heads_w = MLA_HEADS * HEAD_LANES
    cq = _dot(hb, wcq_ref[...])
    q = _dot(_rms(cq, qn_ref[...]).astype(BF16), wuq_ref[...])
    ck = _dot(hb, wck_ref[...])
    ckv = _rms(ck[:, :MLA_KV_LORA], kvn_ref[...])
    kr_block = ck[:, MLA_KV_LORA:MLA_KV_LORA + HEAD_LANES]
    if latent:
        q = _rope_lanes(q, cos_ref[...], sa_ref[...], sb_ref[...])
        kr_block = _rope_lanes(kr_block, cos_ref[...], sa_ref[...], sb_ref[...])
    else:
        kr_t = kr_block.T[MLA_NOPE:MLA_NOPE + MLA_ROPE, :]
        for s in range(seqs_per_tile):
            rows = slice(s * seq_len, (s + 1) * seq_len)
            for out_ref, val in ((ckv_ref, ckv[rows]), (kr_ref, kr_t[:, rows])):
                mine = out_ref.at[s, 0] if cache_layers else out_ref.at[s]
                mine[...] = val
                for other in range(1, cache_layers):
                    out_ref[s, other] = jnp.zeros(out_ref.shape[2:], F32)
    q_ref[...] = (q * ((MLA_NOPE + MLA_ROPE) ** -0.5 * math.log2(math.e))).astype(BF16)
    kv = _dot(ckv.astype(BF16), wukv_ref[...])
    kh_ref[...] = (kv[:, :heads_w] + jnp.concatenate([kr_block] * MLA_HEADS, axis=1)).astype(BF16)
    lane = lax.broadcasted_iota(jnp.int32, (1, heads_w), 1)
    ones_lane = jnp.where(lane % HEAD_LANES == MLA_V, 1.0, 0.0)
    vh_ref[...] = (kv[:, heads_w:] + ones_lane).astype(BF16)
    ret = _dot(hb, wret_ref[...])
    ret_ref[...] = ret
    qk = RET_HEADS * RET_DK
    rkt_ref[...] = ret[:, qk:2 * qk].T
    conf_ref[...] = _dot(hb, wconf_ref[...])


def _inproj(x, mod, lw, B, L, latent, caches_out=None):
    T = B * L
    TM = TOKEN_TILE
    nt = T // TM
    if L >= TM:
        tiles_per_seq, seqs_per_tile = L // TM, 1
        hy_block = (TM, HY_WIDTH)
        hy_map = lambda i: (i % tiles_per_seq, i // tiles_per_seq)
    else:
        tiles_per_seq, seqs_per_tile = 1, TM // L
        hy_block = (L, seqs_per_tile * HY_WIDTH)
        hy_map = lambda i: (0, i)
    row = lambda w: pl.BlockSpec((TM, w), lambda i: (i, 0))
    weights = (lw["norm1_g"], lw["w_in"], lw["mla_q_norm"], lw["mla_kv_norm"], lw["mla_w_uq"], lw["mla_w_ukv"])
    hy_shape = jax.ShapeDtypeStruct((L, B * HY_WIDTH), F32)
    qk = RET_HEADS * RET_DK
    heads_w = MLA_HEADS * HEAD_LANES
    ins = [x, mod, *weights]
    in_specs = ([row(D_MODEL), _mod_spec(lw["layer"], tiles_per_seq, latent)]
                + [_per_layer(w, lw["layer"]) for w in weights])
    out_specs = ([pl.BlockSpec(hy_block, hy_map)] * 3
                 + [row(heads_w)] * 3 + [row(RET_COLS), row(CONF_COLS), pl.BlockSpec((qk, TM), lambda i: (0, i))])
    out_shape = ([hy_shape] * 3 + [jax.ShapeDtypeStruct((T, heads_w), BF16)] * 3
                 + [jax.ShapeDtypeStruct((T, RET_COLS), F32), jax.ShapeDtypeStruct((T, CONF_COLS), F32),
                    jax.ShapeDtypeStruct((qk, T), F32)])
    if latent:
        ins += list(_rope_lane_tables(L))
        in_specs += [pl.BlockSpec((TM, HEAD_LANES), lambda i: (i % tiles_per_seq, 0))] * 3
    aliases = {}
    cache_layers = 0
    if not latent:
        assert tiles_per_seq == 1
        layer, depth = lw["layer"], lw["w_in"].shape[0]
        dims = ((L, MLA_KV_LORA), (MLA_ROPE, L))
        if caches_out is None:
            cache_layers = depth
            out_specs += [pl.BlockSpec((seqs_per_tile, depth) + d, lambda i: (i, 0, 0, 0)) for d in dims]
        else:
            aliases = {len(ins) + k: len(out_shape) + k for k in range(len(dims))}
            ins += list(caches_out)
            in_specs += [pl.BlockSpec(memory_space=pl.ANY)] * len(dims)
            out_specs += [pl.BlockSpec((seqs_per_tile, None) + d, lambda i: (i, layer, 0, 0)) for d in dims]
        out_shape += [jax.ShapeDtypeStruct((B, depth) + d, F32) for d in dims]
    return pl.pallas_call(
        functools.partial(_inproj_kernel, seqs_per_tile=seqs_per_tile, seq_len=L, latent=latent,
                          cache_layers=cache_layers, aliased=bool(aliases)),
        grid=(nt,),
        in_specs=in_specs,
        out_specs=out_specs,
        out_shape=out_shape,
        input_output_aliases=aliases,
        compiler_params=_cparams(("arbitrary",), 48),
        name="inproj",
    )(*ins)


def _dft_tables(L):
    N = 2 * L
    k_lo = min(L, 32)
    k_hi = L // k_lo
    t = jnp.arange(L, dtype=jnp.int32)[None, :]

    def cs(k):
        m = (k[:, None] * t) % N
        ang = m.astype(F32) * (2.0 * math.pi / N)
        return jnp.cos(ang), jnp.sin(ang)

    ca, sa = cs(jnp.arange(k_hi, dtype=jnp.int32) * k_lo)
    cb, sb = cs(jnp.arange(k_lo, dtype=jnp.int32))
    cos = (ca[:, None, :] * cb[None, :, :] - sa[:, None, :] * sb[None, :, :]).reshape(L, L)
    sin = (sa[:, None, :] * cb[None, :, :] + ca[:, None, :] * sb[None, :, :]).reshape(L, L)
    return cos.astype(BF16), (-sin).astype(BF16)


def _filter_features(L):
    t = jnp.linspace(0.0, 1.0, L, dtype=F32)[:, None]
    w = 2.0 * math.pi * jnp.arange(L, dtype=F32)[:, None] / L
    f = jnp.linspace(1e-4, HY_BANDS - 1, HY_BANDS, dtype=F32)[None, :]
    z = jnp.concatenate([t, jnp.cos(f * w), -jnp.sin(f * w)], axis=-1)
    z = jnp.pad(z, ((0, 0), (0, LANES - HY_EMB)))
    max_decay = math.log(HY_TARGET) / HY_FAST_DECAY
    min_decay = math.log(HY_TARGET) / HY_SLOW_DECAY
    deltas = jnp.abs(jnp.linspace(min_decay, max_decay, HY_WIDTH, dtype=F32))
    decay = jnp.exp(-t * deltas[None, :])
    return z, decay


def _alternating(rows):
    t = lax.broadcasted_iota(jnp.int32, (rows, 1), 0)
    return (1 - 2 * (t & 1)).astype(F32)


def _filter_kernel(z_ref, dec_ref, w1_ref, b1_ref, w2_ref, b2_ref, w3_ref, cos_ref, sin_ref,
                   kr_ref, ki_ref, kn_ref, hsum_ref, hdiff_ref, *, L):
    j = pl.program_id(0)
    W = HY_WIDTH
    N = 2 * L

    @pl.when(j == 0)
    def _():
        h = jnp.sin(_dot_exact(z_ref[...], w1_ref[...]) + b1_ref[...])
        h = jnp.sin(_dot_exact(h, w2_ref[...]) + b2_ref[...])
        h = _dot_split(h, w3_ref[...]) * jnp.concatenate([dec_ref[...]] * 4, axis=1)
        cs = jnp.sum(jnp.abs(h), axis=0, keepdims=True)
        s0 = cs[:, 0:W] + cs[:, W:2 * W]
        s1 = cs[:, 2 * W:3 * W] + cs[:, 3 * W:4 * W]
        h = h / jnp.concatenate([s0, s0, s1, s1], axis=1)
        row = lax.broadcasted_iota(jnp.int32, h.shape, 0)
        col = lax.broadcasted_iota(jnp.int32, h.shape, 1)
        backward = (col // W) % 2 == 1
        h = jnp.where(backward & (row == 0), 0.0, h)
        fwd = jnp.concatenate([h[:, 0:W], h[:, 2 * W:3 * W]], axis=1)
        bwd = jnp.concatenate([h[:, W:2 * W], h[:, 3 * W:4 * W]], axis=1)
        hsum_ref[...] = (fwd + bwd).astype(BF16)
        hdiff_ref[...] = (fwd - bwd).astype(BF16)
        nyq = jnp.sum((fwd + bwd) * _alternating(L), axis=0, keepdims=True) * (1.0 / N)
        for o in range(2):
            kn_ref[o] = nyq[:, o * W:(o + 1) * W]

    sr = _dot(cos_ref[...], hsum_ref[...])
    si = _dot(sin_ref[...], hdiff_ref[...])
    row = lax.broadcasted_iota(jnp.int32, (sr.shape[0], W), 0)
    scale = jnp.where((row == 0) & (j == 0), 1.0 / N, 2.0 / N)
    for o in range(2):
        kr_ref[o] = sr[:, o * W:(o + 1) * W] * scale
        ki_ref[o] = si[:, o * W:(o + 1) * W] * scale


def _filter_spectra(lw, L, tables):
    z, decay = _filter_features(L)
    cos, msin = tables
    Tk = min(DFT_TILE, L)
    full = lambda a: pl.BlockSpec(a.shape, lambda j: (0,) * a.ndim)
    mlp = (lw["hy_w1"], lw["hy_b1"], lw["hy_w2"], lw["hy_b2"], lw["hy_w3"])
    ins = (z, decay, *mlp)
    tile = pl.BlockSpec((Tk, L), lambda j: (j, 0))
    spec = pl.BlockSpec((2, Tk, HY_WIDTH), lambda j: (0, j, 0))
    return pl.pallas_call(
        functools.partial(_filter_kernel, L=L),
        grid=(L // Tk,),
        in_specs=[full(z), full(decay)] + [_per_layer(a, lw["layer"]) for a in mlp] + [tile, tile],
        out_specs=[spec, spec, pl.BlockSpec((2, 1, HY_WIDTH), lambda j: (0, 0, 0))],
        out_shape=[jax.ShapeDtypeStruct((2, L, HY_WIDTH), F32)] * 2
        + [jax.ShapeDtypeStruct((2, 1, HY_WIDTH), F32)],
        scratch_shapes=[pltpu.VMEM((L, 2 * HY_WIDTH), BF16)] * 2,
        compiler_params=_cparams(("arbitrary",), 48),
        name="hyena_filter",
    )(*ins, cos, msin)


def _hyena_kernel(v_ref, x1_ref, x2_ref, cw_ref, cb_ref, bias_ref, kr_ref, ki_ref, kn_ref,
                  cos_ref, sin_ref, y_ref, cur_ref, curb_ref, yr_ref, yi_ref, gate_ref,
                  *, L, W, T):
    reps = W // HY_WIDTH
    tiled = lambda a: jnp.concatenate([a] * reps, axis=1) if reps > 1 else a
    alt = _alternating(L)

    def short_conv(u_ref, p):
        u = u_ref[...]
        row = lax.broadcasted_iota(jnp.int32, (L, W), 0)
        prev = jnp.where(row == 0, 0.0, pltpu.roll(u, 1, 0))
        nxt = jnp.where(row == L - 1, 0.0, pltpu.roll(u, L - 1, 0))
        cols = slice(p * HY_WIDTH, (p + 1) * HY_WIDTH)
        w = [tiled(cw_ref[k:k + 1, cols]) for k in range(3)]
        return prev * w[0] + u * w[1] + nxt * w[2] + tiled(cb_ref[:, cols])

    cur_ref[...] = short_conv(v_ref, 0)
    for o, x_ref in enumerate((x1_ref, x2_ref)):
        cur = cur_ref[...]
        curb_ref[...] = cur.astype(BF16)
        gate_ref[...] = short_conv(x_ref, o + 1)
        nyq = jnp.sum(cur * alt, axis=0, keepdims=True) * tiled(kn_ref[o])
        for f in range(L // T):
            rows = slice(f * T, (f + 1) * T)
            cb = curb_ref[...]
            xr = _dot(cos_ref[rows, :], cb)
            xi = _dot(sin_ref[rows, :], cb)
            kr, ki = tiled(kr_ref[o, rows, :]), tiled(ki_ref[o, rows, :])
            yr_ref[rows, :] = (xr * kr - xi * ki).astype(BF16)
            yi_ref[rows, :] = (xr * ki + xi * kr).astype(BF16)
        for t in range(L // T):
            rows = slice(t * T, (t + 1) * T)
            conv = (_dot(cos_ref[rows, :], yr_ref[...]) + _dot(sin_ref[rows, :], yi_ref[...])
                    + alt[rows] * nyq + cur_ref[rows, :] * tiled(bias_ref[o:o + 1, :]))
            out = gate_ref[rows, :] * conv
            if o == 0:
                cur_ref[rows, :] = out
            else:
                y_ref[rows, :] = out


def _hyena(hv, hx1, hx2, lw, spectra, tables, B, L):
    kr, ki, kn = spectra
    cos, msin = tables
    T = min(HYENA_ROWS, L)
    W = HY_WIDTH * max(1, min(B, HYENA_GROUP_ROWS // L))
    ng = (B * HY_WIDTH) // W
    col = pl.BlockSpec((L, W), lambda g: (0, g))
    once = lambda a: pl.BlockSpec(a.shape, lambda g: (0,) * a.ndim, pipeline_mode=pl.Buffered(1))
    params = (lw["hy_conv_w"], lw["hy_conv_b"], lw["hy_bias"])
    consts = (kr, ki, kn, cos, msin)
    return pl.pallas_call(
        functools.partial(_hyena_kernel, L=L, W=W, T=T),
        grid=(ng,),
        in_specs=[col, col, col] + [_per_layer(a, lw["layer"]) for a in params] + [once(a) for a in consts],
        out_specs=col,
        out_shape=jax.ShapeDtypeStruct((L, B * HY_WIDTH), F32),
        scratch_shapes=[pltpu.VMEM((L, W), F32), pltpu.VMEM((L, W), BF16), pltpu.VMEM((L, W), BF16),
                        pltpu.VMEM((L, W), BF16), pltpu.VMEM((L, W), F32)],
        compiler_params=_cparams(("arbitrary",), 60),
        name="hyena_conv",
    )(hv, hx1, hx2, *params, *consts)


def _mla_kernel(*refs, L, latent, seqs):
    if latent:
        q_ref, kh_ref, vh_ref, cckv_ref, ckr_ref, wukv_ref, o_ref, s_ref, p_ref, kctx_ref, vctx_ref = refs
    else:
        q_ref, kh_ref, vh_ref, o_ref, s_ref, p_ref = refs
    TQ = q_ref.shape[0] // seqs
    heads_w = MLA_HEADS * HEAD_LANES
    block = lambda h: slice(h * HEAD_LANES, (h + 1) * HEAD_LANES)

    if latent:
        @pl.when(pl.program_id(1) == 0)
        def _():
            kv = _dot(cckv_ref[...].astype(BF16), wukv_ref[...])
            zeros = lambda w: jnp.zeros((PAST_LEN, w), F32)
            kr_block = jnp.concatenate(
                [zeros(MLA_NOPE), ckr_ref[...], zeros(HEAD_LANES - MLA_NOPE - MLA_ROPE)], axis=1)
            kctx_ref[...] = (kv[:, :heads_w] + jnp.concatenate([kr_block] * MLA_HEADS, axis=1)).astype(BF16)
            lane = lax.broadcasted_iota(jnp.int32, (1, heads_w), 1)
            vctx_ref[...] = (kv[:, heads_w:] + jnp.where(lane % HEAD_LANES == MLA_V, 1.0, 0.0)).astype(BF16)

    def attend(q_rows, k_rows):
        def scores(h):
            qh = q_ref[q_rows, block(h)]
            s_ref[h, :, 0:L] = _dot_nt(qh, kh_ref[k_rows, block(h)])
            if latent:
                s_ref[h, :, L:L + PAST_LEN] = _dot_nt(qh, kctx_ref[:, block(h)])

        outs = []
        scores(0)
        for h in range(MLA_HEADS):
            if h + 1 < MLA_HEADS:
                scores(h + 1)
            for r in range(TQ // MLA_SOFTMAX_ROWS):
                rows = slice(r * MLA_SOFTMAX_ROWS, (r + 1) * MLA_SOFTMAX_ROWS)
                s = s_ref[h, rows, :]
                p_ref[h, rows, :] = jnp.exp2(s - jnp.max(s, axis=-1, keepdims=True)).astype(BF16)
            pv = _dot(p_ref[h, :, 0:L], vh_ref[k_rows, block(h)])
            if latent:
                pv = pv + _dot(p_ref[h, :, L:L + PAST_LEN], vctx_ref[:, block(h)])
            outs.append(pv[:, :MLA_V] / pv[:, MLA_V:MLA_V + 1])
        o_ref[q_rows, :] = jnp.concatenate(outs, axis=-1)

    for sq in range(seqs):
        attend(slice(sq * TQ, (sq + 1) * TQ), slice(sq * L, (sq + 1) * L))


def _rope_tables(L):
    rows = L // GRID_W
    row = jnp.repeat(jnp.arange(rows), GRID_W).astype(F32)
    col = jnp.tile(jnp.arange(GRID_W), rows).astype(F32)
    per_axis = MLA_ROPE // 4
    inv = ROPE_BASE ** (-jnp.arange(per_axis, dtype=F32) / per_axis)
    ang = jnp.concatenate([row[:, None] * inv, col[:, None] * inv], axis=-1)
    return jnp.cos(ang), jnp.sin(ang)


def _rope_lane_tables(L):
    cos, sin = _rope_tables(L)
    pad = HEAD_LANES - MLA_NOPE - MLA_ROPE
    one, zero = jnp.ones((L, MLA_NOPE), F32), jnp.zeros((L, MLA_NOPE), F32)
    half0 = jnp.zeros_like(sin)
    cos_t = jnp.concatenate([one, cos, cos, one[:, :pad]], axis=1)
    sa_t = jnp.concatenate([zero, -sin, half0, zero[:, :pad]], axis=1)
    sb_t = jnp.concatenate([zero, half0, sin, zero[:, :pad]], axis=1)
    return cos_t, sa_t, sb_t


def _mla(qh, kh, vh, lw, B, L, ctx):
    latent = ctx is not None
    TQ = min(Q_TILE, L)
    nq = L // TQ
    seqs = 1 if latent or nq > 1 else _seqs_per_step(B, L)
    Lk = L + PAST_LEN if latent else L
    heads_w = MLA_HEADS * HEAD_LANES
    seq = pl.BlockSpec((seqs * L, heads_w), lambda b, i: (b, 0))
    ins = [qh, kh, vh]
    specs = [pl.BlockSpec((seqs * TQ, heads_w), lambda b, i: (b * nq + i, 0)), seq, seq]
    scratch = [pltpu.VMEM((MLA_HEADS, TQ, Lk), F32), pltpu.VMEM((MLA_HEADS, TQ, Lk), BF16)]
    if latent:
        ins += [ctx[0], ctx[1], lw["mla_w_ukv"]]
        layer = lw["layer"]
        specs += [pl.BlockSpec((None, None, PAST_LEN, MLA_KV_LORA), lambda b, i: (b, layer, 0, 0)),
                  pl.BlockSpec((None, None, PAST_LEN, MLA_ROPE), lambda b, i: (b, layer, 0, 0)),
                  _per_layer(lw["mla_w_ukv"], lw["layer"])]
        scratch += [pltpu.VMEM((PAST_LEN, heads_w), BF16)] * 2
    return pl.pallas_call(
        functools.partial(_mla_kernel, L=L, latent=latent, seqs=seqs),
        grid=(B // seqs, nq),
        in_specs=specs,
        out_specs=pl.BlockSpec((seqs * TQ, MLA_HEADS * MLA_V), lambda b, i: (b * nq + i, 0)),
        out_shape=jax.ShapeDtypeStruct((B * L, MLA_HEADS * MLA_V), F32),
        scratch_shapes=scratch,
        compiler_params=_cparams(("arbitrary", "arbitrary"), 48),
        name="mla_attention",
    )(*ins)


def _ret_kernel(*refs, L, seqs, has_state, emit_state, state_layers, aliased):
    refs = list(refs)
    u_ref, kt_ref, dl_ref = refs[:3]
    pos = 3
    s0_ref = None
    if has_state:
        s0_ref = refs[pos]
        pos += 1
    if aliased:
        pos += 1
    y_ref = refs[pos]
    pos += 1
    sout_ref = None
    if emit_state:
        sout_ref = refs[pos]
        pos += 1
    of_ref, kv_ref, sin_ref, dcomb_ref, dtab_ref, hmask_ref, avg_ref = refs[pos:]

    C = RET_BLOCK
    n = L // C
    H, DK, DV = RET_HEADS, RET_DK, RET_DV
    qk = H * DK
    DK_F, DK_B, DQ_F, DQ_B, DC_F, DC_B, MASK = range(7)
    use_cross = has_state or n > 1

    @pl.when(pl.program_id(0) == 0)
    def _():
        x = dl_ref[...]
        log_g = jnp.minimum(x, 0.0) - jnp.log1p(jnp.exp(-jnp.abs(x)))
        gf = [log_g[0:1, h:h + 1] for h in range(H)]
        gb = [log_g[1:2, h:h + 1] for h in range(H)]
        diff = (lax.broadcasted_iota(jnp.int32, (C, C), 0)
                - lax.broadcasted_iota(jnp.int32, (C, C), 1)).astype(F32)
        idx = lax.broadcasted_iota(jnp.int32, (C, 1), 0).astype(F32)
        for h in range(H):
            dcomb_ref[h] = (jnp.where(diff >= 0, jnp.exp(jnp.maximum(diff, 0.0) * gf[h]), 0.0)
                            + jnp.where(diff <= 0, jnp.exp(jnp.maximum(-diff, 0.0) * gb[h]), 0.0))

        def head_cols(lag, g):
            return jnp.concatenate([jnp.broadcast_to(jnp.exp(lag * g[h]), (C, DK)) for h in range(H)], axis=1)

        def head_rows(lag, g):
            return jnp.concatenate(
                [jnp.broadcast_to(jnp.exp(lag * g[h]) * (DK ** -0.5), (DK, C)) for h in range(H)], axis=0)

        tok = lax.broadcasted_iota(jnp.int32, (1, C), 1).astype(F32)
        dtab_ref[DK_F] = head_rows(C - 1.0 - tok, gf)
        dtab_ref[DK_B] = head_rows(tok, gb)
        dtab_ref[DQ_F] = head_cols(idx + 1.0, gf)
        dtab_ref[DQ_B] = head_cols(C - idx, gb)
        row_head = lax.broadcasted_iota(jnp.int32, (qk, qk), 0) // DK
        col_head = lax.broadcasted_iota(jnp.int32, (qk, qk), 1) // DV
        same_head = row_head == col_head
        for slot, g in ((DC_F, gf), (DC_B, gb)):
            dc = jnp.zeros((qk, qk), F32)
            for h in range(H):
                dc = jnp.where(same_head & (row_head == h), jnp.exp(C * g[h]), dc)
            dtab_ref[slot] = dc
        dtab_ref[MASK] = same_head.astype(F32)
        avg_ref[...] = jnp.where(same_head, 1.0 / DV, 0.0).astype(BF16)
        lane_head = lax.broadcasted_iota(jnp.int32, (C, qk), 1) // DK
        for h in range(H):
            hmask_ref[h] = jnp.where(lane_head == h, 1.0, 0.0).astype(BF16)

    def head_mean(a):
        total = None
        for _ in range(3):
            part = a.astype(BF16)
            a = a - part.astype(F32)
            term = _dot(part, avg_ref[...])
            total = term if total is None else total + term
        return total

    for sq in range(seqs):
        chunk = lambda c: slice(sq * L + c * C, sq * L + (c + 1) * C)

        for c in range(n):
            rows = chunk(c)
            qb = u_ref[rows, 0:qk].astype(BF16)
            kb = (u_ref[rows, qk:2 * qk] * (DK ** -0.5)).astype(BF16)
            vb = u_ref[rows, 2 * qk:2 * qk + H * DV].astype(BF16)
            att = [(_dot_nt(qb * hmask_ref[h], kb) * dcomb_ref[h]).astype(BF16) for h in range(H)]
            v_heads = jnp.concatenate([vb * hmask_ref[h] for h in range(H)], axis=0)
            of_ref[c * C:(c + 1) * C, :] = _dot(jnp.concatenate(att, axis=1), v_heads)
            kt = kt_ref[:, rows]
            for d, slot in ((0, DK_F), (1, DK_B)):
                kv_ref[d, c] = _dot((kt * dtab_ref[slot]).astype(BF16), vb)

        finals = []
        for d, slot in ((0, DC_F), (1, DC_B)):
            if has_state:
                zero = jnp.zeros((DK, DV), F32)
                S = jnp.concatenate(
                    [jnp.concatenate([s0_ref[sq, d, h] if g == h else zero for g in range(H)], axis=1)
                     for h in range(H)], axis=0)
            else:
                S = jnp.zeros((qk, H * DV), F32)
            for c in (range(n) if d == 0 else reversed(range(n))):
                if use_cross:
                    sin_ref[d, c] = S.astype(BF16)
                S = S * dtab_ref[slot] + kv_ref[d, c] * dtab_ref[MASK]
            finals.append(S)
        if emit_state:
            mine = sout_ref.at[sq, 0] if state_layers else sout_ref.at[sq]
            for d in range(2):
                for h in range(H):
                    mine[d, h] = finals[d][h * DK:(h + 1) * DK, h * DV:(h + 1) * DV]
            for other in range(1, state_layers):
                sout_ref[sq, other] = jnp.zeros(sout_ref.shape[2:], F32)

        for c in range(n):
            rows = chunk(c)
            tot = of_ref[c * C:(c + 1) * C, :]
            if use_cross:
                qb = u_ref[rows, 0:qk].astype(BF16)
                tot = (tot + _dot(qb, sin_ref[0, c]) * dtab_ref[DQ_F]
                       + _dot(qb, sin_ref[1, c]) * dtab_ref[DQ_B])
            xc = tot - head_mean(tot)
            normed = xc * lax.rsqrt(head_mean(xc * xc) + EPS)
            gate = u_ref[rows, 2 * qk + H * DV:2 * qk + 2 * H * DV]
            y_ref[rows, :] = _silu(gate) * normed


def _retention(u_ret, kt_ret, lw, B, L, state, emit_state, states_out=None):
    has_state = state is not None
    seqs = _seqs_per_step(B, L)
    st_dims = (2, RET_HEADS, RET_DK, RET_DV)
    ins = [u_ret, kt_ret, lw["ret_decay"]]
    specs = [pl.BlockSpec((seqs * L, RET_COLS), lambda b: (b, 0)),
             pl.BlockSpec((RET_HEADS * RET_DK, seqs * L), lambda b: (0, b)),
             _per_layer(lw["ret_decay"], lw["layer"])]
    if has_state:
        layer = lw["layer"]
        ins.append(state)
        specs.append(pl.BlockSpec((seqs, None) + st_dims, lambda b: (b, layer, 0, 0, 0, 0)))
    vd = RET_HEADS * RET_DV
    out_specs = [pl.BlockSpec((seqs * L, vd), lambda b: (b, 0))]
    out_shape = [jax.ShapeDtypeStruct((B * L, vd), F32)]
    aliases = {}
    state_layers = 0
    if emit_state:
        layer, depth = lw["layer"], lw["ret_decay"].shape[0]
        if states_out is None:
            state_layers = depth
            out_specs.append(pl.BlockSpec((seqs, depth) + st_dims, lambda b: (b, 0, 0, 0, 0, 0)))
        else:
            aliases = {len(ins): 1}
            ins.append(states_out)
            specs.append(pl.BlockSpec(memory_space=pl.ANY))
            out_specs.append(pl.BlockSpec((seqs, None) + st_dims, lambda b: (b, layer, 0, 0, 0, 0)))
        out_shape.append(jax.ShapeDtypeStruct((B, depth) + st_dims, F32))
    res = pl.pallas_call(
        functools.partial(_ret_kernel, L=L, seqs=seqs, has_state=has_state, emit_state=emit_state,
                          state_layers=state_layers, aliased=bool(aliases)),
        grid=(B // seqs,),
        in_specs=specs,
        out_specs=out_specs,
        out_shape=out_shape,
        input_output_aliases=aliases,
        scratch_shapes=[pltpu.VMEM((L, vd), F32),
                        pltpu.VMEM((2, L // RET_BLOCK, vd, vd), F32),
                        pltpu.VMEM((2, L // RET_BLOCK, vd, vd), BF16),
                        pltpu.VMEM((RET_HEADS, RET_BLOCK, RET_BLOCK), F32),
                        pltpu.VMEM((7, RET_BLOCK, vd), F32),
                        pltpu.VMEM((RET_HEADS, RET_BLOCK, vd), BF16),
                        pltpu.VMEM((vd, vd), BF16)],
        compiler_params=_cparams(("arbitrary",), 48),
        name="retention",
    )(*ins)
    return (res[0], res[1]) if emit_state else (res[0], None)


def _conf_kernel(u_ref, w_ref, b_ref, g_ref, be_ref, y_ref, zp_ref, sh_ref, *, L, seqs):
    Wc = CONF_WIDTH
    halo = CONF_HALO
    zp_ref[0:halo, :] = jnp.zeros((halo, Wc), F32)
    zp_ref[halo + L:2 * halo + L, :] = jnp.zeros((halo, Wc), F32)
    first = halo - CONF_KERNEL // 2
    span = sh_ref.shape[1]
    R = CONF_ROWS
    for sq in range(seqs):
        r0 = sq * L
        zp_ref[halo:halo + L, :] = u_ref[r0:r0 + L, 0:Wc] * _sigmoid(u_ref[r0:r0 + L, Wc:2 * Wc])
        for s in range(SUBLANES):
            sh_ref[s] = zp_ref[first + s:first + s + span, :]
        for c in range(L // R):
            acc = jnp.broadcast_to(b_ref[...], (R, Wc))
            for k in range(CONF_KERNEL):
                a, s = divmod(k, SUBLANES)
                acc = acc + w_ref[k:k + 1, :] * sh_ref[s, c * R + SUBLANES * a:c * R + SUBLANES * a + R, :]
            mu = jnp.mean(acc, axis=-1, keepdims=True)
            xc = acc - mu
            z = xc * lax.rsqrt(jnp.mean(xc * xc, axis=-1, keepdims=True) + EPS) * g_ref[...] + be_ref[...]
            y_ref[r0 + c * R:r0 + (c + 1) * R, :] = _silu(z)


def _conformer(u_conf, lw, B, L):
    ws = (lw["conf_dw_w"], lw["conf_dw_b"], lw["conf_ln_g"], lw["conf_ln_b"])
    seqs = _seqs_per_step(B, L)
    return pl.pallas_call(
        functools.partial(_conf_kernel, L=L, seqs=seqs),
        grid=(B // seqs,),
        in_specs=[pl.BlockSpec((seqs * L, CONF_COLS), lambda b: (b, 0))]
        + [_per_layer(w, lw["layer"]) for w in ws],
        out_specs=pl.BlockSpec((seqs * L, CONF_WIDTH), lambda b: (b, 0)),
        out_shape=jax.ShapeDtypeStruct((B * L, CONF_WIDTH), F32),
        scratch_shapes=[pltpu.VMEM((L + 2 * CONF_HALO, CONF_WIDTH), F32),
                        pltpu.VMEM((SUBLANES, L + SUBLANES * ((CONF_KERNEL - 1) // SUBLANES), CONF_WIDTH), F32)],
        compiler_params=_cparams(("arbitrary",), 48),
        name="conformer",
    )(u_conf, *ws)


def _merge_kernel(x_ref, mod_ref, g1_ref, yhy_ref, ymla_ref, yret_ref, yconf_ref,
                  gw_ref, gb_ref, why_ref, wmla_ref, wret_ref, wconf_ref, wo_ref, o_ref, *, seqs_per_tile):
    x = x_ref[...]
    h = _rms(x, g1_ref[...]) * (1.0 + mod_ref[1:2, :]) + mod_ref[0:1, :]
    hb = h.astype(BF16)
    if seqs_per_tile == 1:
        yhy = yhy_ref[...]
    else:
        yhy = jnp.concatenate(
            [yhy_ref[:, s * HY_WIDTH:(s + 1) * HY_WIDTH] for s in range(seqs_per_tile)], axis=0)
    branches = ((yhy, why_ref), (ymla_ref[...], wmla_ref), (yret_ref[...], wret_ref), (yconf_ref[...], wconf_ref))
    D = D_MODEL
    merged = None
    for i, (y, w_ref) in enumerate(branches):
        gate = _sigmoid(_dot(hb, gw_ref[:, i * D:(i + 1) * D]) + gb_ref[:, i * D:(i + 1) * D])
        term = gate * _dot(y.astype(BF16), w_ref[...])
        merged = term if merged is None else merged + term
    o_ref[...] = x + mod_ref[2:3, :] * _dot(merged.astype(BF16), wo_ref[...])


def _merge(x, mod, lw, y_hy, y_mla, y_ret, y_conf, B, L, latent):
    T = B * L
    TM = WIDE_TOKEN_TILE
    if L >= TM:
        tiles_per_seq, seqs_per_tile = L // TM, 1
        hy_spec = pl.BlockSpec((TM, HY_WIDTH), lambda i: (i % tiles_per_seq, i // tiles_per_seq))
    else:
        tiles_per_seq, seqs_per_tile = 1, TM // L
        hy_spec = pl.BlockSpec((L, seqs_per_tile * HY_WIDTH), lambda i: (0, i))
    once = lambda a: _per_layer(a, lw["layer"], single_buffer=True)
    row = lambda w: pl.BlockSpec((TM, w), lambda i: (i, 0))
    ws = (lw["gate_w"], lw["gate_b"], lw["hy_out"], lw["mla_out"], lw["ret_out"], lw["conf_out"], lw["w_o"])
    return pl.pallas_call(
        functools.partial(_merge_kernel, seqs_per_tile=seqs_per_tile),
        grid=(T // TM,),
        in_specs=[row(D_MODEL), _mod_spec(lw["layer"], tiles_per_seq, latent),
                  once(lw["norm1_g"]), hy_spec, row(MLA_HEADS * MLA_V), row(RET_HEADS * RET_DV),
                  row(CONF_WIDTH)] + [once(w) for w in ws],
        out_specs=row(D_MODEL),
        out_shape=jax.ShapeDtypeStruct((T, D_MODEL), F32),
        compiler_params=_cparams(("arbitrary",), 56),
        name="merge",
    )(x, mod, lw["norm1_g"], y_hy, y_mla, y_ret, y_conf, *ws)


def _ffn_kernel(x_ref, mod_ref, g2_ref, w1_ref, w2_ref, fg_ref, o_ref, *, final):
    x = x_ref[...]
    h2 = (_rms(x, g2_ref[...]) * (1.0 + mod_ref[4:5, :]) + mod_ref[3:4, :]).astype(BF16)
    acc = None
    for c0 in range(0, D_FF, FFN_CHUNK):
        c1 = min(c0 + FFN_CHUNK, D_FF)
        a = _dot(h2, w1_ref[:, c0:c1])
        b = _dot(h2, w1_ref[:, D_FF + c0:D_FF + c1])
        part = _dot((_silu(a) * b).astype(BF16), w2_ref[c0:c1, :])
        acc = part if acc is None else acc + part
    out = x + mod_ref[5:6, :] * acc
    if final:
        out = _rms(out, fg_ref[...])
    o_ref[...] = out


def _ffn(x, mod, lw, final_g, B, L, latent, final):
    T = B * L
    TM = WIDE_TOKEN_TILE
    tiles_per_seq = max(L // TM, 1)
    once = lambda a: _per_layer(a, lw["layer"], single_buffer=True)
    row = pl.BlockSpec((TM, D_MODEL), lambda i: (i, 0))
    ws = (lw["norm2_g"], lw["ffn_w1"], lw["ffn_w2"])
    return pl.pallas_call(
        functools.partial(_ffn_kernel, final=final),
        grid=(T // TM,),
        in_specs=[row, _mod_spec(lw["layer"], tiles_per_seq, latent)]
        + [once(w) for w in ws] + [pl.BlockSpec(final_g.shape, lambda i: (0, 0))],
        out_specs=row,
        out_shape=jax.ShapeDtypeStruct((T, D_MODEL), F32),
        compiler_params=_cparams(("arbitrary",), 56),
        name="ffn",
    )(x, mod, *ws, final_g)


def _trunk_layer(x, mod, lw, final_g, tables, B, L, ctx, state, final, caches_out=None, states_out=None):
    latent = ctx is not None
    hv, hx1, hx2, qh, kh, vh, u_ret, u_conf, kt_ret, *cache = _inproj(x, mod, lw, B, L, latent, caches_out)
    ckv, kr = cache if cache else (None, None)
    spectra = _filter_spectra(lw, L, tables)
    y_hy = _hyena(hv, hx1, hx2, lw, spectra, tables, B, L)
    y_mla = _mla(qh, kh, vh, lw, B, L, ctx)
    y_ret, S = _retention(u_ret, kt_ret, lw, B, L, state, emit_state=not latent, states_out=states_out)
    y_conf = _conformer(u_conf, lw, B, L)
    x = _merge(x, mod, lw, y_hy, y_mla, y_ret, y_conf, B, L, latent)
    x = _ffn(x, mod, lw, final_g, B, L, latent, final)
    return x, ckv, kr, S


def _regroup_kernel(wt_ref, o_ref):
    d = wt_ref.shape[1]
    rope_key = HY_COLS + MLA_COLS - MLA_ROPE

    def move(dst, src, n):
        for c in range(0, n, REGROUP_COLS):
            m = min(REGROUP_COLS, n - c)
            o_ref[:, dst + c:dst + c + m] = wt_ref[src + c:src + c + m, :].T.astype(BF16)

    move(0, 0, rope_key)
    slab = pltpu.roll(wt_ref[rope_key:rope_key + HEAD_LANES, :].T, MLA_NOPE, 1)
    lane = lax.broadcasted_iota(jnp.int32, (d, HEAD_LANES), 1)
    rope_lanes = (lane >= MLA_NOPE) & (lane < MLA_NOPE + MLA_ROPE)
    o_ref[:, rope_key:rope_key + HEAD_LANES] = jnp.where(rope_lanes, slab, 0.0).astype(BF16)
    move(rope_key + HEAD_LANES, rope_key + MLA_ROPE, wt_ref.shape[0] - rope_key - MLA_ROPE)


def _regroup_w_in(w_in):
    depth, d, cols = w_in.shape
    out_cols = cols - MLA_ROPE + HEAD_LANES
    return pl.pallas_call(
        _regroup_kernel,
        grid=(depth,),
        in_specs=[pl.BlockSpec((None, cols, d), lambda l: (l, 0, 0))],
        out_specs=pl.BlockSpec((None, d, out_cols), lambda l: (l, 0, 0)),
        out_shape=jax.ShapeDtypeStruct((depth, d, out_cols), BF16),
        compiler_params=_cparams(("arbitrary",), 48),
        name="regroup_w_in",
    )(jnp.swapaxes(w_in, 1, 2))


def _stacked_weights(w_in, p):
    depth = w_in.shape[0]
    w_in_all = _regroup_w_in(w_in)
    dq = MLA_NOPE + MLA_ROPE
    head_pad = lambda a: jnp.pad(a, ((0, 0),) * 3 + ((0, HEAD_LANES - a.shape[3]),)).reshape(
        depth, a.shape[1], MLA_HEADS * HEAD_LANES)
    w_uq = head_pad(p["mla_w_uq"].reshape(depth, MLA_Q_LORA, MLA_HEADS, dq))
    w_ukv = p["mla_w_ukv"].reshape(depth, MLA_KV_LORA, MLA_HEADS, MLA_NOPE + MLA_V)
    w_ukv = jnp.concatenate([head_pad(w_ukv[..., :MLA_NOPE]), head_pad(w_ukv[..., MLA_NOPE:])], axis=2)
    row = lambda name: p[name].reshape(depth, 1, -1)
    rows = ("norm1_g", "norm2_g", "mla_q_norm", "mla_kv_norm", "hy_conv_b", "hy_b1", "hy_b2",
            "conf_dw_b", "conf_ln_g", "conf_ln_b", "gate_b")
    as_is = ("hy_conv_w", "hy_w2", "hy_w3", "hy_bias", "ret_decay", "conf_dw_w")
    bf16 = ("gate_w", "hy_out", "mla_out", "ret_out", "conf_out", "w_o", "ffn_w1", "ffn_w2")
    return {
        "w_in": w_in_all,
        "mla_w_uq": w_uq.astype(BF16), "mla_w_ukv": w_ukv.astype(BF16),
        "hy_w1": jnp.pad(p["hy_w1"], ((0, 0), (0, LANES - HY_EMB), (0, 0))),
        **{name: row(name) for name in rows},
        **{name: p[name] for name in as_is},
        **{name: p[name].astype(BF16) for name in bf16},
    }


def kernel(x_prompt, x_sample, cache_mla_ckv, cache_mla_krope, state_ret, c, c_ctx, ada_w, ada_b, norm1_g, w_in, hy_conv_w, hy_conv_b, hy_w1, hy_b1, hy_w2, hy_b2, hy_w3, hy_bias, hy_out, mla_q_norm, mla_w_uq, mla_kv_norm, mla_w_ukv, mla_out, ret_decay, ret_out, conf_dw_w, conf_dw_b, conf_ln_g, conf_ln_b, conf_out, gate_w, gate_b, w_o, norm2_g, ffn_w1, ffn_w2, final_norm_g):
    p = dict(norm1_g=norm1_g, hy_conv_w=hy_conv_w, hy_conv_b=hy_conv_b, hy_w1=hy_w1, hy_b1=hy_b1,
             hy_w2=hy_w2, hy_b2=hy_b2, hy_w3=hy_w3, hy_bias=hy_bias, hy_out=hy_out,
             mla_q_norm=mla_q_norm, mla_w_uq=mla_w_uq, mla_kv_norm=mla_kv_norm, mla_w_ukv=mla_w_ukv,
             mla_out=mla_out, ret_decay=ret_decay, ret_out=ret_out, conf_dw_w=conf_dw_w,
             conf_dw_b=conf_dw_b, conf_ln_g=conf_ln_g, conf_ln_b=conf_ln_b, conf_out=conf_out,
             gate_w=gate_w, gate_b=gate_b, w_o=w_o, norm2_g=norm2_g, ffn_w1=ffn_w1, ffn_w2=ffn_w2)
    Bp, Lp, D = x_prompt.shape
    Bs, Ls, _ = x_sample.shape
    depth = w_in.shape[0]

    cond = jnp.concatenate([c_ctx[None, :], c, jnp.zeros((MOD_ROWS - 1 - Bs, D), F32)], axis=0)
    mod = _modulation(cond, ada_w, ada_b)
    tables_p = _dft_tables(Lp)
    tables_s = _dft_tables(Ls)
    final_g = final_norm_g.reshape(1, D)

    xp = x_prompt.reshape(Bp * Lp, D)
    xs = x_sample.reshape(Bs * Ls, D)
    caches, states = None, None
    weights = _stacked_weights(w_in, p)
    for l in range(depth):
        lw = dict(weights, layer=l)
        final = l == depth - 1
        xp, ckv, kr, states = _trunk_layer(xp, mod, lw, final_g, tables_p, Bp, Lp, None, None, final,
                                           caches, states)
        caches = (ckv, kr)
        xs, _, _, _ = _trunk_layer(xs, mod, lw, final_g, tables_s, Bs, Ls,
                                   (cache_mla_ckv, cache_mla_krope), state_ret, final)
    new_ckv, new_krope_t = caches
    return (xp.reshape(Bp, Lp, D), xs.reshape(Bs, Ls, D), new_ckv, jnp.swapaxes(new_krope_t, 2, 3), states)
```

```python
import functools
import math
from typing import NamedTuple

import jax
import jax.numpy as jnp
from jax import lax
from jax.experimental import pallas as pl
from jax.experimental.pallas import tpu as pltpu

F32 = jnp.float32
BF16 = jnp.bfloat16

D_MODEL = 1024
DEPTH = 2
PAST_LEN = 256
EPS = 1e-6
GRID_W = 64

HY_WIDTH = 256
HY_EMB = 33
HY_BANDS = (HY_EMB - 1) // 2
HY_FFN = 64
HY_FAST_DECAY = 0.3
HY_SLOW_DECAY = 1.5
HY_TARGET = 1e-2

MLA_HEADS = 4
MLA_Q_LORA = 256
MLA_KV_LORA = 128
MLA_NOPE = 64
MLA_ROPE = 32
MLA_V = 64
ROPE_BASE = 10000.0

RET_HEADS = 4
RET_DK = 64
RET_DV = 64
RET_BLOCK = 256

CONF_WIDTH = 256
CONF_KERNEL = 31

D_FF = ((8 * D_MODEL // 3 + 255) // 256) * 256

HY_COLS = 3 * HY_WIDTH
MLA_COLS = MLA_Q_LORA + MLA_KV_LORA + MLA_ROPE
RET_COLS = 2 * RET_HEADS * RET_DK + 2 * RET_HEADS * RET_DV
CONF_COLS = 2 * CONF_WIDTH

VMEM_BYTES_V7X = 64 * 1024 * 1024
SUBLANES = 8
LANES = 128
TOKEN_TILE = 512
WIDE_TOKEN_TILE = 1024
Q_TILE = 512
MLA_SOFTMAX_ROWS = 16
DFT_TILE = 512
SHORT_SEQ_ROWS = 1024
HYENA_GROUP_ROWS = 1024
HYENA_ROWS = 1024
CONF_ROWS = 128
REGROUP_COLS = 128
CONF_HALO = 16
MXU_DIM_V7X = 256
FFN_CHUNK = 4 * MXU_DIM_V7X
MOD_ROWS = 8


def _cparams(sem, vmem_mb):
    vmem_bytes = vmem_mb * 1024 * 1024
    assert vmem_bytes < VMEM_BYTES_V7X
    return pltpu.CompilerParams(dimension_semantics=sem, vmem_limit_bytes=vmem_bytes)


def _dot(a, b):
    return jnp.dot(a, b, preferred_element_type=F32)


def _dot_nt(a, b):
    return lax.dot_general(a, b, (((1,), (1,)), ((), ())), preferred_element_type=F32)


def _dot_exact(a, b):
    return jnp.dot(a, b, preferred_element_type=F32, precision=lax.Precision.HIGHEST)


def _dot_split(a, b):
    a_hi, b_hi = a.astype(BF16), b.astype(BF16)
    a_lo = (a - a_hi.astype(F32)).astype(BF16)
    b_lo = (b - b_hi.astype(F32)).astype(BF16)
    return _dot(a_hi, b_hi) + (_dot(a_lo, b_hi) + _dot(a_hi, b_lo))


def _rms(x, g):
    return x * lax.rsqrt(jnp.mean(x * x, axis=-1, keepdims=True) + EPS) * g


def _sigmoid(x):
    return 0.5 * jnp.tanh(0.5 * x) + 0.5


def _silu(x):
    return x * _sigmoid(x)


def _mod_kernel(c_ref, w_ref, b_ref, o_ref):
    s = _silu(c_ref[...]).astype(BF16)
    o_ref[...] = _dot(s, w_ref[...].astype(BF16)) + b_ref[...]


def _modulation(cond, ada_w, ada_b):
    depth, d, cols = ada_w.shape
    blk = 1024
    out = pl.pallas_call(
        _mod_kernel,
        grid=(depth, cols // blk),
        in_specs=[
            pl.BlockSpec((MOD_ROWS, d), lambda l, j: (0, 0)),
            pl.BlockSpec((None, d, blk), lambda l, j: (l, 0, j)),
            pl.BlockSpec((None, 1, blk), lambda l, j: (l, 0, j)),
        ],
        out_specs=pl.BlockSpec((None, MOD_ROWS, blk), lambda l, j: (l, 0, j)),
        out_shape=jax.ShapeDtypeStruct((depth, MOD_ROWS, cols), F32),
        compiler_params=_cparams(("arbitrary", "arbitrary"), 32),
        name="modulation",
    )(cond, ada_w, ada_b.reshape(depth, 1, cols))
    return out.reshape(depth, MOD_ROWS, 6, d)


def _seqs_per_step(B, L):
    return max(1, min(B, SHORT_SEQ_ROWS // L))


class _VecField(NamedTuple):
    packed: jax.Array
    offset: int
    width: int


def _pack_vectors(p, fields):
    depth = p[fields[0][0]].shape[0]
    parts, offsets, total = [], {}, 0
    for name, width, block in fields:
        assert total % block == 0
        offsets[name] = total
        parts.append(jnp.pad(p[name].reshape(depth, width), ((0, 0), (0, block - width))))
        total += block
    packed = jnp.concatenate(parts, axis=1).reshape(depth, 1, total)
    return {name: _VecField(packed, offsets[name], block) for name, _, block in fields}


def _operand(a):
    return a.packed if isinstance(a, _VecField) else a


def _per_layer(a, layer, single_buffer=False):
    mode = dict(pipeline_mode=pl.Buffered(1)) if single_buffer else {}
    if isinstance(a, _VecField):
        return pl.BlockSpec((None, 1, a.width), lambda *_: (layer, 0, a.offset // a.width), **mode)
    return pl.BlockSpec((None,) + a.shape[1:], lambda *_: (layer,) + (0,) * (a.ndim - 1), **mode)


def _mod_spec(layer, tiles_per_seq, latent):
    group = (lambda i: 1 + i // tiles_per_seq) if latent else (lambda i: 0)
    return pl.BlockSpec((None, None, 6, D_MODEL), lambda i: (layer, group(i), 0, 0))


HEAD_LANES = LANES


def _rope_lanes(x, cos, sa, sb):
    reps = x.shape[1] // HEAD_LANES
    wide = lambda t: jnp.concatenate([t] * reps, axis=1) if reps > 1 else t
    half = MLA_ROPE // 2
    return (x * wide(cos) + pltpu.roll(x, x.shape[1] - half, 1) * wide(sa)
            + pltpu.roll(x, half, 1) * wide(sb))


def _inproj_kernel(*refs, seqs_per_tile, seq_len, latent, cache_layers, aliased):
    x_ref, mod_ref, g1_ref, win_ref, qn_ref, kvn_ref, wuq_ref, wukv_ref = refs[:8]
    refs = refs[8:]
    c_cq = HY_COLS
    c_ck = c_cq + MLA_Q_LORA
    c_ret = c_ck + MLA_KV_LORA + HEAD_LANES
    c_conf = c_ret + RET_COLS
    why_ref, wcq_ref, wck_ref, wret_ref, wconf_ref = (
        win_ref.at[:, a:b] for a, b in ((0, c_cq), (c_cq, c_ck), (c_ck, c_ret), (c_ret, c_conf),
                                        (c_conf, c_conf + CONF_COLS)))
    if latent:
        cos_ref, sa_ref, sb_ref = refs[:3]
        refs = refs[3:]
    if aliased:
        refs = refs[2:]
    hv_ref, hx1_ref, hx2_ref, q_ref, kh_ref, vh_ref, ret_ref, conf_ref, rkt_ref = refs[:9]
    if not latent:
        ckv_ref, kr_ref = refs[9:]
    x = x_ref[...]
    h = _rms(x, g1_ref[...]) * (1.0 + mod_ref[1:2, :]) + mod_ref[0:1, :]
    hb = h.astype(BF16)

    u = _dot(hb, why_ref[...])
    for p, o_ref in enumerate((hv_ref, hx1_ref, hx2_ref)):
        part = u[:, p * HY_WIDTH:(p + 1) * HY_WIDTH]
        if seqs_per_tile == 1:
            o_ref[...] = part
        else:
            for s in range(seqs_per_tile):
                o_ref[:, s * HY_WIDTH:(s + 1) * HY_WIDTH] = part[s * seq_len:(s + 1) * seq_len]

    heads_w = MLA_HEADS * HEAD_LANES
    cq = _dot(hb, wcq_ref[...])
    q = _dot(_rms(cq, qn_ref[...]).astype(BF16), wuq_ref[...])
    ck = _dot(hb, wck_ref[...])
    ckv = _rms(ck[:, :MLA_KV_LORA], kvn_ref[...])
    kr_block = ck[:, MLA_KV_LORA:MLA_KV_LORA + HEAD_LANES]
    if latent:
        q = _rope_lanes(q, cos_ref[...], sa_ref[...], sb_ref[...])
        kr_block = _rope_lanes(kr_block, cos_ref[...], sa_ref[...], sb_ref[...])
    else:
        kr_t = kr_block.T[MLA_NOPE:MLA_NOPE + MLA_ROPE, :]
        for s in range(seqs_per_tile):
            rows = slice(s * seq_len, (s + 1) * seq_len)
            for out_ref, val in ((ckv_ref, ckv[rows]), (kr_ref, kr_t[:, rows])):
                mine = out_ref.at[s, 0] if cache_layers else out_ref.at[s]
                mine[...] = val
                for other in range(1, cache_layers):
                    out_ref[s, other] = jnp.zeros(out_ref.shape[2:], F32)
    q_ref[...] = (q * ((MLA_NOPE + MLA_ROPE) ** -0.5 * math.log2(math.e))).astype(BF16)
    kv = _dot(ckv.astype(BF16), wukv_ref[...])
    kh_ref[...] = (kv[:, :heads_w] + jnp.concatenate([kr_block] * MLA_HEADS, axis=1)).astype(BF16)
    lane = lax.broadcasted_iota(jnp.int32, (1, heads_w), 1)
    ones_lane = jnp.where(lane % HEAD_LANES == MLA_V, 1.0, 0.0)
    vh_ref[...] = (kv[:, heads_w:] + ones_lane).astype(BF16)
    ret = _dot(hb, wret_ref[...])
    ret_ref[...] = ret
    qk = RET_HEADS * RET_DK
    rkt_ref[...] = ret[:, qk:2 * qk].T
    conf_ref[...] = _dot(hb, wconf_ref[...])


def _inproj(x, mod, lw, B, L, latent, caches_out=None):
    T = B * L
    TM = TOKEN_TILE
    nt = T // TM
    if L >= TM:
        tiles_per_seq, seqs_per_tile = L // TM, 1
        hy_block = (TM, HY_WIDTH)
        hy_map = lambda i: (i % tiles_per_seq, i // tiles_per_seq)
    else:
        tiles_per_seq, seqs_per_tile = 1, TM // L
        hy_block = (L, seqs_per_tile * HY_WIDTH)
        hy_map = lambda i: (0, i)
    row = lambda w: pl.BlockSpec((TM, w), lambda i: (i, 0))
    weights = (lw["norm1_g"], lw["w_in"], lw["mla_q_norm"], lw["mla_kv_norm"], lw["mla_w_uq"], lw["mla_w_ukv"])
    hy_shape = jax.ShapeDtypeStruct((L, B * HY_WIDTH), F32)
    qk = RET_HEADS * RET_DK
    heads_w = MLA_HEADS * HEAD_LANES
    ins = [x, mod, *map(_operand, weights)]
    in_specs = ([row(D_MODEL), _mod_spec(lw["layer"], tiles_per_seq, latent)]
                + [_per_layer(w, lw["layer"]) for w in weights])
    out_specs = ([pl.BlockSpec(hy_block, hy_map)] * 3
                 + [row(heads_w)] * 3 + [row(RET_COLS), row(CONF_COLS), pl.BlockSpec((qk, TM), lambda i: (0, i))])
    out_shape = ([hy_shape] * 3 + [jax.ShapeDtypeStruct((T, heads_w), BF16)] * 3
                 + [jax.ShapeDtypeStruct((T, RET_COLS), F32), jax.ShapeDtypeStruct((T, CONF_COLS), F32),
                    jax.ShapeDtypeStruct((qk, T), F32)])
    if latent:
        ins += list(_rope_lane_tables(L))
        in_specs += [pl.BlockSpec((TM, HEAD_LANES), lambda i: (i % tiles_per_seq, 0))] * 3
    aliases = {}
    cache_layers = 0
    if not latent:
        assert tiles_per_seq == 1
        layer, depth = lw["layer"], lw["w_in"].shape[0]
        dims = ((L, MLA_KV_LORA), (MLA_ROPE, L))
        if caches_out is None:
            cache_layers = depth
            out_specs += [pl.BlockSpec((seqs_per_tile, depth) + d, lambda i: (i, 0, 0, 0)) for d in dims]
        else:
            aliases = {len(ins) + k: len(out_shape) + k for k in range(len(dims))}
            ins += list(caches_out)
            in_specs += [pl.BlockSpec(memory_space=pl.ANY)] * len(dims)
            out_specs += [pl.BlockSpec((seqs_per_tile, None) + d, lambda i: (i, layer, 0, 0)) for d in dims]
        out_shape += [jax.ShapeDtypeStruct((B, depth) + d, F32) for d in dims]
    return pl.pallas_call(
        functools.partial(_inproj_kernel, seqs_per_tile=seqs_per_tile, seq_len=L, latent=latent,
                          cache_layers=cache_layers, aliased=bool(aliases)),
        grid=(nt,),
        in_specs=in_specs,
        out_specs=out_specs,
        out_shape=out_shape,
        input_output_aliases=aliases,
        compiler_params=_cparams(("arbitrary",), 48),
        name="inproj",
    )(*ins)


def _dft_tables(L):
    N = 2 * L
    k_lo = min(L, 32)
    k_hi = L // k_lo
    t = jnp.arange(L, dtype=jnp.int32)[None, :]

    def cs(k):
        m = (k[:, None] * t) % N
        ang = m.astype(F32) * (2.0 * math.pi / N)
        return jnp.cos(ang), jnp.sin(ang)

    ca, sa = cs(jnp.arange(k_hi, dtype=jnp.int32) * k_lo)
    cb, sb = cs(jnp.arange(k_lo, dtype=jnp.int32))
    cos = (ca[:, None, :] * cb[None, :, :] - sa[:, None, :] * sb[None, :, :]).reshape(L, L)
    sin = (sa[:, None, :] * cb[None, :, :] + ca[:, None, :] * sb[None, :, :]).reshape(L, L)
    return cos.astype(BF16), (-sin).astype(BF16)


def _filter_features(L):
    t = jnp.linspace(0.0, 1.0, L, dtype=F32)[:, None]
    w = 2.0 * math.pi * jnp.arange(L, dtype=F32)[:, None] / L
    f = jnp.linspace(1e-4, HY_BANDS - 1, HY_BANDS, dtype=F32)[None, :]
    z = jnp.concatenate([t, jnp.cos(f * w), -jnp.sin(f * w)], axis=-1)
    z = jnp.pad(z, ((0, 0), (0, LANES - HY_EMB)))
    max_decay = math.log(HY_TARGET) / HY_FAST_DECAY
    min_decay = math.log(HY_TARGET) / HY_SLOW_DECAY
    deltas = jnp.abs(jnp.linspace(min_decay, max_decay, HY_WIDTH, dtype=F32))
    decay = jnp.exp(-t * deltas[None, :])
    return z, decay


def _alternating(rows):
    t = lax.broadcasted_iota(jnp.int32, (rows, 1), 0)
    return (1 - 2 * (t & 1)).astype(F32)


def _filter_kernel(z_ref, dec_ref, w1_ref, b1_ref, w2_ref, b2_ref, w3_ref, cos_ref, sin_ref,
                   kr_ref, ki_ref, kn_ref, hsum_ref, hdiff_ref, *, L):
    j = pl.program_id(0)
    W = HY_WIDTH
    N = 2 * L

    @pl.when(j == 0)
    def _():
        h = jnp.sin(_dot_exact(z_ref[...], w1_ref[...]) + b1_ref[:, :HY_FFN])
        h = jnp.sin(_dot_exact(h, w2_ref[...]) + b2_ref[:, :HY_FFN])
        h = _dot_split(h, w3_ref[...]) * jnp.concatenate([dec_ref[...]] * 4, axis=1)
        cs = jnp.sum(jnp.abs(h), axis=0, keepdims=True)
        s0 = cs[:, 0:W] + cs[:, W:2 * W]
        s1 = cs[:, 2 * W:3 * W] + cs[:, 3 * W:4 * W]
        h = h / jnp.concatenate([s0, s0, s1, s1], axis=1)
        row = lax.broadcasted_iota(jnp.int32, h.shape, 0)
        col = lax.broadcasted_iota(jnp.int32, h.shape, 1)
        backward = (col // W) % 2 == 1
        h = jnp.where(backward & (row == 0), 0.0, h)
        fwd = jnp.concatenate([h[:, 0:W], h[:, 2 * W:3 * W]], axis=1)
        bwd = jnp.concatenate([h[:, W:2 * W], h[:, 3 * W:4 * W]], axis=1)
        hsum_ref[...] = (fwd + bwd).astype(BF16)
        hdiff_ref[...] = (fwd - bwd).astype(BF16)
        nyq = jnp.sum((fwd + bwd) * _alternating(L), axis=0, keepdims=True) * (1.0 / N)
        for o in range(2):
            kn_ref[o] = nyq[:, o * W:(o + 1) * W]

    sr = _dot(cos_ref[...], hsum_ref[...])
    si = _dot(sin_ref[...], hdiff_ref[...])
    row = lax.broadcasted_iota(jnp.int32, (sr.shape[0], W), 0)
    scale = jnp.where((row == 0) & (j == 0), 1.0 / N, 2.0 / N)
    for o in range(2):
        kr_ref[o] = sr[:, o * W:(o + 1) * W] * scale
        ki_ref[o] = si[:, o * W:(o + 1) * W] * scale


def _filter_spectra(lw, L, tables):
    z, decay = _filter_features(L)
    cos, msin = tables
    Tk = min(DFT_TILE, L)
    full = lambda a: pl.BlockSpec(a.shape, lambda j: (0,) * a.ndim)
    mlp = (lw["hy_w1"], lw["hy_b1"], lw["hy_w2"], lw["hy_b2"], lw["hy_w3"])
    ins = (z, decay, *map(_operand, mlp))
    tile = pl.BlockSpec((Tk, L), lambda j: (j, 0))
    spec = pl.BlockSpec((2, Tk, HY_WIDTH), lambda j: (0, j, 0))
    return pl.pallas_call(
        functools.partial(_filter_kernel, L=L),
        grid=(L // Tk,),
        in_specs=[full(z), full(decay)] + [_per_layer(a, lw["layer"]) for a in mlp] + [tile, tile],
        out_specs=[spec, spec, pl.BlockSpec((2, 1, HY_WIDTH), lambda j: (0, 0, 0))],
        out_shape=[jax.ShapeDtypeStruct((2, L, HY_WIDTH), F32)] * 2
        + [jax.ShapeDtypeStruct((2, 1, HY_WIDTH), F32)],
        scratch_shapes=[pltpu.VMEM((L, 2 * HY_WIDTH), BF16)] * 2,
        compiler_params=_cparams(("arbitrary",), 48),
        name="hyena_filter",
    )(*ins, cos, msin)


def _hyena_kernel(v_ref, x1_ref, x2_ref, cw_ref, cb_ref, bias_ref, kr_ref, ki_ref, kn_ref,
                  cos_ref, sin_ref, y_ref, cur_ref, curb_ref, yr_ref, yi_ref, gate_ref,
                  *, L, W, T):
    reps = W // HY_WIDTH
    tiled = lambda a: jnp.concatenate([a] * reps, axis=1) if reps > 1 else a
    alt = _alternating(L)

    def short_conv(u_ref, p):
        u = u_ref[...]
        row = lax.broadcasted_iota(jnp.int32, (L, W), 0)
        prev = jnp.where(row == 0, 0.0, pltpu.roll(u, 1, 0))
        nxt = jnp.where(row == L - 1, 0.0, pltpu.roll(u, L - 1, 0))
        cols = slice(p * HY_WIDTH, (p + 1) * HY_WIDTH)
        w = [tiled(cw_ref[k:k + 1, cols]) for k in range(3)]
        return prev * w[0] + u * w[1] + nxt * w[2] + tiled(cb_ref[:, cols])

    cur_ref[...] = short_conv(v_ref, 0)
    for o, x_ref in enumerate((x1_ref, x2_ref)):
        cur = cur_ref[...]
        curb_ref[...] = cur.astype(BF16)
        gate_ref[...] = short_conv(x_ref, o + 1)
        nyq = jnp.sum(cur * alt, axis=0, keepdims=True) * tiled(kn_ref[o])
        for f in range(L // T):
            rows = slice(f * T, (f + 1) * T)
            cb = curb_ref[...]
            xr = _dot(cos_ref[rows, :], cb)
            xi = _dot(sin_ref[rows, :], cb)
            kr, ki = tiled(kr_ref[o, rows, :]), tiled(ki_ref[o, rows, :])
            yr_ref[rows, :] = (xr * kr - xi * ki).astype(BF16)
            yi_ref[rows, :] = (xr * ki + xi * kr).astype(BF16)
        for t in range(L // T):
            rows = slice(t * T, (t + 1) * T)
            conv = (_dot(cos_ref[rows, :], yr_ref[...]) + _dot(sin_ref[rows, :], yi_ref[...])
                    + alt[rows] * nyq + cur_ref[rows, :] * tiled(bias_ref[o:o + 1, :]))
            out = gate_ref[rows, :] * conv
            if o == 0:
                cur_ref[rows, :] = out
            else:
                y_ref[rows, :] = out


def _hyena(hv, hx1, hx2, lw, spectra, tables, B, L):
    kr, ki, kn = spectra
    cos, msin = tables
    T = min(HYENA_ROWS, L)
    W = HY_WIDTH * max(1, min(B, HYENA_GROUP_ROWS // L))
    ng = (B * HY_WIDTH) // W
    col = pl.BlockSpec((L, W), lambda g: (0, g))
    once = lambda a: pl.BlockSpec(a.shape, lambda g: (0,) * a.ndim, pipeline_mode=pl.Buffered(1))
    params = (lw["hy_conv_w"], lw["hy_conv_b"], lw["hy_bias"])
    consts = (kr, ki, kn, cos, msin)
    return pl.pallas_call(
        functools.partial(_hyena_kernel, L=L, W=W, T=T),
        grid=(ng,),
        in_specs=[col, col, col] + [_per_layer(a, lw["layer"]) for a in params] + [once(a) for a in consts],
        out_specs=col,
        out_shape=jax.ShapeDtypeStruct((L, B * HY_WIDTH), F32),
        scratch_shapes=[pltpu.VMEM((L, W), F32), pltpu.VMEM((L, W), BF16), pltpu.VMEM((L, W), BF16),
                        pltpu.VMEM((L, W), BF16), pltpu.VMEM((L, W), F32)],
        compiler_params=_cparams(("arbitrary",), 60),
        name="hyena_conv",
    )(hv, hx1, hx2, *map(_operand, params), *consts)


def _mla_kernel(*refs, L, latent, seqs):
    if latent:
        q_ref, kh_ref, vh_ref, cckv_ref, ckr_ref, wukv_ref, o_ref, s_ref, p_ref, kctx_ref, vctx_ref = refs
    else:
        q_ref, kh_ref, vh_ref, o_ref, s_ref, p_ref = refs
    TQ = q_ref.shape[0] // seqs
    heads_w = MLA_HEADS * HEAD_LANES
    block = lambda h: slice(h * HEAD_LANES, (h + 1) * HEAD_LANES)

    if latent:
        @pl.when(pl.program_id(1) == 0)
        def _():
            kv = _dot(cckv_ref[...].astype(BF16), wukv_ref[...])
            zeros = lambda w: jnp.zeros((PAST_LEN, w), F32)
            kr_block = jnp.concatenate(
                [zeros(MLA_NOPE), ckr_ref[...], zeros(HEAD_LANES - MLA_NOPE - MLA_ROPE)], axis=1)
            kctx_ref[...] = (kv[:, :heads_w] + jnp.concatenate([kr_block] * MLA_HEADS, axis=1)).astype(BF16)
            lane = lax.broadcasted_iota(jnp.int32, (1, heads_w), 1)
            vctx_ref[...] = (kv[:, heads_w:] + jnp.where(lane % HEAD_LANES == MLA_V, 1.0, 0.0)).astype(BF16)

    def attend(q_rows, k_rows):
        def scores(h):
            qh = q_ref[q_rows, block(h)]
            s_ref[h, :, 0:L] = _dot_nt(qh, kh_ref[k_rows, block(h)])
            if latent:
                s_ref[h, :, L:L + PAST_LEN] = _dot_nt(qh, kctx_ref[:, block(h)])

        outs = []
        scores(0)
        for h in range(MLA_HEADS):
            if h + 1 < MLA_HEADS:
                scores(h + 1)
            for r in range(TQ // MLA_SOFTMAX_ROWS):
                rows = slice(r * MLA_SOFTMAX_ROWS, (r + 1) * MLA_SOFTMAX_ROWS)
                s = s_ref[h, rows, :]
                p_ref[h, rows, :] = jnp.exp2(s - jnp.max(s, axis=-1, keepdims=True)).astype(BF16)
            pv = _dot(p_ref[h, :, 0:L], vh_ref[k_rows, block(h)])
            if latent:
                pv = pv + _dot(p_ref[h, :, L:L + PAST_LEN], vctx_ref[:, block(h)])
            outs.append(pv[:, :MLA_V] / pv[:, MLA_V:MLA_V + 1])
        o_ref[q_rows, :] = jnp.concatenate(outs, axis=-1)

    for sq in range(seqs):
        attend(slice(sq * TQ, (sq + 1) * TQ), slice(sq * L, (sq + 1) * L))


def _rope_tables(L):
    rows = L // GRID_W
    row = jnp.repeat(jnp.arange(rows), GRID_W).astype(F32)
    col = jnp.tile(jnp.arange(GRID_W), rows).astype(F32)
    per_axis = MLA_ROPE // 4
    inv = ROPE_BASE ** (-jnp.arange(per_axis, dtype=F32) / per_axis)
    ang = jnp.concatenate([row[:, None] * inv, col[:, None] * inv], axis=-1)
    return jnp.cos(ang), jnp.sin(ang)


def _rope_lane_tables(L):
    cos, sin = _rope_tables(L)
    pad = HEAD_LANES - MLA_NOPE - MLA_ROPE
    one, zero = jnp.ones((L, MLA_NOPE), F32), jnp.zeros((L, MLA_NOPE), F32)
    half0 = jnp.zeros_like(sin)
    cos_t = jnp.concatenate([one, cos, cos, one[:, :pad]], axis=1)
    sa_t = jnp.concatenate([zero, -sin, half0, zero[:, :pad]], axis=1)
    sb_t = jnp.concatenate([zero, half0, sin, zero[:, :pad]], axis=1)
    return cos_t, sa_t, sb_t


def _mla(qh, kh, vh, lw, B, L, ctx):
    latent = ctx is not None
    TQ = min(Q_TILE, L)
    nq = L // TQ
    seqs = 1 if latent or nq > 1 else _seqs_per_step(B, L)
    Lk = L + PAST_LEN if latent else L
    heads_w = MLA_HEADS * HEAD_LANES
    seq = pl.BlockSpec((seqs * L, heads_w), lambda b, i: (b, 0))
    ins = [qh, kh, vh]
    specs = [pl.BlockSpec((seqs * TQ, heads_w), lambda b, i: (b * nq + i, 0)), seq, seq]
    scratch = [pltpu.VMEM((MLA_HEADS, TQ, Lk), F32), pltpu.VMEM((MLA_HEADS, TQ, Lk), BF16)]
    if latent:
        ins += [ctx[0], ctx[1], lw["mla_w_ukv"]]
        layer = lw["layer"]
        specs += [pl.BlockSpec((None, None, PAST_LEN, MLA_KV_LORA), lambda b, i: (b, layer, 0, 0)),
                  pl.BlockSpec((None, None, PAST_LEN, MLA_ROPE), lambda b, i: (b, layer, 0, 0)),
                  _per_layer(lw["mla_w_ukv"], lw["layer"])]
        scratch += [pltpu.VMEM((PAST_LEN, heads_w), BF16)] * 2
    return pl.pallas_call(
        functools.partial(_mla_kernel, L=L, latent=latent, seqs=seqs),
        grid=(B // seqs, nq),
        in_specs=specs,
        out_specs=pl.BlockSpec((seqs * TQ, MLA_HEADS * MLA_V), lambda b, i: (b * nq + i, 0)),
        out_shape=jax.ShapeDtypeStruct((B * L, MLA_HEADS * MLA_V), F32),
        scratch_shapes=scratch,
        compiler_params=_cparams(("arbitrary", "arbitrary"), 48),
        name="mla_attention",
    )(*ins)


def _ret_kernel(*refs, L, seqs, has_state, emit_state, state_layers, aliased):
    refs = list(refs)
    u_ref, kt_ref, dl_ref = refs[:3]
    pos = 3
    s0_ref = None
    if has_state:
        s0_ref = refs[pos]
        pos += 1
    if aliased:
        pos += 1
    y_ref = refs[pos]
    pos += 1
    sout_ref = None
    if emit_state:
        sout_ref = refs[pos]
        pos += 1
    of_ref, kv_ref, sin_ref, dcomb_ref, dtab_ref, hmask_ref, avg_ref = refs[pos:]

    C = RET_BLOCK
    n = L // C
    H, DK, DV = RET_HEADS, RET_DK, RET_DV
    qk = H * DK
    DK_F, DK_B, DQ_F, DQ_B, DC_F, DC_B, MASK = range(7)
    use_cross = has_state or n > 1

    @pl.when(pl.program_id(0) == 0)
    def _():
        x = dl_ref[...]
        log_g = jnp.minimum(x, 0.0) - jnp.log1p(jnp.exp(-jnp.abs(x)))
        gf = [log_g[0:1, h:h + 1] for h in range(H)]
        gb = [log_g[1:2, h:h + 1] for h in range(H)]
        diff = (lax.broadcasted_iota(jnp.int32, (C, C), 0)
                - lax.broadcasted_iota(jnp.int32, (C, C), 1)).astype(F32)
        idx = lax.broadcasted_iota(jnp.int32, (C, 1), 0).astype(F32)
        for h in range(H):
            dcomb_ref[h] = (jnp.where(diff >= 0, jnp.exp(jnp.maximum(diff, 0.0) * gf[h]), 0.0)
                            + jnp.where(diff <= 0, jnp.exp(jnp.maximum(-diff, 0.0) * gb[h]), 0.0))

        def head_cols(lag, g):
            return jnp.concatenate([jnp.broadcast_to(jnp.exp(lag * g[h]), (C, DK)) for h in range(H)], axis=1)

        def head_rows(lag, g):
            return jnp.concatenate(
                [jnp.broadcast_to(jnp.exp(lag * g[h]) * (DK ** -0.5), (DK, C)) for h in range(H)], axis=0)

        tok = lax.broadcasted_iota(jnp.int32, (1, C), 1).astype(F32)
        dtab_ref[DK_F] = head_rows(C - 1.0 - tok, gf)
        dtab_ref[DK_B] = head_rows(tok, gb)
        dtab_ref[DQ_F] = head_cols(idx + 1.0, gf)
        dtab_ref[DQ_B] = head_cols(C - idx, gb)
        row_head = lax.broadcasted_iota(jnp.int32, (qk, qk), 0) // DK
        col_head = lax.broadcasted_iota(jnp.int32, (qk, qk), 1) // DV
        same_head = row_head == col_head
        for slot, g in ((DC_F, gf), (DC_B, gb)):
            dc = jnp.zeros((qk, qk), F32)
            for h in range(H):
                dc = jnp.where(same_head & (row_head == h), jnp.exp(C * g[h]), dc)
            dtab_ref[slot] = dc
        dtab_ref[MASK] = same_head.astype(F32)
        avg_ref[...] = jnp.where(same_head, 1.0 / DV, 0.0).astype(BF16)
        lane_head = lax.broadcasted_iota(jnp.int32, (C, qk), 1) // DK
        for h in range(H):
            hmask_ref[h] = jnp.where(lane_head == h, 1.0, 0.0).astype(BF16)

    def head_mean(a):
        total = None
        for _ in range(3):
            part = a.astype(BF16)
            a = a - part.astype(F32)
            term = _dot(part, avg_ref[...])
            total = term if total is None else total + term
        return total

    for sq in range(seqs):
        chunk = lambda c: slice(sq * L + c * C, sq * L + (c + 1) * C)

        for c in range(n):
            rows = chunk(c)
            qb = u_ref[rows, 0:qk].astype(BF16)
            kb = (u_ref[rows, qk:2 * qk] * (DK ** -0.5)).astype(BF16)
            vb = u_ref[rows, 2 * qk:2 * qk + H * DV].astype(BF16)
            att = [(_dot_nt(qb * hmask_ref[h], kb) * dcomb_ref[h]).astype(BF16) for h in range(H)]
            v_heads = jnp.concatenate([vb * hmask_ref[h] for h in range(H)], axis=0)
            of_ref[c * C:(c + 1) * C, :] = _dot(jnp.concatenate(att, axis=1), v_heads)
            kt = kt_ref[:, rows]
            for d, slot in ((0, DK_F), (1, DK_B)):
                kv_ref[d, c] = _dot((kt * dtab_ref[slot]).astype(BF16), vb)

        finals = []
        for d, slot in ((0, DC_F), (1, DC_B)):
            if has_state:
                zero = jnp.zeros((DK, DV), F32)
                S = jnp.concatenate(
                    [jnp.concatenate([s0_ref[sq, d, h] if g == h else zero for g in range(H)], axis=1)
                     for h in range(H)], axis=0)
            else:
                S = jnp.zeros((qk, H * DV), F32)
            for c in (range(n) if d == 0 else reversed(range(n))):
                if use_cross:
                    sin_ref[d, c] = S.astype(BF16)
                S = S * dtab_ref[slot] + kv_ref[d, c] * dtab_ref[MASK]
            finals.append(S)
        if emit_state:
            mine = sout_ref.at[sq, 0] if state_layers else sout_ref.at[sq]
            for d in range(2):
                for h in range(H):
                    mine[d, h] = finals[d][h * DK:(h + 1) * DK, h * DV:(h + 1) * DV]
            for other in range(1, state_layers):
                sout_ref[sq, other] = jnp.zeros(sout_ref.shape[2:], F32)

        for c in range(n):
            rows = chunk(c)
            tot = of_ref[c * C:(c + 1) * C, :]
            if use_cross:
                qb = u_ref[rows, 0:qk].astype(BF16)
                tot = (tot + _dot(qb, sin_ref[0, c]) * dtab_ref[DQ_F]
                       + _dot(qb, sin_ref[1, c]) * dtab_ref[DQ_B])
            xc = tot - head_mean(tot)
            normed = xc * lax.rsqrt(head_mean(xc * xc) + EPS)
            gate = u_ref[rows, 2 * qk + H * DV:2 * qk + 2 * H * DV]
            y_ref[rows, :] = _silu(gate) * normed


def _retention(u_ret, kt_ret, lw, B, L, state, emit_state, states_out=None):
    has_state = state is not None
    seqs = _seqs_per_step(B, L)
    st_dims = (2, RET_HEADS, RET_DK, RET_DV)
    ins = [u_ret, kt_ret, lw["ret_decay"]]
    specs = [pl.BlockSpec((seqs * L, RET_COLS), lambda b: (b, 0)),
             pl.BlockSpec((RET_HEADS * RET_DK, seqs * L), lambda b: (0, b)),
             _per_layer(lw["ret_decay"], lw["layer"])]
    if has_state:
        layer = lw["layer"]
        ins.append(state)
        specs.append(pl.BlockSpec((seqs, None) + st_dims, lambda b: (b, layer, 0, 0, 0, 0)))
    vd = RET_HEADS * RET_DV
    out_specs = [pl.BlockSpec((seqs * L, vd), lambda b: (b, 0))]
    out_shape = [jax.ShapeDtypeStruct((B * L, vd), F32)]
    aliases = {}
    state_layers = 0
    if emit_state:
        layer, depth = lw["layer"], lw["ret_decay"].shape[0]
        if states_out is None:
            state_layers = depth
            out_specs.append(pl.BlockSpec((seqs, depth) + st_dims, lambda b: (b, 0, 0, 0, 0, 0)))
        else:
            aliases = {len(ins): 1}
            ins.append(states_out)
            specs.append(pl.BlockSpec(memory_space=pl.ANY))
            out_specs.append(pl.BlockSpec((seqs, None) + st_dims, lambda b: (b, layer, 0, 0, 0, 0)))
        out_shape.append(jax.ShapeDtypeStruct((B, depth) + st_dims, F32))
    res = pl.pallas_call(
        functools.partial(_ret_kernel, L=L, seqs=seqs, has_state=has_state, emit_state=emit_state,
                          state_layers=state_layers, aliased=bool(aliases)),
        grid=(B // seqs,),
        in_specs=specs,
        out_specs=out_specs,
        out_shape=out_shape,
        input_output_aliases=aliases,
        scratch_shapes=[pltpu.VMEM((L, vd), F32),
                        pltpu.VMEM((2, L // RET_BLOCK, vd, vd), F32),
                        pltpu.VMEM((2, L // RET_BLOCK, vd, vd), BF16),
                        pltpu.VMEM((RET_HEADS, RET_BLOCK, RET_BLOCK), F32),
                        pltpu.VMEM((7, RET_BLOCK, vd), F32),
                        pltpu.VMEM((RET_HEADS, RET_BLOCK, vd), BF16),
                        pltpu.VMEM((vd, vd), BF16)],
        compiler_params=_cparams(("arbitrary",), 48),
        name="retention",
    )(*ins)
    return (res[0], res[1]) if emit_state else (res[0], None)


def _conf_kernel(u_ref, w_ref, b_ref, g_ref, be_ref, y_ref, zp_ref, sh_ref, *, L, seqs):
    Wc = CONF_WIDTH
    halo = CONF_HALO
    zp_ref[0:halo, :] = jnp.zeros((halo, Wc), F32)
    zp_ref[halo + L:2 * halo + L, :] = jnp.zeros((halo, Wc), F32)
    first = halo - CONF_KERNEL // 2
    span = sh_ref.shape[1]
    R = CONF_ROWS
    for sq in range(seqs):
        r0 = sq * L
        zp_ref[halo:halo + L, :] = u_ref[r0:r0 + L, 0:Wc] * _sigmoid(u_ref[r0:r0 + L, Wc:2 * Wc])
        for s in range(SUBLANES):
            sh_ref[s] = zp_ref[first + s:first + s + span, :]
        for c in range(L // R):
            acc = jnp.broadcast_to(b_ref[...], (R, Wc))
            for k in range(CONF_KERNEL):
                a, s = divmod(k, SUBLANES)
                acc = acc + w_ref[k:k + 1, :] * sh_ref[s, c * R + SUBLANES * a:c * R + SUBLANES * a + R, :]
            mu = jnp.mean(acc, axis=-1, keepdims=True)
            xc = acc - mu
            z = xc * lax.rsqrt(jnp.mean(xc * xc, axis=-1, keepdims=True) + EPS) * g_ref[...] + be_ref[...]
            y_ref[r0 + c * R:r0 + (c + 1) * R, :] = _silu(z)


def _conformer(u_conf, lw, B, L):
    ws = (lw["conf_dw_w"], lw["conf_dw_b"], lw["conf_ln_g"], lw["conf_ln_b"])
    seqs = _seqs_per_step(B, L)
    return pl.pallas_call(
        functools.partial(_conf_kernel, L=L, seqs=seqs),
        grid=(B // seqs,),
        in_specs=[pl.BlockSpec((seqs * L, CONF_COLS), lambda b: (b, 0))]
        + [_per_layer(w, lw["layer"]) for w in ws],
        out_specs=pl.BlockSpec((seqs * L, CONF_WIDTH), lambda b: (b, 0)),
        out_shape=jax.ShapeDtypeStruct((B * L, CONF_WIDTH), F32),
        scratch_shapes=[pltpu.VMEM((L + 2 * CONF_HALO, CONF_WIDTH), F32),
                        pltpu.VMEM((SUBLANES, L + SUBLANES * ((CONF_KERNEL - 1) // SUBLANES), CONF_WIDTH), F32)],
        compiler_params=_cparams(("arbitrary",), 48),
        name="conformer",
    )(u_conf, *map(_operand, ws))


def _merge_kernel(x_ref, mod_ref, g1_ref, yhy_ref, ymla_ref, yret_ref, yconf_ref,
                  gw_ref, gb_ref, why_ref, wmla_ref, wret_ref, wconf_ref, wo_ref, o_ref, *, seqs_per_tile):
    x = x_ref[...]
    h = _rms(x, g1_ref[...]) * (1.0 + mod_ref[1:2, :]) + mod_ref[0:1, :]
    hb = h.astype(BF16)
    if seqs_per_tile == 1:
        yhy = yhy_ref[...]
    else:
        yhy = jnp.concatenate(
            [yhy_ref[:, s * HY_WIDTH:(s + 1) * HY_WIDTH] for s in range(seqs_per_tile)], axis=0)
    branches = ((yhy, why_ref), (ymla_ref[...], wmla_ref), (yret_ref[...], wret_ref), (yconf_ref[...], wconf_ref))
    D = D_MODEL
    merged = None
    for i, (y, w_ref) in enumerate(branches):
        gate = _sigmoid(_dot(hb, gw_ref[:, i * D:(i + 1) * D]) + gb_ref[:, i * D:(i + 1) * D])
        term = gate * _dot(y.astype(BF16), w_ref[...])
        merged = term if merged is None else merged + term
    o_ref[...] = x + mod_ref[2:3, :] * _dot(merged.astype(BF16), wo_ref[...])


def _merge(x, mod, lw, y_hy, y_mla, y_ret, y_conf, B, L, latent):
    T = B * L
    TM = WIDE_TOKEN_TILE
    if L >= TM:
        tiles_per_seq, seqs_per_tile = L // TM, 1
        hy_spec = pl.BlockSpec((TM, HY_WIDTH), lambda i: (i % tiles_per_seq, i // tiles_per_seq))
    else:
        tiles_per_seq, seqs_per_tile = 1, TM // L
        hy_spec = pl.BlockSpec((L, seqs_per_tile * HY_WIDTH), lambda i: (0, i))
    once = lambda a: _per_layer(a, lw["layer"], single_buffer=True)
    row = lambda w: pl.BlockSpec((TM, w), lambda i: (i, 0))
    ws = (lw["gate_w"], lw["gate_b"], lw["hy_out"], lw["mla_out"], lw["ret_out"], lw["conf_out"], lw["w_o"])
    return pl.pallas_call(
        functools.partial(_merge_kernel, seqs_per_tile=seqs_per_tile),
        grid=(T // TM,),
        in_specs=[row(D_MODEL), _mod_spec(lw["layer"], tiles_per_seq, latent),
                  once(lw["norm1_g"]), hy_spec, row(MLA_HEADS * MLA_V), row(RET_HEADS * RET_DV),
                  row(CONF_WIDTH)] + [once(w) for w in ws],
        out_specs=row(D_MODEL),
        out_shape=jax.ShapeDtypeStruct((T, D_MODEL), F32),
        compiler_params=_cparams(("arbitrary",), 56),
        name="merge",
    )(x, mod, _operand(lw["norm1_g"]), y_hy, y_mla, y_ret, y_conf, *map(_operand, ws))


def _ffn_kernel(x_ref, mod_ref, g2_ref, w1_ref, w2_ref, fg_ref, o_ref, *, final):
    x = x_ref[...]
    h2 = (_rms(x, g2_ref[...]) * (1.0 + mod_ref[4:5, :]) + mod_ref[3:4, :]).astype(BF16)
    acc = None
    for c0 in range(0, D_FF, FFN_CHUNK):
        c1 = min(c0 + FFN_CHUNK, D_FF)
        a = _dot(h2, w1_ref[:, c0:c1])
        b = _dot(h2, w1_ref[:, D_FF + c0:D_FF + c1])
        part = _dot((_silu(a) * b).astype(BF16), w2_ref[c0:c1, :])
        acc = part if acc is None else acc + part
    out = x + mod_ref[5:6, :] * acc
    if final:
        out = _rms(out, fg_ref[...])
    o_ref[...] = out


def _ffn(x, mod, lw, final_g, B, L, latent, final):
    T = B * L
    TM = WIDE_TOKEN_TILE
    tiles_per_seq = max(L // TM, 1)
    once = lambda a: _per_layer(a, lw["layer"], single_buffer=True)
    row = pl.BlockSpec((TM, D_MODEL), lambda i: (i, 0))
    ws = (lw["norm2_g"], lw["ffn_w1"], lw["ffn_w2"])
    return pl.pallas_call(
        functools.partial(_ffn_kernel, final=final),
        grid=(T // TM,),
        in_specs=[row, _mod_spec(lw["layer"], tiles_per_seq, latent)]
        + [once(w) for w in ws] + [pl.BlockSpec(final_g.shape, lambda i: (0, 0))],
        out_specs=row,
        out_shape=jax.ShapeDtypeStruct((T, D_MODEL), F32),
        compiler_params=_cparams(("arbitrary",), 56),
        name="ffn",
    )(x, mod, *map(_operand, ws), final_g)


def _trunk_layer(x, mod, lw, final_g, tables, B, L, ctx, state, final, caches_out=None, states_out=None):
    latent = ctx is not None
    hv, hx1, hx2, qh, kh, vh, u_ret, u_conf, kt_ret, *cache = _inproj(x, mod, lw, B, L, latent, caches_out)
    ckv, kr = cache if cache else (None, None)
    spectra = _filter_spectra(lw, L, tables)
    y_hy = _hyena(hv, hx1, hx2, lw, spectra, tables, B, L)
    y_mla = _mla(qh, kh, vh, lw, B, L, ctx)
    y_ret, S = _retention(u_ret, kt_ret, lw, B, L, state, emit_state=not latent, states_out=states_out)
    y_conf = _conformer(u_conf, lw, B, L)
    x = _merge(x, mod, lw, y_hy, y_mla, y_ret, y_conf, B, L, latent)
    x = _ffn(x, mod, lw, final_g, B, L, latent, final)
    return x, ckv, kr, S


def _regroup_kernel(wt_ref, o_ref):
    d = wt_ref.shape[1]
    rope_key = HY_COLS + MLA_COLS - MLA_ROPE

    def move(dst, src, n):
        for c in range(0, n, REGROUP_COLS):
            m = min(REGROUP_COLS, n - c)
            o_ref[:, dst + c:dst + c + m] = wt_ref[src + c:src + c + m, :].T.astype(BF16)

    move(0, 0, rope_key)
    slab = pltpu.roll(wt_ref[rope_key:rope_key + HEAD_LANES, :].T, MLA_NOPE, 1)
    lane = lax.broadcasted_iota(jnp.int32, (d, HEAD_LANES), 1)
    rope_lanes = (lane >= MLA_NOPE) & (lane < MLA_NOPE + MLA_ROPE)
    o_ref[:, rope_key:rope_key + HEAD_LANES] = jnp.where(rope_lanes, slab, 0.0).astype(BF16)
    move(rope_key + HEAD_LANES, rope_key + MLA_ROPE, wt_ref.shape[0] - rope_key - MLA_ROPE)


def _regroup_w_in(w_in):
    depth, d, cols = w_in.shape
    out_cols = cols - MLA_ROPE + HEAD_LANES
    return pl.pallas_call(
        _regroup_kernel,
        grid=(depth,),
        in_specs=[pl.BlockSpec((None, cols, d), lambda l: (l, 0, 0))],
        out_specs=pl.BlockSpec((None, d, out_cols), lambda l: (l, 0, 0)),
        out_shape=jax.ShapeDtypeStruct((depth, d, out_cols), BF16),
        compiler_params=_cparams(("arbitrary",), 48),
        name="regroup_w_in",
    )(jnp.swapaxes(w_in, 1, 2))


def _stacked_weights(w_in, p):
    depth = w_in.shape[0]
    w_in_all = _regroup_w_in(w_in)
    dq = MLA_NOPE + MLA_ROPE
    head_pad = lambda a: jnp.pad(a, ((0, 0),) * 3 + ((0, HEAD_LANES - a.shape[3]),)).reshape(
        depth, a.shape[1], MLA_HEADS * HEAD_LANES)
    w_uq = head_pad(p["mla_w_uq"].reshape(depth, MLA_Q_LORA, MLA_HEADS, dq))
    w_ukv = p["mla_w_ukv"].reshape(depth, MLA_KV_LORA, MLA_HEADS, MLA_NOPE + MLA_V)
    w_ukv = jnp.concatenate([head_pad(w_ukv[..., :MLA_NOPE]), head_pad(w_ukv[..., MLA_NOPE:])], axis=2)
    vectors = (("gate_b", 4 * D_MODEL, 4 * D_MODEL), ("norm1_g", D_MODEL, D_MODEL), ("norm2_g", D_MODEL, D_MODEL),
               ("hy_conv_b", HY_COLS, HY_COLS), ("mla_q_norm", MLA_Q_LORA, MLA_Q_LORA),
               ("conf_dw_b", CONF_WIDTH, CONF_WIDTH), ("conf_ln_g", CONF_WIDTH, CONF_WIDTH),
               ("conf_ln_b", CONF_WIDTH, CONF_WIDTH), ("mla_kv_norm", MLA_KV_LORA, LANES),
               ("hy_b1", HY_FFN, LANES), ("hy_b2", HY_FFN, LANES))
    as_is = ("hy_conv_w", "hy_w2", "hy_w3", "hy_bias", "ret_decay", "conf_dw_w")
    bf16 = ("gate_w", "hy_out", "mla_out", "ret_out", "conf_out", "w_o", "ffn_w1", "ffn_w2")
    return {
        "w_in": w_in_all,
        "mla_w_uq": w_uq.astype(BF16), "mla_w_ukv": w_ukv.astype(BF16),
        "hy_w1": jnp.pad(p["hy_w1"], ((0, 0), (0, LANES - HY_EMB), (0, 0))),
        **_pack_vectors(p, vectors),
        **{name: p[name] for name in as_is},
        **{name: p[name].astype(BF16) for name in bf16},
    }


def kernel(x_prompt, x_sample, cache_mla_ckv, cache_mla_krope, state_ret, c, c_ctx, ada_w, ada_b, norm1_g, w_in, hy_conv_w, hy_conv_b, hy_w1, hy_b1, hy_w2, hy_b2, hy_w3, hy_bias, hy_out, mla_q_norm, mla_w_uq, mla_kv_norm, mla_w_ukv, mla_out, ret_decay, ret_out, conf_dw_w, conf_dw_b, conf_ln_g, conf_ln_b, conf_out, gate_w, gate_b, w_o, norm2_g, ffn_w1, ffn_w2, final_norm_g):
    p = dict(norm1_g=norm1_g, hy_conv_w=hy_conv_w, hy_conv_b=hy_conv_b, hy_w1=hy_w1, hy_b1=hy_b1,
             hy_w2=hy_w2, hy_b2=hy_b2, hy_w3=hy_w3, hy_bias=hy_bias, hy_out=hy_out,
             mla_q_norm=mla_q_norm, mla_w_uq=mla_w_uq, mla_kv_norm=mla_kv_norm, mla_w_ukv=mla_w_ukv,
             mla_out=mla_out, ret_decay=ret_decay, ret_out=ret_out, conf_dw_w=conf_dw_w,
             conf_dw_b=conf_dw_b, conf_ln_g=conf_ln_g, conf_ln_b=conf_ln_b, conf_out=conf_out,
             gate_w=gate_w, gate_b=gate_b, w_o=w_o, norm2_g=norm2_g, ffn_w1=ffn_w1, ffn_w2=ffn_w2)
    Bp, Lp, D = x_prompt.shape
    Bs, Ls, _ = x_sample.shape
    depth = w_in.shape[0]

    cond = jnp.concatenate([c_ctx[None, :], c, jnp.zeros((MOD_ROWS - 1 - Bs, D), F32)], axis=0)
    mod = _modulation(cond, ada_w, ada_b)
    tables_p = _dft_tables(Lp)
    tables_s = _dft_tables(Ls)
    final_g = final_norm_g.reshape(1, D)

    xp = x_prompt.reshape(Bp * Lp, D)
    xs = x_sample.reshape(Bs * Ls, D)
    caches, states = None, None
    weights = _stacked_weights(w_in, p)
    for l in range(depth):
        lw = dict(weights, layer=l)
        final = l == depth - 1
        xp, ckv, kr, states = _trunk_layer(xp, mod, lw, final_g, tables_p, Bp, Lp, None, None, final,
                                           caches, states)
        caches = (ckv, kr)
        xs, _, _, _ = _trunk_layer(xs, mod, lw, final_g, tables_s, Bs, Ls,
                                   (cache_mla_ckv, cache_mla_krope), state_ret, final)
    new_ckv, new_krope_t = caches
    return (xp.reshape(Bp, Lp, D), xs.reshape(Bs, Ls, D), new_ckv, jnp.swapaxes(new_krope_t, 2, 3), states)
```

```python
import functools
import math
from typing import NamedTuple

import jax
import jax.numpy as jnp
from jax import lax
from jax.experimental import pallas as pl
from jax.experimental.pallas import tpu as pltpu

F32 = jnp.float32
BF16 = jnp.bfloat16

D_MODEL = 1024
DEPTH = 2
PAST_LEN = 256
EPS = 1e-6
GRID_W = 64

HY_WIDTH = 256
HY_EMB = 33
HY_BANDS = (HY_EMB - 1) // 2
HY_FFN = 64
HY_FAST_DECAY = 0.3
HY_SLOW_DECAY = 1.5
HY_TARGET = 1e-2

MLA_HEADS = 4
MLA_Q_LORA = 256
MLA_KV_LORA = 128
MLA_NOPE = 64
MLA_ROPE = 32
MLA_V = 64
ROPE_BASE = 10000.0

RET_HEADS = 4
RET_DK = 64
RET_DV = 64
RET_BLOCK = 256

CONF_WIDTH = 256
CONF_KERNEL = 31

D_FF = ((8 * D_MODEL // 3 + 255) // 256) * 256

HY_COLS = 3 * HY_WIDTH
MLA_COLS = MLA_Q_LORA + MLA_KV_LORA + MLA_ROPE
RET_COLS = 2 * RET_HEADS * RET_DK + 2 * RET_HEADS * RET_DV
CONF_COLS = 2 * CONF_WIDTH

VMEM_BYTES_V7X = 64 * 1024 * 1024
SUBLANES = 8
LANES = 128
TOKEN_TILE = 512
WIDE_TOKEN_TILE = 1024
Q_TILE = 512
MLA_SOFTMAX_ROWS = 16
DFT_TILE = 512
SHORT_SEQ_ROWS = 1024
HYENA_GROUP_ROWS = 1024
HYENA_ROWS = 1024
CONF_ROWS = 128
REGROUP_COLS = 128
CONF_HALO = 16
MXU_DIM_V7X = 256
FFN_CHUNK = 4 * MXU_DIM_V7X
MOD_ROWS = 8


def _cparams(sem, vmem_mb):
    vmem_bytes = vmem_mb * 1024 * 1024
    assert vmem_bytes < VMEM_BYTES_V7X
    return pltpu.CompilerParams(dimension_semantics=sem, vmem_limit_bytes=vmem_bytes)


def _dot(a, b):
    return jnp.dot(a, b, preferred_element_type=F32)


def _dot_nt(a, b):
    return lax.dot_general(a, b, (((1,), (1,)), ((), ())), preferred_element_type=F32)


def _dot_exact(a, b):
    return jnp.dot(a, b, preferred_element_type=F32, precision=lax.Precision.HIGHEST)


def _dot_split(a, b):
    a_hi, b_hi = a.astype(BF16), b.astype(BF16)
    a_lo = (a - a_hi.astype(F32)).astype(BF16)
    b_lo = (b - b_hi.astype(F32)).astype(BF16)
    return _dot(a_hi, b_hi) + (_dot(a_lo, b_hi) + _dot(a_hi, b_lo))


def _rms(x, g):
    return x * lax.rsqrt(jnp.mean(x * x, axis=-1, keepdims=True) + EPS) * g


def _sigmoid(x):
    return 0.5 * jnp.tanh(0.5 * x) + 0.5


def _silu(x):
    return x * _sigmoid(x)


def _mod_kernel(c_ref, w_ref, b_ref, o_ref):
    s = _silu(c_ref[...]).astype(BF16)
    o_ref[...] = _dot(s, w_ref[...].astype(BF16)) + b_ref[...]


def _modulation(cond, ada_w, ada_b):
    depth, d, cols = ada_w.shape
    blk = 1024
    out = pl.pallas_call(
        _mod_kernel,
        grid=(depth, cols // blk),
        in_specs=[
            pl.BlockSpec((MOD_ROWS, d), lambda l, j: (0, 0)),
            pl.BlockSpec((None, d, blk), lambda l, j: (l, 0, j)),
            pl.BlockSpec((None, 1, blk), lambda l, j: (l, 0, j)),
        ],
        out_specs=pl.BlockSpec((None, MOD_ROWS, blk), lambda l, j: (l, 0, j)),
        out_shape=jax.ShapeDtypeStruct((depth, MOD_ROWS, cols), F32),
        compiler_params=_cparams(("arbitrary", "arbitrary"), 32),
        name="modulation",
    )(cond, ada_w, ada_b.reshape(depth, 1, cols))
    return out.reshape(depth, MOD_ROWS, 6, d)


def _seqs_per_step(B, L):
    return max(1, min(B, SHORT_SEQ_ROWS // L))


class _VecField(NamedTuple):
    packed: jax.Array
    offset: int
    width: int


def _pack_vectors(p, fields):
    depth = p[fields[0][0]].shape[0]
    parts, offsets, total = [], {}, 0
    for name, width, block in fields:
        assert total % block == 0
        offsets[name] = total
        parts.append(jnp.pad(p[name].reshape(depth, width), ((0, 0), (0, block - width))))
        total += block
    packed = jnp.concatenate(parts, axis=1).reshape(depth, 1, total)
    return {name: _VecField(packed, offsets[name], block) for name, _, block in fields}


def _operand(a):
    return a.packed if isinstance(a, _VecField) else a


def _per_layer(a, layer, single_buffer=False):
    mode = dict(pipeline_mode=pl.Buffered(1)) if single_buffer else {}
    if isinstance(a, _VecField):
        return pl.BlockSpec((None, 1, a.width), lambda *_: (layer, 0, a.offset // a.width), **mode)
    return pl.BlockSpec((None,) + a.shape[1:], lambda *_: (layer,) + (0,) * (a.ndim - 1), **mode)


def _mod_spec(layer, tiles_per_seq, latent):
    group = (lambda i: 1 + i // tiles_per_seq) if latent else (lambda i: 0)
    return pl.BlockSpec((None, None, 6, D_MODEL), lambda i: (layer, group(i), 0, 0))


HEAD_LANES = LANES


def _rope_lanes(x, cos, sa, sb):
    reps = x.shape[1] // HEAD_LANES
    wide = lambda t: jnp.concatenate([t] * reps, axis=1) if reps > 1 else t
    half = MLA_ROPE // 2
    return (x * wide(cos) + pltpu.roll(x, x.shape[1] - half, 1) * wide(sa)
            + pltpu.roll(x, half, 1) * wide(sb))


def _inproj_kernel(*refs, seqs_per_tile, seq_len, latent, cache_layers, aliased):
    x_ref, mod_ref, g1_ref, win_ref, qn_ref, kvn_ref, wuq_ref, wukv_ref = refs[:8]
    refs = refs[8:]
    c_cq = HY_COLS
    c_ck = c_cq + MLA_Q_LORA
    c_ret = c_ck + MLA_KV_LORA + HEAD_LANES
    c_conf = c_ret + RET_COLS
    why_ref, wcq_ref, wck_ref, wret_ref, wconf_ref = (
        win_ref.at[:, a:b] for a, b in ((0, c_cq), (c_cq, c_ck), (c_ck, c_ret), (c_ret, c_conf),
                                        (c_conf, c_conf + CONF_COLS)))
    if latent:
        cos_ref, sa_ref, sb_ref = refs[:3]
        refs = refs[3:]
    if aliased:
        refs = refs[2:]
    hv_ref, hx1_ref, hx2_ref, q_ref, kh_ref, vh_ref, ret_ref, conf_ref, rkt_ref = refs[:9]
    if not latent:
        ckv_ref, kr_ref = refs[9:]
    x = x_ref[...]
    h = _rms(x, g1_ref[...]) * (1.0 + mod_ref[1:2, :]) + mod_ref[0:1, :]
    hb = h.astype(BF16)

    u = _dot(hb, why_ref[...])
    for p, o_ref in enumerate((hv_ref, hx1_ref, hx2_ref)):
        part = u[:, p * HY_WIDTH:(p + 1) * HY_WIDTH]
        if seqs_per_tile == 1:
            o_ref[...] = part
        else:
            for s in range(seqs_per_tile):
                o_ref[:, s * HY_WIDTH:(s + 1) * HY_WIDTH] = part[s * seq_len:(s + 1) * seq_len]

    heads_w = MLA_HEADS * HEAD_LANES
    cq = _dot(hb, wcq_ref[...])
    q = _dot(_rms(cq, qn_ref[...]).astype(BF16), wuq_ref[...])
    ck = _dot(hb, wck_ref[...])
    ckv = _rms(ck[:, :MLA_KV_LORA], kvn_ref[...])
    kr_block = ck[:, MLA_KV_LORA:MLA_KV_LORA + HEAD_LANES]
    if latent:
        q = _rope_lanes(q, cos_ref[...], sa_ref[...], sb_ref[...])
        kr_block = _rope_lanes(kr_block, cos_ref[...], sa_ref[...], sb_ref[...])
    else:
        kr = kr_block[:, MLA_NOPE:MLA_NOPE + MLA_ROPE]
        for s in range(seqs_per_tile):
            rows = slice(s * seq_len, (s + 1) * seq_len)
            for out_ref, val in ((ckv_ref, ckv[rows]), (kr_ref, kr[rows])):
                mine = out_ref.at[s, 0] if cache_layers else out_ref.at[s]
                mine[...] = val
                for other in range(1, cache_layers):
                    out_ref[s, other] = jnp.zeros(out_ref.shape[2:], F32)
    q_ref[...] = (q * ((MLA_NOPE + MLA_ROPE) ** -0.5 * math.log2(math.e))).astype(BF16)
    kv = _dot(ckv.astype(BF16), wukv_ref[...])
    kh_ref[...] = (kv[:, :heads_w] + jnp.concatenate([kr_block] * MLA_HEADS, axis=1)).astype(BF16)
    lane = lax.broadcasted_iota(jnp.int32, (1, heads_w), 1)
    ones_lane = jnp.where(lane % HEAD_LANES == MLA_V, 1.0, 0.0)
    vh_ref[...] = (kv[:, heads_w:] + ones_lane).astype(BF16)
    ret = _dot(hb, wret_ref[...])
    ret_ref[...] = ret
    qk = RET_HEADS * RET_DK
    rkt_ref[...] = ret[:, qk:2 * qk].T
    conf_ref[...] = _dot(hb, wconf_ref[...])


def _inproj(x, mod, lw, B, L, latent, caches_out=None):
    T = B * L
    TM = TOKEN_TILE
    nt = T // TM
    if L >= TM:
        tiles_per_seq, seqs_per_tile = L // TM, 1
        hy_block = (TM, HY_WIDTH)
        hy_map = lambda i: (i % tiles_per_seq, i // tiles_per_seq)
    else:
        tiles_per_seq, seqs_per_tile = 1, TM // L
        hy_block = (L, seqs_per_tile * HY_WIDTH)
        hy_map = lambda i: (0, i)
    row = lambda w: pl.BlockSpec((TM, w), lambda i: (i, 0))
    weights = (lw["norm1_g"], lw["w_in"], lw["mla_q_norm"], lw["mla_kv_norm"], lw["mla_w_uq"], lw["mla_w_ukv"])
    hy_shape = jax.ShapeDtypeStruct((L, B * HY_WIDTH), F32)
    qk = RET_HEADS * RET_DK
    heads_w = MLA_HEADS * HEAD_LANES
    ins = [x, mod, *map(_operand, weights)]
    in_specs = ([row(D_MODEL), _mod_spec(lw["layer"], tiles_per_seq, latent)]
                + [_per_layer(w, lw["layer"]) for w in weights])
    out_specs = ([pl.BlockSpec(hy_block, hy_map)] * 3
                 + [row(heads_w)] * 3 + [row(RET_COLS), row(CONF_COLS), pl.BlockSpec((qk, TM), lambda i: (0, i))])
    out_shape = ([hy_shape] * 3 + [jax.ShapeDtypeStruct((T, heads_w), BF16)] * 3
                 + [jax.ShapeDtypeStruct((T, RET_COLS), F32), jax.ShapeDtypeStruct((T, CONF_COLS), F32),
                    jax.ShapeDtypeStruct((qk, T), F32)])
    if latent:
        ins += list(_rope_lane_tables(L))
        in_specs += [pl.BlockSpec((TM, HEAD_LANES), lambda i: (i % tiles_per_seq, 0))] * 3
    aliases = {}
    cache_layers = 0
    if not latent:
        assert tiles_per_seq == 1
        layer, depth = lw["layer"], lw["w_in"].shape[0]
        dims = ((L, MLA_KV_LORA), (L, MLA_ROPE))
        if caches_out is None:
            cache_layers = depth
            out_specs += [pl.BlockSpec((seqs_per_tile, depth) + d, lambda i: (i, 0, 0, 0)) for d in dims]
        else:
            aliases = {len(ins) + k: len(out_shape) + k for k in range(len(dims))}
            ins += list(caches_out)
            in_specs += [pl.BlockSpec(memory_space=pl.ANY)] * len(dims)
            out_specs += [pl.BlockSpec((seqs_per_tile, None) + d, lambda i: (i, layer, 0, 0)) for d in dims]
        out_shape += [jax.ShapeDtypeStruct((B, depth) + d, F32) for d in dims]
    return pl.pallas_call(
        functools.partial(_inproj_kernel, seqs_per_tile=seqs_per_tile, seq_len=L, latent=latent,
                          cache_layers=cache_layers, aliased=bool(aliases)),
        grid=(nt,),
        in_specs=in_specs,
        out_specs=out_specs,
        out_shape=out_shape,
        input_output_aliases=aliases,
        compiler_params=_cparams(("arbitrary",), 48),
        name="inproj",
    )(*ins)


def _dft_tables(L):
    N = 2 * L
    k_lo = min(L, 32)
    k_hi = L // k_lo
    t = jnp.arange(L, dtype=jnp.int32)[None, :]

    def cs(k):
        m = (k[:, None] * t) % N
        ang = m.astype(F32) * (2.0 * math.pi / N)
        return jnp.cos(ang), jnp.sin(ang)

    ca, sa = cs(jnp.arange(k_hi, dtype=jnp.int32) * k_lo)
    cb, sb = cs(jnp.arange(k_lo, dtype=jnp.int32))
    cos = (ca[:, None, :] * cb[None, :, :] - sa[:, None, :] * sb[None, :, :]).reshape(L, L)
    sin = (sa[:, None, :] * cb[None, :, :] + ca[:, None, :] * sb[None, :, :]).reshape(L, L)
    return cos.astype(BF16), (-sin).astype(BF16)


def _filter_features(L):
    t = jnp.linspace(0.0, 1.0, L, dtype=F32)[:, None]
    w = 2.0 * math.pi * jnp.arange(L, dtype=F32)[:, None] / L
    f = jnp.linspace(1e-4, HY_BANDS - 1, HY_BANDS, dtype=F32)[None, :]
    z = jnp.concatenate([t, jnp.cos(f * w), -jnp.sin(f * w)], axis=-1)
    z = jnp.pad(z, ((0, 0), (0, LANES - HY_EMB)))
    max_decay = math.log(HY_TARGET) / HY_FAST_DECAY
    min_decay = math.log(HY_TARGET) / HY_SLOW_DECAY
    deltas = jnp.abs(jnp.linspace(min_decay, max_decay, HY_WIDTH, dtype=F32))
    decay = jnp.exp(-t * deltas[None, :])
    return z, decay


def _alternating(rows):
    t = lax.broadcasted_iota(jnp.int32, (rows, 1), 0)
    return (1 - 2 * (t & 1)).astype(F32)


def _filter_kernel(z_ref, dec_ref, w1_ref, b1_ref, w2_ref, b2_ref, w3_ref, cos_ref, sin_ref,
                   kr_ref, ki_ref, kn_ref, hsum_ref, hdiff_ref, *, L):
    j = pl.program_id(0)
    W = HY_WIDTH
    N = 2 * L

    @pl.when(j == 0)
    def _():
        h = jnp.sin(_dot_exact(z_ref[...], w1_ref[...]) + b1_ref[:, :HY_FFN])
        h = jnp.sin(_dot_exact(h, w2_ref[...]) + b2_ref[:, :HY_FFN])
        h = _dot_split(h, w3_ref[...]) * jnp.concatenate([dec_ref[...]] * 4, axis=1)
        cs = jnp.sum(jnp.abs(h), axis=0, keepdims=True)
        s0 = cs[:, 0:W] + cs[:, W:2 * W]
        s1 = cs[:, 2 * W:3 * W] + cs[:, 3 * W:4 * W]
        h = h / jnp.concatenate([s0, s0, s1, s1], axis=1)
        row = lax.broadcasted_iota(jnp.int32, h.shape, 0)
        col = lax.broadcasted_iota(jnp.int32, h.shape, 1)
        backward = (col // W) % 2 == 1
        h = jnp.where(backward & (row == 0), 0.0, h)
        fwd = jnp.concatenate([h[:, 0:W], h[:, 2 * W:3 * W]], axis=1)
        bwd = jnp.concatenate([h[:, W:2 * W], h[:, 3 * W:4 * W]], axis=1)
        hsum_ref[...] = (fwd + bwd).astype(BF16)
        hdiff_ref[...] = (fwd - bwd).astype(BF16)
        nyq = jnp.sum((fwd + bwd) * _alternating(L), axis=0, keepdims=True) * (1.0 / N)
        for o in range(2):
            kn_ref[o] = nyq[:, o * W:(o + 1) * W]

    sr = _dot(cos_ref[...], hsum_ref[...])
    si = _dot(sin_ref[...], hdiff_ref[...])
    row = lax.broadcasted_iota(jnp.int32, (sr.shape[0], W), 0)
    scale = jnp.where((row == 0) & (j == 0), 1.0 / N, 2.0 / N)
    for o in range(2):
        kr_ref[o] = sr[:, o * W:(o + 1) * W] * scale
        ki_ref[o] = si[:, o * W:(o + 1) * W] * scale


def _filter_spectra(lw, L, tables):
    z, decay = _filter_features(L)
    cos, msin = tables
    Tk = min(DFT_TILE, L)
    full = lambda a: pl.BlockSpec(a.shape, lambda j: (0,) * a.ndim)
    mlp = (lw["hy_w1"], lw["hy_b1"], lw["hy_w2"], lw["hy_b2"], lw["hy_w3"])
    ins = (z, decay, *map(_operand, mlp))
    tile = pl.BlockSpec((Tk, L), lambda j: (j, 0))
    spec = pl.BlockSpec((2, Tk, HY_WIDTH), lambda j: (0, j, 0))
    return pl.pallas_call(
        functools.partial(_filter_kernel, L=L),
        grid=(L // Tk,),
        in_specs=[full(z), full(decay)] + [_per_layer(a, lw["layer"]) for a in mlp] + [tile, tile],
        out_specs=[spec, spec, pl.BlockSpec((2, 1, HY_WIDTH), lambda j: (0, 0, 0))],
        out_shape=[jax.ShapeDtypeStruct((2, L, HY_WIDTH), F32)] * 2
        + [jax.ShapeDtypeStruct((2, 1, HY_WIDTH), F32)],
        scratch_shapes=[pltpu.VMEM((L, 2 * HY_WIDTH), BF16)] * 2,
        compiler_params=_cparams(("arbitrary",), 48),
        name="hyena_filter",
    )(*ins, cos, msin)


def _hyena_kernel(v_ref, x1_ref, x2_ref, cw_ref, cb_ref, bias_ref, kr_ref, ki_ref, kn_ref,
                  cos_ref, sin_ref, y_ref, cur_ref, curb_ref, yr_ref, yi_ref, gate_ref,
                  *, L, W, T):
    reps = W // HY_WIDTH
    tiled = lambda a: jnp.concatenate([a] * reps, axis=1) if reps > 1 else a
    alt = _alternating(L)

    def short_conv(u_ref, p):
        u = u_ref[...]
        row = lax.broadcasted_iota(jnp.int32, (L, W), 0)
        prev = jnp.where(row == 0, 0.0, pltpu.roll(u, 1, 0))
        nxt = jnp.where(row == L - 1, 0.0, pltpu.roll(u, L - 1, 0))
        cols = slice(p * HY_WIDTH, (p + 1) * HY_WIDTH)
        w = [tiled(cw_ref[k:k + 1, cols]) for k in range(3)]
        return prev * w[0] + u * w[1] + nxt * w[2] + tiled(cb_ref[:, cols])

    cur_ref[...] = short_conv(v_ref, 0)
    for o, x_ref in enumerate((x1_ref, x2_ref)):
        cur = cur_ref[...]
        curb_ref[...] = cur.astype(BF16)
        gate_ref[...] = short_conv(x_ref, o + 1)
        nyq = jnp.sum(cur * alt, axis=0, keepdims=True) * tiled(kn_ref[o])
        for f in range(L // T):
            rows = slice(f * T, (f + 1) * T)
            cb = curb_ref[...]
            xr = _dot(cos_ref[rows, :], cb)
            xi = _dot(sin_ref[rows, :], cb)
            kr, ki = tiled(kr_ref[o, rows, :]), tiled(ki_ref[o, rows, :])
            yr_ref[rows, :] = (xr * kr - xi * ki).astype(BF16)
            yi_ref[rows, :] = (xr * ki + xi * kr).astype(BF16)
        for t in range(L // T):
            rows = slice(t * T, (t + 1) * T)
            conv = (_dot(cos_ref[rows, :], yr_ref[...]) + _dot(sin_ref[rows, :], yi_ref[...])
                    + alt[rows] * nyq + cur_ref[rows, :] * tiled(bias_ref[o:o + 1, :]))
            out = gate_ref[rows, :] * conv
            if o == 0:
                cur_ref[rows, :] = out
            else:
                y_ref[rows, :] = out


def _hyena(hv, hx1, hx2, lw, spectra, tables, B, L):
    kr, ki, kn = spectra
    cos, msin = tables
    T = min(HYENA_ROWS, L)
    W = HY_WIDTH * max(1, min(B, HYENA_GROUP_ROWS // L))
    ng = (B * HY_WIDTH) // W
    col = pl.BlockSpec((L, W), lambda g: (0, g))
    once = lambda a: pl.BlockSpec(a.shape, lambda g: (0,) * a.ndim, pipeline_mode=pl.Buffered(1))
    params = (lw["hy_conv_w"], lw["hy_conv_b"], lw["hy_bias"])
    consts = (kr, ki, kn, cos, msin)
    return pl.pallas_call(
        functools.partial(_hyena_kernel, L=L, W=W, T=T),
        grid=(ng,),
        in_specs=[col, col, col] + [_per_layer(a, lw["layer"]) for a in params] + [once(a) for a in consts],
        out_specs=col,
        out_shape=jax.ShapeDtypeStruct((L, B * HY_WIDTH), F32),
        scratch_shapes=[pltpu.VMEM((L, W), F32), pltpu.VMEM((L, W), BF16), pltpu.VMEM((L, W), BF16),
                        pltpu.VMEM((L, W), BF16), pltpu.VMEM((L, W), F32)],
        compiler_params=_cparams(("arbitrary",), 60),
        name="hyena_conv",
    )(hv, hx1, hx2, *map(_operand, params), *consts)


def _mla_kernel(*refs, L, latent, seqs):
    if latent:
        q_ref, kh_ref, vh_ref, cckv_ref, ckr_ref, wukv_ref, o_ref, s_ref, p_ref, kctx_ref, vctx_ref = refs
    else:
        q_ref, kh_ref, vh_ref, o_ref, s_ref, p_ref = refs
    TQ = q_ref.shape[0] // seqs
    heads_w = MLA_HEADS * HEAD_LANES
    block = lambda h: slice(h * HEAD_LANES, (h + 1) * HEAD_LANES)

    if latent:
        @pl.when(pl.program_id(1) == 0)
        def _():
            kv = _dot(cckv_ref[...].astype(BF16), wukv_ref[...])
            zeros = lambda w: jnp.zeros((PAST_LEN, w), F32)
            kr_block = jnp.concatenate(
                [zeros(MLA_NOPE), ckr_ref[...], zeros(HEAD_LANES - MLA_NOPE - MLA_ROPE)], axis=1)
            kctx_ref[...] = (kv[:, :heads_w] + jnp.concatenate([kr_block] * MLA_HEADS, axis=1)).astype(BF16)
            lane = lax.broadcasted_iota(jnp.int32, (1, heads_w), 1)
            vctx_ref[...] = (kv[:, heads_w:] + jnp.where(lane % HEAD_LANES == MLA_V, 1.0, 0.0)).astype(BF16)

    def attend(q_rows, k_rows):
        def scores(h):
            qh = q_ref[q_rows, block(h)]
            s_ref[h, :, 0:L] = _dot_nt(qh, kh_ref[k_rows, block(h)])
            if latent:
                s_ref[h, :, L:L + PAST_LEN] = _dot_nt(qh, kctx_ref[:, block(h)])

        outs = []
        scores(0)
        for h in range(MLA_HEADS):
            if h + 1 < MLA_HEADS:
                scores(h + 1)
            for r in range(TQ // MLA_SOFTMAX_ROWS):
                rows = slice(r * MLA_SOFTMAX_ROWS, (r + 1) * MLA_SOFTMAX_ROWS)
                s = s_ref[h, rows, :]
                p_ref[h, rows, :] = jnp.exp2(s - jnp.max(s, axis=-1, keepdims=True)).astype(BF16)
            pv = _dot(p_ref[h, :, 0:L], vh_ref[k_rows, block(h)])
            if latent:
                pv = pv + _dot(p_ref[h, :, L:L + PAST_LEN], vctx_ref[:, block(h)])
            outs.append(pv[:, :MLA_V] / pv[:, MLA_V:MLA_V + 1])
        o_ref[q_rows, :] = jnp.concatenate(outs, axis=-1)

    for sq in range(seqs):
        attend(slice(sq * TQ, (sq + 1) * TQ), slice(sq * L, (sq + 1) * L))


def _rope_tables(L):
    rows = L // GRID_W
    row = jnp.repeat(jnp.arange(rows), GRID_W).astype(F32)
    col = jnp.tile(jnp.arange(GRID_W), rows).astype(F32)
    per_axis = MLA_ROPE // 4
    inv = ROPE_BASE ** (-jnp.arange(per_axis, dtype=F32) / per_axis)
    ang = jnp.concatenate([row[:, None] * inv, col[:, None] * inv], axis=-1)
    return jnp.cos(ang), jnp.sin(ang)


def _rope_lane_tables(L):
    cos, sin = _rope_tables(L)
    pad = HEAD_LANES - MLA_NOPE - MLA_ROPE
    one, zero = jnp.ones((L, MLA_NOPE), F32), jnp.zeros((L, MLA_NOPE), F32)
    half0 = jnp.zeros_like(sin)
    cos_t = jnp.concatenate([one, cos, cos, one[:, :pad]], axis=1)
    sa_t = jnp.concatenate([zero, -sin, half0, zero[:, :pad]], axis=1)
    sb_t = jnp.concatenate([zero, half0, sin, zero[:, :pad]], axis=1)
    return cos_t, sa_t, sb_t


def _mla(qh, kh, vh, lw, B, L, ctx):
    latent = ctx is not None
    TQ = min(Q_TILE, L)
    nq = L // TQ
    seqs = 1 if latent or nq > 1 else _seqs_per_step(B, L)
    Lk = L + PAST_LEN if latent else L
    heads_w = MLA_HEADS * HEAD_LANES
    seq = pl.BlockSpec((seqs * L, heads_w), lambda b, i: (b, 0))
    ins = [qh, kh, vh]
    specs = [pl.BlockSpec((seqs * TQ, heads_w), lambda b, i: (b * nq + i, 0)), seq, seq]
    scratch = [pltpu.VMEM((MLA_HEADS, TQ, Lk), F32), pltpu.VMEM((MLA_HEADS, TQ, Lk), BF16)]
    if latent:
        ins += [ctx[0], ctx[1], lw["mla_w_ukv"]]
        layer = lw["layer"]
        specs += [pl.BlockSpec((None, None, PAST_LEN, MLA_KV_LORA), lambda b, i: (b, layer, 0, 0)),
                  pl.BlockSpec((None, None, PAST_LEN, MLA_ROPE), lambda b, i: (b, layer, 0, 0)),
                  _per_layer(lw["mla_w_ukv"], lw["layer"])]
        scratch += [pltpu.VMEM((PAST_LEN, heads_w), BF16)] * 2
    return pl.pallas_call(
        functools.partial(_mla_kernel, L=L, latent=latent, seqs=seqs),
        grid=(B // seqs, nq),
        in_specs=specs,
        out_specs=pl.BlockSpec((seqs * TQ, MLA_HEADS * MLA_V), lambda b, i: (b * nq + i, 0)),
        out_shape=jax.ShapeDtypeStruct((B * L, MLA_HEADS * MLA_V), F32),
        scratch_shapes=scratch,
        compiler_params=_cparams(("arbitrary", "arbitrary"), 48),
        name="mla_attention",
    )(*ins)


def _ret_kernel(*refs, L, seqs, has_state, emit_state, state_layers, aliased):
    refs = list(refs)
    u_ref, kt_ref, dl_ref = refs[:3]
    pos = 3
    s0_ref = None
    if has_state:
        s0_ref = refs[pos]
        pos += 1
    if aliased:
        pos += 1
    y_ref = refs[pos]
    pos += 1
    sout_ref = None
    if emit_state:
        sout_ref = refs[pos]
        pos += 1
    of_ref, kv_ref, sin_ref, dcomb_ref, dtab_ref, hmask_ref, avg_ref = refs[pos:]

    C = RET_BLOCK
    n = L // C
    H, DK, DV = RET_HEADS, RET_DK, RET_DV
    qk = H * DK
    DK_F, DK_B, DQ_F, DQ_B, DC_F, DC_B, MASK = range(7)
    use_cross = has_state or n > 1

    @pl.when(pl.program_id(0) == 0)
    def _():
        x = dl_ref[...]
        log_g = jnp.minimum(x, 0.0) - jnp.log1p(jnp.exp(-jnp.abs(x)))
        gf = [log_g[0:1, h:h + 1] for h in range(H)]
        gb = [log_g[1:2, h:h + 1] for h in range(H)]
        diff = (lax.broadcasted_iota(jnp.int32, (C, C), 0)
                - lax.broadcasted_iota(jnp.int32, (C, C), 1)).astype(F32)
        idx = lax.broadcasted_iota(jnp.int32, (C, 1), 0).astype(F32)
        for h in range(H):
            dcomb_ref[h] = (jnp.where(diff >= 0, jnp.exp(jnp.maximum(diff, 0.0) * gf[h]), 0.0)
                            + jnp.where(diff <= 0, jnp.exp(jnp.maximum(-diff, 0.0) * gb[h]), 0.0))

        def head_cols(lag, g):
            return jnp.concatenate([jnp.broadcast_to(jnp.exp(lag * g[h]), (C, DK)) for h in range(H)], axis=1)

        def head_rows(lag, g):
            return jnp.concatenate(
                [jnp.broadcast_to(jnp.exp(lag * g[h]) * (DK ** -0.5), (DK, C)) for h in range(H)], axis=0)

        tok = lax.broadcasted_iota(jnp.int32, (1, C), 1).astype(F32)
        dtab_ref[DK_F] = head_rows(C - 1.0 - tok, gf)
        dtab_ref[DK_B] = head_rows(tok, gb)
        dtab_ref[DQ_F] = head_cols(idx + 1.0, gf)
        dtab_ref[DQ_B] = head_cols(C - idx, gb)
        row_head = lax.broadcasted_iota(jnp.int32, (qk, qk), 0) // DK
        col_head = lax.broadcasted_iota(jnp.int32, (qk, qk), 1) // DV
        same_head = row_head == col_head
        for slot, g in ((DC_F, gf), (DC_B, gb)):
            dc = jnp.zeros((qk, qk), F32)
            for h in range(H):
                dc = jnp.where(same_head & (row_head == h), jnp.exp(C * g[h]), dc)
            dtab_ref[slot] = dc
        dtab_ref[MASK] = same_head.astype(F32)
        avg_ref[...] = jnp.where(same_head, 1.0 / DV, 0.0).astype(BF16)
        lane_head = lax.broadcasted_iota(jnp.int32, (C, qk), 1) // DK
        for h in range(H):
            hmask_ref[h] = jnp.where(lane_head == h, 1.0, 0.0).astype(BF16)

    def head_mean(a):
        total = None
        for _ in range(3):
            part = a.astype(BF16)
            a = a - part.astype(F32)
            term = _dot(part, avg_ref[...])
            total = term if total is None else total + term
        return total

    for sq in range(seqs):
        chunk = lambda c: slice(sq * L + c * C, sq * L + (c + 1) * C)

        for c in range(n):
            rows = chunk(c)
            qb = u_ref[rows, 0:qk].astype(BF16)
            kb = (u_ref[rows, qk:2 * qk] * (DK ** -0.5)).astype(BF16)
            vb = u_ref[rows, 2 * qk:2 * qk + H * DV].astype(BF16)
            att = [(_dot_nt(qb * hmask_ref[h], kb) * dcomb_ref[h]).astype(BF16) for h in range(H)]
            v_heads = jnp.concatenate([vb * hmask_ref[h] for h in range(H)], axis=0)
            of_ref[c * C:(c + 1) * C, :] = _dot(jnp.concatenate(att, axis=1), v_heads)
            kt = kt_ref[:, rows]
            for d, slot in ((0, DK_F), (1, DK_B)):
                kv_ref[d, c] = _dot((kt * dtab_ref[slot]).astype(BF16), vb)

        finals = []
        for d, slot in ((0, DC_F), (1, DC_B)):
            if has_state:
                zero = jnp.zeros((DK, DV), F32)
                S = jnp.concatenate(
                    [jnp.concatenate([s0_ref[sq, d, h] if g == h else zero for g in range(H)], axis=1)
                     for h in range(H)], axis=0)
            else:
                S = jnp.zeros((qk, H * DV), F32)
            for c in (range(n) if d == 0 else reversed(range(n))):
                if use_cross:
                    sin_ref[d, c] = S.astype(BF16)
                S = S * dtab_ref[slot] + kv_ref[d, c] * dtab_ref[MASK]
            finals.append(S)
        if emit_state:
            mine = sout_ref.at[sq, 0] if state_layers else sout_ref.at[sq]
            for d in range(2):
                for h in range(H):
                    mine[d, h] = finals[d][h * DK:(h + 1) * DK, h * DV:(h + 1) * DV]
            for other in range(1, state_layers):
                sout_ref[sq, other] = jnp.zeros(sout_ref.shape[2:], F32)

        for c in range(n):
            rows = chunk(c)
            tot = of_ref[c * C:(c + 1) * C, :]
            if use_cross:
                qb = u_ref[rows, 0:qk].astype(BF16)
                tot = (tot + _dot(qb, sin_ref[0, c]) * dtab_ref[DQ_F]
                       + _dot(qb, sin_ref[1, c]) * dtab_ref[DQ_B])
            xc = tot - head_mean(tot)
            normed = xc * lax.rsqrt(head_mean(xc * xc) + EPS)
            gate = u_ref[rows, 2 * qk + H * DV:2 * qk + 2 * H * DV]
            y_ref[rows, :] = _silu(gate) * normed


def _retention(u_ret, kt_ret, lw, B, L, state, emit_state, states_out=None):
    has_state = state is not None
    seqs = _seqs_per_step(B, L)
    st_dims = (2, RET_HEADS, RET_DK, RET_DV)
    ins = [u_ret, kt_ret, lw["ret_decay"]]
    specs = [pl.BlockSpec((seqs * L, RET_COLS), lambda b: (b, 0)),
             pl.BlockSpec((RET_HEADS * RET_DK, seqs * L), lambda b: (0, b)),
             _per_layer(lw["ret_decay"], lw["layer"])]
    if has_state:
        layer = lw["layer"]
        ins.append(state)
        specs.append(pl.BlockSpec((seqs, None) + st_dims, lambda b: (b, layer, 0, 0, 0, 0)))
    vd = RET_HEADS * RET_DV
    out_specs = [pl.BlockSpec((seqs * L, vd), lambda b: (b, 0))]
    out_shape = [jax.ShapeDtypeStruct((B * L, vd), F32)]
    aliases = {}
    state_layers = 0
    if emit_state:
        layer, depth = lw["layer"], lw["ret_decay"].shape[0]
        if states_out is None:
            state_layers = depth
            out_specs.append(pl.BlockSpec((seqs, depth) + st_dims, lambda b: (b, 0, 0, 0, 0, 0)))
        else:
            aliases = {len(ins): 1}
            ins.append(states_out)
            specs.append(pl.BlockSpec(memory_space=pl.ANY))
            out_specs.append(pl.BlockSpec((seqs, None) + st_dims, lambda b: (b, layer, 0, 0, 0, 0)))
        out_shape.append(jax.ShapeDtypeStruct((B, depth) + st_dims, F32))
    res = pl.pallas_call(
        functools.partial(_ret_kernel, L=L, seqs=seqs, has_state=has_state, emit_state=emit_state,
                          state_layers=state_layers, aliased=bool(aliases)),
        grid=(B // seqs,),
        in_specs=specs,
        out_specs=out_specs,
        out_shape=out_shape,
        input_output_aliases=aliases,
        scratch_shapes=[pltpu.VMEM((L, vd), F32),
                        pltpu.VMEM((2, L // RET_BLOCK, vd, vd), F32),
                        pltpu.VMEM((2, L // RET_BLOCK, vd, vd), BF16),
                        pltpu.VMEM((RET_HEADS, RET_BLOCK, RET_BLOCK), F32),
                        pltpu.VMEM((7, RET_BLOCK, vd), F32),
                        pltpu.VMEM((RET_HEADS, RET_BLOCK, vd), BF16),
                        pltpu.VMEM((vd, vd), BF16)],
        compiler_params=_cparams(("arbitrary",), 48),
        name="retention",
    )(*ins)
    return (res[0], res[1]) if emit_state else (res[0], None)


def _conf_kernel(u_ref, w_ref, b_ref, g_ref, be_ref, y_ref, zp_ref, sh_ref, *, L, seqs):
    Wc = CONF_WIDTH
    halo = CONF_HALO
    zp_ref[0:halo, :] = jnp.zeros((halo, Wc), F32)
    zp_ref[halo + L:2 * halo + L, :] = jnp.zeros((halo, Wc), F32)
    first = halo - CONF_KERNEL // 2
    span = sh_ref.shape[1]
    R = CONF_ROWS
    for sq in range(seqs):
        r0 = sq * L
        zp_ref[halo:halo + L, :] = u_ref[r0:r0 + L, 0:Wc] * _sigmoid(u_ref[r0:r0 + L, Wc:2 * Wc])
        for s in range(SUBLANES):
            sh_ref[s] = zp_ref[first + s:first + s + span, :]
        for c in range(L // R):
            acc = jnp.broadcast_to(b_ref[...], (R, Wc))
            for k in range(CONF_KERNEL):
                a, s = divmod(k, SUBLANES)
                acc = acc + w_ref[k:k + 1, :] * sh_ref[s, c * R + SUBLANES * a:c * R + SUBLANES * a + R, :]
            mu = jnp.mean(acc, axis=-1, keepdims=True)
            xc = acc - mu
            z = xc * lax.rsqrt(jnp.mean(xc * xc, axis=-1, keepdims=True) + EPS) * g_ref[...] + be_ref[...]
            y_ref[r0 + c * R:r0 + (c + 1) * R, :] = _silu(z)


def _conformer(u_conf, lw, B, L):
    ws = (lw["conf_dw_w"], lw["conf_dw_b"], lw["conf_ln_g"], lw["conf_ln_b"])
    seqs = _seqs_per_step(B, L)
    return pl.pallas_call(
        functools.partial(_conf_kernel, L=L, seqs=seqs),
        grid=(B // seqs,),
        in_specs=[pl.BlockSpec((seqs * L, CONF_COLS), lambda b: (b, 0))]
        + [_per_layer(w, lw["layer"]) for w in ws],
        out_specs=pl.BlockSpec((seqs * L, CONF_WIDTH), lambda b: (b, 0)),
        out_shape=jax.ShapeDtypeStruct((B * L, CONF_WIDTH), F32),
        scratch_shapes=[pltpu.VMEM((L + 2 * CONF_HALO, CONF_WIDTH), F32),
                        pltpu.VMEM((SUBLANES, L + SUBLANES * ((CONF_KERNEL - 1) // SUBLANES), CONF_WIDTH), F32)],
        compiler_params=_cparams(("arbitrary",), 48),
        name="conformer",
    )(u_conf, *map(_operand, ws))


def _merge_kernel(x_ref, mod_ref, g1_ref, yhy_ref, ymla_ref, yret_ref, yconf_ref,
                  gw_ref, gb_ref, why_ref, wmla_ref, wret_ref, wconf_ref, wo_ref, o_ref, *, seqs_per_tile):
    x = x_ref[...]
    h = _rms(x, g1_ref[...]) * (1.0 + mod_ref[1:2, :]) + mod_ref[0:1, :]
    hb = h.astype(BF16)
    if seqs_per_tile == 1:
        yhy = yhy_ref[...]
    else:
        yhy = jnp.concatenate(
            [yhy_ref[:, s * HY_WIDTH:(s + 1) * HY_WIDTH] for s in range(seqs_per_tile)], axis=0)
    branches = ((yhy, why_ref), (ymla_ref[...], wmla_ref), (yret_ref[...], wret_ref), (yconf_ref[...], wconf_ref))
    D = D_MODEL
    merged = None
    for i, (y, w_ref) in enumerate(branches):
        gate = _sigmoid(_dot(hb, gw_ref[:, i * D:(i + 1) * D]) + gb_ref[:, i * D:(i + 1) * D])
        term = gate * _dot(y.astype(BF16), w_ref[...])
        merged = term if merged is None else merged + term
    o_ref[...] = x + mod_ref[2:3, :] * _dot(merged.astype(BF16), wo_ref[...])


def _merge(x, mod, lw, y_hy, y_mla, y_ret, y_conf, B, L, latent):
    T = B * L
    TM = WIDE_TOKEN_TILE
    if L >= TM:
        tiles_per_seq, seqs_per_tile = L // TM, 1
        hy_spec = pl.BlockSpec((TM, HY_WIDTH), lambda i: (i % tiles_per_seq, i // tiles_per_seq))
    else:
        tiles_per_seq, seqs_per_tile = 1, TM // L
        hy_spec = pl.BlockSpec((L, seqs_per_tile * HY_WIDTH), lambda i: (0, i))
    once = lambda a: _per_layer(a, lw["layer"], single_buffer=True)
    row = lambda w: pl.BlockSpec((TM, w), lambda i: (i, 0))
    ws = (lw["gate_w"], lw["gate_b"], lw["hy_out"], lw["mla_out"], lw["ret_out"], lw["conf_out"], lw["w_o"])
    return pl.pallas_call(
        functools.partial(_merge_kernel, seqs_per_tile=seqs_per_tile),
        grid=(T // TM,),
        in_specs=[row(D_MODEL), _mod_spec(lw["layer"], tiles_per_seq, latent),
                  once(lw["norm1_g"]), hy_spec, row(MLA_HEADS * MLA_V), row(RET_HEADS * RET_DV),
                  row(CONF_WIDTH)] + [once(w) for w in ws],
        out_specs=row(D_MODEL),
        out_shape=jax.ShapeDtypeStruct((T, D_MODEL), F32),
        compiler_params=_cparams(("arbitrary",), 56),
        name="merge",
    )(x, mod, _operand(lw["norm1_g"]), y_hy, y_mla, y_ret, y_conf, *map(_operand, ws))


def _ffn_kernel(x_ref, mod_ref, g2_ref, w1_ref, w2_ref, fg_ref, o_ref, *, final):
    x = x_ref[...]
    h2 = (_rms(x, g2_ref[...]) * (1.0 + mod_ref[4:5, :]) + mod_ref[3:4, :]).astype(BF16)
    acc = None
    for c0 in range(0, D_FF, FFN_CHUNK):
        c1 = min(c0 + FFN_CHUNK, D_FF)
        a = _dot(h2, w1_ref[:, c0:c1])
        b = _dot(h2, w1_ref[:, D_FF + c0:D_FF + c1])
        part = _dot((_silu(a) * b).astype(BF16), w2_ref[c0:c1, :])
        acc = part if acc is None else acc + part
    out = x + mod_ref[5:6, :] * acc
    if final:
        out = _rms(out, fg_ref[...])
    o_ref[...] = out


def _ffn(x, mod, lw, final_g, B, L, latent, final):
    T = B * L
    TM = WIDE_TOKEN_TILE
    tiles_per_seq = max(L // TM, 1)
    once = lambda a: _per_layer(a, lw["layer"], single_buffer=True)
    row = pl.BlockSpec((TM, D_MODEL), lambda i: (i, 0))
    ws = (lw["norm2_g"], lw["ffn_w1"], lw["ffn_w2"])
    return pl.pallas_call(
        functools.partial(_ffn_kernel, final=final),
        grid=(T // TM,),
        in_specs=[row, _mod_spec(lw["layer"], tiles_per_seq, latent)]
        + [once(w) for w in ws] + [pl.BlockSpec(final_g.shape, lambda i: (0, 0))],
        out_specs=row,
        out_shape=jax.ShapeDtypeStruct((T, D_MODEL), F32),
        compiler_params=_cparams(("arbitrary",), 56),
        name="ffn",
    )(x, mod, *map(_operand, ws), final_g)


def _trunk_layer(x, mod, lw, final_g, tables, B, L, ctx, state, final, caches_out=None, states_out=None):
    latent = ctx is not None
    hv, hx1, hx2, qh, kh, vh, u_ret, u_conf, kt_ret, *cache = _inproj(x, mod, lw, B, L, latent, caches_out)
    ckv, kr = cache if cache else (None, None)
    spectra = _filter_spectra(lw, L, tables)
    y_hy = _hyena(hv, hx1, hx2, lw, spectra, tables, B, L)
    y_mla = _mla(qh, kh, vh, lw, B, L, ctx)
    y_ret, S = _retention(u_ret, kt_ret, lw, B, L, state, emit_state=not latent, states_out=states_out)
    y_conf = _conformer(u_conf, lw, B, L)
    x = _merge(x, mod, lw, y_hy, y_mla, y_ret, y_conf, B, L, latent)
    x = _ffn(x, mod, lw, final_g, B, L, latent, final)
    return x, ckv, kr, S


def _regroup_kernel(wt_ref, o_ref):
    d = wt_ref.shape[1]
    rope_key = HY_COLS + MLA_COLS - MLA_ROPE

    def move(dst, src, n):
        for c in range(0, n, REGROUP_COLS):
            m = min(REGROUP_COLS, n - c)
            o_ref[:, dst + c:dst + c + m] = wt_ref[src + c:src + c + m, :].T.astype(BF16)

    move(0, 0, rope_key)
    slab = pltpu.roll(wt_ref[rope_key:rope_key + HEAD_LANES, :].T, MLA_NOPE, 1)
    lane = lax.broadcasted_iota(jnp.int32, (d, HEAD_LANES), 1)
    rope_lanes = (lane >= MLA_NOPE) & (lane < MLA_NOPE + MLA_ROPE)
    o_ref[:, rope_key:rope_key + HEAD_LANES] = jnp.where(rope_lanes, slab, 0.0).astype(BF16)
    move(rope_key + HEAD_LANES, rope_key + MLA_ROPE, wt_ref.shape[0] - rope_key - MLA_ROPE)


def _regroup_w_in(w_in):
    depth, d, cols = w_in.shape
    out_cols = cols - MLA_ROPE + HEAD_LANES
    return pl.pallas_call(
        _regroup_kernel,
        grid=(depth,),
        in_specs=[pl.BlockSpec((None, cols, d), lambda l: (l, 0, 0))],
        out_specs=pl.BlockSpec((None, d, out_cols), lambda l: (l, 0, 0)),
        out_shape=jax.ShapeDtypeStruct((depth, d, out_cols), BF16),
        compiler_params=_cparams(("arbitrary",), 48),
        name="regroup_w_in",
    )(jnp.swapaxes(w_in, 1, 2))


def _stacked_weights(w_in, p):
    depth = w_in.shape[0]
    w_in_all = _regroup_w_in(w_in)
    dq = MLA_NOPE + MLA_ROPE
    head_pad = lambda a: jnp.pad(a, ((0, 0),) * 3 + ((0, HEAD_LANES - a.shape[3]),)).reshape(
        depth, a.shape[1], MLA_HEADS * HEAD_LANES)
    w_uq = head_pad(p["mla_w_uq"].reshape(depth, MLA_Q_LORA, MLA_HEADS, dq))
    w_ukv = p["mla_w_ukv"].reshape(depth, MLA_KV_LORA, MLA_HEADS, MLA_NOPE + MLA_V)
    w_ukv = jnp.concatenate([head_pad(w_ukv[..., :MLA_NOPE]), head_pad(w_ukv[..., MLA_NOPE:])], axis=2)
    vectors = (("gate_b", 4 * D_MODEL, 4 * D_MODEL), ("norm1_g", D_MODEL, D_MODEL), ("norm2_g", D_MODEL, D_MODEL),
               ("hy_conv_b", HY_COLS, HY_COLS), ("mla_q_norm", MLA_Q_LORA, MLA_Q_LORA),
               ("conf_dw_b", CONF_WIDTH, CONF_WIDTH), ("conf_ln_g", CONF_WIDTH, CONF_WIDTH),
               ("conf_ln_b", CONF_WIDTH, CONF_WIDTH), ("mla_kv_norm", MLA_KV_LORA, LANES),
               ("hy_b1", HY_FFN, LANES), ("hy_b2", HY_FFN, LANES))
    as_is = ("hy_conv_w", "hy_w2", "hy_w3", "hy_bias", "ret_decay", "conf_dw_w")
    bf16 = ("gate_w", "hy_out", "mla_out", "ret_out", "conf_out", "w_o", "ffn_w1", "ffn_w2")
    return {
        "w_in": w_in_all,
        "mla_w_uq": w_uq.astype(BF16), "mla_w_ukv": w_ukv.astype(BF16),
        "hy_w1": jnp.pad(p["hy_w1"], ((0, 0), (0, LANES - HY_EMB), (0, 0))),
        **_pack_vectors(p, vectors),
        **{name: p[name] for name in as_is},
        **{name: p[name].astype(BF16) for name in bf16},
    }


def kernel(x_prompt, x_sample, cache_mla_ckv, cache_mla_krope, state_ret, c, c_ctx, ada_w, ada_b, norm1_g, w_in, hy_conv_w, hy_conv_b, hy_w1, hy_b1, hy_w2, hy_b2, hy_w3, hy_bias, hy_out, mla_q_norm, mla_w_uq, mla_kv_norm, mla_w_ukv, mla_out, ret_decay, ret_out, conf_dw_w, conf_dw_b, conf_ln_g, conf_ln_b, conf_out, gate_w, gate_b, w_o, norm2_g, ffn_w1, ffn_w2, final_norm_g):
    p = dict(norm1_g=norm1_g, hy_conv_w=hy_conv_w, hy_conv_b=hy_conv_b, hy_w1=hy_w1, hy_b1=hy_b1,
             hy_w2=hy_w2, hy_b2=hy_b2, hy_w3=hy_w3, hy_bias=hy_bias, hy_out=hy_out,
             mla_q_norm=mla_q_norm, mla_w_uq=mla_w_uq, mla_kv_norm=mla_kv_norm, mla_w_ukv=mla_w_ukv,
             mla_out=mla_out, ret_decay=ret_decay, ret_out=ret_out, conf_dw_w=conf_dw_w,
             conf_dw_b=conf_dw_b, conf_ln_g=conf_ln_g, conf_ln_b=conf_ln_b, conf_out=conf_out,
             gate_w=gate_w, gate_b=gate_b, w_o=w_o, norm2_g=norm2_g, ffn_w1=ffn_w1, ffn_w2=ffn_w2)
    Bp, Lp, D = x_prompt.shape
    Bs, Ls, _ = x_sample.shape
    depth = w_in.shape[0]

    cond = jnp.concatenate([c_ctx[None, :], c, jnp.zeros((MOD_ROWS - 1 - Bs, D), F32)], axis=0)
    mod = _modulation(cond, ada_w, ada_b)
    tables_p = _dft_tables(Lp)
    tables_s = _dft_tables(Ls)
    final_g = final_norm_g.reshape(1, D)

    xp = x_prompt.reshape(Bp * Lp, D)
    xs = x_sample.reshape(Bs * Ls, D)
    caches, states = None, None
    weights = _stacked_weights(w_in, p)
    for l in range(depth):
        lw = dict(weights, layer=l)
        final = l == depth - 1
        xp, ckv, kr, states = _trunk_layer(xp, mod, lw, final_g, tables_p, Bp, Lp, None, None, final,
                                           caches, states)
        caches = (ckv, kr)
        xs, _, _, _ = _trunk_layer(xs, mod, lw, final_g, tables_s, Bs, Ls,
                                   (cache_mla_ckv, cache_mla_krope), state_ret, final)
    return (xp.reshape(Bp, Lp, D), xs.reshape(Bs, Ls, D), *caches, states)
```

```python
import functools
import math

import jax
import jax.numpy as jnp
from jax import lax
from jax.experimental import pallas as pl
from jax.experimental.pallas import tpu as pltpu

F32 = jnp.float32
BF16 = jnp.bfloat16

D_MODEL = 1024
DEPTH = 2
PAST_LEN = 256
EPS = 1e-6
GRID_W = 64

HY_WIDTH = 256
HY_EMB = 33
HY_BANDS = (HY_EMB - 1) // 2
HY_FFN = 64
HY_FAST_DECAY = 0.3
HY_SLOW_DECAY = 1.5
HY_TARGET = 1e-2

MLA_HEADS = 4
MLA_Q_LORA = 256
MLA_KV_LORA = 128
MLA_NOPE = 64
MLA_ROPE = 32
MLA_V = 64
ROPE_BASE = 10000.0

RET_HEADS = 4
RET_DK = 64
RET_DV = 64
RET_BLOCK = 256

CONF_WIDTH = 256
CONF_KERNEL = 31

D_FF = ((8 * D_MODEL // 3 + 255) // 256) * 256

HY_COLS = 3 * HY_WIDTH
MLA_COLS = MLA_Q_LORA + MLA_KV_LORA + MLA_ROPE
RET_COLS = 2 * RET_HEADS * RET_DK + 2 * RET_HEADS * RET_DV
CONF_COLS = 2 * CONF_WIDTH

VMEM_BYTES_V7X = 64 * 1024 * 1024
SUBLANES = 8
LANES = 128
TOKEN_TILE = 512
WIDE_TOKEN_TILE = 1024
Q_TILE = 512
MLA_SOFTMAX_ROWS = 16
DFT_TILE = 512
SHORT_SEQ_ROWS = 1024
HYENA_GROUP_ROWS = 1024
HYENA_ROWS = 1024
CONF_ROWS = 128
REGROUP_COLS = 128
CONF_HALO = 16
MXU_DIM_V7X = 256
FFN_CHUNK = 4 * MXU_DIM_V7X
MOD_ROWS = 8


def _cparams(sem, vmem_mb):
    vmem_bytes = vmem_mb * 1024 * 1024
    assert vmem_bytes < VMEM_BYTES_V7X
    return pltpu.CompilerParams(dimension_semantics=sem, vmem_limit_bytes=vmem_bytes)


def _dot(a, b):
    return jnp.dot(a, b, preferred_element_type=F32)


def _dot_nt(a, b):
    return lax.dot_general(a, b, (((1,), (1,)), ((), ())), preferred_element_type=F32)


def _dot_exact(a, b):
    return jnp.dot(a, b, preferred_element_type=F32, precision=lax.Precision.HIGHEST)


def _dot_split(a, b):
    a_hi, b_hi = a.astype(BF16), b.astype(BF16)
    a_lo = (a - a_hi.astype(F32)).astype(BF16)
    b_lo = (b - b_hi.astype(F32)).astype(BF16)
    return _dot(a_hi, b_hi) + (_dot(a_lo, b_hi) + _dot(a_hi, b_lo))


def _rms(x, g):
    return x * lax.rsqrt(jnp.mean(x * x, axis=-1, keepdims=True) + EPS) * g


def _sigmoid(x):
    return 0.5 * jnp.tanh(0.5 * x) + 0.5


def _silu(x):
    return x * _sigmoid(x)


def _mod_kernel(c_ref, w_ref, b_ref, o_ref):
    s = _silu(c_ref[...]).astype(BF16)
    o_ref[...] = _dot(s, w_ref[...].astype(BF16)) + b_ref[...]


def _modulation(cond, ada_w, ada_b):
    depth, d, cols = ada_w.shape
    blk = 1024
    out = pl.pallas_call(
        _mod_kernel,
        grid=(depth, cols // blk),
        in_specs=[
            pl.BlockSpec((MOD_ROWS, d), lambda l, j: (0, 0)),
            pl.BlockSpec((None, d, blk), lambda l, j: (l, 0, j)),
            pl.BlockSpec((None, 1, blk), lambda l, j: (l, 0, j)),
        ],
        out_specs=pl.BlockSpec((None, MOD_ROWS, blk), lambda l, j: (l, 0, j)),
        out_shape=jax.ShapeDtypeStruct((depth, MOD_ROWS, cols), F32),
        compiler_params=_cparams(("arbitrary", "arbitrary"), 32),
        name="modulation",
    )(cond, ada_w, ada_b.reshape(depth, 1, cols))
    return out.reshape(depth, MOD_ROWS, 6, d)


def _seqs_per_step(B, L):
    return max(1, min(B, SHORT_SEQ_ROWS // L))


def _per_layer(a, layer, single_buffer=False):
    mode = dict(pipeline_mode=pl.Buffered(1)) if single_buffer else {}
    return pl.BlockSpec((None,) + a.shape[1:], lambda *_: (layer,) + (0,) * (a.ndim - 1), **mode)


def _mod_spec(layer, tiles_per_seq, latent):
    group = (lambda i: 1 + i // tiles_per_seq) if latent else (lambda i: 0)
    return pl.BlockSpec((None, None, 6, D_MODEL), lambda i: (layer, group(i), 0, 0))


HEAD_LANES = LANES


def _rope_lanes(x, cos, sa, sb):
    reps = x.shape[1] // HEAD_LANES
    wide = lambda t: jnp.concatenate([t] * reps, axis=1) if reps > 1 else t
    half = MLA_ROPE // 2
    return (x * wide(cos) + pltpu.roll(x, x.shape[1] - half, 1) * wide(sa)
            + pltpu.roll(x, half, 1) * wide(sb))


def _inproj_kernel(*refs, seqs_per_tile, seq_len, latent, cache_layers, aliased):
    x_ref, mod_ref, g1_ref, win_ref, qn_ref, kvn_ref, wuq_ref, wukv_ref = refs[:8]
    refs = refs[8:]
    c_cq = HY_COLS
    c_ck = c_cq + MLA_Q_LORA
    c_ret = c_ck + MLA_KV_LORA + HEAD_LANES
    c_conf = c_ret + RET_COLS
    why_ref, wcq_ref, wck_ref, wret_ref, wconf_ref = (
        win_ref.at[:, a:b] for a, b in ((0, c_cq), (c_cq, c_ck), (c_ck, c_ret), (c_ret, c_conf),
                                        (c_conf, c_conf + CONF_COLS)))
    if latent:
        cos_ref, sa_ref, sb_ref = refs[:3]
        refs = refs[3:]
    if aliased:
        refs = refs[2:]
    hv_ref, hx1_ref, hx2_ref, q_ref, kh_ref, vh_ref, ret_ref, conf_ref, rkt_ref = refs[:9]
    if not latent:
        ckv_ref, kr_ref = refs[9:]
    x = x_ref[...]
    h = _rms(x, g1_ref[...]) * (1.0 + mod_ref[1:2, :]) + mod_ref[0:1, :]
    hb = h.astype(BF16)

    u = _dot(hb, why_ref[...])
    for p, o_ref in enumerate((hv_ref, hx1_ref, hx2_ref)):
        part = u[:, p * HY_WIDTH:(p + 1) * HY_WIDTH]
        if seqs_per_tile == 1:
            o_ref[...] = part
        else:
            for s in range(seqs_per_tile):
                o_ref[:, s * HY_WIDTH:(s + 1) * HY_WIDTH] = part[s * seq_len:(s + 1) * seq_len]

    heads_w = MLA_HEADS * HEAD_LANES
    cq = _dot(hb, wcq_ref[...])
    q = _dot(_rms(cq, qn_ref[...]).astype(BF16), wuq_ref[...])
    ck = _dot(hb, wck_ref[...])
    ckv = _rms(ck[:, :MLA_KV_LORA], kvn_ref[...])
    kr_block = ck[:, MLA_KV_LORA:MLA_KV_LORA + HEAD_LANES]
    if latent:
        q = _rope_lanes(q, cos_ref[...], sa_ref[...], sb_ref[...])
        kr_block = _rope_lanes(kr_block, cos_ref[...], sa_ref[...], sb_ref[...])
    else:
        kr = kr_block[:, MLA_NOPE:MLA_NOPE + MLA_ROPE]
        for s in range(seqs_per_tile):
            rows = slice(s * seq_len, (s + 1) * seq_len)
            for out_ref, val in ((ckv_ref, ckv[rows]), (kr_ref, kr[rows])):
                mine = out_ref.at[s, 0] if cache_layers else out_ref.at[s]
                mine[...] = val
                for other in range(1, cache_layers):
                    out_ref[s, other] = jnp.zeros(out_ref.shape[2:], F32)
    q_ref[...] = (q * ((MLA_NOPE + MLA_ROPE) ** -0.5 * math.log2(math.e))).astype(BF16)
    kv = _dot(ckv.astype(BF16), wukv_ref[...])
    kh_ref[...] = (kv[:, :heads_w] + jnp.concatenate([kr_block] * MLA_HEADS, axis=1)).astype(BF16)
    lane = lax.broadcasted_iota(jnp.int32, (1, heads_w), 1)
    ones_lane = jnp.where(lane % HEAD_LANES == MLA_V, 1.0, 0.0)
    vh_ref[...] = (kv[:, heads_w:] + ones_lane).astype(BF16)
    ret = _dot(hb, wret_ref[...])
    ret_ref[...] = ret
    qk = RET_HEADS * RET_DK
    rkt_ref[...] = ret[:, qk:2 * qk].T
    conf_ref[...] = _dot(hb, wconf_ref[...])


def _inproj(x, mod, lw, B, L, latent, caches_out=None):
    T = B * L
    TM = TOKEN_TILE
    nt = T // TM
    if L >= TM:
        tiles_per_seq, seqs_per_tile = L // TM, 1
        hy_block = (TM, HY_WIDTH)
        hy_map = lambda i: (i % tiles_per_seq, i // tiles_per_seq)
    else:
        tiles_per_seq, seqs_per_tile = 1, TM // L
        hy_block = (L, seqs_per_tile * HY_WIDTH)
        hy_map = lambda i: (0, i)
    row = lambda w: pl.BlockSpec((TM, w), lambda i: (i, 0))
    weights = (lw["norm1_g"], lw["w_in"], lw["mla_q_norm"], lw["mla_kv_norm"], lw["mla_w_uq"], lw["mla_w_ukv"])
    hy_shape = jax.ShapeDtypeStruct((L, B * HY_WIDTH), F32)
    qk = RET_HEADS * RET_DK
    heads_w = MLA_HEADS * HEAD_LANES
    ins = [x, mod, *weights]
    in_specs = ([row(D_MODEL), _mod_spec(lw["layer"], tiles_per_seq, latent)]
                + [_per_layer(w, lw["layer"]) for w in weights])
    out_specs = ([pl.BlockSpec(hy_block, hy_map)] * 3
                 + [row(heads_w)] * 3 + [row(RET_COLS), row(CONF_COLS), pl.BlockSpec((qk, TM), lambda i: (0, i))])
    out_shape = ([hy_shape] * 3 + [jax.ShapeDtypeStruct((T, heads_w), BF16)] * 3
                 + [jax.ShapeDtypeStruct((T, RET_COLS), F32), jax.ShapeDtypeStruct((T, CONF_COLS), F32),
                    jax.ShapeDtypeStruct((qk, T), F32)])
    if latent:
        ins += list(_rope_lane_tables(L))
        in_specs += [pl.BlockSpec((TM, HEAD_LANES), lambda i: (i % tiles_per_seq, 0))] * 3
    aliases = {}
    cache_layers = 0
    if not latent:
        assert tiles_per_seq == 1
        layer, depth = lw["layer"], lw["w_in"].shape[0]
        dims = ((L, MLA_KV_LORA), (L, MLA_ROPE))
        if caches_out is None:
            cache_layers = depth
            out_specs += [pl.BlockSpec((seqs_per_tile, depth) + d, lambda i: (i, 0, 0, 0)) for d in dims]
        else:
            aliases = {len(ins) + k: len(out_shape) + k for k in range(len(dims))}
            ins += list(caches_out)
            in_specs += [pl.BlockSpec(memory_space=pl.ANY)] * len(dims)
            out_specs += [pl.BlockSpec((seqs_per_tile, None) + d, lambda i: (i, layer, 0, 0)) for d in dims]
        out_shape += [jax.ShapeDtypeStruct((B, depth) + d, F32) for d in dims]
    return pl.pallas_call(
        functools.partial(_inproj_kernel, seqs_per_tile=seqs_per_tile, seq_len=L, latent=latent,
                          cache_layers=cache_layers, aliased=bool(aliases)),
        grid=(nt,),
        in_specs=in_specs,
        out_specs=out_specs,
        out_shape=out_shape,
        input_output_aliases=aliases,
        compiler_params=_cparams(("arbitrary",), 48),
        name="inproj",
    )(*ins)


def _dft_tables(L):
    N = 2 * L
    k_lo = min(L, 32)
    k_hi = L // k_lo
    t = jnp.arange(L, dtype=jnp.int32)[None, :]

    def cs(k):
        m = (k[:, None] * t) % N
        ang = m.astype(F32) * (2.0 * math.pi / N)
        return jnp.cos(ang), jnp.sin(ang)

    ca, sa = cs(jnp.arange(k_hi, dtype=jnp.int32) * k_lo)
    cb, sb = cs(jnp.arange(k_lo, dtype=jnp.int32))
    cos = (ca[:, None, :] * cb[None, :, :] - sa[:, None, :] * sb[None, :, :]).reshape(L, L)
    sin = (sa[:, None, :] * cb[None, :, :] + ca[:, None, :] * sb[None, :, :]).reshape(L, L)
    return cos.astype(BF16), (-sin).astype(BF16)


def _filter_features(L):
    t = jnp.linspace(0.0, 1.0, L, dtype=F32)[:, None]
    w = 2.0 * math.pi * jnp.arange(L, dtype=F32)[:, None] / L
    f = jnp.linspace(1e-4, HY_BANDS - 1, HY_BANDS, dtype=F32)[None, :]
    z = jnp.concatenate([t, jnp.cos(f * w), -jnp.sin(f * w)], axis=-1)
    z = jnp.pad(z, ((0, 0), (0, LANES - HY_EMB)))
    max_decay = math.log(HY_TARGET) / HY_FAST_DECAY
    min_decay = math.log(HY_TARGET) / HY_SLOW_DECAY
    deltas = jnp.abs(jnp.linspace(min_decay, max_decay, HY_WIDTH, dtype=F32))
    decay = jnp.exp(-t * deltas[None, :])
    return z, decay


def _alternating(rows):
    t = lax.broadcasted_iota(jnp.int32, (rows, 1), 0)
    return (1 - 2 * (t & 1)).astype(F32)


def _filter_kernel(z_ref, dec_ref, w1_ref, b1_ref, w2_ref, b2_ref, w3_ref, cos_ref, sin_ref,
                   kr_ref, ki_ref, kn_ref, hsum_ref, hdiff_ref, *, L):
    j = pl.program_id(0)
    W = HY_WIDTH
    N = 2 * L

    @pl.when(j == 0)
    def _():
        h = jnp.sin(_dot_exact(z_ref[...], w1_ref[...]) + b1_ref[...])
        h = jnp.sin(_dot_exact(h, w2_ref[...]) + b2_ref[...])
        h = _dot_split(h, w3_ref[...]) * jnp.concatenate([dec_ref[...]] * 4, axis=1)
        cs = jnp.sum(jnp.abs(h), axis=0, keepdims=True)
        s0 = cs[:, 0:W] + cs[:, W:2 * W]
        s1 = cs[:, 2 * W:3 * W] + cs[:, 3 * W:4 * W]
        h = h / jnp.concatenate([s0, s0, s1, s1], axis=1)
        row = lax.broadcasted_iota(jnp.int32, h.shape, 0)
        col = lax.broadcasted_iota(jnp.int32, h.shape, 1)
        backward = (col // W) % 2 == 1
        h = jnp.where(backward & (row == 0), 0.0, h)
        fwd = jnp.concatenate([h[:, 0:W], h[:, 2 * W:3 * W]], axis=1)
        bwd = jnp.concatenate([h[:, W:2 * W], h[:, 3 * W:4 * W]], axis=1)
        hsum_ref[...] = (fwd + bwd).astype(BF16)
        hdiff_ref[...] = (fwd - bwd).astype(BF16)
        nyq = jnp.sum((fwd + bwd) * _alternating(L), axis=0, keepdims=True) * (1.0 / N)
        for o in range(2):
            kn_ref[o] = nyq[:, o * W:(o + 1) * W]

    sr = _dot(cos_ref[...], hsum_ref[...])
    si = _dot(sin_ref[...], hdiff_ref[...])
    row = lax.broadcasted_iota(jnp.int32, (sr.shape[0], W), 0)
    scale = jnp.where((row == 0) & (j == 0), 1.0 / N, 2.0 / N)
    for o in range(2):
        kr_ref[o] = sr[:, o * W:(o + 1) * W] * scale
        ki_ref[o] = si[:, o * W:(o + 1) * W] * scale


def _filter_spectra(lw, L, tables):
    z, decay = _filter_features(L)
    cos, msin = tables
    Tk = min(DFT_TILE, L)
    full = lambda a: pl.BlockSpec(a.shape, lambda j: (0,) * a.ndim)
    mlp = (lw["hy_w1"], lw["hy_b1"], lw["hy_w2"], lw["hy_b2"], lw["hy_w3"])
    ins = (z, decay, *mlp)
    tile = pl.BlockSpec((Tk, L), lambda j: (j, 0))
    spec = pl.BlockSpec((2, Tk, HY_WIDTH), lambda j: (0, j, 0))
    return pl.pallas_call(
        functools.partial(_filter_kernel, L=L),
        grid=(L // Tk,),
        in_specs=[full(z), full(decay)] + [_per_layer(a, lw["layer"]) for a in mlp] + [tile, tile],
        out_specs=[spec, spec, pl.BlockSpec((2, 1, HY_WIDTH), lambda j: (0, 0, 0))],
        out_shape=[jax.ShapeDtypeStruct((2, L, HY_WIDTH), F32)] * 2
        + [jax.ShapeDtypeStruct((2, 1, HY_WIDTH), F32)],
        scratch_shapes=[pltpu.VMEM((L, 2 * HY_WIDTH), BF16)] * 2,
        compiler_params=_cparams(("arbitrary",), 48),
        name="hyena_filter",
    )(*ins, cos, msin)


def _hyena_kernel(v_ref, x1_ref, x2_ref, cw_ref, cb_ref, bias_ref, kr_ref, ki_ref, kn_ref,
                  cos_ref, sin_ref, y_ref, cur_ref, curb_ref, yr_ref, yi_ref, gate_ref,
                  *, L, W, T):
    reps = W // HY_WIDTH
    tiled = lambda a: jnp.concatenate([a] * reps, axis=1) if reps > 1 else a
    alt = _alternating(L)

    def short_conv(u_ref, p):
        u = u_ref[...]
        row = lax.broadcasted_iota(jnp.int32, (L, W), 0)
        prev = jnp.where(row == 0, 0.0, pltpu.roll(u, 1, 0))
        nxt = jnp.where(row == L - 1, 0.0, pltpu.roll(u, L - 1, 0))
        cols = slice(p * HY_WIDTH, (p + 1) * HY_WIDTH)
        w = [tiled(cw_ref[k:k + 1, cols]) for k in range(3)]
        return prev * w[0] + u * w[1] + nxt * w[2] + tiled(cb_ref[:, cols])

    cur_ref[...] = short_conv(v_ref, 0)
    for o, x_ref in enumerate((x1_ref, x2_ref)):
        cur = cur_ref[...]
        curb_ref[...] = cur.astype(BF16)
        gate_ref[...] = short_conv(x_ref, o + 1)
        nyq = jnp.sum(cur * alt, axis=0, keepdims=True) * tiled(kn_ref[o])
        for f in range(L // T):
            rows = slice(f * T, (f + 1) * T)
            cb = curb_ref[...]
            xr = _dot(cos_ref[rows, :], cb)
            xi = _dot(sin_ref[rows, :], cb)
            kr, ki = tiled(kr_ref[o, rows, :]), tiled(ki_ref[o, rows, :])
            yr_ref[rows, :] = (xr * kr - xi * ki).astype(BF16)
            yi_ref[rows, :] = (xr * ki + xi * kr).astype(BF16)
        for t in range(L // T):
            rows = slice(t * T, (t + 1) * T)
            conv = (_dot(cos_ref[rows, :], yr_ref[...]) + _dot(sin_ref[rows, :], yi_ref[...])
                    + alt[rows] * nyq + cur_ref[rows, :] * tiled(bias_ref[o:o + 1, :]))
            out = gate_ref[rows, :] * conv
            if o == 0:
                cur_ref[rows, :] = out
            else:
                y_ref[rows, :] = out


def _hyena(hv, hx1, hx2, lw, spectra, tables, B, L):
    kr, ki, kn = spectra
    cos, msin = tables
    T = min(HYENA_ROWS, L)
    W = HY_WIDTH * max(1, min(B, HYENA_GROUP_ROWS // L))
    ng = (B * HY_WIDTH) // W
    col = pl.BlockSpec((L, W), lambda g: (0, g))
    once = lambda a: pl.BlockSpec(a.shape, lambda g: (0,) * a.ndim, pipeline_mode=pl.Buffered(1))
    params = (lw["hy_conv_w"], lw["hy_conv_b"], lw["hy_bias"])
    consts = (kr, ki, kn, cos, msin)
    return pl.pallas_call(
        functools.partial(_hyena_kernel, L=L, W=W, T=T),
        grid=(ng,),
        in_specs=[col, col, col] + [_per_layer(a, lw["layer"]) for a in params] + [once(a) for a in consts],
        out_specs=col,
        out_shape=jax.ShapeDtypeStruct((L, B * HY_WIDTH), F32),
        scratch_shapes=[pltpu.VMEM((L, W), F32), pltpu.VMEM((L, W), BF16), pltpu.VMEM((L, W), BF16),
                        pltpu.VMEM((L, W), BF16), pltpu.VMEM((L, W), F32)],
        compiler_params=_cparams(("arbitrary",), 60),
        name="hyena_conv",
    )(hv, hx1, hx2, *params, *consts)


def _mla_kernel(*refs, L, latent, seqs):
    if latent:
        q_ref, kh_ref, vh_ref, cckv_ref, ckr_ref, wukv_ref, o_ref, s_ref, p_ref, kctx_ref, vctx_ref = refs
    else:
        q_ref, kh_ref, vh_ref, o_ref, s_ref, p_ref = refs
    TQ = q_ref.shape[0] // seqs
    heads_w = MLA_HEADS * HEAD_LANES
    block = lambda h: slice(h * HEAD_LANES, (h + 1) * HEAD_LANES)

    if latent:
        @pl.when(pl.program_id(1) == 0)
        def _():
            kv = _dot(cckv_ref[...].astype(BF16), wukv_ref[...])
            zeros = lambda w: jnp.zeros((PAST_LEN, w), F32)
            kr_block = jnp.concatenate(
                [zeros(MLA_NOPE), ckr_ref[...], zeros(HEAD_LANES - MLA_NOPE - MLA_ROPE)], axis=1)
            kctx_ref[...] = (kv[:, :heads_w] + jnp.concatenate([kr_block] * MLA_HEADS, axis=1)).astype(BF16)
            lane = lax.broadcasted_iota(jnp.int32, (1, heads_w), 1)
            vctx_ref[...] = (kv[:, heads_w:] + jnp.where(lane % HEAD_LANES == MLA_V, 1.0, 0.0)).astype(BF16)

    def attend(q_rows, k_rows):
        def scores(h):
            qh = q_ref[q_rows, block(h)]
            s_ref[h, :, 0:L] = _dot_nt(qh, kh_ref[k_rows, block(h)])
            if latent:
                s_ref[h, :, L:L + PAST_LEN] = _dot_nt(qh, kctx_ref[:, block(h)])

        outs = []
        scores(0)
        for h in range(MLA_HEADS):
            if h + 1 < MLA_HEADS:
                scores(h + 1)
            for r in range(TQ // MLA_SOFTMAX_ROWS):
                rows = slice(r * MLA_SOFTMAX_ROWS, (r + 1) * MLA_SOFTMAX_ROWS)
                s = s_ref[h, rows, :]
                p_ref[h, rows, :] = jnp.exp2(s - jnp.max(s, axis=-1, keepdims=True)).astype(BF16)
            pv = _dot(p_ref[h, :, 0:L], vh_ref[k_rows, block(h)])
            if latent:
                pv = pv + _dot(p_ref[h, :, L:L + PAST_LEN], vctx_ref[:, block(h)])
            outs.append(pv[:, :MLA_V] / pv[:, MLA_V:MLA_V + 1])
        o_ref[q_rows, :] = jnp.concatenate(outs, axis=-1)

    for sq in range(seqs):
        attend(slice(sq * TQ, (sq + 1) * TQ), slice(sq * L, (sq + 1) * L))


def _rope_tables(L):
    rows = L // GRID_W
    row = jnp.repeat(jnp.arange(rows), GRID_W).astype(F32)
    col = jnp.tile(jnp.arange(GRID_W), rows).astype(F32)
    per_axis = MLA_ROPE // 4
    inv = ROPE_BASE ** (-jnp.arange(per_axis, dtype=F32) / per_axis)
    ang = jnp.concatenate([row[:, None] * inv, col[:, None] * inv], axis=-1)
    return jnp.cos(ang), jnp.sin(ang)


def _rope_lane_tables(L):
    cos, sin = _rope_tables(L)
    pad = HEAD_LANES - MLA_NOPE - MLA_ROPE
    one, zero = jnp.ones((L, MLA_NOPE), F32), jnp.zeros((L, MLA_NOPE), F32)
    half0 = jnp.zeros_like(sin)
    cos_t = jnp.concatenate([one, cos, cos, one[:, :pad]], axis=1)
    sa_t = jnp.concatenate([zero, -sin, half0, zero[:, :pad]], axis=1)
    sb_t = jnp.concatenate([zero, half0, sin, zero[:, :pad]], axis=1)
    return cos_t, sa_t, sb_t


def _mla(qh, kh, vh, lw, B, L, ctx):
    latent = ctx is not None
    TQ = min(Q_TILE, L)
    nq = L // TQ
    seqs = 1 if latent or nq > 1 else _seqs_per_step(B, L)
    Lk = L + PAST_LEN if latent else L
    heads_w = MLA_HEADS * HEAD_LANES
    seq = pl.BlockSpec((seqs * L, heads_w), lambda b, i: (b, 0))
    ins = [qh, kh, vh]
    specs = [pl.BlockSpec((seqs * TQ, heads_w), lambda b, i: (b * nq + i, 0)), seq, seq]
    scratch = [pltpu.VMEM((MLA_HEADS, TQ, Lk), F32), pltpu.VMEM((MLA_HEADS, TQ, Lk), BF16)]
    if latent:
        ins += [ctx[0], ctx[1], lw["mla_w_ukv"]]
        layer = lw["layer"]
        specs += [pl.BlockSpec((None, None, PAST_LEN, MLA_KV_LORA), lambda b, i: (b, layer, 0, 0)),
                  pl.BlockSpec((None, None, PAST_LEN, MLA_ROPE), lambda b, i: (b, layer, 0, 0)),
                  _per_layer(lw["mla_w_ukv"], lw["layer"])]
        scratch += [pltpu.VMEM((PAST_LEN, heads_w), BF16)] * 2
    return pl.pallas_call(
        functools.partial(_mla_kernel, L=L, latent=latent, seqs=seqs),
        grid=(B // seqs, nq),
        in_specs=specs,
        out_specs=pl.BlockSpec((seqs * TQ, MLA_HEADS * MLA_V), lambda b, i: (b * nq + i, 0)),
        out_shape=jax.ShapeDtypeStruct((B * L, MLA_HEADS * MLA_V), F32),
        scratch_shapes=scratch,
        compiler_params=_cparams(("arbitrary", "arbitrary"), 48),
        name="mla_attention",
    )(*ins)


def _ret_kernel(*refs, L, seqs, has_state, emit_state, state_layers, aliased):
    refs = list(refs)
    u_ref, kt_ref, dl_ref = refs[:3]
    pos = 3
    s0_ref = None
    if has_state:
        s0_ref = refs[pos]
        pos += 1
    if aliased:
        pos += 1
    y_ref = refs[pos]
    pos += 1
    sout_ref = None
    if emit_state:
        sout_ref = refs[pos]
        pos += 1
    of_ref, kv_ref, sin_ref, dcomb_ref, dtab_ref, hmask_ref, avg_ref = refs[pos:]

    C = RET_BLOCK
    n = L // C
    H, DK, DV = RET_HEADS, RET_DK, RET_DV
    qk = H * DK
    DK_F, DK_B, DQ_F, DQ_B, DC_F, DC_B, MASK = range(7)
    use_cross = has_state or n > 1

    @pl.when(pl.program_id(0) == 0)
    def _():
        x = dl_ref[...]
        log_g = jnp.minimum(x, 0.0) - jnp.log1p(jnp.exp(-jnp.abs(x)))
        gf = [log_g[0:1, h:h + 1] for h in range(H)]
        gb = [log_g[1:2, h:h + 1] for h in range(H)]
        diff = (lax.broadcasted_iota(jnp.int32, (C, C), 0)
                - lax.broadcasted_iota(jnp.int32, (C, C), 1)).astype(F32)
        idx = lax.broadcasted_iota(jnp.int32, (C, 1), 0).astype(F32)
        for h in range(H):
            dcomb_ref[h] = (jnp.where(diff >= 0, jnp.exp(jnp.maximum(diff, 0.0) * gf[h]), 0.0)
                            + jnp.where(diff <= 0, jnp.exp(jnp.maximum(-diff, 0.0) * gb[h]), 0.0))

        def head_cols(lag, g):
            return jnp.concatenate([jnp.broadcast_to(jnp.exp(lag * g[h]), (C, DK)) for h in range(H)], axis=1)

        def head_rows(lag, g):
            return jnp.concatenate(
                [jnp.broadcast_to(jnp.exp(lag * g[h]) * (DK ** -0.5), (DK, C)) for h in range(H)], axis=0)

        tok = lax.broadcasted_iota(jnp.int32, (1, C), 1).astype(F32)
        dtab_ref[DK_F] = head_rows(C - 1.0 - tok, gf)
        dtab_ref[DK_B] = head_rows(tok, gb)
        dtab_ref[DQ_F] = head_cols(idx + 1.0, gf)
        dtab_ref[DQ_B] = head_cols(C - idx, gb)
        row_head = lax.broadcasted_iota(jnp.int32, (qk, qk), 0) // DK
        col_head = lax.broadcasted_iota(jnp.int32, (qk, qk), 1) // DV
        same_head = row_head == col_head
        for slot, g in ((DC_F, gf), (DC_B, gb)):
            dc = jnp.zeros((qk, qk), F32)
            for h in range(H):
                dc = jnp.where(same_head & (row_head == h), jnp.exp(C * g[h]), dc)
            dtab_ref[slot] = dc
        dtab_ref[MASK] = same_head.astype(F32)
        avg_ref[...] = jnp.where(same_head, 1.0 / DV, 0.0).astype(BF16)
        lane_head = lax.broadcasted_iota(jnp.int32, (C, qk), 1) // DK
        for h in range(H):
            hmask_ref[h] = jnp.where(lane_head == h, 1.0, 0.0).astype(BF16)

    def head_mean(a):
        total = None
        for _ in range(3):
            part = a.astype(BF16)
            a = a - part.astype(F32)
            term = _dot(part, avg_ref[...])
            total = term if total is None else total + term
        return total

    for sq in range(seqs):
        chunk = lambda c: slice(sq * L + c * C, sq * L + (c + 1) * C)

        for c in range(n):
            rows = chunk(c)
            qb = u_ref[rows, 0:qk].astype(BF16)
            kb = (u_ref[rows, qk:2 * qk] * (DK ** -0.5)).astype(BF16)
            vb = u_ref[rows, 2 * qk:2 * qk + H * DV].astype(BF16)
            att = [(_dot_nt(qb * hmask_ref[h], kb) * dcomb_ref[h]).astype(BF16) for h in range(H)]
            v_heads = jnp.concatenate([vb * hmask_ref[h] for h in range(H)], axis=0)
            of_ref[c * C:(c + 1) * C, :] = _dot(jnp.concatenate(att, axis=1), v_heads)
            kt = kt_ref[:, rows]
            for d, slot in ((0, DK_F), (1, DK_B)):
                kv_ref[d, c] = _dot((kt * dtab_ref[slot]).astype(BF16), vb)

        finals = []
        for d, slot in ((0, DC_F), (1, DC_B)):
            if has_state:
                zero = jnp.zeros((DK, DV), F32)
                S = jnp.concatenate(
                    [jnp.concatenate([s0_ref[sq, d, h] if g == h else zero for g in range(H)], axis=1)
                     for h in range(H)], axis=0)
            else:
                S = jnp.zeros((qk, H * DV), F32)
            for c in (range(n) if d == 0 else reversed(range(n))):
                if use_cross:
                    sin_ref[d, c] = S.astype(BF16)
                S = S * dtab_ref[slot] + kv_ref[d, c] * dtab_ref[MASK]
            finals.append(S)
        if emit_state:
            mine = sout_ref.at[sq, 0] if state_layers else sout_ref.at[sq]
            for d in range(2):
                for h in range(H):
                    mine[d, h] = finals[d][h * DK:(h + 1) * DK, h * DV:(h + 1) * DV]
            for other in range(1, state_layers):
                sout_ref[sq, other] = jnp.zeros(sout_ref.shape[2:], F32)

        for c in range(n):
            rows = chunk(c)
            tot = of_ref[c * C:(c + 1) * C, :]
            if use_cross:
                qb = u_ref[rows, 0:qk].astype(BF16)
                tot = (tot + _dot(qb, sin_ref[0, c]) * dtab_ref[DQ_F]
                       + _dot(qb, sin_ref[1, c]) * dtab_ref[DQ_B])
            xc = tot - head_mean(tot)
            normed = xc * lax.rsqrt(head_mean(xc * xc) + EPS)
            gate = u_ref[rows, 2 * qk + H * DV:2 * qk + 2 * H * DV]
            y_ref[rows, :] = _silu(gate) * normed


def _retention(u_ret, kt_ret, lw, B, L, state, emit_state, states_out=None):
    has_state = state is not None
    seqs = _seqs_per_step(B, L)
    st_dims = (2, RET_HEADS, RET_DK, RET_DV)
    ins = [u_ret, kt_ret, lw["ret_decay"]]
    specs = [pl.BlockSpec((seqs * L, RET_COLS), lambda b: (b, 0)),
             pl.BlockSpec((RET_HEADS * RET_DK, seqs * L), lambda b: (0, b)),
             _per_layer(lw["ret_decay"], lw["layer"])]
    if has_state:
        layer = lw["layer"]
        ins.append(state)
        specs.append(pl.BlockSpec((seqs, None) + st_dims, lambda b: (b, layer, 0, 0, 0, 0)))
    vd = RET_HEADS * RET_DV
    out_specs = [pl.BlockSpec((seqs * L, vd), lambda b: (b, 0))]
    out_shape = [jax.ShapeDtypeStruct((B * L, vd), F32)]
    aliases = {}
    state_layers = 0
    if emit_state:
        layer, depth = lw["layer"], lw["ret_decay"].shape[0]
        if states_out is None:
            state_layers = depth
            out_specs.append(pl.BlockSpec((seqs, depth) + st_dims, lambda b: (b, 0, 0, 0, 0, 0)))
        else:
            aliases = {len(ins): 1}
            ins.append(states_out)
            specs.append(pl.BlockSpec(memory_space=pl.ANY))
            out_specs.append(pl.BlockSpec((seqs, None) + st_dims, lambda b: (b, layer, 0, 0, 0, 0)))
        out_shape.append(jax.ShapeDtypeStruct((B, depth) + st_dims, F32))
    res = pl.pallas_call(
        functools.partial(_ret_kernel, L=L, seqs=seqs, has_state=has_state, emit_state=emit_state,
                          state_layers=state_layers, aliased=bool(aliases)),
        grid=(B // seqs,),
        in_specs=specs,
        out_specs=out_specs,
        out_shape=out_shape,
        input_output_aliases=aliases,
        scratch_shapes=[pltpu.VMEM((L, vd), F32),
                        pltpu.VMEM((2, L // RET_BLOCK, vd, vd), F32),
                        pltpu.VMEM((2, L // RET_BLOCK, vd, vd), BF16),
                        pltpu.VMEM((RET_HEADS, RET_BLOCK, RET_BLOCK), F32),
                        pltpu.VMEM((7, RET_BLOCK, vd), F32),
                        pltpu.VMEM((RET_HEADS, RET_BLOCK, vd), BF16),
                        pltpu.VMEM((vd, vd), BF16)],
        compiler_params=_cparams(("arbitrary",), 48),
        name="retention",
    )(*ins)
    return (res[0], res[1]) if emit_state else (res[0], None)


def _conf_kernel(u_ref, w_ref, b_ref, g_ref, be_ref, y_ref, zp_ref, sh_ref, *, L, seqs):
    Wc = CONF_WIDTH
    halo = CONF_HALO
    zp_ref[0:halo, :] = jnp.zeros((halo, Wc), F32)
    zp_ref[halo + L:2 * halo + L, :] = jnp.zeros((halo, Wc), F32)
    first = halo - CONF_KERNEL // 2
    span = sh_ref.shape[1]
    R = CONF_ROWS
    for sq in range(seqs):
        r0 = sq * L
        zp_ref[halo:halo + L, :] = u_ref[r0:r0 + L, 0:Wc] * _sigmoid(u_ref[r0:r0 + L, Wc:2 * Wc])
        for s in range(SUBLANES):
            sh_ref[s] = zp_ref[first + s:first + s + span, :]
        for c in range(L // R):
            acc = jnp.broadcast_to(b_ref[...], (R, Wc))
            for k in range(CONF_KERNEL):
                a, s = divmod(k, SUBLANES)
                acc = acc + w_ref[k:k + 1, :] * sh_ref[s, c * R + SUBLANES * a:c * R + SUBLANES * a + R, :]
            mu = jnp.mean(acc, axis=-1, keepdims=True)
            xc = acc - mu
            z = xc * lax.rsqrt(jnp.mean(xc * xc, axis=-1, keepdims=True) + EPS) * g_ref[...] + be_ref[...]
            y_ref[r0 + c * R:r0 + (c + 1) * R, :] = _silu(z)


def _conformer(u_conf, lw, B, L):
    ws = (lw["conf_dw_w"], lw["conf_dw_b"], lw["conf_ln_g"], lw["conf_ln_b"])
    seqs = _seqs_per_step(B, L)
    return pl.pallas_call(
        functools.partial(_conf_kernel, L=L, seqs=seqs),
        grid=(B // seqs,),
        in_specs=[pl.BlockSpec((seqs * L, CONF_COLS), lambda b: (b, 0))]
        + [_per_layer(w, lw["layer"]) for w in ws],
        out_specs=pl.BlockSpec((seqs * L, CONF_WIDTH), lambda b: (b, 0)),
        out_shape=jax.ShapeDtypeStruct((B * L, CONF_WIDTH), F32),
        scratch_shapes=[pltpu.VMEM((L + 2 * CONF_HALO, CONF_WIDTH), F32),
                        pltpu.VMEM((SUBLANES, L + SUBLANES * ((CONF_KERNEL - 1) // SUBLANES), CONF_WIDTH), F32)],
        compiler_params=_cparams(("arbitrary",), 48),
        name="conformer",
    )(u_conf, *ws)


def _merge_kernel(x_ref, mod_ref, g1_ref, yhy_ref, ymla_ref, yret_ref, yconf_ref,
                  gw_ref, gb_ref, why_ref, wmla_ref, wret_ref, wconf_ref, wo_ref, o_ref, *, seqs_per_tile):
    x = x_ref[...]
    h = _rms(x, g1_ref[...]) * (1.0 + mod_ref[1:2, :]) + mod_ref[0:1, :]
    hb = h.astype(BF16)
    if seqs_per_tile == 1:
        yhy = yhy_ref[...]
    else:
        yhy = jnp.concatenate(
            [yhy_ref[:, s * HY_WIDTH:(s + 1) * HY_WIDTH] for s in range(seqs_per_tile)], axis=0)
    branches = ((yhy, why_ref), (ymla_ref[...], wmla_ref), (yret_ref[...], wret_ref), (yconf_ref[...], wconf_ref))
    D = D_MODEL
    merged = None
    for i, (y, w_ref) in enumerate(branches):
        gate = _sigmoid(_dot(hb, gw_ref[:, i * D:(i + 1) * D]) + gb_ref[:, i * D:(i + 1) * D])
        term = gate * _dot(y.astype(BF16), w_ref[...])
        merged = term if merged is None else merged + term
    o_ref[...] = x + mod_ref[2:3, :] * _dot(merged.astype(BF16), wo_ref[...])


def _merge(x, mod, lw, y_hy, y_mla, y_ret, y_conf, B, L, latent):
    T = B * L
    TM = WIDE_TOKEN_TILE
    if L >= TM:
        tiles_per_seq, seqs_per_tile = L // TM, 1
        hy_spec = pl.BlockSpec((TM, HY_WIDTH), lambda i: (i % tiles_per_seq, i // tiles_per_seq))
    else:
        tiles_per_seq, seqs_per_tile = 1, TM // L
        hy_spec = pl.BlockSpec((L, seqs_per_tile * HY_WIDTH), lambda i: (0, i))
    once = lambda a: _per_layer(a, lw["layer"], single_buffer=True)
    row = lambda w: pl.BlockSpec((TM, w), lambda i: (i, 0))
    ws = (lw["gate_w"], lw["gate_b"], lw["hy_out"], lw["mla_out"], lw["ret_out"], lw["conf_out"], lw["w_o"])
    return pl.pallas_call(
        functools.partial(_merge_kernel, seqs_per_tile=seqs_per_tile),
        grid=(T // TM,),
        in_specs=[row(D_MODEL), _mod_spec(lw["layer"], tiles_per_seq, latent),
                  once(lw["norm1_g"]), hy_spec, row(MLA_HEADS * MLA_V), row(RET_HEADS * RET_DV),
                  row(CONF_WIDTH)] + [once(w) for w in ws],
        out_specs=row(D_MODEL),
        out_shape=jax.ShapeDtypeStruct((T, D_MODEL), F32),
        compiler_params=_cparams(("arbitrary",), 56),
        name="merge",
    )(x, mod, lw["norm1_g"], y_hy, y_mla, y_ret, y_conf, *ws)


def _ffn_kernel(x_ref, mod_ref, g2_ref, w1_ref, w2_ref, fg_ref, o_ref, *, final):
    x = x_ref[...]
    h2 = (_rms(x, g2_ref[...]) * (1.0 + mod_ref[4:5, :]) + mod_ref[3:4, :]).astype(BF16)
    acc = None
    for c0 in range(0, D_FF, FFN_CHUNK):
        c1 = min(c0 + FFN_CHUNK, D_FF)
        a = _dot(h2, w1_ref[:, c0:c1])
        b = _dot(h2, w1_ref[:, D_FF + c0:D_FF + c1])
        part = _dot((_silu(a) * b).astype(BF16), w2_ref[c0:c1, :])
        acc = part if acc is None else acc + part
    out = x + mod_ref[5:6, :] * acc
    if final:
        out = _rms(out, fg_ref[...])
    o_ref[...] = out


def _ffn(x, mod, lw, final_g, B, L, latent, final):
    T = B * L
    TM = WIDE_TOKEN_TILE
    tiles_per_seq = max(L // TM, 1)
    once = lambda a: _per_layer(a, lw["layer"], single_buffer=True)
    row = pl.BlockSpec((TM, D_MODEL), lambda i: (i, 0))
    ws = (lw["norm2_g"], lw["ffn_w1"], lw["ffn_w2"])
    return pl.pallas_call(
        functools.partial(_ffn_kernel, final=final),
        grid=(T // TM,),
        in_specs=[row, _mod_spec(lw["layer"], tiles_per_seq, latent)]
        + [once(w) for w in ws] + [pl.BlockSpec(final_g.shape, lambda i: (0, 0))],
        out_specs=row,
        out_shape=jax.ShapeDtypeStruct((T, D_MODEL), F32),
        compiler_params=_cparams(("arbitrary",), 56),
        name="ffn",
    )(x, mod, *ws, final_g)


def _trunk_layer(x, mod, lw, final_g, tables, B, L, ctx, state, final, caches_out=None, states_out=None):
    latent = ctx is not None
    hv, hx1, hx2, qh, kh, vh, u_ret, u_conf, kt_ret, *cache = _inproj(x, mod, lw, B, L, latent, caches_out)
    ckv, kr = cache if cache else (None, None)
    spectra = _filter_spectra(lw, L, tables)
    y_hy = _hyena(hv, hx1, hx2, lw, spectra, tables, B, L)
    y_mla = _mla(qh, kh, vh, lw, B, L, ctx)
    y_ret, S = _retention(u_ret, kt_ret, lw, B, L, state, emit_state=not latent, states_out=states_out)
    y_conf = _conformer(u_conf, lw, B, L)
    x = _merge(x, mod, lw, y_hy, y_mla, y_ret, y_conf, B, L, latent)
    x = _ffn(x, mod, lw, final_g, B, L, latent, final)
    return x, ckv, kr, S


def _regroup_kernel(wt_ref, o_ref):
    d = wt_ref.shape[1]
    rope_key = HY_COLS + MLA_COLS - MLA_ROPE

    def move(dst, src, n):
        for c in range(0, n, REGROUP_COLS):
            m = min(REGROUP_COLS, n - c)
            o_ref[:, dst + c:dst + c + m] = wt_ref[src + c:src + c + m, :].T.astype(BF16)

    move(0, 0, rope_key)
    slab = pltpu.roll(wt_ref[rope_key:rope_key + HEAD_LANES, :].T, MLA_NOPE, 1)
    lane = lax.broadcasted_iota(jnp.int32, (d, HEAD_LANES), 1)
    rope_lanes = (lane >= MLA_NOPE) & (lane < MLA_NOPE + MLA_ROPE)
    o_ref[:, rope_key:rope_key + HEAD_LANES] = jnp.where(rope_lanes, slab, 0.0).astype(BF16)
    move(rope_key + HEAD_LANES, rope_key + MLA_ROPE, wt_ref.shape[0] - rope_key - MLA_ROPE)


def _regroup_w_in(w_in):
    depth, d, cols = w_in.shape
    out_cols = cols - MLA_ROPE + HEAD_LANES
    return pl.pallas_call(
        _regroup_kernel,
        grid=(depth,),
        in_specs=[pl.BlockSpec((None, cols, d), lambda l: (l, 0, 0))],
        out_specs=pl.BlockSpec((None, d, out_cols), lambda l: (l, 0, 0)),
        out_shape=jax.ShapeDtypeStruct((depth, d, out_cols), BF16),
        compiler_params=_cparams(("arbitrary",), 48),
        name="regroup_w_in",
    )(jnp.swapaxes(w_in, 1, 2))


def _stacked_weights(w_in, p):
    depth = w_in.shape[0]
    w_in_all = _regroup_w_in(w_in)
    dq = MLA_NOPE + MLA_ROPE
    head_pad = lambda a: jnp.pad(a, ((0, 0),) * 3 + ((0, HEAD_LANES - a.shape[3]),)).reshape(
        depth, a.shape[1], MLA_HEADS * HEAD_LANES)
    w_uq = head_pad(p["mla_w_uq"].reshape(depth, MLA_Q_LORA, MLA_HEADS, dq))
    w_ukv = p["mla_w_ukv"].reshape(depth, MLA_KV_LORA, MLA_HEADS, MLA_NOPE + MLA_V)
    w_ukv = jnp.concatenate([head_pad(w_ukv[..., :MLA_NOPE]), head_pad(w_ukv[..., MLA_NOPE:])], axis=2)
    row = lambda name: p[name].reshape(depth, 1, -1)
    rows = ("norm1_g", "norm2_g", "mla_q_norm", "mla_kv_norm", "hy_conv_b", "hy_b1", "hy_b2",
            "conf_dw_b", "conf_ln_g", "conf_ln_b", "gate_b")
    as_is = ("hy_conv_w", "hy_w2", "hy_w3", "hy_bias", "ret_decay", "conf_dw_w")
    bf16 = ("gate_w", "hy_out", "mla_out", "ret_out", "conf_out", "w_o", "ffn_w1", "ffn_w2")
    return {
        "w_in": w_in_all,
        "mla_w_uq": w_uq.astype(BF16), "mla_w_ukv": w_ukv.astype(BF16),
        "hy_w1": jnp.pad(p["hy_w1"], ((0, 0), (0, LANES - HY_EMB), (0, 0))),
        **{name: row(name) for name in rows},
        **{name: p[name] for name in as_is},
        **{name: p[name].astype(BF16) for name in bf16},
    }


def kernel(x_prompt, x_sample, cache_mla_ckv, cache_mla_krope, state_ret, c, c_ctx, ada_w, ada_b, norm1_g, w_in, hy_conv_w, hy_conv_b, hy_w1, hy_b1, hy_w2, hy_b2, hy_w3, hy_bias, hy_out, mla_q_norm, mla_w_uq, mla_kv_norm, mla_w_ukv, mla_out, ret_decay, ret_out, conf_dw_w, conf_dw_b, conf_ln_g, conf_ln_b, conf_out, gate_w, gate_b, w_o, norm2_g, ffn_w1, ffn_w2, final_norm_g):
    p = dict(norm1_g=norm1_g, hy_conv_w=hy_conv_w, hy_conv_b=hy_conv_b, hy_w1=hy_w1, hy_b1=hy_b1,
             hy_w2=hy_w2, hy_b2=hy_b2, hy_w3=hy_w3, hy_bias=hy_bias, hy_out=hy_out,
             mla_q_norm=mla_q_norm, mla_w_uq=mla_w_uq, mla_kv_norm=mla_kv_norm, mla_w_ukv=mla_w_ukv,
             mla_out=mla_out, ret_decay=ret_decay, ret_out=ret_out, conf_dw_w=conf_dw_w,
             conf_dw_b=conf_dw_b, conf_ln_g=conf_ln_g, conf_ln_b=conf_ln_b, conf_out=conf_out,
             gate_w=gate_w, gate_b=gate_b, w_o=w_o, norm2_g=norm2_g, ffn_w1=ffn_w1, ffn_w2=ffn_w2)
    Bp, Lp, D = x_prompt.shape
    Bs, Ls, _ = x_sample.shape
    depth = w_in.shape[0]

    cond = jnp.concatenate([c_ctx[None, :], c, jnp.zeros((MOD_ROWS - 1 - Bs, D), F32)], axis=0)
    mod = _modulation(cond, ada_w, ada_b)
    tables_p = _dft_tables(Lp)
    tables_s = _dft_tables(Ls)
    final_g = final_norm_g.reshape(1, D)

    xp = x_prompt.reshape(Bp * Lp, D)
    xs = x_sample.reshape(Bs * Ls, D)
    caches, states = None, None
    weights = _stacked_weights(w_in, p)
    for l in range(depth):
        lw = dict(weights, layer=l)
        final = l == depth - 1
        xp, ckv, kr, states = _trunk_layer(xp, mod, lw, final_g, tables_p, Bp, Lp, None, None, final,
                                           caches, states)
        caches = (ckv, kr)
        xs, _, _, _ = _trunk_layer(xs, mod, lw, final_g, tables_s, Bs, Ls,
                                   (cache_mla_ckv, cache_mla_krope), state_ret, final)
    return (xp.reshape(Bp, Lp, D), xs.reshape(Bs, Ls, D), *caches, states)
```

```python
import functools
import math

import jax
import jax.numpy as jnp
from jax import lax
from jax.experimental import pallas as pl
from jax.experimental.pallas import tpu as pltpu

F32 = jnp.float32
BF16 = jnp.bfloat16

D_MODEL = 1024
DEPTH = 2
PAST_LEN = 256
EPS = 1e-6
GRID_W = 64

HY_WIDTH = 256
HY_EMB = 33
HY_BANDS = (HY_EMB - 1) // 2
HY_FFN = 64
HY_FAST_DECAY = 0.3
HY_SLOW_DECAY = 1.5
HY_TARGET = 1e-2

MLA_HEADS = 4
MLA_Q_LORA = 256
MLA_KV_LORA = 128
MLA_NOPE = 64
MLA_ROPE = 32
MLA_V = 64
ROPE_BASE = 10000.0

RET_HEADS = 4
RET_DK = 64
RET_DV = 64
RET_BLOCK = 256

CONF_WIDTH = 256
CONF_KERNEL = 31

D_FF = ((8 * D_MODEL // 3 + 255) // 256) * 256

HY_COLS = 3 * HY_WIDTH
MLA_COLS = MLA_Q_LORA + MLA_KV_LORA + MLA_ROPE
RET_COLS = 2 * RET_HEADS * RET_DK + 2 * RET_HEADS * RET_DV
CONF_COLS = 2 * CONF_WIDTH

VMEM_BYTES_V7X = 64 * 1024 * 1024
SUBLANES = 8
LANES = 128
TOKEN_TILE = 1024
WIDE_TOKEN_TILE = 1024
Q_TILE = 512
MLA_SOFTMAX_ROWS = 16
DFT_TILE = 512
SHORT_SEQ_ROWS = 1024
HYENA_GROUP_ROWS = 1024
HYENA_ROWS = 1024
CONF_ROWS = 128
REGROUP_COLS = 128
CONF_HALO = 16
MXU_DIM_V7X = 256
FFN_CHUNK = 4 * MXU_DIM_V7X
MOD_ROWS = 8


def _cparams(sem, vmem_mb):
    vmem_bytes = vmem_mb * 1024 * 1024
    assert vmem_bytes < VMEM_BYTES_V7X
    return pltpu.CompilerParams(dimension_semantics=sem, vmem_limit_bytes=vmem_bytes)


def _dot(a, b):
    return jnp.dot(a, b, preferred_element_type=F32)


def _dot_nt(a, b):
    return lax.dot_general(a, b, (((1,), (1,)), ((), ())), preferred_element_type=F32)


def _dot_exact(a, b):
    return jnp.dot(a, b, preferred_element_type=F32, precision=lax.Precision.HIGHEST)


def _dot_split(a, b):
    a_hi, b_hi = a.astype(BF16), b.astype(BF16)
    a_lo = (a - a_hi.astype(F32)).astype(BF16)
    b_lo = (b - b_hi.astype(F32)).astype(BF16)
    return _dot(a_hi, b_hi) + (_dot(a_lo, b_hi) + _dot(a_hi, b_lo))


def _rms(x, g):
    return x * lax.rsqrt(jnp.mean(x * x, axis=-1, keepdims=True) + EPS) * g


def _sigmoid(x):
    return 0.5 * jnp.tanh(0.5 * x) + 0.5


def _silu(x):
    return x * _sigmoid(x)


def _mod_kernel(c_ref, w_ref, b_ref, o_ref):
    s = _silu(c_ref[...]).astype(BF16)
    o_ref[...] = _dot(s, w_ref[...].astype(BF16)) + b_ref[...]


def _modulation(cond, ada_w, ada_b):
    depth, d, cols = ada_w.shape
    blk = 1024
    out = pl.pallas_call(
        _mod_kernel,
        grid=(depth, cols // blk),
        in_specs=[
            pl.BlockSpec((MOD_ROWS, d), lambda l, j: (0, 0)),
            pl.BlockSpec((None, d, blk), lambda l, j: (l, 0, j)),
            pl.BlockSpec((None, 1, blk), lambda l, j: (l, 0, j)),
        ],
        out_specs=pl.BlockSpec((None, MOD_ROWS, blk), lambda l, j: (l, 0, j)),
        out_shape=jax.ShapeDtypeStruct((depth, MOD_ROWS, cols), F32),
        compiler_params=_cparams(("arbitrary", "arbitrary"), 32),
        name="modulation",
    )(cond, ada_w, ada_b.reshape(depth, 1, cols))
    return out.reshape(depth, MOD_ROWS, 6, d)


def _seqs_per_step(B, L):
    return max(1, min(B, SHORT_SEQ_ROWS // L))


def _per_layer(a, layer, single_buffer=False):
    mode = dict(pipeline_mode=pl.Buffered(1)) if single_buffer else {}
    return pl.BlockSpec((None,) + a.shape[1:], lambda *_: (layer,) + (0,) * (a.ndim - 1), **mode)


def _mod_spec(layer, tiles_per_seq, latent):
    group = (lambda i: 1 + i // tiles_per_seq) if latent else (lambda i: 0)
    return pl.BlockSpec((None, None, 6, D_MODEL), lambda i: (layer, group(i), 0, 0))


HEAD_LANES = LANES


def _rope_lanes(x, cos, sa, sb):
    reps = x.shape[1] // HEAD_LANES
    wide = lambda t: jnp.concatenate([t] * reps, axis=1) if reps > 1 else t
    half = MLA_ROPE // 2
    return (x * wide(cos) + pltpu.roll(x, x.shape[1] - half, 1) * wide(sa)
            + pltpu.roll(x, half, 1) * wide(sb))


def _inproj_kernel(*refs, seqs_per_tile, seq_len, latent, cache_layers, aliased):
    x_ref, mod_ref, g1_ref, win_ref, qn_ref, kvn_ref, wuq_ref, wukv_ref = refs[:8]
    refs = refs[8:]
    c_cq = HY_COLS
    c_ck = c_cq + MLA_Q_LORA
    c_ret = c_ck + MLA_KV_LORA + HEAD_LANES
    c_conf = c_ret + RET_COLS
    why_ref, wcq_ref, wck_ref, wret_ref, wconf_ref = (
        win_ref.at[:, a:b] for a, b in ((0, c_cq), (c_cq, c_ck), (c_ck, c_ret), (c_ret, c_conf),
                                        (c_conf, c_conf + CONF_COLS)))
    if latent:
        cos_ref, sa_ref, sb_ref = refs[:3]
        refs = refs[3:]
    if aliased:
        refs = refs[2:]
    hv_ref, hx1_ref, hx2_ref, q_ref, kh_ref, vh_ref, ret_ref, conf_ref, rkt_ref = refs[:9]
    if not latent:
        ckv_ref, kr_ref = refs[9:]
    x = x_ref[...]
    h = _rms(x, g1_ref[...]) * (1.0 + mod_ref[1:2, :]) + mod_ref[0:1, :]
    hb = h.astype(BF16)

    u = _dot(hb, why_ref[...])
    for p, o_ref in enumerate((hv_ref, hx1_ref, hx2_ref)):
        part = u[:, p * HY_WIDTH:(p + 1) * HY_WIDTH]
        if seqs_per_tile == 1:
            o_ref[...] = part
        else:
            for s in range(seqs_per_tile):
                o_ref[:, s * HY_WIDTH:(s + 1) * HY_WIDTH] = part[s * seq_len:(s + 1) * seq_len]

    heads_w = MLA_HEADS * HEAD_LANES
    cq = _dot(hb, wcq_ref[...])
    q = _dot(_rms(cq, qn_ref[...]).astype(BF16), wuq_ref[...])
    ck = _dot(hb, wck_ref[...])
    ckv = _rms(ck[:, :MLA_KV_LORA], kvn_ref[...])
    kr_block = ck[:, MLA_KV_LORA:MLA_KV_LORA + HEAD_LANES]
    if latent:
        q = _rope_lanes(q, cos_ref[...], sa_ref[...], sb_ref[...])
        kr_block = _rope_lanes(kr_block, cos_ref[...], sa_ref[...], sb_ref[...])
    else:
        kr = kr_block[:, MLA_NOPE:MLA_NOPE + MLA_ROPE]
        for s in range(seqs_per_tile):
            rows = slice(s * seq_len, (s + 1) * seq_len)
            for out_ref, val in ((ckv_ref, ckv[rows]), (kr_ref, kr[rows])):
                mine = out_ref.at[s, 0] if cache_layers else out_ref.at[s]
                mine[...] = val
                for other in range(1, cache_layers):
                    out_ref[s, other] = jnp.zeros(out_ref.shape[2:], F32)
    q_ref[...] = (q * ((MLA_NOPE + MLA_ROPE) ** -0.5 * math.log2(math.e))).astype(BF16)
    kv = _dot(ckv.astype(BF16), wukv_ref[...])
    kh_ref[...] = (kv[:, :heads_w] + jnp.concatenate([kr_block] * MLA_HEADS, axis=1)).astype(BF16)
    lane = lax.broadcasted_iota(jnp.int32, (1, heads_w), 1)
    ones_lane = jnp.where(lane % HEAD_LANES == MLA_V, 1.0, 0.0)
    vh_ref[...] = (kv[:, heads_w:] + ones_lane).astype(BF16)
    ret = _dot(hb, wret_ref[...])
    ret_ref[...] = ret
    qk = RET_HEADS * RET_DK
    rkt_ref[...] = ret[:, qk:2 * qk].T
    conf_ref[...] = _dot(hb, wconf_ref[...])


def _inproj(x, mod, lw, B, L, latent, caches_out=None):
    T = B * L
    TM = TOKEN_TILE
    nt = T // TM
    if L >= TM:
        tiles_per_seq, seqs_per_tile = L // TM, 1
        hy_block = (TM, HY_WIDTH)
        hy_map = lambda i: (i % tiles_per_seq, i // tiles_per_seq)
    else:
        tiles_per_seq, seqs_per_tile = 1, TM // L
        hy_block = (L, seqs_per_tile * HY_WIDTH)
        hy_map = lambda i: (0, i)
    row = lambda w: pl.BlockSpec((TM, w), lambda i: (i, 0))
    weights = (lw["norm1_g"], lw["w_in"], lw["mla_q_norm"], lw["mla_kv_norm"], lw["mla_w_uq"], lw["mla_w_ukv"])
    hy_shape = jax.ShapeDtypeStruct((L, B * HY_WIDTH), F32)
    qk = RET_HEADS * RET_DK
    heads_w = MLA_HEADS * HEAD_LANES
    ins = [x, mod, *weights]
    in_specs = ([row(D_MODEL), _mod_spec(lw["layer"], tiles_per_seq, latent)]
                + [_per_layer(w, lw["layer"]) for w in weights])
    out_specs = ([pl.BlockSpec(hy_block, hy_map)] * 3
                 + [row(heads_w)] * 3 + [row(RET_COLS), row(CONF_COLS), pl.BlockSpec((qk, TM), lambda i: (0, i))])
    out_shape = ([hy_shape] * 3 + [jax.ShapeDtypeStruct((T, heads_w), BF16)] * 3
                 + [jax.ShapeDtypeStruct((T, RET_COLS), F32), jax.ShapeDtypeStruct((T, CONF_COLS), F32),
                    jax.ShapeDtypeStruct((qk, T), F32)])
    if latent:
        ins += list(_rope_lane_tables(L))
        in_specs += [pl.BlockSpec((TM, HEAD_LANES), lambda i: (i % tiles_per_seq, 0))] * 3
    aliases = {}
    cache_layers = 0
    if not latent:
        assert tiles_per_seq == 1
        layer, depth = lw["layer"], lw["w_in"].shape[0]
        dims = ((L, MLA_KV_LORA), (L, MLA_ROPE))
        if caches_out is None:
            cache_layers = depth
            out_specs += [pl.BlockSpec((seqs_per_tile, depth) + d, lambda i: (i, 0, 0, 0)) for d in dims]
        else:
            aliases = {len(ins) + k: len(out_shape) + k for k in range(len(dims))}
            ins += list(caches_out)
            in_specs += [pl.BlockSpec(memory_space=pl.ANY)] * len(dims)
            out_specs += [pl.BlockSpec((seqs_per_tile, None) + d, lambda i: (i, layer, 0, 0)) for d in dims]
        out_shape += [jax.ShapeDtypeStruct((B, depth) + d, F32) for d in dims]
    return pl.pallas_call(
        functools.partial(_inproj_kernel, seqs_per_tile=seqs_per_tile, seq_len=L, latent=latent,
                          cache_layers=cache_layers, aliased=bool(aliases)),
        grid=(nt,),
        in_specs=in_specs,
        out_specs=out_specs,
        out_shape=out_shape,
        input_output_aliases=aliases,
        compiler_params=_cparams(("arbitrary",), 60),
        name="inproj",
    )(*ins)


def _dft_tables(L):
    N = 2 * L
    k_lo = min(L, 32)
    k_hi = L // k_lo
    t = jnp.arange(L, dtype=jnp.int32)[None, :]

    def cs(k):
        m = (k[:, None] * t) % N
        ang = m.astype(F32) * (2.0 * math.pi / N)
        return jnp.cos(ang), jnp.sin(ang)

    ca, sa = cs(jnp.arange(k_hi, dtype=jnp.int32) * k_lo)
    cb, sb = cs(jnp.arange(k_lo, dtype=jnp.int32))
    cos = (ca[:, None, :] * cb[None, :, :] - sa[:, None, :] * sb[None, :, :]).reshape(L, L)
    sin = (sa[:, None, :] * cb[None, :, :] + ca[:, None, :] * sb[None, :, :]).reshape(L, L)
    return cos.astype(BF16), (-sin).astype(BF16)


def _filter_features(L):
    t = jnp.linspace(0.0, 1.0, L, dtype=F32)[:, None]
    w = 2.0 * math.pi * jnp.arange(L, dtype=F32)[:, None] / L
    f = jnp.linspace(1e-4, HY_BANDS - 1, HY_BANDS, dtype=F32)[None, :]
    z = jnp.concatenate([t, jnp.cos(f * w), -jnp.sin(f * w)], axis=-1)
    z = jnp.pad(z, ((0, 0), (0, LANES - HY_EMB)))
    max_decay = math.log(HY_TARGET) / HY_FAST_DECAY
    min_decay = math.log(HY_TARGET) / HY_SLOW_DECAY
    deltas = jnp.abs(jnp.linspace(min_decay, max_decay, HY_WIDTH, dtype=F32))
    decay = jnp.exp(-t * deltas[None, :])
    return z, decay


def _alternating(rows):
    t = lax.broadcasted_iota(jnp.int32, (rows, 1), 0)
    return (1 - 2 * (t & 1)).astype(F32)


def _filter_kernel(z_ref, dec_ref, w1_ref, b1_ref, w2_ref, b2_ref, w3_ref, cos_ref, sin_ref,
                   kr_ref, ki_ref, kn_ref, hsum_ref, hdiff_ref, *, L):
    j = pl.program_id(0)
    W = HY_WIDTH
    N = 2 * L

    @pl.when(j == 0)
    def _():
        h = jnp.sin(_dot_exact(z_ref[...], w1_ref[...]) + b1_ref[...])
        h = jnp.sin(_dot_exact(h, w2_ref[...]) + b2_ref[...])
        h = _dot_split(h, w3_ref[...]) * jnp.concatenate([dec_ref[...]] * 4, axis=1)
        cs = jnp.sum(jnp.abs(h), axis=0, keepdims=True)
        s0 = cs[:, 0:W] + cs[:, W:2 * W]
        s1 = cs[:, 2 * W:3 * W] + cs[:, 3 * W:4 * W]
        h = h / jnp.concatenate([s0, s0, s1, s1], axis=1)
        row = lax.broadcasted_iota(jnp.int32, h.shape, 0)
        col = lax.broadcasted_iota(jnp.int32, h.shape, 1)
        backward = (col // W) % 2 == 1
        h = jnp.where(backward & (row == 0), 0.0, h)
        fwd = jnp.concatenate([h[:, 0:W], h[:, 2 * W:3 * W]], axis=1)
        bwd = jnp.concatenate([h[:, W:2 * W], h[:, 3 * W:4 * W]], axis=1)
        hsum_ref[...] = (fwd + bwd).astype(BF16)
        hdiff_ref[...] = (fwd - bwd).astype(BF16)
        nyq = jnp.sum((fwd + bwd) * _alternating(L), axis=0, keepdims=True) * (1.0 / N)
        for o in range(2):
            kn_ref[o] = nyq[:, o * W:(o + 1) * W]

    sr = _dot(cos_ref[...], hsum_ref[...])
    si = _dot(sin_ref[...], hdiff_ref[...])
    row = lax.broadcasted_iota(jnp.int32, (sr.shape[0], W), 0)
    scale = jnp.where((row == 0) & (j == 0), 1.0 / N, 2.0 / N)
    for o in range(2):
        kr_ref[o] = sr[:, o * W:(o + 1) * W] * scale
        ki_ref[o] = si[:, o * W:(o + 1) * W] * scale


def _filter_spectra(lw, L, tables):
    z, decay = _filter_features(L)
    cos, msin = tables
    Tk = min(DFT_TILE, L)
    full = lambda a: pl.BlockSpec(a.shape, lambda j: (0,) * a.ndim)
    mlp = (lw["hy_w1"], lw["hy_b1"], lw["hy_w2"], lw["hy_b2"], lw["hy_w3"])
    ins = (z, decay, *mlp)
    tile = pl.BlockSpec((Tk, L), lambda j: (j, 0))
    spec = pl.BlockSpec((2, Tk, HY_WIDTH), lambda j: (0, j, 0))
    return pl.pallas_call(
        functools.partial(_filter_kernel, L=L),
        grid=(L // Tk,),
        in_specs=[full(z), full(decay)] + [_per_layer(a, lw["layer"]) for a in mlp] + [tile, tile],
        out_specs=[spec, spec, pl.BlockSpec((2, 1, HY_WIDTH), lambda j: (0, 0, 0))],
        out_shape=[jax.ShapeDtypeStruct((2, L, HY_WIDTH), F32)] * 2
        + [jax.ShapeDtypeStruct((2, 1, HY_WIDTH), F32)],
        scratch_shapes=[pltpu.VMEM((L, 2 * HY_WIDTH), BF16)] * 2,
        compiler_params=_cparams(("arbitrary",), 48),
        name="hyena_filter",
    )(*ins, cos, msin)


def _hyena_kernel(v_ref, x1_ref, x2_ref, cw_ref, cb_ref, bias_ref, kr_ref, ki_ref, kn_ref,
                  cos_ref, sin_ref, y_ref, cur_ref, curb_ref, yr_ref, yi_ref, gate_ref,
                  *, L, W, T):
    reps = W // HY_WIDTH
    tiled = lambda a: jnp.concatenate([a] * reps, axis=1) if reps > 1 else a
    alt = _alternating(L)

    def short_conv(u_ref, p):
        u = u_ref[...]
        row = lax.broadcasted_iota(jnp.int32, (L, W), 0)
        prev = jnp.where(row == 0, 0.0, pltpu.roll(u, 1, 0))
        nxt = jnp.where(row == L - 1, 0.0, pltpu.roll(u, L - 1, 0))
        cols = slice(p * HY_WIDTH, (p + 1) * HY_WIDTH)
        w = [tiled(cw_ref[k:k + 1, cols]) for k in range(3)]
        return prev * w[0] + u * w[1] + nxt * w[2] + tiled(cb_ref[:, cols])

    cur_ref[...] = short_conv(v_ref, 0)
    for o, x_ref in enumerate((x1_ref, x2_ref)):
        cur = cur_ref[...]
        curb_ref[...] = cur.astype(BF16)
        gate_ref[...] = short_conv(x_ref, o + 1)
        nyq = jnp.sum(cur * alt, axis=0, keepdims=True) * tiled(kn_ref[o])
        for f in range(L // T):
            rows = slice(f * T, (f + 1) * T)
            cb = curb_ref[...]
            xr = _dot(cos_ref[rows, :], cb)
            xi = _dot(sin_ref[rows, :], cb)
            kr, ki = tiled(kr_ref[o, rows, :]), tiled(ki_ref[o, rows, :])
            yr_ref[rows, :] = (xr * kr - xi * ki).astype(BF16)
            yi_ref[rows, :] = (xr * ki + xi * kr).astype(BF16)
        for t in range(L // T):
            rows = slice(t * T, (t + 1) * T)
            conv = (_dot(cos_ref[rows, :], yr_ref[...]) + _dot(sin_ref[rows, :], yi_ref[...])
                    + alt[rows] * nyq + cur_ref[rows, :] * tiled(bias_ref[o:o + 1, :]))
            out = gate_ref[rows, :] * conv
            if o == 0:
                cur_ref[rows, :] = out
            else:
                y_ref[rows, :] = out


def _hyena(hv, hx1, hx2, lw, spectra, tables, B, L):
    kr, ki, kn = spectra
    cos, msin = tables
    T = min(HYENA_ROWS, L)
    W = HY_WIDTH * max(1, min(B, HYENA_GROUP_ROWS // L))
    ng = (B * HY_WIDTH) // W
    col = pl.BlockSpec((L, W), lambda g: (0, g))
    once = lambda a: pl.BlockSpec(a.shape, lambda g: (0,) * a.ndim, pipeline_mode=pl.Buffered(1))
    params = (lw["hy_conv_w"], lw["hy_conv_b"], lw["hy_bias"])
    consts = (kr, ki, kn, cos, msin)
    return pl.pallas_call(
        functools.partial(_hyena_kernel, L=L, W=W, T=T),
        grid=(ng,),
        in_specs=[col, col, col] + [_per_layer(a, lw["layer"]) for a in params] + [once(a) for a in consts],
        out_specs=col,
        out_shape=jax.ShapeDtypeStruct((L, B * HY_WIDTH), F32),
        scratch_shapes=[pltpu.VMEM((L, W), F32), pltpu.VMEM((L, W), BF16), pltpu.VMEM((L, W), BF16),
                        pltpu.VMEM((L, W), BF16), pltpu.VMEM((L, W), F32)],
        compiler_params=_cparams(("arbitrary",), 60),
        name="hyena_conv",
    )(hv, hx1, hx2, *params, *consts)


def _mla_kernel(*refs, L, latent, seqs):
    if latent:
        q_ref, kh_ref, vh_ref, cckv_ref, ckr_ref, wukv_ref, o_ref, s_ref, p_ref, kctx_ref, vctx_ref = refs
    else:
        q_ref, kh_ref, vh_ref, o_ref, s_ref, p_ref = refs
    TQ = q_ref.shape[0] // seqs
    heads_w = MLA_HEADS * HEAD_LANES
    block = lambda h: slice(h * HEAD_LANES, (h + 1) * HEAD_LANES)

    if latent:
        @pl.when(pl.program_id(1) == 0)
        def _():
            kv = _dot(cckv_ref[...].astype(BF16), wukv_ref[...])
            zeros = lambda w: jnp.zeros((PAST_LEN, w), F32)
            kr_block = jnp.concatenate(
                [zeros(MLA_NOPE), ckr_ref[...], zeros(HEAD_LANES - MLA_NOPE - MLA_ROPE)], axis=1)
            kctx_ref[...] = (kv[:, :heads_w] + jnp.concatenate([kr_block] * MLA_HEADS, axis=1)).astype(BF16)
            lane = lax.broadcasted_iota(jnp.int32, (1, heads_w), 1)
            vctx_ref[...] = (kv[:, heads_w:] + jnp.where(lane % HEAD_LANES == MLA_V, 1.0, 0.0)).astype(BF16)

    def attend(q_rows, k_rows):
        def scores(h):
            qh = q_ref[q_rows, block(h)]
            s_ref[h, :, 0:L] = _dot_nt(qh, kh_ref[k_rows, block(h)])
            if latent:
                s_ref[h, :, L:L + PAST_LEN] = _dot_nt(qh, kctx_ref[:, block(h)])

        outs = []
        scores(0)
        for h in range(MLA_HEADS):
            if h + 1 < MLA_HEADS:
                scores(h + 1)
            for r in range(TQ // MLA_SOFTMAX_ROWS):
                rows = slice(r * MLA_SOFTMAX_ROWS, (r + 1) * MLA_SOFTMAX_ROWS)
                s = s_ref[h, rows, :]
                p_ref[h, rows, :] = jnp.exp2(s - jnp.max(s, axis=-1, keepdims=True)).astype(BF16)
            pv = _dot(p_ref[h, :, 0:L], vh_ref[k_rows, block(h)])
            if latent:
                pv = pv + _dot(p_ref[h, :, L:L + PAST_LEN], vctx_ref[:, block(h)])
            outs.append(pv[:, :MLA_V] / pv[:, MLA_V:MLA_V + 1])
        o_ref[q_rows, :] = jnp.concatenate(outs, axis=-1)

    for sq in range(seqs):
        attend(slice(sq * TQ, (sq + 1) * TQ), slice(sq * L, (sq + 1) * L))


def _rope_tables(L):
    rows = L // GRID_W
    row = jnp.repeat(jnp.arange(rows), GRID_W).astype(F32)
    col = jnp.tile(jnp.arange(GRID_W), rows).astype(F32)
    per_axis = MLA_ROPE // 4
    inv = ROPE_BASE ** (-jnp.arange(per_axis, dtype=F32) / per_axis)
    ang = jnp.concatenate([row[:, None] * inv, col[:, None] * inv], axis=-1)
    return jnp.cos(ang), jnp.sin(ang)


def _rope_lane_tables(L):
    cos, sin = _rope_tables(L)
    pad = HEAD_LANES - MLA_NOPE - MLA_ROPE
    one, zero = jnp.ones((L, MLA_NOPE), F32), jnp.zeros((L, MLA_NOPE), F32)
    half0 = jnp.zeros_like(sin)
    cos_t = jnp.concatenate([one, cos, cos, one[:, :pad]], axis=1)
    sa_t = jnp.concatenate([zero, -sin, half0, zero[:, :pad]], axis=1)
    sb_t = jnp.concatenate([zero, half0, sin, zero[:, :pad]], axis=1)
    return cos_t, sa_t, sb_t


def _mla(qh, kh, vh, lw, B, L, ctx):
    latent = ctx is not None
    TQ = min(Q_TILE, L)
    nq = L // TQ
    seqs = 1 if latent or nq > 1 else _seqs_per_step(B, L)
    Lk = L + PAST_LEN if latent else L
    heads_w = MLA_HEADS * HEAD_LANES
    seq = pl.BlockSpec((seqs * L, heads_w), lambda b, i: (b, 0))
    ins = [qh, kh, vh]
    specs = [pl.BlockSpec((seqs * TQ, heads_w), lambda b, i: (b * nq + i, 0)), seq, seq]
    scratch = [pltpu.VMEM((MLA_HEADS, TQ, Lk), F32), pltpu.VMEM((MLA_HEADS, TQ, Lk), BF16)]
    if latent:
        ins += [ctx[0], ctx[1], lw["mla_w_ukv"]]
        layer = lw["layer"]
        specs += [pl.BlockSpec((None, None, PAST_LEN, MLA_KV_LORA), lambda b, i: (b, layer, 0, 0)),
                  pl.BlockSpec((None, None, PAST_LEN, MLA_ROPE), lambda b, i: (b, layer, 0, 0)),
                  _per_layer(lw["mla_w_ukv"], lw["layer"])]
        scratch += [pltpu.VMEM((PAST_LEN, heads_w), BF16)] * 2
    return pl.pallas_call(
        functools.partial(_mla_kernel, L=L, latent=latent, seqs=seqs),
        grid=(B // seqs, nq),
        in_specs=specs,
        out_specs=pl.BlockSpec((seqs * TQ, MLA_HEADS * MLA_V), lambda b, i: (b * nq + i, 0)),
        out_shape=jax.ShapeDtypeStruct((B * L, MLA_HEADS * MLA_V), F32),
        scratch_shapes=scratch,
        compiler_params=_cparams(("arbitrary", "arbitrary"), 48),
        name="mla_attention",
    )(*ins)


def _ret_kernel(*refs, L, seqs, has_state, emit_state, state_layers, aliased):
    refs = list(refs)
    u_ref, kt_ref, dl_ref = refs[:3]
    pos = 3
    s0_ref = None
    if has_state:
        s0_ref = refs[pos]
        pos += 1
    if aliased:
        pos += 1
    y_ref = refs[pos]
    pos += 1
    sout_ref = None
    if emit_state:
        sout_ref = refs[pos]
        pos += 1
    of_ref, kv_ref, sin_ref, dcomb_ref, dtab_ref, hmask_ref, avg_ref = refs[pos:]

    C = RET_BLOCK
    n = L // C
    H, DK, DV = RET_HEADS, RET_DK, RET_DV
    qk = H * DK
    DK_F, DK_B, DQ_F, DQ_B, DC_F, DC_B, MASK = range(7)
    use_cross = has_state or n > 1

    @pl.when(pl.program_id(0) == 0)
    def _():
        x = dl_ref[...]
        log_g = jnp.minimum(x, 0.0) - jnp.log1p(jnp.exp(-jnp.abs(x)))
        gf = [log_g[0:1, h:h + 1] for h in range(H)]
        gb = [log_g[1:2, h:h + 1] for h in range(H)]
        diff = (lax.broadcasted_iota(jnp.int32, (C, C), 0)
                - lax.broadcasted_iota(jnp.int32, (C, C), 1)).astype(F32)
        idx = lax.broadcasted_iota(jnp.int32, (C, 1), 0).astype(F32)
        for h in range(H):
            dcomb_ref[h] = (jnp.where(diff >= 0, jnp.exp(jnp.maximum(diff, 0.0) * gf[h]), 0.0)
                            + jnp.where(diff <= 0, jnp.exp(jnp.maximum(-diff, 0.0) * gb[h]), 0.0))

        def head_cols(lag, g):
            return jnp.concatenate([jnp.broadcast_to(jnp.exp(lag * g[h]), (C, DK)) for h in range(H)], axis=1)

        def head_rows(lag, g):
            return jnp.concatenate(
                [jnp.broadcast_to(jnp.exp(lag * g[h]) * (DK ** -0.5), (DK, C)) for h in range(H)], axis=0)

        tok = lax.broadcasted_iota(jnp.int32, (1, C), 1).astype(F32)
        dtab_ref[DK_F] = head_rows(C - 1.0 - tok, gf)
        dtab_ref[DK_B] = head_rows(tok, gb)
        dtab_ref[DQ_F] = head_cols(idx + 1.0, gf)
        dtab_ref[DQ_B] = head_cols(C - idx, gb)
        row_head = lax.broadcasted_iota(jnp.int32, (qk, qk), 0) // DK
        col_head = lax.broadcasted_iota(jnp.int32, (qk, qk), 1) // DV
        same_head = row_head == col_head
        for slot, g in ((DC_F, gf), (DC_B, gb)):
            dc = jnp.zeros((qk, qk), F32)
            for h in range(H):
                dc = jnp.where(same_head & (row_head == h), jnp.exp(C * g[h]), dc)
            dtab_ref[slot] = dc
        dtab_ref[MASK] = same_head.astype(F32)
        avg_ref[...] = jnp.where(same_head, 1.0 / DV, 0.0).astype(BF16)
        lane_head = lax.broadcasted_iota(jnp.int32, (C, qk), 1) // DK
        for h in range(H):
            hmask_ref[h] = jnp.where(lane_head == h, 1.0, 0.0).astype(BF16)

    def head_mean(a):
        total = None
        for _ in range(3):
            part = a.astype(BF16)
            a = a - part.astype(F32)
            term = _dot(part, avg_ref[...])
            total = term if total is None else total + term
        return total

    for sq in range(seqs):
        chunk = lambda c: slice(sq * L + c * C, sq * L + (c + 1) * C)

        for c in range(n):
            rows = chunk(c)
            qb = u_ref[rows, 0:qk].astype(BF16)
            kb = (u_ref[rows, qk:2 * qk] * (DK ** -0.5)).astype(BF16)
            vb = u_ref[rows, 2 * qk:2 * qk + H * DV].astype(BF16)
            att = [(_dot_nt(qb * hmask_ref[h], kb) * dcomb_ref[h]).astype(BF16) for h in range(H)]
            v_heads = jnp.concatenate([vb * hmask_ref[h] for h in range(H)], axis=0)
            of_ref[c * C:(c + 1) * C, :] = _dot(jnp.concatenate(att, axis=1), v_heads)
            kt = kt_ref[:, rows]
            for d, slot in ((0, DK_F), (1, DK_B)):
                kv_ref[d, c] = _dot((kt * dtab_ref[slot]).astype(BF16), vb)

        finals = []
        for d, slot in ((0, DC_F), (1, DC_B)):
            if has_state:
                zero = jnp.zeros((DK, DV), F32)
                S = jnp.concatenate(
                    [jnp.concatenate([s0_ref[sq, d, h] if g == h else zero for g in range(H)], axis=1)
                     for h in range(H)], axis=0)
            else:
                S = jnp.zeros((qk, H * DV), F32)
            for c in (range(n) if d == 0 else reversed(range(n))):
                if use_cross:
                    sin_ref[d, c] = S.astype(BF16)
                S = S * dtab_ref[slot] + kv_ref[d, c] * dtab_ref[MASK]
            finals.append(S)
        if emit_state:
            mine = sout_ref.at[sq, 0] if state_layers else sout_ref.at[sq]
            for d in range(2):
                for h in range(H):
                    mine[d, h] = finals[d][h * DK:(h + 1) * DK, h * DV:(h + 1) * DV]
            for other in range(1, state_layers):
                sout_ref[sq, other] = jnp.zeros(sout_ref.shape[2:], F32)

        for c in range(n):
            rows = chunk(c)
            tot = of_ref[c * C:(c + 1) * C, :]
            if use_cross:
                qb = u_ref[rows, 0:qk].astype(BF16)
                tot = (tot + _dot(qb, sin_ref[0, c]) * dtab_ref[DQ_F]
                       + _dot(qb, sin_ref[1, c]) * dtab_ref[DQ_B])
            xc = tot - head_mean(tot)
            normed = xc * lax.rsqrt(head_mean(xc * xc) + EPS)
            gate = u_ref[rows, 2 * qk + H * DV:2 * qk + 2 * H * DV]
            y_ref[rows, :] = _silu(gate) * normed


def _retention(u_ret, kt_ret, lw, B, L, state, emit_state, states_out=None):
    has_state = state is not None
    seqs = _seqs_per_step(B, L)
    st_dims = (2, RET_HEADS, RET_DK, RET_DV)
    ins = [u_ret, kt_ret, lw["ret_decay"]]
    specs = [pl.BlockSpec((seqs * L, RET_COLS), lambda b: (b, 0)),
             pl.BlockSpec((RET_HEADS * RET_DK, seqs * L), lambda b: (0, b)),
             _per_layer(lw["ret_decay"], lw["layer"])]
    if has_state:
        layer = lw["layer"]
        ins.append(state)
        specs.append(pl.BlockSpec((seqs, None) + st_dims, lambda b: (b, layer, 0, 0, 0, 0)))
    vd = RET_HEADS * RET_DV
    out_specs = [pl.BlockSpec((seqs * L, vd), lambda b: (b, 0))]
    out_shape = [jax.ShapeDtypeStruct((B * L, vd), F32)]
    aliases = {}
    state_layers = 0
    if emit_state:
        layer, depth = lw["layer"], lw["ret_decay"].shape[0]
        if states_out is None:
            state_layers = depth
            out_specs.append(pl.BlockSpec((seqs, depth) + st_dims, lambda b: (b, 0, 0, 0, 0, 0)))
        else:
            aliases = {len(ins): 1}
            ins.append(states_out)
            specs.append(pl.BlockSpec(memory_space=pl.ANY))
            out_specs.append(pl.BlockSpec((seqs, None) + st_dims, lambda b: (b, layer, 0, 0, 0, 0)))
        out_shape.append(jax.ShapeDtypeStruct((B, depth) + st_dims, F32))
    res = pl.pallas_call(
        functools.partial(_ret_kernel, L=L, seqs=seqs, has_state=has_state, emit_state=emit_state,
                          state_layers=state_layers, aliased=bool(aliases)),
        grid=(B // seqs,),
        in_specs=specs,
        out_specs=out_specs,
        out_shape=out_shape,
        input_output_aliases=aliases,
        scratch_shapes=[pltpu.VMEM((L, vd), F32),
                        pltpu.VMEM((2, L // RET_BLOCK, vd, vd), F32),
                        pltpu.VMEM((2, L // RET_BLOCK, vd, vd), BF16),
                        pltpu.VMEM((RET_HEADS, RET_BLOCK, RET_BLOCK), F32),
                        pltpu.VMEM((7, RET_BLOCK, vd), F32),
                        pltpu.VMEM((RET_HEADS, RET_BLOCK, vd), BF16),
                        pltpu.VMEM((vd, vd), BF16)],
        compiler_params=_cparams(("arbitrary",), 48),
        name="retention",
    )(*ins)
    return (res[0], res[1]) if emit_state else (res[0], None)


def _conf_kernel(u_ref, w_ref, b_ref, g_ref, be_ref, y_ref, zp_ref, sh_ref, *, L, seqs):
    Wc = CONF_WIDTH
    halo = CONF_HALO
    zp_ref[0:halo, :] = jnp.zeros((halo, Wc), F32)
    zp_ref[halo + L:2 * halo + L, :] = jnp.zeros((halo, Wc), F32)
    first = halo - CONF_KERNEL // 2
    span = sh_ref.shape[1]
    R = CONF_ROWS
    for sq in range(seqs):
        r0 = sq * L
        zp_ref[halo:halo + L, :] = u_ref[r0:r0 + L, 0:Wc] * _sigmoid(u_ref[r0:r0 + L, Wc:2 * Wc])
        for s in range(SUBLANES):
            sh_ref[s] = zp_ref[first + s:first + s + span, :]
        for c in range(L // R):
            acc = jnp.broadcast_to(b_ref[...], (R, Wc))
            for k in range(CONF_KERNEL):
                a, s = divmod(k, SUBLANES)
                acc = acc + w_ref[k:k + 1, :] * sh_ref[s, c * R + SUBLANES * a:c * R + SUBLANES * a + R, :]
            mu = jnp.mean(acc, axis=-1, keepdims=True)
            xc = acc - mu
            z = xc * lax.rsqrt(jnp.mean(xc * xc, axis=-1, keepdims=True) + EPS) * g_ref[...] + be_ref[...]
            y_ref[r0 + c * R:r0 + (c + 1) * R, :] = _silu(z)


def _conformer(u_conf, lw, B, L):
    ws = (lw["conf_dw_w"], lw["conf_dw_b"], lw["conf_ln_g"], lw["conf_ln_b"])
    seqs = _seqs_per_step(B, L)
    return pl.pallas_call(
        functools.partial(_conf_kernel, L=L, seqs=seqs),
        grid=(B // seqs,),
        in_specs=[pl.BlockSpec((seqs * L, CONF_COLS), lambda b: (b, 0))]
        + [_per_layer(w, lw["layer"]) for w in ws],
        out_specs=pl.BlockSpec((seqs * L, CONF_WIDTH), lambda b: (b, 0)),
        out_shape=jax.ShapeDtypeStruct((B * L, CONF_WIDTH), F32),
        scratch_shapes=[pltpu.VMEM((L + 2 * CONF_HALO, CONF_WIDTH), F32),
                        pltpu.VMEM((SUBLANES, L + SUBLANES * ((CONF_KERNEL - 1) // SUBLANES), CONF_WIDTH), F32)],
        compiler_params=_cparams(("arbitrary",), 48),
        name="conformer",
    )(u_conf, *ws)


def _merge_kernel(x_ref, mod_ref, g1_ref, yhy_ref, ymla_ref, yret_ref, yconf_ref,
                  gw_ref, gb_ref, why_ref, wmla_ref, wret_ref, wconf_ref, wo_ref, o_ref, *, seqs_per_tile):
    x = x_ref[...]
    h = _rms(x, g1_ref[...]) * (1.0 + mod_ref[1:2, :]) + mod_ref[0:1, :]
    hb = h.astype(BF16)
    if seqs_per_tile == 1:
        yhy = yhy_ref[...]
    else:
        yhy = jnp.concatenate(
            [yhy_ref[:, s * HY_WIDTH:(s + 1) * HY_WIDTH] for s in range(seqs_per_tile)], axis=0)
    branches = ((yhy, why_ref), (ymla_ref[...], wmla_ref), (yret_ref[...], wret_ref), (yconf_ref[...], wconf_ref))
    D = D_MODEL
    merged = None
    for i, (y, w_ref) in enumerate(branches):
        gate = _sigmoid(_dot(hb, gw_ref[:, i * D:(i + 1) * D]) + gb_ref[:, i * D:(i + 1) * D])
        term = gate * _dot(y.astype(BF16), w_ref[...])
        merged = term if merged is None else merged + term
    o_ref[...] = x + mod_ref[2:3, :] * _dot(merged.astype(BF16), wo_ref[...])


def _merge(x, mod, lw, y_hy, y_mla, y_ret, y_conf, B, L, latent):
    T = B * L
    TM = WIDE_TOKEN_TILE
    if L >= TM:
        tiles_per_seq, seqs_per_tile = L // TM, 1
        hy_spec = pl.BlockSpec((TM, HY_WIDTH), lambda i: (i % tiles_per_seq, i // tiles_per_seq))
    else:
        tiles_per_seq, seqs_per_tile = 1, TM // L
        hy_spec = pl.BlockSpec((L, seqs_per_tile * HY_WIDTH), lambda i: (0, i))
    once = lambda a: _per_layer(a, lw["layer"], single_buffer=True)
    row = lambda w: pl.BlockSpec((TM, w), lambda i: (i, 0))
    ws = (lw["gate_w"], lw["gate_b"], lw["hy_out"], lw["mla_out"], lw["ret_out"], lw["conf_out"], lw["w_o"])
    return pl.pallas_call(
        functools.partial(_merge_kernel, seqs_per_tile=seqs_per_tile),
        grid=(T // TM,),
        in_specs=[row(D_MODEL), _mod_spec(lw["layer"], tiles_per_seq, latent),
                  once(lw["norm1_g"]), hy_spec, row(MLA_HEADS * MLA_V), row(RET_HEADS * RET_DV),
                  row(CONF_WIDTH)] + [once(w) for w in ws],
        out_specs=row(D_MODEL),
        out_shape=jax.ShapeDtypeStruct((T, D_MODEL), F32),
        compiler_params=_cparams(("arbitrary",), 56),
        name="merge",
    )(x, mod, lw["norm1_g"], y_hy, y_mla, y_ret, y_conf, *ws)


def _ffn_kernel(x_ref, mod_ref, g2_ref, w1_ref, w2_ref, fg_ref, o_ref, *, final):
    x = x_ref[...]
    h2 = (_rms(x, g2_ref[...]) * (1.0 + mod_ref[4:5, :]) + mod_ref[3:4, :]).astype(BF16)
    acc = None
    for c0 in range(0, D_FF, FFN_CHUNK):
        c1 = min(c0 + FFN_CHUNK, D_FF)
        a = _dot(h2, w1_ref[:, c0:c1])
        b = _dot(h2, w1_ref[:, D_FF + c0:D_FF + c1])
        part = _dot((_silu(a) * b).astype(BF16), w2_ref[c0:c1, :])
        acc = part if acc is None else acc + part
    out = x + mod_ref[5:6, :] * acc
    if final:
        out = _rms(out, fg_ref[...])
    o_ref[...] = out


def _ffn(x, mod, lw, final_g, B, L, latent, final):
    T = B * L
    TM = WIDE_TOKEN_TILE
    tiles_per_seq = max(L // TM, 1)
    once = lambda a: _per_layer(a, lw["layer"], single_buffer=True)
    row = pl.BlockSpec((TM, D_MODEL), lambda i: (i, 0))
    ws = (lw["norm2_g"], lw["ffn_w1"], lw["ffn_w2"])
    return pl.pallas_call(
        functools.partial(_ffn_kernel, final=final),
        grid=(T // TM,),
        in_specs=[row, _mod_spec(lw["layer"], tiles_per_seq, latent)]
        + [once(w) for w in ws] + [pl.BlockSpec(final_g.shape, lambda i: (0, 0))],
        out_specs=row,
        out_shape=jax.ShapeDtypeStruct((T, D_MODEL), F32),
        compiler_params=_cparams(("arbitrary",), 56),
        name="ffn",
    )(x, mod, *ws, final_g)


def _trunk_layer(x, mod, lw, final_g, tables, B, L, ctx, state, final, caches_out=None, states_out=None):
    latent = ctx is not None
    hv, hx1, hx2, qh, kh, vh, u_ret, u_conf, kt_ret, *cache = _inproj(x, mod, lw, B, L, latent, caches_out)
    ckv, kr = cache if cache else (None, None)
    spectra = _filter_spectra(lw, L, tables)
    y_hy = _hyena(hv, hx1, hx2, lw, spectra, tables, B, L)
    y_mla = _mla(qh, kh, vh, lw, B, L, ctx)
    y_ret, S = _retention(u_ret, kt_ret, lw, B, L, state, emit_state=not latent, states_out=states_out)
    y_conf = _conformer(u_conf, lw, B, L)
    x = _merge(x, mod, lw, y_hy, y_mla, y_ret, y_conf, B, L, latent)
    x = _ffn(x, mod, lw, final_g, B, L, latent, final)
    return x, ckv, kr, S


def _regroup_kernel(wt_ref, o_ref):
    d = wt_ref.shape[1]
    rope_key = HY_COLS + MLA_COLS - MLA_ROPE

    def move(dst, src, n):
        for c in range(0, n, REGROUP_COLS):
            m = min(REGROUP_COLS, n - c)
            o_ref[:, dst + c:dst + c + m] = wt_ref[src + c:src + c + m, :].T.astype(BF16)

    move(0, 0, rope_key)
    slab = pltpu.roll(wt_ref[rope_key:rope_key + HEAD_LANES, :].T, MLA_NOPE, 1)
    lane = lax.broadcasted_iota(jnp.int32, (d, HEAD_LANES), 1)
    rope_lanes = (lane >= MLA_NOPE) & (lane < MLA_NOPE + MLA_ROPE)
    o_ref[:, rope_key:rope_key + HEAD_LANES] = jnp.where(rope_lanes, slab, 0.0).astype(BF16)
    move(rope_key + HEAD_LANES, rope_key + MLA_ROPE, wt_ref.shape[0] - rope_key - MLA_ROPE)


def _regroup_w_in(w_in):
    depth, d, cols = w_in.shape
    out_cols = cols - MLA_ROPE + HEAD_LANES
    return pl.pallas_call(
        _regroup_kernel,
        grid=(depth,),
        in_specs=[pl.BlockSpec((None, cols, d), lambda l: (l, 0, 0))],
        out_specs=pl.BlockSpec((None, d, out_cols), lambda l: (l, 0, 0)),
        out_shape=jax.ShapeDtypeStruct((depth, d, out_cols), BF16),
        compiler_params=_cparams(("arbitrary",), 48),
        name="regroup_w_in",
    )(jnp.swapaxes(w_in, 1, 2))


def _stacked_weights(w_in, p):
    depth = w_in.shape[0]
    w_in_all = _regroup_w_in(w_in)
    dq = MLA_NOPE + MLA_ROPE
    head_pad = lambda a: jnp.pad(a, ((0, 0),) * 3 + ((0, HEAD_LANES - a.shape[3]),)).reshape(
        depth, a.shape[1], MLA_HEADS * HEAD_LANES)
    w_uq = head_pad(p["mla_w_uq"].reshape(depth, MLA_Q_LORA, MLA_HEADS, dq))
    w_ukv = p["mla_w_ukv"].reshape(depth, MLA_KV_LORA, MLA_HEADS, MLA_NOPE + MLA_V)
    w_ukv = jnp.concatenate([head_pad(w_ukv[..., :MLA_NOPE]), head_pad(w_ukv[..., MLA_NOPE:])], axis=2)
    row = lambda name: p[name].reshape(depth, 1, -1)
    rows = ("norm1_g", "norm2_g", "mla_q_norm", "mla_kv_norm", "hy_conv_b", "hy_b1", "hy_b2",
            "conf_dw_b", "conf_ln_g", "conf_ln_b", "gate_b")
    as_is = ("hy_conv_w", "hy_w2", "hy_w3", "hy_bias", "ret_decay", "conf_dw_w")
    bf16 = ("gate_w", "hy_out", "mla_out", "ret_out", "conf_out", "w_o", "ffn_w1", "ffn_w2")
    return {
        "w_in": w_in_all,
        "mla_w_uq": w_uq.astype(BF16), "mla_w_ukv": w_ukv.astype(BF16),
        "hy_w1": jnp.pad(p["hy_w1"], ((0, 0), (0, LANES - HY_EMB), (0, 0))),
        **{name: row(name) for name in rows},
        **{name: p[name] for name in as_is},
        **{name: p[name].astype(BF16) for name in bf16},
    }


def kernel(x_prompt, x_sample, cache_mla_ckv, cache_mla_krope, state_ret, c, c_ctx, ada_w, ada_b, norm1_g, w_in, hy_conv_w, hy_conv_b, hy_w1, hy_b1, hy_w2, hy_b2, hy_w3, hy_bias, hy_out, mla_q_norm, mla_w_uq, mla_kv_norm, mla_w_ukv, mla_out, ret_decay, ret_out, conf_dw_w, conf_dw_b, conf_ln_g, conf_ln_b, conf_out, gate_w, gate_b, w_o, norm2_g, ffn_w1, ffn_w2, final_norm_g):
    p = dict(norm1_g=norm1_g, hy_conv_w=hy_conv_w, hy_conv_b=hy_conv_b, hy_w1=hy_w1, hy_b1=hy_b1,
             hy_w2=hy_w2, hy_b2=hy_b2, hy_w3=hy_w3, hy_bias=hy_bias, hy_out=hy_out,
             mla_q_norm=mla_q_norm, mla_w_uq=mla_w_uq, mla_kv_norm=mla_kv_norm, mla_w_ukv=mla_w_ukv,
             mla_out=mla_out, ret_decay=ret_decay, ret_out=ret_out, conf_dw_w=conf_dw_w,
             conf_dw_b=conf_dw_b, conf_ln_g=conf_ln_g, conf_ln_b=conf_ln_b, conf_out=conf_out,
             gate_w=gate_w, gate_b=gate_b, w_o=w_o, norm2_g=norm2_g, ffn_w1=ffn_w1, ffn_w2=ffn_w2)
    Bp, Lp, D = x_prompt.shape
    Bs, Ls, _ = x_sample.shape
    depth = w_in.shape[0]

    cond = jnp.concatenate([c_ctx[None, :], c, jnp.zeros((MOD_ROWS - 1 - Bs, D), F32)], axis=0)
    mod = _modulation(cond, ada_w, ada_b)
    tables_p = _dft_tables(Lp)
    tables_s = _dft_tables(Ls)
    final_g = final_norm_g.reshape(1, D)

    xp = x_prompt.reshape(Bp * Lp, D)
    xs = x_sample.reshape(Bs * Ls, D)
    caches, states = None, None
    weights = _stacked_weights(w_in, p)
    for l in range(depth):
        lw = dict(weights, layer=l)
        final = l == depth - 1
        xp, ckv, kr, states = _trunk_layer(xp, mod, lw, final_g, tables_p, Bp, Lp, None, None, final,
                                           caches, states)
        caches = (ckv, kr)
        xs, _, _, _ = _trunk_layer(xs, mod, lw, final_g, tables_s, Bs, Ls,
                                   (cache_mla_ckv, cache_mla_krope), state_ret, final)
    return (xp.reshape(Bp, Lp, D), xs.reshape(Bs, Ls, D), *caches, states)
```
